```python
import math
import jax
import jax.numpy as jnp
from jax import lax
import numpy as np

D_MODEL = 1024
BATCH = 16
SEQ = 256
DEPTH = 2
DEC_BATCH = 2
DEC_SEQ = 4096
PAST_LEN = 512

GRID_W = 64
Q_BLOCK = 128
HEAD_DIM = 64
ROPE_BASE = 10000.0
EPS = 1e-6
NEG_INF = -1e30
A_HEADS = 4
A_QK = 32
A_V = 64
B_HEADS = 4
B_KV = 2
WINDOW = 128
C_HEADS = 4
C_KV = 2
D_HEADS = 4
D_Q_LORA = 256
D_KV_LORA = 128
D_NOPE = 64
D_ROPE = 32
D_V = 64
D_QK = D_NOPE + D_ROPE
N_BRANCH = 4
BRANCH_W = 256
PEER_HEADS = 8
PEER_KEYS = 128
PEER_EXPERTS = PEER_KEYS * PEER_KEYS
PEER_QDIM = 256
PEER_HALF = PEER_QDIM // 2
PEER_TOPK = 16
PEER_CHUNK = 128
IN_SIZES = (A_HEADS * 2 * A_QK, A_HEADS * 2 * A_QK, A_HEADS * A_V,
            B_HEADS * HEAD_DIM, B_KV * HEAD_DIM, B_KV * HEAD_DIM,
            C_HEADS * HEAD_DIM, C_KV * HEAD_DIM, C_KV * HEAD_DIM,
            D_Q_LORA, D_KV_LORA, D_ROPE, N_BRANCH * D_MODEL)
IN_COLS = sum(IN_SIZES)

kernel_name = 'hybrid_diffusion_prefix_trunk_step'


def _rms(x, g):
    x32 = x.astype(jnp.float32)
    y = x32 * lax.rsqrt(jnp.mean(x32 * x32, axis=-1, keepdims=True) + EPS)
    return y.astype(x.dtype) * g


def _axial_rope(x):
    T, d = x.shape[1], x.shape[-1]
    rows = T // GRID_W
    q4 = d // 4
    inv = ROPE_BASE ** (-jnp.arange(q4, dtype=jnp.float32) / q4)
    row = jnp.repeat(jnp.arange(rows, dtype=jnp.float32), GRID_W)
    col = jnp.tile(jnp.arange(GRID_W, dtype=jnp.float32), rows)
    ar = row[:, None] * inv
    ac = col[:, None] * inv
    ang = jnp.concatenate([ar, ar, ac, ac], axis=-1)
    shape = (1, T) + (1,) * (x.ndim - 3) + (d,)
    cos = jnp.cos(ang).reshape(shape)
    sin = jnp.sin(ang).reshape(shape)
    xr = x.reshape(x.shape[:-1] + (2, 2, q4))
    rot = jnp.stack([-xr[..., 1, :], xr[..., 0, :]], axis=-2).reshape(x.shape)
    return (x * cos + rot * sin).astype(x.dtype)


def _split_in(proj):
    offs = [int(o) for o in np.cumsum(IN_SIZES)[:-1]]
    return jnp.split(proj, offs, axis=-1)


def _to_blocks(x):
    B, T = x.shape[:2]
    return x.reshape((B, T // Q_BLOCK, Q_BLOCK) + x.shape[2:]).swapaxes(0, 1)


def _from_blocks(y):
    nb, B = y.shape[:2]
    return y.swapaxes(0, 1).reshape((B, nb * Q_BLOCK) + y.shape[3:])


def _attend(q, k, v, scale, bias=None, sink=None):
    s = jnp.einsum('bqhgd,bkhd->bhgqk', q, k).astype(jnp.float32) * scale
    if bias is not None:
        s = s + bias
    if sink is None:
        p = jax.nn.softmax(s, axis=-1)
    else:
        sk = sink.astype(jnp.float32)[None, :, :, None, None]
        m = jnp.maximum(jnp.max(s, axis=-1, keepdims=True), sk)
        e = jnp.exp(s - m)
        p = e / (jnp.sum(e, axis=-1, keepdims=True) + jnp.exp(sk - m))
    return jnp.einsum('bhgqk,bkhd->bqhgd', p.astype(v.dtype), v)


def _dense_gqa(q, k, v, scale, sink=None):
    B, T, H, d = q.shape
    n_kv = k.shape[2]
    qg = q.reshape(B, T, n_kv, H // n_kv, d)
    out = lax.map(lambda qb: _attend(qb, k, v, scale, sink=sink), _to_blocks(qg))
    return _from_blocks(out).reshape(B, T, H * v.shape[-1])


def _window_gqa(q, k_lat, v_lat, k_ctx, v_ctx, scale, sink):
    B, T, H, d = q.shape
    n_kv = k_lat.shape[2]
    nb = T // Q_BLOCK
    L = k_ctx.shape[1]

    def band(x):
        xp = jnp.pad(x, ((0, 0), (Q_BLOCK, Q_BLOCK), (0, 0), (0, 0)))
        xb = xp.reshape((B, nb + 2, Q_BLOCK) + x.shape[2:])
        xb = jnp.concatenate([xb[:, :-2], xb[:, 1:-1], xb[:, 2:]], axis=2)
        return xb.swapaxes(0, 1)

    kb, vb = band(k_lat), band(v_lat)
    r = jnp.arange(Q_BLOCK)
    cpos = jnp.arange(3 * Q_BLOCK)
    rel = cpos[None, :] - Q_BLOCK - r[:, None]

    def blk(args):
        qb, kbb, vbb, i = args
        kpos = (i - 1) * Q_BLOCK + cpos
        valid = (jnp.abs(rel) <= WINDOW) & ((kpos >= 0) & (kpos < T))[None, :]
        bias = jnp.concatenate([jnp.zeros((Q_BLOCK, L), jnp.float32),
                                jnp.where(valid, 0.0, NEG_INF).astype(jnp.float32)], axis=1)
        kk = jnp.concatenate([k_ctx, kbb], axis=1)
        vv = jnp.concatenate([v_ctx, vbb], axis=1)
        return _attend(qb, kk, vv, scale, bias=bias, sink=sink)

    qg = q.reshape(B, T, n_kv, H // n_kv, d)
    out = lax.map(blk, (_to_blocks(qg), kb, vb, jnp.arange(nb)))
    return _from_blocks(out).reshape(B, T, H * v_lat.shape[-1])


def _diff_dense(q, k, v, lam, scale):
    def blk(qb):
        s = jnp.einsum('bqhid,bkhid->bhiqk', qb, k).astype(jnp.float32) * scale
        p = jax.nn.softmax(s, axis=-1)
        p = p[:, :, 0] - lam * p[:, :, 1]
        return jnp.einsum('bhqk,bkhd->bqhd', p.astype(v.dtype), v)
    return _from_blocks(lax.map(blk, _to_blocks(q)))


def _mla_kv(ckv, kr, w_kvb, kn_g):
    B, L, _ = ckv.shape
    kv = (ckv @ w_kvb).reshape(B, L, D_HEADS, D_NOPE + D_V)
    k = jnp.concatenate([kv[..., :D_NOPE],
                         jnp.broadcast_to(kr[:, :, None, :], (B, L, D_HEADS, D_ROPE))], axis=-1)
    return _rms(k, kn_g), kv[..., D_NOPE:]


def _peer(h, wq, subkeys, u_tab, v_tab):
    B, T, D = h.shape
    n = B * T
    x = h.reshape(n, D)
    q = (x @ wq).reshape(n, PEER_HEADS, 2, PEER_HALF)
    s = jnp.einsum('nhpd,hpkd->nhpk', q, subkeys).astype(jnp.float32)
    sv, si = lax.top_k(s, PEER_TOPK)
    cand = (sv[:, :, 0, :, None] + sv[:, :, 1, None, :]).reshape(n, PEER_HEADS, PEER_TOPK * PEER_TOPK)
    cv, ci = lax.top_k(cand, PEER_TOPK)
    e1 = jnp.take_along_axis(si[:, :, 0, :], ci // PEER_TOPK, axis=-1)
    e2 = jnp.take_along_axis(si[:, :, 1, :], ci % PEER_TOPK, axis=-1)
    experts = (e1 * PEER_KEYS + e2).reshape(n // PEER_CHUNK, PEER_CHUNK, PEER_HEADS * PEER_TOPK)
    gates = jax.nn.softmax(cv, axis=-1).astype(h.dtype).reshape(n // PEER_CHUNK, PEER_CHUNK, PEER_HEADS * PEER_TOPK)
    xb = x.reshape(n // PEER_CHUNK, PEER_CHUNK, D)

    def blk(args):
        xc, ec, gc = args
        u = jnp.take(u_tab, ec, axis=0)
        a = jax.nn.gelu(jnp.einsum('cd,ckd->ck', xc, u), approximate=False)
        v = jnp.take(v_tab, ec, axis=0)
        return jnp.einsum('ck,ckd->cd', gc * a, v)

    return lax.map(blk, (xb, experts, gates)).reshape(B, T, D)


def _layer(x, mod, ctx, p, lam_init):
    is_ctx = ctx is None
    B, T, _ = x.shape
    sh1, sc1, g1, sh2, sc2, g2 = jnp.split(mod, 6, axis=-1)
    h = _rms(x, p['norm1_g']) * (1 + sc1) + sh1
    aq, ak, av, bq, bk, bv, cq, ck, cv, dqa, dckv, dkr, gates = _split_in(h @ p['w_in'])
    pos = (lambda t: t) if is_ctx else _axial_rope

    aq = pos(_rms(aq.reshape(B, T, A_HEADS, 2, A_QK), p['a_qk_g'][0]))
    ak = pos(_rms(ak.reshape(B, T, A_HEADS, 2, A_QK), p['a_qk_g'][1]))
    av = av.reshape(B, T, A_HEADS, A_V)
    lp = p['a_lam'].astype(jnp.float32)
    lam = jnp.exp(jnp.sum(lp[0] * lp[1])) - jnp.exp(jnp.sum(lp[2] * lp[3])) + lam_init
    ka = ak if is_ctx else jnp.concatenate([ctx['a_k'], ak], axis=1)
    va = av if is_ctx else jnp.concatenate([ctx['a_v'], av], axis=1)
    oa = _rms(_diff_dense(aq, ka, va, lam, A_QK ** -0.5), p['a_subln_g']) * (1.0 - lam_init)
    oa = oa.reshape(B, T, A_HEADS * A_V)

    bq = pos(_rms(bq.reshape(B, T, B_HEADS, HEAD_DIM), p['b_qk_g'][0]))
    bk = pos(_rms(bk.reshape(B, T, B_KV, HEAD_DIM), p['b_qk_g'][1]))
    bv = bv.reshape(B, T, B_KV, HEAD_DIM)
    sink = p['b_sink'].reshape(B_KV, B_HEADS // B_KV)
    if is_ctx:
        ob = _dense_gqa(bq, bk, bv, HEAD_DIM ** -0.5, sink)
    else:
        ob = _window_gqa(bq, bk, bv, ctx['b_k'], ctx['b_v'], HEAD_DIM ** -0.5, sink)

    cq = pos(_rms(cq.reshape(B, T, C_HEADS, HEAD_DIM), p['c_qk_g'][0]))
    ck = pos(_rms(ck.reshape(B, T, C_KV, HEAD_DIM), p['c_qk_g'][1]))
    cv = cv.reshape(B, T, C_KV, HEAD_DIM)
    kc = ck if is_ctx else jnp.concatenate([ctx['c_k'], ck], axis=1)
    vc = cv if is_ctx else jnp.concatenate([ctx['c_v'], cv], axis=1)
    oc = _dense_gqa(cq, kc, vc, HEAD_DIM ** -0.5)

    dq = (_rms(dqa, p['d_qnorm_g']) @ p['w_d_qb']).reshape(B, T, D_HEADS, D_QK)
    dq = _rms(dq, p['d_qk_g'][0])
    dckv = _rms(dckv, p['d_kvnorm_g'])
    kd, vd = _mla_kv(dckv, dkr, p['w_d_kvb'], p['d_qk_g'][1])
    if not is_ctx:
        dq = jnp.concatenate([dq[..., :D_NOPE], _axial_rope(dq[..., D_NOPE:])], axis=-1)
        kd = jnp.concatenate([kd[..., :D_NOPE], _axial_rope(kd[..., D_NOPE:])], axis=-1)
        kx, vx = _mla_kv(ctx['d_ckv'], ctx['d_krope'], p['w_d_kvb'], p['d_qk_g'][1])
        kd = jnp.concatenate([kx, kd], axis=1)
        vd = jnp.concatenate([vx, vd], axis=1)
    od = _dense_gqa(dq, kd, vd, D_QK ** -0.5)

    outs = jnp.stack([oa, ob, oc, od], axis=2)
    br = jnp.einsum('btie,ied->btid', outs, p['w_branch'])
    gt = jax.nn.sigmoid(gates.reshape(B, T, N_BRANCH, D_MODEL))
    y = jnp.sum(gt * br, axis=2) @ p['w_out']
    x = x + g1 * y

    h2 = _rms(x, p['norm2_g']) * (1 + sc2) + sh2
    x = x + g2 * _peer(h2, p['peer_wq'], p['peer_subkeys'], p['peer_u'], p['peer_v'])
    state = (ak, av, bk, bv, ck, cv, dckv, dkr) if is_ctx else None
    return x, state


def setup_inputs(seed: int = 0) -> dict:
    key = jax.random.key(seed)
    ks = iter(jax.random.split(key, 48))

    def nrm(shape, scale=1.0):
        return jax.random.normal(next(ks), shape, jnp.float32) * scale

    def gain(shape):
        return 1.0 + nrm(shape, 0.01)

    return {
        'x_prompt': nrm((BATCH, SEQ, D_MODEL)),
        'x_sample': nrm((DEC_BATCH, DEC_SEQ, D_MODEL)),
        'c': nrm((DEC_BATCH, D_MODEL)),
        'cache_a_k': nrm((DEC_BATCH, DEPTH, PAST_LEN, A_HEADS, 2, A_QK)),
        'cache_a_v': nrm((DEC_BATCH, DEPTH, PAST_LEN, A_HEADS, A_V)),
        'cache_b_k': nrm((DEC_BATCH, DEPTH, PAST_LEN, B_KV, HEAD_DIM)),
        'cache_b_v': nrm((DEC_BATCH, DEPTH, PAST_LEN, B_KV, HEAD_DIM)),
        'cache_c_k': nrm((DEC_BATCH, DEPTH, PAST_LEN, C_KV, HEAD_DIM)),
        'cache_c_v': nrm((DEC_BATCH, DEPTH, PAST_LEN, C_KV, HEAD_DIM)),
        'cache_d_ckv': nrm((DEC_BATCH, DEPTH, PAST_LEN, D_KV_LORA)),
        'cache_d_krope': nrm((DEC_BATCH, DEPTH, PAST_LEN, D_ROPE)),
        'c_ctx': nrm((D_MODEL,)),
        'ada_w': nrm((DEPTH, D_MODEL, 6 * D_MODEL), 0.5 * D_MODEL ** -0.5),
        'ada_b': nrm((DEPTH, 6 * D_MODEL), 0.01),
        'norm1_g': gain((DEPTH, D_MODEL)),
        'norm2_g': gain((DEPTH, D_MODEL)),
        'w_in': nrm((DEPTH, D_MODEL, IN_COLS), D_MODEL ** -0.5),
        'a_lam': nrm((DEPTH, 4, A_QK), 0.1),
        'a_qk_g': gain((DEPTH, 2, A_QK)),
        'a_subln_g': gain((DEPTH, A_V)),
        'b_sink': nrm((DEPTH, B_HEADS)),
        'b_qk_g': gain((DEPTH, 2, HEAD_DIM)),
        'c_qk_g': gain((DEPTH, 2, HEAD_DIM)),
        'd_qnorm_g': gain((DEPTH, D_Q_LORA)),
        'd_kvnorm_g': gain((DEPTH, D_KV_LORA)),
        'w_d_qb': nrm((DEPTH, D_Q_LORA, D_HEADS * D_QK), D_Q_LORA ** -0.5),
        'w_d_kvb': nrm((DEPTH, D_KV_LORA, D_HEADS * (D_NOPE + D_V)), D_KV_LORA ** -0.5),
        'd_qk_g': gain((DEPTH, 2, D_QK)),
        'w_branch': nrm((DEPTH, N_BRANCH, BRANCH_W, D_MODEL), BRANCH_W ** -0.5),
        'w_out': nrm((DEPTH, D_MODEL, D_MODEL), D_MODEL ** -0.5),
        'peer_wq': nrm((DEPTH, D_MODEL, PEER_HEADS * PEER_QDIM), D_MODEL ** -0.5),
        'peer_subkeys': nrm((DEPTH, PEER_HEADS, 2, PEER_KEYS, PEER_HALF), PEER_HALF ** -0.5),
        'peer_u': nrm((DEPTH, PEER_EXPERTS, D_MODEL), D_MODEL ** -0.5),
        'peer_v': nrm((DEPTH, PEER_EXPERTS, D_MODEL), PEER_HEADS ** -0.5),
    }


def reference(x_prompt, x_sample, c, cache_a_k, cache_a_v, cache_b_k, cache_b_v, cache_c_k, cache_c_v,
              cache_d_ckv, cache_d_krope, c_ctx, ada_w, ada_b, norm1_g, norm2_g, w_in, a_lam, a_qk_g,
              a_subln_g, b_sink, b_qk_g, c_qk_g, d_qnorm_g, d_kvnorm_g, w_d_qb, w_d_kvb, d_qk_g,
              w_branch, w_out, peer_wq, peer_subkeys, peer_u, peer_v):
    y_prompt = x_prompt
    y_sample = x_sample
    states = [[] for _ in range(8)]
    for l in range(DEPTH):
        p = {'norm1_g': norm1_g[l], 'norm2_g': norm2_g[l], 'w_in': w_in[l], 'a_lam': a_lam[l],
             'a_qk_g': a_qk_g[l], 'a_subln_g': a_subln_g[l], 'b_sink': b_sink[l], 'b_qk_g': b_qk_g[l],
             'c_qk_g': c_qk_g[l], 'd_qnorm_g': d_qnorm_g[l], 'd_kvnorm_g': d_kvnorm_g[l],
             'w_d_qb': w_d_qb[l], 'w_d_kvb': w_d_kvb[l], 'd_qk_g': d_qk_g[l], 'w_branch': w_branch[l],
             'w_out': w_out[l], 'peer_wq': peer_wq[l], 'peer_subkeys': peer_subkeys[l],
             'peer_u': peer_u[l], 'peer_v': peer_v[l]}
        lam_init = 0.8 - 0.6 * math.exp(-0.3 * l)
        mod_ctx = (jax.nn.silu(c_ctx) @ ada_w[l] + ada_b[l])[None, None, :]
        y_prompt, st = _layer(y_prompt, mod_ctx, None, p, lam_init)
        for lst, s in zip(states, st):
            lst.append(s)
        ctx = {'a_k': cache_a_k[:, l], 'a_v': cache_a_v[:, l], 'b_k': cache_b_k[:, l], 'b_v': cache_b_v[:, l],
               'c_k': cache_c_k[:, l], 'c_v': cache_c_v[:, l], 'd_ckv': cache_d_ckv[:, l],
               'd_krope': cache_d_krope[:, l]}
        mod_lat = (jax.nn.silu(c) @ ada_w[l] + ada_b[l])[:, None, :]
        y_sample, _ = _layer(y_sample, mod_lat, ctx, p, lam_init)
    new_a_k = jnp.stack(states[0], axis=1)
    new_a_v = jnp.stack(states[1], axis=1)
    new_b_k = jnp.stack(states[2], axis=1)
    new_b_v = jnp.stack(states[3], axis=1)
    new_c_k = jnp.stack(states[4], axis=1)
    new_c_v = jnp.stack(states[5], axis=1)
    new_d_ckv = jnp.stack(states[6], axis=1)
    new_d_krope = jnp.stack(states[7], axis=1)
    return (y_prompt, y_sample, new_a_k, new_a_v, new_b_k, new_b_v, new_c_k, new_c_v, new_d_ckv, new_d_krope)
```

```python
import functools
import math

import jax
import jax.numpy as jnp
from jax import lax
from jax.experimental import pallas as pl
from jax.experimental.pallas import tpu as pltpu

F32 = jnp.float32
BF16 = jnp.bfloat16

D_MODEL = 1024
BATCH = 16
SEQ = 256
DEPTH = 2
DEC_BATCH = 2
DEC_SEQ = 4096
PAST_LEN = 512
GRID_W = 64
Q_BLOCK = 128
HEAD_DIM = 64
ROPE_BASE = 10000.0
EPS = 1e-6
NEG_INF = -1e30
A_HEADS = 4
A_QK = 32
A_V = 64
B_HEADS = 4
B_KV = 2
WINDOW = 128
C_HEADS = 4
C_KV = 2
D_HEADS = 4
D_Q_LORA = 256
D_KV_LORA = 128
D_NOPE = 64
D_ROPE = 32
D_V = 64
D_QK = D_NOPE + D_ROPE
N_BRANCH = 4
BRANCH_W = 256
PEER_HEADS = 8
PEER_KEYS = 128
PEER_EXPERTS = PEER_KEYS * PEER_KEYS
PEER_QDIM = 256
PEER_HALF = PEER_QDIM // 2
PEER_TOPK = 16

N_CTX = BATCH * SEQ
N_LAT = DEC_BATCH * DEC_SEQ
N_TOK = N_CTX + N_LAT
KV_LAT = PAST_LEN + DEC_SEQ
N_MODROWS = 8
SMALL_COLS = 2208
SMALL_PAD = 2304

VMEM_LIMIT = 56 * 1024 * 1024
EXHAUSTED = -3.0e38


def _params(sem):
    return pltpu.CompilerParams(dimension_semantics=sem, vmem_limit_bytes=VMEM_LIMIT)


def _nt_dot(a, b):
    return lax.dot_general(a, b, (((1,), (1,)), ((), ())), preferred_element_type=F32)


def _dot(a, b):
    return jnp.dot(a, b, preferred_element_type=F32)


def _mod_kernel(c_ref, w_ref, b_ref, o_ref):
    c = c_ref[...]
    s = c * jax.nn.sigmoid(c)
    o_ref[...] = jnp.dot(s, w_ref[...], preferred_element_type=F32, precision=lax.Precision.HIGHEST) + b_ref[...]


def _modulation(cvec, ada_w, ada_b):
    depth = ada_w.shape[0]
    ncol = ada_w.shape[2] // D_MODEL
    return pl.pallas_call(
        _mod_kernel,
        grid=(depth, ncol),
        in_specs=[
            pl.BlockSpec((N_MODROWS, D_MODEL), lambda l, j: (0, 0)),
            pl.BlockSpec((None, D_MODEL, D_MODEL), lambda l, j: (l, 0, j)),
            pl.BlockSpec((None, 1, D_MODEL), lambda l, j: (l, 0, j)),
        ],
        out_specs=pl.BlockSpec((None, N_MODROWS, D_MODEL), lambda l, j: (l, 0, j)),
        out_shape=jax.ShapeDtypeStruct((depth, N_MODROWS, ada_w.shape[2]), F32),
        compiler_params=_params(("arbitrary", "arbitrary")),
        name="modulation",
    )(cvec, ada_w, ada_b.reshape(depth, 1, -1))


def _group_rms(x, ones_bd, gain, d):
    sq = x * x
    hi = sq.astype(BF16)
    lo = (sq - hi.astype(F32)).astype(BF16)
    ss = _dot(hi, ones_bd) + _dot(lo, ones_bd)
    return x * lax.rsqrt(ss * (1.0 / d) + EPS) * gain


def _row_rms(x, gain):
    ms = jnp.mean(x * x, axis=-1, keepdims=True)
    return x * lax.rsqrt(ms + EPS) * gain


def _rope(x, tab_ref, q4):
    w = x.shape[-1]
    return x * tab_ref[0] + pltpu.roll(x, w - q4, 1) * tab_ref[1] + pltpu.roll(x, q4, 1) * tab_ref[2]


GAIN_ROWS = 16
GAIN_W = D_HEADS * D_QK


def _in_kernel(x_ref, mod_ref, n1g_ref, ws_ref, gains_ref, e32_ref, e64_ref, e96_ref, wdq_ref,
               ta_ref, tb_ref, td_ref,
               h_ref, aq_ref, ak_ref, av_ref, bq_ref, bk_ref, bv_ref, cq_ref, ck_ref, cv_ref, dq_ref,
               akt_ref, avt_ref, bkt_ref, bvt_ref, ckt_ref, cvt_ref, dckv_ref, dkr_ref, *, tm, n_ctx):
    is_lat = pl.program_id(0) * tm >= n_ctx
    x = x_ref[...]
    h = _row_rms(x, n1g_ref[...]) * (1.0 + mod_ref[1:2, :]) + mod_ref[0:1, :]
    hb = h.astype(BF16)
    h_ref[...] = hb
    p = _dot(hb, ws_ref[...])

    def gain(r, w):
        return gains_ref[r:r + 1, 0:w]

    def heads_out(ref, val, nh, d):
        for i in range(nh):
            ref[i] = val[:, i * d:(i + 1) * d].astype(BF16)

    def maybe_rope(val, tab_ref, q4):
        return jnp.where(is_lat, _rope(val, tab_ref, q4), val)

    aq = maybe_rope(_group_rms(p[:, 0:256], e32_ref[...], gain(0, 256), A_QK), ta_ref, A_QK // 4)
    ak = maybe_rope(_group_rms(p[:, 256:512], e32_ref[...], gain(1, 256), A_QK), ta_ref, A_QK // 4)
    av = p[:, 512:768]
    heads_out(aq_ref, aq * (A_QK ** -0.5), 2 * A_HEADS, A_QK)
    heads_out(ak_ref, ak, 2 * A_HEADS, A_QK)
    heads_out(av_ref, av, A_HEADS, A_V)
    akt_ref[...] = ak
    avt_ref[...] = av
    bq = maybe_rope(_group_rms(p[:, 768:1024], e64_ref[...], gain(2, 256), HEAD_DIM), tb_ref, HEAD_DIM // 4)
    bk = _group_rms(p[:, 1024:1152], e64_ref[0:128, 0:128], gain(3, 128), HEAD_DIM)
    bk = jnp.where(is_lat, bk * tb_ref[0, :, 0:128] + pltpu.roll(bk, 128 - 16, 1) * tb_ref[1, :, 0:128]
                   + pltpu.roll(bk, 16, 1) * tb_ref[2, :, 0:128], bk)
    bv = p[:, 1152:1280]
    heads_out(bq_ref, bq * (HEAD_DIM ** -0.5), B_HEADS, HEAD_DIM)
    heads_out(bk_ref, bk, B_KV, HEAD_DIM)
    heads_out(bv_ref, bv, B_KV, HEAD_DIM)
    bkt_ref[...] = bk
    bvt_ref[...] = bv
    cq = maybe_rope(_group_rms(p[:, 1280:1536], e64_ref[...], gain(4, 256), HEAD_DIM), tb_ref, HEAD_DIM // 4)
    ck = _group_rms(p[:, 1536:1664], e64_ref[0:128, 0:128], gain(5, 128), HEAD_DIM)
    ck = jnp.where(is_lat, ck * tb_ref[0, :, 0:128] + pltpu.roll(ck, 128 - 16, 1) * tb_ref[1, :, 0:128]
                   + pltpu.roll(ck, 16, 1) * tb_ref[2, :, 0:128], ck)
    cv = p[:, 1664:1792]
    heads_out(cq_ref, cq * (HEAD_DIM ** -0.5), C_HEADS, HEAD_DIM)
    heads_out(ck_ref, ck, C_KV, HEAD_DIM)
    heads_out(cv_ref, cv, C_KV, HEAD_DIM)
    ckt_ref[...] = ck
    cvt_ref[...] = cv
    dqa = _row_rms(p[:, 1792:2048], gain(6, 256))
    dq = _dot(dqa.astype(BF16), wdq_ref[...])
    dq = maybe_rope(_group_rms(dq, e96_ref[...], gain(8, GAIN_W), D_QK), td_ref, D_ROPE // 4)
    heads_out(dq_ref, dq * (D_QK ** -0.5), D_HEADS, D_QK)
    dckv_ref[...] = _row_rms(p[:, 2048:2176], gain(7, 128))
    dkr_ref[...] = p[:, 2176:2208]


def _input_stage(x, mod3, n1g, ws, gains, e32, e64, e96, wdq, tab_a, tab_b, tab_d, *, tm=256):
    n = x.shape[0]
    nt = n // tm
    lat0 = N_CTX // tm
    pos_blocks = DEC_SEQ // tm

    def tok(i):
        return (i, 0)

    def pos(i):
        return (0, jnp.maximum(i - lat0, 0) % pos_blocks, 0)

    def const2(i):
        return (0, 0)

    def hm(i):
        return (0, i, 0)

    def hspec(nh, d):
        return pl.BlockSpec((nh, tm, d), hm)

    def hshape(nh, d):
        return jax.ShapeDtypeStruct((nh, n, d), BF16)

    def tspec(w):
        return pl.BlockSpec((tm, w), tok)

    def tshape(w):
        return jax.ShapeDtypeStruct((n, w), F32)

    in_specs = [
        pl.BlockSpec((tm, D_MODEL), tok),
        pl.BlockSpec((None, 6, D_MODEL), lambda i: ((i * tm) // DEC_SEQ, 0, 0)),
        pl.BlockSpec((1, D_MODEL), const2),
        pl.BlockSpec((D_MODEL, SMALL_PAD), const2),
        pl.BlockSpec((GAIN_ROWS, GAIN_W), const2),
        pl.BlockSpec((256, 256), const2),
        pl.BlockSpec((256, 256), const2),
        pl.BlockSpec((GAIN_W, GAIN_W), const2),
        pl.BlockSpec((D_Q_LORA, GAIN_W), const2),
        pl.BlockSpec((3, tm, 256), pos),
        pl.BlockSpec((3, tm, 256), pos),
        pl.BlockSpec((3, tm, GAIN_W), pos),
    ]
    out_specs = [
        pl.BlockSpec((tm, D_MODEL), tok),
        hspec(8, A_QK), hspec(8, A_QK), hspec(4, A_V),
        hspec(4, HEAD_DIM), hspec(2, HEAD_DIM), hspec(2, HEAD_DIM),
        hspec(4, HEAD_DIM), hspec(2, HEAD_DIM), hspec(2, HEAD_DIM),
        hspec(4, D_QK),
        tspec(256), tspec(256), tspec(128), tspec(128), tspec(128), tspec(128), tspec(128), tspec(D_ROPE),
    ]
    out_shape = [
        jax.ShapeDtypeStruct((n, D_MODEL), BF16),
        hshape(8, A_QK), hshape(8, A_QK), hshape(4, A_V),
        hshape(4, HEAD_DIM), hshape(2, HEAD_DIM), hshape(2, HEAD_DIM),
        hshape(4, HEAD_DIM), hshape(2, HEAD_DIM), hshape(2, HEAD_DIM),
        hshape(4, D_QK),
        tshape(256), tshape(256), tshape(128), tshape(128), tshape(128), tshape(128), tshape(128), tshape(D_ROPE),
    ]
    return pl.pallas_call(
        functools.partial(_in_kernel, tm=tm, n_ctx=N_CTX),
        grid=(nt,),
        in_specs=in_specs,
        out_specs=out_specs,
        out_shape=out_shape,
        compiler_params=_params(("arbitrary",)),
        name="input_stage",
    )(x, mod3, n1g, ws, gains, e32, e64, e96, wdq, tab_a, tab_b, tab_d)


def _mla_kernel(ckv_ref, kr_ref, wkvb_ref, g_ref, rot_ref, tab_ref, kd_ref, vd_ref, *, tm, lat_tiles, per_batch_tiles):
    i = pl.program_id(0)
    rotate = jnp.logical_and(i < lat_tiles, i % per_batch_tiles >= PAST_LEN // tm)
    kv = _dot(ckv_ref[...].astype(BF16), wkvb_ref[...])
    kr = kr_ref[...]
    ssr = jnp.sum(kr * kr, axis=-1, keepdims=True)
    z = kr * g_ref[:, D_NOPE:D_QK]
    zh = z.astype(BF16)
    zl = (z - zh.astype(F32)).astype(BF16)
    rot = _dot(zh, rot_ref[...]) + _dot(zl, rot_ref[...])
    rz = jnp.where(rotate, z * tab_ref[0] + rot * tab_ref[1], z)
    g_nope = g_ref[:, 0:D_NOPE]
    for hd in range(D_HEADS):
        kn = kv[:, hd * 128:hd * 128 + D_NOPE]
        r = lax.rsqrt((jnp.sum(kn * kn, axis=-1, keepdims=True) + ssr) * (1.0 / D_QK) + EPS)
        kd_ref[hd] = jnp.concatenate([kn * r * g_nope, rz * r], axis=-1).astype(BF16)
        vd_ref[hd] = kv[:, hd * 128 + D_NOPE:(hd + 1) * 128].astype(BF16)


def _mla_expand(ckv, kr, wkvb, g, rot, tab, *, tm=512):
    rows = ckv.shape[0]
    per_batch_tiles = KV_LAT // tm
    lat_tiles = DEC_BATCH * per_batch_tiles
    past_tiles = PAST_LEN // tm
    pos_blocks = DEC_SEQ // tm

    def pos(i):
        return (0, jnp.maximum(i % per_batch_tiles - past_tiles, 0) % pos_blocks, 0)

    return pl.pallas_call(
        functools.partial(_mla_kernel, tm=tm, lat_tiles=lat_tiles, per_batch_tiles=per_batch_tiles),
        grid=(rows // tm,),
        in_specs=[
            pl.BlockSpec((tm, D_KV_LORA), lambda i: (i, 0)),
            pl.BlockSpec((tm, D_ROPE), lambda i: (i, 0)),
            pl.BlockSpec((D_KV_LORA, D_HEADS * (D_NOPE + D_V)), lambda i: (0, 0)),
            pl.BlockSpec((1, D_QK), lambda i: (0, 0)),
            pl.BlockSpec((D_ROPE, D_ROPE), lambda i: (0, 0)),
            pl.BlockSpec((2, tm, D_ROPE), pos),
        ],
        out_specs=[
            pl.BlockSpec((D_HEADS, tm, D_QK), lambda i: (0, i, 0)),
            pl.BlockSpec((D_HEADS, tm, D_V), lambda i: (0, i, 0)),
        ],
        out_shape=[
            jax.ShapeDtypeStruct((D_HEADS, rows, D_QK), BF16),
            jax.ShapeDtypeStruct((D_HEADS, rows, D_V), BF16),
        ],
        compiler_params=_params(("arbitrary",)),
        name="mla_expand",
    )(ckv, kr, wkvb, g, rot, tab)


def _attn_kernel(*refs, g_q, g_k, g_v, dv, mode, lam_init, has_prev):
    refs = list(refs)
    q_ref, k_ref, v_ref = refs[:3]
    pos = 3
    sink_ref = lam_ref = sub_ref = None
    if mode == "sink":
        sink_ref = refs[pos]
        pos += 1
    elif mode == "diff":
        lam_ref, sub_ref = refs[pos], refs[pos + 1]
        pos += 2
    if has_prev:
        pos += 1
    o_ref, m_scr, l_scr, acc_scr = refs[pos:pos + 4]
    grp = pl.program_id(1)
    ki = pl.program_id(3)
    tq = q_ref.shape[1]

    @pl.when(ki == 0)
    def _init():
        for g in range(g_q):
            if mode == "sink":
                m_scr[g] = jnp.full((tq, 1), sink_ref[grp * g_q + g], F32)
                l_scr[g] = jnp.ones((tq, 1), F32)
            else:
                m_scr[g] = jnp.full((tq, 1), NEG_INF, F32)
                l_scr[g] = jnp.zeros((tq, 1), F32)
            acc_scr[g] = jnp.zeros((tq, dv), F32)

    for g in range(g_q):
        s = _nt_dot(q_ref[g], k_ref[g * g_k // g_q])
        m_old = m_scr[g]
        m_new = jnp.maximum(m_old, jnp.max(s, axis=-1, keepdims=True))
        alpha = jnp.exp(m_old - m_new)
        p = jnp.exp(s - m_new)
        l_scr[g] = alpha * l_scr[g] + jnp.sum(p, axis=-1, keepdims=True)
        acc_scr[g] = alpha * acc_scr[g] + _dot(p.astype(BF16), v_ref[g * g_v // g_q])
        m_scr[g] = m_new

    @pl.when(ki == pl.num_programs(3) - 1)
    def _fin():
        outs = [acc_scr[g] * (1.0 / l_scr[g]) for g in range(g_q)]
        if mode == "diff":
            lp = lam_ref[...]
            lam = (jnp.exp(jnp.sum(lp[0:1] * lp[1:2], axis=-1, keepdims=True))
                   - jnp.exp(jnp.sum(lp[2:3] * lp[3:4], axis=-1, keepdims=True)) + lam_init)
            for i in range(g_q // 2):
                d = outs[2 * i] - lam * outs[2 * i + 1]
                o_ref[:, i * dv:(i + 1) * dv] = _row_rms(d, sub_ref[...]) * (1.0 - lam_init)
        else:
            for g in range(g_q):
                o_ref[:, g * dv:(g + 1) * dv] = outs[g]


def _attention(q, k, v, extras, o_prev, *, batch, tq_rows, tk_rows, q_off, k_off, o_off, n_out,
               g_q, g_k, g_v, dv, mode, lam_init, tq, tk):
    hq, _, dq = q.shape
    n_groups = hq // g_q
    g_out = g_q // 2 if mode == "diff" else g_q
    nq = tq_rows // tq
    nk = tk_rows // tk
    in_specs = [
        pl.BlockSpec((g_q, tq, dq), lambda b, g, qi, ki: (g, (q_off + b * tq_rows) // tq + qi, 0)),
        pl.BlockSpec((g_k, tk, dq), lambda b, g, qi, ki: (g, (k_off + b * tk_rows) // tk + ki, 0)),
        pl.BlockSpec((g_v, tk, dv), lambda b, g, qi, ki: (g, (k_off + b * tk_rows) // tk + ki, 0)),
    ]
    args = [q, k, v]
    if mode == "sink":
        in_specs.append(pl.BlockSpec(memory_space=pltpu.SMEM))
        args.append(extras[0])
    elif mode == "diff":
        in_specs.append(pl.BlockSpec((4, A_QK), lambda b, g, qi, ki: (0, 0)))
        in_specs.append(pl.BlockSpec((1, dv), lambda b, g, qi, ki: (0, 0)))
        args += list(extras)
    aliases = {}
    if o_prev is not None:
        in_specs.append(pl.BlockSpec(memory_space=pl.ANY))
        aliases = {len(args): 0}
        args.append(o_prev)
    return pl.pallas_call(
        functools.partial(_attn_kernel, g_q=g_q, g_k=g_k, g_v=g_v, dv=dv, mode=mode, lam_init=lam_init,
                          has_prev=o_prev is not None),
        grid=(batch, n_groups, nq, nk),
        in_specs=in_specs,
        out_specs=pl.BlockSpec((tq, g_out * dv), lambda b, g, qi, ki: ((o_off + b * tq_rows) // tq + qi, g)),
        out_shape=jax.ShapeDtypeStruct((n_out, n_groups * g_out * dv), F32),
        scratch_shapes=[
            pltpu.VMEM((g_q, tq, 1), F32),
            pltpu.VMEM((g_q, tq, 1), F32),
            pltpu.VMEM((g_q, tq, dv), F32),
        ],
        input_output_aliases=aliases,
        compiler_params=_params(("arbitrary", "arbitrary", "arbitrary", "arbitrary")),
        name="attention_" + mode,
    )(*args)


def _win_kernel(q_ref, k_ref, v_ref, sink_ref, prev_ref, o_ref, *, g_q, dv, tq, seq, past, band):
    del prev_ref
    grp = pl.program_id(1)
    qi = pl.program_id(2)
    start = jnp.clip(qi * tq - WINDOW, 0, seq - band)
    row0 = pl.multiple_of(past + start, WINDOW)
    k_ctx = k_ref[0:past, :]
    v_ctx = v_ref[0:past, :]
    k_band = k_ref[pl.ds(row0, band), :]
    v_band = v_ref[pl.ds(row0, band), :]
    qpos = qi * tq + lax.broadcasted_iota(jnp.int32, (tq, band), 0)
    kpos = start + lax.broadcasted_iota(jnp.int32, (tq, band), 1)
    valid = jnp.abs(kpos - qpos) <= WINDOW
    for g in range(g_q):
        q = q_ref[g]
        s_ctx = _nt_dot(q, k_ctx)
        s_band = jnp.where(valid, _nt_dot(q, k_band), NEG_INF)
        sk = sink_ref[grp * g_q + g]
        m = jnp.maximum(jnp.maximum(jnp.max(s_ctx, axis=-1, keepdims=True),
                                    jnp.max(s_band, axis=-1, keepdims=True)), sk)
        p_ctx = jnp.exp(s_ctx - m)
        p_band = jnp.exp(s_band - m)
        den = (jnp.sum(p_ctx, axis=-1, keepdims=True) + jnp.sum(p_band, axis=-1, keepdims=True)
               + jnp.exp(sk - m))
        num = _dot(p_ctx.astype(BF16), v_ctx) + _dot(p_band.astype(BF16), v_band)
        o_ref[:, g * dv:(g + 1) * dv] = num * (1.0 / den)


def _window_attention(q, k, v, sink, o_prev, *, q_off, o_off, tq=256):
    hq, _, d = q.shape
    hk = k.shape[0]
    g_q = hq // hk
    band = tq + 2 * WINDOW
    nq = DEC_SEQ // tq
    return pl.pallas_call(
        functools.partial(_win_kernel, g_q=g_q, dv=d, tq=tq, seq=DEC_SEQ, past=PAST_LEN, band=band),
        grid=(DEC_BATCH, hk, nq),
        in_specs=[
            pl.BlockSpec((g_q, tq, d), lambda b, g, qi: (g, (q_off + b * DEC_SEQ) // tq + qi, 0)),
            pl.BlockSpec((None, None, KV_LAT, d), lambda b, g, qi: (g, b, 0, 0)),
            pl.BlockSpec((None, None, KV_LAT, d), lambda b, g, qi: (g, b, 0, 0)),
            pl.BlockSpec(memory_space=pltpu.SMEM),
            pl.BlockSpec(memory_space=pl.ANY),
        ],
        out_specs=pl.BlockSpec((tq, g_q * d), lambda b, g, qi: ((o_off + b * DEC_SEQ) // tq + qi, g)),
        out_shape=jax.ShapeDtypeStruct(o_prev.shape, F32),
        input_output_aliases={4: 0},
        compiler_params=_params(("arbitrary", "arbitrary", "arbitrary")),
        name="attention_window",
    )(q, k, v, sink, o_prev)


def _merge_kernel(x_ref, h_ref, oa_ref, ob_ref, oc_ref, od_ref, mod_ref, wg_ref, wb_ref, wo_ref, n2g_ref,
                  x1_ref, h2_ref):
    hb = h_ref[...]
    outs = (oa_ref, ob_ref, oc_ref, od_ref)
    acc = None
    for i in range(N_BRANCH):
        gt = jax.nn.sigmoid(_dot(hb, wg_ref[:, i * D_MODEL:(i + 1) * D_MODEL]))
        br = _dot(outs[i][...].astype(BF16), wb_ref[i])
        acc = gt * br if acc is None else acc + gt * br
    y = _dot(acc.astype(BF16), wo_ref[...])
    x1 = x_ref[...] + mod_ref[2:3, :] * y
    x1_ref[...] = x1
    h2 = _row_rms(x1, n2g_ref[...]) * (1.0 + mod_ref[4:5, :]) + mod_ref[3:4, :]
    h2_ref[...] = h2.astype(BF16)


def _merge_stage(x, h, oa, ob, oc, od, mod3, wg, wb, wo, n2g, *, tm=256):
    n = x.shape[0]

    def tok(i):
        return (i, 0)

    return pl.pallas_call(
        _merge_kernel,
        grid=(n // tm,),
        in_specs=[
            pl.BlockSpec((tm, D_MODEL), tok),
            pl.BlockSpec((tm, D_MODEL), tok),
            pl.BlockSpec((tm, BRANCH_W), tok),
            pl.BlockSpec((tm, BRANCH_W), tok),
            pl.BlockSpec((tm, BRANCH_W), tok),
            pl.BlockSpec((tm, BRANCH_W), tok),
            pl.BlockSpec((None, 6, D_MODEL), lambda i: ((i * tm) // DEC_SEQ, 0, 0)),
            pl.BlockSpec((D_MODEL, N_BRANCH * D_MODEL), lambda i: (0, 0)),
            pl.BlockSpec((N_BRANCH, BRANCH_W, D_MODEL), lambda i: (0, 0, 0)),
            pl.BlockSpec((D_MODEL, D_MODEL), lambda i: (0, 0)),
            pl.BlockSpec((1, D_MODEL), lambda i: (0, 0)),
        ],
        out_specs=[pl.BlockSpec((tm, D_MODEL), tok), pl.BlockSpec((tm, D_MODEL), tok)],
        out_shape=[jax.ShapeDtypeStruct((n, D_MODEL), F32), jax.ShapeDtypeStruct((n, D_MODEL), BF16)],
        compiler_params=_params(("arbitrary",)),
        name="merge_stage",
    )(x, h, oa, ob, oc, od, mod3, wg, wb, wo, n2g)


def _top_distinct(x, iters):
    vals, cnts = [], []
    for _ in range(iters):
        m = jnp.max(x, axis=0, keepdims=True)
        eq = x == m
        c = jnp.sum(jnp.where(eq, 1.0, 0.0), axis=0, keepdims=True)
        x = jnp.where(eq, EXHAUSTED, x)
        vals.append(m)
        cnts.append(jnp.where(m > EXHAUSTED, c, 0.0))
    return jnp.concatenate(vals, axis=0), jnp.concatenate(cnts, axis=0)


def _pair_threshold(v1, c1, v2, c2, k):
    half = k // 2
    cands = [v1[0:1] + v2]
    mults = [c1[0:1] * c2]
    for i in range(1, half):
        cands.append(v1[i:i + 1] + v2[0:half])
        mults.append(c1[i:i + 1] * c2[0:half])
    cands.append(v1[half:k] + v2[0:1])
    mults.append(c1[half:k] * c2[0:1])
    cand = jnp.concatenate(cands, axis=0)
    mult = jnp.concatenate(mults, axis=0)
    top = None
    cum = tau = z = None
    for _ in range(k):
        m = jnp.max(cand, axis=0, keepdims=True)
        eq = cand == m
        c = jnp.sum(jnp.where(eq, mult, 0.0), axis=0, keepdims=True)
        cand = jnp.where(eq, EXHAUSTED, cand)
        if top is None:
            top, tau, cum, z = m, m, c, c
        else:
            active = cum < k
            tau = jnp.where(active, m, tau)
            z = z + jnp.where(active, c * jnp.exp(m - top), 0.0)
            cum = cum + jnp.where(active, c, 0.0)
    return tau, top, z


def _peer_query_kernel(h2_ref, wqt_ref, sk_ref, s1_ref, s2_ref, p1_ref, p2_ref, tau_ref):
    qt = _nt_dot(wqt_ref[...], h2_ref[...]).astype(BF16)
    for hd in range(PEER_HEADS):
        s1 = _dot(sk_ref[2 * hd], qt[(2 * hd) * PEER_HALF:(2 * hd + 1) * PEER_HALF, :])
        s2 = _dot(sk_ref[2 * hd + 1], qt[(2 * hd + 1) * PEER_HALF:(2 * hd + 2) * PEER_HALF, :])
        v1, c1 = _top_distinct(s1, PEER_TOPK)
        v2, c2 = _top_distinct(s2, PEER_TOPK)
        tau, _, z = _pair_threshold(v1, c1, v2, c2, PEER_TOPK)
        s1_ref[hd] = s1
        s2_ref[hd] = s2
        p1_ref[hd] = jnp.exp(s1 - v1[0:1])
        p2_ref[hd] = jnp.exp(s2 - v2[0:1]) * (1.0 / z)
        tau_ref[hd:hd + 1, :] = tau


def _peer_query(h2, wqt, sk, *, tm=512):
    n = h2.shape[0]
    big = jax.ShapeDtypeStruct((PEER_HEADS, PEER_KEYS, n), F32)
    bspec = pl.BlockSpec((PEER_HEADS, PEER_KEYS, tm), lambda i: (0, 0, i))
    return pl.pallas_call(
        _peer_query_kernel,
        grid=(n // tm,),
        in_specs=[
            pl.BlockSpec((tm, D_MODEL), lambda i: (i, 0)),
            pl.BlockSpec((PEER_HEADS * PEER_QDIM, D_MODEL), lambda i: (0, 0)),
            pl.BlockSpec((2 * PEER_HEADS, PEER_KEYS, PEER_HALF), lambda i: (0, 0, 0)),
        ],
        out_specs=[bspec, bspec, bspec, bspec, pl.BlockSpec((PEER_HEADS, tm), lambda i: (0, i))],
        out_shape=[big, big, big, big, jax.ShapeDtypeStruct((PEER_HEADS, n), F32)],
        compiler_params=_params(("arbitrary",)),
        name="peer_query",
    )(h2, wqt, sk)


def _peer_dense_kernel(h2_ref, x1_ref, mod_ref, u_ref, vt_ref, s1_ref, s2_ref, p1_ref, p2_ref, tau_ref,
                       o_ref, acc_ref, *, e1_per_step):
    step = pl.program_id(1)

    @pl.when(step == 0)
    def _init():
        acc_ref[...] = jnp.zeros_like(acc_ref)

    a = _nt_dot(u_ref[...], h2_ref[...])
    act = 0.5 * a * (1.0 + lax.erf(a * (2.0 ** -0.5)))
    gates = []
    for c in range(e1_per_step):
        e1 = step * e1_per_step + c
        g = None
        for hd in range(PEER_HEADS):
            s1row = s1_ref[hd, pl.ds(e1, 1), :]
            p1row = p1_ref[hd, pl.ds(e1, 1), :]
            sel = (s2_ref[hd] + s1row) >= tau_ref[hd:hd + 1, :]
            w = jnp.where(sel, p2_ref[hd] * p1row, 0.0)
            g = w if g is None else g + w
        gates.append(g)
    gate = jnp.concatenate(gates, axis=0)
    acc_ref[...] += _dot(vt_ref[...], (gate * act).astype(BF16))

    @pl.when(step == pl.num_programs(1) - 1)
    def _fin():
        o_ref[...] = x1_ref[...] + mod_ref[5:6, :] * acc_ref[...].T


def _peer_dense(h2, x1, mod3, u, vt, s1, s2, p1, p2, tau, *, tm=512, e1_per_step=4):
    n = h2.shape[0]
    te = e1_per_step * PEER_KEYS
    tok = lambda i, j: (i, 0)
    big = pl.BlockSpec((PEER_HEADS, PEER_KEYS, tm), lambda i, j: (0, 0, i))
    return pl.pallas_call(
        functools.partial(_peer_dense_kernel, e1_per_step=e1_per_step),
        grid=(n // tm, PEER_EXPERTS // te),
        in_specs=[
            pl.BlockSpec((tm, D_MODEL), tok),
            pl.BlockSpec((tm, D_MODEL), tok),
            pl.BlockSpec((None, 6, D_MODEL), lambda i, j: ((i * tm) // DEC_SEQ, 0, 0)),
            pl.BlockSpec((te, D_MODEL), lambda i, j: (j, 0)),
            pl.BlockSpec((D_MODEL, te), lambda i, j: (0, j)),
            big, big, big, big,
            pl.BlockSpec((PEER_HEADS, tm), lambda i, j: (0, i)),
        ],
        out_specs=pl.BlockSpec((tm, D_MODEL), tok),
        out_shape=jax.ShapeDtypeStruct((n, D_MODEL), F32),
        scratch_shapes=[pltpu.VMEM((D_MODEL, tm), F32)],
        compiler_params=_params(("arbitrary", "arbitrary")),
        name="peer_dense",
    )(h2, x1, mod3, u, vt, s1, s2, p1, p2, tau)


def _block_ones(width, d):
    idx = jnp.arange(width) // d
    return (idx[:, None] == idx[None, :]).astype(BF16)


def _rope_angles(d):
    q4 = d // 4
    inv = ROPE_BASE ** (-jnp.arange(q4, dtype=F32) / q4)
    t = jnp.arange(DEC_SEQ)
    ar = (t // GRID_W).astype(F32)[:, None] * inv
    ac = (t % GRID_W).astype(F32)[:, None] * inv
    return jnp.concatenate([ar, ar, ac, ac], axis=-1)


def _rope_table(d, group, n_groups):
    ang = _rope_angles(d)
    q4 = d // 4
    first = (jnp.arange(d) % (2 * q4)) < q4
    cos = jnp.cos(ang)
    sin = jnp.sin(ang)
    sa = jnp.where(first[None, :], -sin, 0.0)
    sb = jnp.where(first[None, :], 0.0, sin)
    pad = group - d
    cos = jnp.pad(cos, ((0, 0), (pad, 0)), constant_values=1.0)
    sa = jnp.pad(sa, ((0, 0), (pad, 0)))
    sb = jnp.pad(sb, ((0, 0), (pad, 0)))
    return jnp.stack([jnp.tile(cos, (1, n_groups)), jnp.tile(sa, (1, n_groups)), jnp.tile(sb, (1, n_groups))])


def _rot_matrix(d):
    q4 = d // 4
    i = jnp.arange(d)
    first = (i % (2 * q4)) < q4
    src = jnp.where(first, i + q4, i - q4)
    sign = jnp.where(first, -1.0, 1.0)
    return (jnp.zeros((d, d), F32).at[src, i].set(sign)).astype(BF16)


def _pad_row(v, width):
    return jnp.pad(v, (0, width - v.shape[0]))


def kernel(x_prompt, x_sample, c, cache_a_k, cache_a_v, cache_b_k, cache_b_v, cache_c_k, cache_c_v, cache_d_ckv, cache_d_krope, c_ctx, ada_w, ada_b, norm1_g, norm2_g, w_in, a_lam, a_qk_g, a_subln_g, b_sink, b_qk_g, c_qk_g, d_qnorm_g, d_kvnorm_g, w_d_qb, w_d_kvb, d_qk_g, w_branch, w_out, peer_wq, peer_subkeys, peer_u, peer_v):
    x = jnp.concatenate([x_prompt.reshape(N_CTX, D_MODEL), x_sample.reshape(N_LAT, D_MODEL)], axis=0)
    cvec = jnp.concatenate([c_ctx[None, :], c, jnp.zeros((N_MODROWS - 1 - DEC_BATCH, D_MODEL), F32)], axis=0)
    mod_all = _modulation(cvec, ada_w, ada_b)

    e32 = _block_ones(256, A_QK)
    e64 = _block_ones(256, HEAD_DIM)
    e96 = _block_ones(GAIN_W, D_QK)
    tab_a = _rope_table(A_QK, A_QK, 2 * A_HEADS)
    tab_b = _rope_table(HEAD_DIM, HEAD_DIM, B_HEADS)
    tab_d = _rope_table(D_ROPE, D_QK, D_HEADS)
    ang_r = _rope_angles(D_ROPE)
    tab_r = jnp.stack([jnp.cos(ang_r), jnp.sin(ang_r)])
    rot_r = _rot_matrix(D_ROPE)

    states = [[] for _ in range(8)]
    for l in range(DEPTH):
        lam_init = 0.8 - 0.6 * math.exp(-0.3 * l)
        mod3 = mod_all[l, :3].reshape(3, 6, D_MODEL)
        ws = jnp.pad(w_in[l, :, :SMALL_COLS], ((0, 0), (0, SMALL_PAD - SMALL_COLS))).astype(BF16)
        wg = w_in[l, :, SMALL_COLS:].astype(BF16)
        gains = jnp.stack([
            _pad_row(jnp.tile(a_qk_g[l, 0], 2 * A_HEADS), GAIN_W),
            _pad_row(jnp.tile(a_qk_g[l, 1], 2 * A_HEADS), GAIN_W),
            _pad_row(jnp.tile(b_qk_g[l, 0], B_HEADS), GAIN_W),
            _pad_row(jnp.tile(b_qk_g[l, 1], B_KV), GAIN_W),
            _pad_row(jnp.tile(c_qk_g[l, 0], C_HEADS), GAIN_W),
            _pad_row(jnp.tile(c_qk_g[l, 1], C_KV), GAIN_W),
            _pad_row(d_qnorm_g[l], GAIN_W),
            _pad_row(d_kvnorm_g[l], GAIN_W),
            jnp.tile(d_qk_g[l, 0], D_HEADS),
        ] + [jnp.zeros((GAIN_W,), F32)] * (GAIN_ROWS - 9))

        (h, aq, ak, av, bq, bk, bv, cq, ck, cv, dq,
         akt, avt, bkt, bvt, ckt, cvt, dckv, dkr) = _input_stage(
            x, mod3, norm1_g[l][None, :], ws, gains, e32, e64, e96, w_d_qb[l].astype(BF16), tab_a, tab_b, tab_d)

        for lst, s_ in zip(states, (
                akt[:N_CTX].reshape(BATCH, SEQ, A_HEADS, 2, A_QK), avt[:N_CTX].reshape(BATCH, SEQ, A_HEADS, A_V),
                bkt[:N_CTX].reshape(BATCH, SEQ, B_KV, HEAD_DIM), bvt[:N_CTX].reshape(BATCH, SEQ, B_KV, HEAD_DIM),
                ckt[:N_CTX].reshape(BATCH, SEQ, C_KV, HEAD_DIM), cvt[:N_CTX].reshape(BATCH, SEQ, C_KV, HEAD_DIM),
                dckv[:N_CTX].reshape(BATCH, SEQ, D_KV_LORA), dkr[:N_CTX].reshape(BATCH, SEQ, D_ROPE))):
            lst.append(s_)

        ckv_rows = jnp.concatenate([t for b in range(DEC_BATCH) for t in (
            cache_d_ckv[b, l], dckv[N_CTX + b * DEC_SEQ:N_CTX + (b + 1) * DEC_SEQ])] + [dckv[:N_CTX]], axis=0)
        kr_rows = jnp.concatenate([t for b in range(DEC_BATCH) for t in (
            cache_d_krope[b, l], dkr[N_CTX + b * DEC_SEQ:N_CTX + (b + 1) * DEC_SEQ])] + [dkr[:N_CTX]], axis=0)
        kd, vd = _mla_expand(ckv_rows, kr_rows, w_d_kvb[l].astype(BF16), d_qk_g[l, 1][None, :], rot_r, tab_r)

        def lat_kv(new, cache, nh, d):
            cache = jnp.moveaxis(cache.reshape(DEC_BATCH, PAST_LEN, nh, d), 2, 0).astype(BF16)
            new = new[:, N_CTX:].reshape(nh, DEC_BATCH, DEC_SEQ, d)
            return jnp.concatenate([cache, new], axis=2).reshape(nh, DEC_BATCH * KV_LAT, d)

        ak_l = lat_kv(ak, cache_a_k[:, l], 2 * A_HEADS, A_QK)
        av_l = lat_kv(av, cache_a_v[:, l], A_HEADS, A_V)
        bk_l = lat_kv(bk, cache_b_k[:, l], B_KV, HEAD_DIM).reshape(B_KV, DEC_BATCH, KV_LAT, HEAD_DIM)
        bv_l = lat_kv(bv, cache_b_v[:, l], B_KV, HEAD_DIM).reshape(B_KV, DEC_BATCH, KV_LAT, HEAD_DIM)
        ck_l = lat_kv(ck, cache_c_k[:, l], C_KV, HEAD_DIM)
        cv_l = lat_kv(cv, cache_c_v[:, l], C_KV, HEAD_DIM)

        ctx = dict(batch=BATCH, tq_rows=SEQ, tk_rows=SEQ, q_off=0, o_off=0, n_out=N_TOK,
                   lam_init=lam_init, tq=SEQ, tk=SEQ)
        lat = dict(batch=DEC_BATCH, tq_rows=DEC_SEQ, tk_rows=KV_LAT, q_off=N_CTX, k_off=0, o_off=N_CTX, n_out=N_TOK,
                   lam_init=lam_init, tq=512, tk=KV_LAT // 3)
        a_extra = (a_lam[l], a_subln_g[l][None, :])
        oa = _attention(aq, ak, av, a_extra, None, k_off=0, g_q=4, g_k=4, g_v=2, dv=A_V, mode="diff", **ctx)
        oa = _attention(aq, ak_l, av_l, a_extra, oa, g_q=4, g_k=4, g_v=2, dv=A_V, mode="diff", **lat)
        ob = _attention(bq, bk, bv, (b_sink[l],), None, k_off=0, g_q=2, g_k=1, g_v=1, dv=HEAD_DIM, mode="sink",
                        **ctx)
        ob = _window_attention(bq, bk_l, bv_l, b_sink[l], ob, q_off=N_CTX, o_off=N_CTX)
        oc = _attention(cq, ck, cv, (), None, k_off=0, g_q=2, g_k=1, g_v=1, dv=HEAD_DIM, mode="plain", **ctx)
        oc = _attention(cq, ck_l, cv_l, (), oc, g_q=2, g_k=1, g_v=1, dv=HEAD_DIM, mode="plain", **lat)
        od = _attention(dq, kd, vd, (), None, k_off=DEC_BATCH * KV_LAT, g_q=2, g_k=2, g_v=2, dv=D_V,
                        mode="plain", **ctx)
        od = _attention(dq, kd, vd, (), od, g_q=2, g_k=2, g_v=2, dv=D_V, mode="plain", **lat)

        x1, h2 = _merge_stage(x, h, oa, ob, oc, od, mod3, wg, w_branch[l].astype(BF16), w_out[l].astype(BF16),
                              norm2_g[l][None, :])

        s1, s2, p1, p2, tau = _peer_query(
            h2, peer_wq[l].T.astype(BF16),
            peer_subkeys[l].reshape(2 * PEER_HEADS, PEER_KEYS, PEER_HALF).astype(BF16))
        x = _peer_dense(h2, x1, mod3, peer_u[l].astype(BF16), peer_v[l].T.astype(BF16), s1, s2, p1, p2, tau)

    y_prompt = x[:N_CTX].reshape(BATCH, SEQ, D_MODEL)
    y_sample = x[N_CTX:].reshape(DEC_BATCH, DEC_SEQ, D_MODEL)
    return (y_prompt, y_sample) + tuple(jnp.stack(s_, axis=1) for s_ in states)
```

```python
import functools
import math

import jax
import jax.numpy as jnp
from jax import lax
from jax.experimental import pallas as pl
from jax.experimental.pallas import tpu as pltpu

F32 = jnp.float32
BF16 = jnp.bfloat16

D_MODEL = 1024
BATCH = 16
SEQ = 256
DEPTH = 2
DEC_BATCH = 2
DEC_SEQ = 4096
PAST_LEN = 512
GRID_W = 64
Q_BLOCK = 128
HEAD_DIM = 64
ROPE_BASE = 10000.0
EPS = 1e-6
NEG_INF = -1e30
A_HEADS = 4
A_QK = 32
A_V = 64
B_HEADS = 4
B_KV = 2
WINDOW = 128
C_HEADS = 4
C_KV = 2
D_HEADS = 4
D_Q_LORA = 256
D_KV_LORA = 128
D_NOPE = 64
D_ROPE = 32
D_V = 64
D_QK = D_NOPE + D_ROPE
N_BRANCH = 4
BRANCH_W = 256
PEER_HEADS = 8
PEER_KEYS = 128
PEER_EXPERTS = PEER_KEYS * PEER_KEYS
PEER_QDIM = 256
PEER_HALF = PEER_QDIM // 2
PEER_TOPK = 16

N_CTX = BATCH * SEQ
N_LAT = DEC_BATCH * DEC_SEQ
N_TOK = N_CTX + N_LAT
KV_LAT = PAST_LEN + DEC_SEQ
N_MODROWS = 8
SMALL_COLS = 2208
SMALL_PAD = 2304

VMEM_LIMIT = 56 * 1024 * 1024
EXHAUSTED = -3.0e38


def _params(sem, flags=None):
    return pltpu.CompilerParams(dimension_semantics=sem, vmem_limit_bytes=VMEM_LIMIT, flags=flags)


def _nt_dot(a, b):
    return lax.dot_general(a, b, (((1,), (1,)), ((), ())), preferred_element_type=F32)


def _dot(a, b):
    return jnp.dot(a, b, preferred_element_type=F32)


def _mod_kernel(c_ref, w_ref, b_ref, o_ref):
    c = c_ref[...]
    s = c * jax.nn.sigmoid(c)
    o_ref[...] = jnp.dot(s, w_ref[...], preferred_element_type=F32, precision=lax.Precision.HIGHEST) + b_ref[...]


def _modulation(cvec, ada_w, ada_b):
    depth = ada_w.shape[0]
    ncol = ada_w.shape[2] // D_MODEL
    return pl.pallas_call(
        _mod_kernel,
        grid=(depth, ncol),
        in_specs=[
            pl.BlockSpec((N_MODROWS, D_MODEL), lambda l, j: (0, 0)),
            pl.BlockSpec((None, D_MODEL, D_MODEL), lambda l, j: (l, 0, j)),
            pl.BlockSpec((None, 1, D_MODEL), lambda l, j: (l, 0, j)),
        ],
        out_specs=pl.BlockSpec((None, N_MODROWS, D_MODEL), lambda l, j: (l, 0, j)),
        out_shape=jax.ShapeDtypeStruct((depth, N_MODROWS, ada_w.shape[2]), F32),
        compiler_params=_params(("arbitrary", "arbitrary")),
        name="modulation",
    )(cvec, ada_w, ada_b.reshape(depth, 1, -1))


def _group_rms(x, ones_bd, gain, d):
    sq = x * x
    hi = sq.astype(BF16)
    lo = (sq - hi.astype(F32)).astype(BF16)
    ss = _dot(hi, ones_bd) + _dot(lo, ones_bd)
    return x * lax.rsqrt(ss * (1.0 / d) + EPS) * gain


def _row_rms(x, gain):
    ms = jnp.mean(x * x, axis=-1, keepdims=True)
    return x * lax.rsqrt(ms + EPS) * gain


def _rope(x, tab_ref, q4):
    w = x.shape[-1]
    return x * tab_ref[0] + pltpu.roll(x, w - q4, 1) * tab_ref[1] + pltpu.roll(x, q4, 1) * tab_ref[2]


GAIN_ROWS = 16
GAIN_W = D_HEADS * D_QK


def _in_kernel(x_ref, mod_ref, n1g_ref, ws_ref, gains_ref, e32_ref, e64_ref, e96_ref, wdq_ref,
               ta_ref, tb_ref, td_ref,
               h_ref, aq_ref, ak_ref, av_ref, bq_ref, bk_ref, bv_ref, cq_ref, ck_ref, cv_ref, dq_ref,
               akt_ref, avt_ref, bkt_ref, bvt_ref, ckt_ref, cvt_ref, dckv_ref, dkr_ref, *, tm, n_ctx):
    is_lat = pl.program_id(0) * tm >= n_ctx
    x = x_ref[...]
    h = _row_rms(x, n1g_ref[...]) * (1.0 + mod_ref[1:2, :]) + mod_ref[0:1, :]
    hb = h.astype(BF16)
    h_ref[...] = hb
    p = _dot(hb, ws_ref[...])

    def gain(r, w):
        return gains_ref[r:r + 1, 0:w]

    def heads_out(ref, val, nh, d):
        for i in range(nh):
            ref[i] = val[:, i * d:(i + 1) * d].astype(BF16)

    def maybe_rope(val, tab_ref, q4):
        return jnp.where(is_lat, _rope(val, tab_ref, q4), val)

    aq = maybe_rope(_group_rms(p[:, 0:256], e32_ref[...], gain(0, 256), A_QK), ta_ref, A_QK // 4)
    ak = maybe_rope(_group_rms(p[:, 256:512], e32_ref[...], gain(1, 256), A_QK), ta_ref, A_QK // 4)
    av = p[:, 512:768]
    heads_out(aq_ref, aq * (A_QK ** -0.5), 2 * A_HEADS, A_QK)
    heads_out(ak_ref, ak, 2 * A_HEADS, A_QK)
    heads_out(av_ref, av, A_HEADS, A_V)
    akt_ref[...] = ak
    avt_ref[...] = av
    bq = maybe_rope(_group_rms(p[:, 768:1024], e64_ref[...], gain(2, 256), HEAD_DIM), tb_ref, HEAD_DIM // 4)
    bk = _group_rms(p[:, 1024:1152], e64_ref[0:128, 0:128], gain(3, 128), HEAD_DIM)
    bk = jnp.where(is_lat, bk * tb_ref[0, :, 0:128] + pltpu.roll(bk, 128 - 16, 1) * tb_ref[1, :, 0:128]
                   + pltpu.roll(bk, 16, 1) * tb_ref[2, :, 0:128], bk)
    bv = p[:, 1152:1280]
    heads_out(bq_ref, bq * (HEAD_DIM ** -0.5), B_HEADS, HEAD_DIM)
    heads_out(bk_ref, bk, B_KV, HEAD_DIM)
    heads_out(bv_ref, bv, B_KV, HEAD_DIM)
    bkt_ref[...] = bk
    bvt_ref[...] = bv
    cq = maybe_rope(_group_rms(p[:, 1280:1536], e64_ref[...], gain(4, 256), HEAD_DIM), tb_ref, HEAD_DIM // 4)
    ck = _group_rms(p[:, 1536:1664], e64_ref[0:128, 0:128], gain(5, 128), HEAD_DIM)
    ck = jnp.where(is_lat, ck * tb_ref[0, :, 0:128] + pltpu.roll(ck, 128 - 16, 1) * tb_ref[1, :, 0:128]
                   + pltpu.roll(ck, 16, 1) * tb_ref[2, :, 0:128], ck)
    cv = p[:, 1664:1792]
    heads_out(cq_ref, cq * (HEAD_DIM ** -0.5), C_HEADS, HEAD_DIM)
    heads_out(ck_ref, ck, C_KV, HEAD_DIM)
    heads_out(cv_ref, cv, C_KV, HEAD_DIM)
    ckt_ref[...] = ck
    cvt_ref[...] = cv
    dqa = _row_rms(p[:, 1792:2048], gain(6, 256))
    dq = _dot(dqa.astype(BF16), wdq_ref[...])
    dq = maybe_rope(_group_rms(dq, e96_ref[...], gain(8, GAIN_W), D_QK), td_ref, D_ROPE // 4)
    heads_out(dq_ref, dq * (D_QK ** -0.5), D_HEADS, D_QK)
    dckv_ref[...] = _row_rms(p[:, 2048:2176], gain(7, 128))
    dkr_ref[...] = p[:, 2176:2208]


def _input_stage(x, mod3, n1g, ws, gains, e32, e64, e96, wdq, tab_a, tab_b, tab_d, *, tm=256):
    n = x.shape[0]
    nt = n // tm
    lat0 = N_CTX // tm
    pos_blocks = DEC_SEQ // tm

    def tok(i):
        return (i, 0)

    def pos(i):
        return (0, jnp.maximum(i - lat0, 0) % pos_blocks, 0)

    def const2(i):
        return (0, 0)

    def hm(i):
        return (0, i, 0)

    def hspec(nh, d):
        return pl.BlockSpec((nh, tm, d), hm)

    def hshape(nh, d):
        return jax.ShapeDtypeStruct((nh, n, d), BF16)

    def tspec(w):
        return pl.BlockSpec((tm, w), tok)

    def tshape(w):
        return jax.ShapeDtypeStruct((n, w), F32)

    in_specs = [
        pl.BlockSpec((tm, D_MODEL), tok),
        pl.BlockSpec((None, 6, D_MODEL), lambda i: ((i * tm) // DEC_SEQ, 0, 0)),
        pl.BlockSpec((1, D_MODEL), const2),
        pl.BlockSpec((D_MODEL, SMALL_PAD), const2),
        pl.BlockSpec((GAIN_ROWS, GAIN_W), const2),
        pl.BlockSpec((256, 256), const2),
        pl.BlockSpec((256, 256), const2),
        pl.BlockSpec((GAIN_W, GAIN_W), const2),
        pl.BlockSpec((D_Q_LORA, GAIN_W), const2),
        pl.BlockSpec((3, tm, 256), pos),
        pl.BlockSpec((3, tm, 256), pos),
        pl.BlockSpec((3, tm, GAIN_W), pos),
    ]
    out_specs = [
        pl.BlockSpec((tm, D_MODEL), tok),
        hspec(8, A_QK), hspec(8, A_QK), hspec(4, A_V),
        hspec(4, HEAD_DIM), hspec(2, HEAD_DIM), hspec(2, HEAD_DIM),
        hspec(4, HEAD_DIM), hspec(2, HEAD_DIM), hspec(2, HEAD_DIM),
        hspec(4, D_QK),
        tspec(256), tspec(256), tspec(128), tspec(128), tspec(128), tspec(128), tspec(128), tspec(D_ROPE),
    ]
    out_shape = [
        jax.ShapeDtypeStruct((n, D_MODEL), BF16),
        hshape(8, A_QK), hshape(8, A_QK), hshape(4, A_V),
        hshape(4, HEAD_DIM), hshape(2, HEAD_DIM), hshape(2, HEAD_DIM),
        hshape(4, HEAD_DIM), hshape(2, HEAD_DIM), hshape(2, HEAD_DIM),
        hshape(4, D_QK),
        tshape(256), tshape(256), tshape(128), tshape(128), tshape(128), tshape(128), tshape(128), tshape(D_ROPE),
    ]
    return pl.pallas_call(
        functools.partial(_in_kernel, tm=tm, n_ctx=N_CTX),
        grid=(nt,),
        in_specs=in_specs,
        out_specs=out_specs,
        out_shape=out_shape,
        compiler_params=_params(("arbitrary",)),
        name="input_stage",
    )(x, mod3, n1g, ws, gains, e32, e64, e96, wdq, tab_a, tab_b, tab_d)


def _mla_kernel(ckv_ref, kr_ref, wkvb_ref, g_ref, rot_ref, tab_ref, kd_ref, vd_ref, *, tm, lat_tiles, per_batch_tiles):
    i = pl.program_id(0)
    rotate = jnp.logical_and(i < lat_tiles, i % per_batch_tiles >= PAST_LEN // tm)
    kv = _dot(ckv_ref[...].astype(BF16), wkvb_ref[...])
    kr = kr_ref[...]
    ssr = jnp.sum(kr * kr, axis=-1, keepdims=True)
    z = kr * g_ref[:, D_NOPE:D_QK]
    zh = z.astype(BF16)
    zl = (z - zh.astype(F32)).astype(BF16)
    rot = _dot(zh, rot_ref[...]) + _dot(zl, rot_ref[...])
    rz = jnp.where(rotate, z * tab_ref[0] + rot * tab_ref[1], z)
    g_nope = g_ref[:, 0:D_NOPE]
    for hd in range(D_HEADS):
        kn = kv[:, hd * 128:hd * 128 + D_NOPE]
        r = lax.rsqrt((jnp.sum(kn * kn, axis=-1, keepdims=True) + ssr) * (1.0 / D_QK) + EPS)
        kd_ref[hd] = jnp.concatenate([kn * r * g_nope, rz * r], axis=-1).astype(BF16)
        vd_ref[hd] = kv[:, hd * 128 + D_NOPE:(hd + 1) * 128].astype(BF16)


def _mla_expand(ckv, kr, wkvb, g, rot, tab, *, tm=512):
    rows = ckv.shape[0]
    per_batch_tiles = KV_LAT // tm
    lat_tiles = DEC_BATCH * per_batch_tiles
    past_tiles = PAST_LEN // tm
    pos_blocks = DEC_SEQ // tm

    def pos(i):
        return (0, jnp.maximum(i % per_batch_tiles - past_tiles, 0) % pos_blocks, 0)

    return pl.pallas_call(
        functools.partial(_mla_kernel, tm=tm, lat_tiles=lat_tiles, per_batch_tiles=per_batch_tiles),
        grid=(rows // tm,),
        in_specs=[
            pl.BlockSpec((tm, D_KV_LORA), lambda i: (i, 0)),
            pl.BlockSpec((tm, D_ROPE), lambda i: (i, 0)),
            pl.BlockSpec((D_KV_LORA, D_HEADS * (D_NOPE + D_V)), lambda i: (0, 0)),
            pl.BlockSpec((1, D_QK), lambda i: (0, 0)),
            pl.BlockSpec((D_ROPE, D_ROPE), lambda i: (0, 0)),
            pl.BlockSpec((2, tm, D_ROPE), pos),
        ],
        out_specs=[
            pl.BlockSpec((D_HEADS, tm, D_QK), lambda i: (0, i, 0)),
            pl.BlockSpec((D_HEADS, tm, D_V), lambda i: (0, i, 0)),
        ],
        out_shape=[
            jax.ShapeDtypeStruct((D_HEADS, rows, D_QK), BF16),
            jax.ShapeDtypeStruct((D_HEADS, rows, D_V), BF16),
        ],
        compiler_params=_params(("arbitrary",)),
        name="mla_expand",
    )(ckv, kr, wkvb, g, rot, tab)


def _attn_kernel(*refs, g_q, g_k, g_v, dv, mode, lam_init, has_prev):
    refs = list(refs)
    q_ref, k_ref, v_ref = refs[:3]
    pos = 3
    sink_ref = lam_ref = sub_ref = None
    if mode == "sink":
        sink_ref = refs[pos]
        pos += 1
    elif mode == "diff":
        lam_ref, sub_ref = refs[pos], refs[pos + 1]
        pos += 2
    if has_prev:
        pos += 1
    o_ref, m_scr, l_scr, acc_scr = refs[pos:pos + 4]
    grp = pl.program_id(1)
    ki = pl.program_id(3)
    tq = q_ref.shape[1]

    @pl.when(ki == 0)
    def _init():
        for g in range(g_q):
            if mode == "sink":
                m_scr[g] = jnp.full((tq, 1), sink_ref[grp * g_q + g], F32)
                l_scr[g] = jnp.ones((tq, 1), F32)
            else:
                m_scr[g] = jnp.full((tq, 1), NEG_INF, F32)
                l_scr[g] = jnp.zeros((tq, 1), F32)
            acc_scr[g] = jnp.zeros((tq, dv), F32)

    for g in range(g_q):
        s = _nt_dot(q_ref[g], k_ref[g * g_k // g_q])
        m_old = m_scr[g]
        m_new = jnp.maximum(m_old, jnp.max(s, axis=-1, keepdims=True))
        alpha = jnp.exp(m_old - m_new)
        p = jnp.exp(s - m_new)
        l_scr[g] = alpha * l_scr[g] + jnp.sum(p, axis=-1, keepdims=True)
        acc_scr[g] = alpha * acc_scr[g] + _dot(p.astype(BF16), v_ref[g * g_v // g_q])
        m_scr[g] = m_new

    @pl.when(ki == pl.num_programs(3) - 1)
    def _fin():
        outs = [acc_scr[g] * (1.0 / l_scr[g]) for g in range(g_q)]
        if mode == "diff":
            lp = lam_ref[...]
            lam = (jnp.exp(jnp.sum(lp[0:1] * lp[1:2], axis=-1, keepdims=True))
                   - jnp.exp(jnp.sum(lp[2:3] * lp[3:4], axis=-1, keepdims=True)) + lam_init)
            for i in range(g_q // 2):
                d = outs[2 * i] - lam * outs[2 * i + 1]
                o_ref[:, i * dv:(i + 1) * dv] = _row_rms(d, sub_ref[...]) * (1.0 - lam_init)
        else:
            for g in range(g_q):
                o_ref[:, g * dv:(g + 1) * dv] = outs[g]


def _attention(q, k, v, extras, o_prev, *, batch, tq_rows, tk_rows, q_off, k_off, o_off, n_out,
               g_q, g_k, g_v, dv, mode, lam_init, tq, tk):
    hq, _, dq = q.shape
    n_groups = hq // g_q
    g_out = g_q // 2 if mode == "diff" else g_q
    nq = tq_rows // tq
    nk = tk_rows // tk
    in_specs = [
        pl.BlockSpec((g_q, tq, dq), lambda b, g, qi, ki: (g, (q_off + b * tq_rows) // tq + qi, 0)),
        pl.BlockSpec((g_k, tk, dq), lambda b, g, qi, ki: (g, (k_off + b * tk_rows) // tk + ki, 0)),
        pl.BlockSpec((g_v, tk, dv), lambda b, g, qi, ki: (g, (k_off + b * tk_rows) // tk + ki, 0)),
    ]
    args = [q, k, v]
    if mode == "sink":
        in_specs.append(pl.BlockSpec(memory_space=pltpu.SMEM))
        args.append(extras[0])
    elif mode == "diff":
        in_specs.append(pl.BlockSpec((4, A_QK), lambda b, g, qi, ki: (0, 0)))
        in_specs.append(pl.BlockSpec((1, dv), lambda b, g, qi, ki: (0, 0)))
        args += list(extras)
    aliases = {}
    if o_prev is not None:
        in_specs.append(pl.BlockSpec(memory_space=pl.ANY))
        aliases = {len(args): 0}
        args.append(o_prev)
    return pl.pallas_call(
        functools.partial(_attn_kernel, g_q=g_q, g_k=g_k, g_v=g_v, dv=dv, mode=mode, lam_init=lam_init,
                          has_prev=o_prev is not None),
        grid=(batch, n_groups, nq, nk),
        in_specs=in_specs,
        out_specs=pl.BlockSpec((tq, g_out * dv), lambda b, g, qi, ki: ((o_off + b * tq_rows) // tq + qi, g)),
        out_shape=jax.ShapeDtypeStruct((n_out, n_groups * g_out * dv), F32),
        scratch_shapes=[
            pltpu.VMEM((g_q, tq, 1), F32),
            pltpu.VMEM((g_q, tq, 1), F32),
            pltpu.VMEM((g_q, tq, dv), F32),
        ],
        input_output_aliases=aliases,
        compiler_params=_params(("arbitrary", "arbitrary", "arbitrary", "arbitrary")),
        name="attention_" + mode,
    )(*args)


def _win_kernel(q_ref, k_ref, v_ref, sink_ref, prev_ref, o_ref, *, g_q, dv, tq, seq, past, band):
    del prev_ref
    grp = pl.program_id(1)
    qi = pl.program_id(2)
    start = jnp.clip(qi * tq - WINDOW, 0, seq - band)
    row0 = pl.multiple_of(past + start, WINDOW)
    k_ctx = k_ref[0:past, :]
    v_ctx = v_ref[0:past, :]
    k_band = k_ref[pl.ds(row0, band), :]
    v_band = v_ref[pl.ds(row0, band), :]
    qpos = qi * tq + lax.broadcasted_iota(jnp.int32, (tq, band), 0)
    kpos = start + lax.broadcasted_iota(jnp.int32, (tq, band), 1)
    valid = jnp.abs(kpos - qpos) <= WINDOW
    for g in range(g_q):
        q = q_ref[g]
        s_ctx = _nt_dot(q, k_ctx)
        s_band = jnp.where(valid, _nt_dot(q, k_band), NEG_INF)
        sk = sink_ref[grp * g_q + g]
        m = jnp.maximum(jnp.maximum(jnp.max(s_ctx, axis=-1, keepdims=True),
                                    jnp.max(s_band, axis=-1, keepdims=True)), sk)
        p_ctx = jnp.exp(s_ctx - m)
        p_band = jnp.exp(s_band - m)
        den = (jnp.sum(p_ctx, axis=-1, keepdims=True) + jnp.sum(p_band, axis=-1, keepdims=True)
               + jnp.exp(sk - m))
        num = _dot(p_ctx.astype(BF16), v_ctx) + _dot(p_band.astype(BF16), v_band)
        o_ref[:, g * dv:(g + 1) * dv] = num * (1.0 / den)


def _window_attention(q, k, v, sink, o_prev, *, q_off, o_off, tq=256):
    hq, _, d = q.shape
    hk = k.shape[0]
    g_q = hq // hk
    band = tq + 2 * WINDOW
    nq = DEC_SEQ // tq
    return pl.pallas_call(
        functools.partial(_win_kernel, g_q=g_q, dv=d, tq=tq, seq=DEC_SEQ, past=PAST_LEN, band=band),
        grid=(DEC_BATCH, hk, nq),
        in_specs=[
            pl.BlockSpec((g_q, tq, d), lambda b, g, qi: (g, (q_off + b * DEC_SEQ) // tq + qi, 0)),
            pl.BlockSpec((None, None, KV_LAT, d), lambda b, g, qi: (g, b, 0, 0)),
            pl.BlockSpec((None, None, KV_LAT, d), lambda b, g, qi: (g, b, 0, 0)),
            pl.BlockSpec(memory_space=pltpu.SMEM),
            pl.BlockSpec(memory_space=pl.ANY),
        ],
        out_specs=pl.BlockSpec((tq, g_q * d), lambda b, g, qi: ((o_off + b * DEC_SEQ) // tq + qi, g)),
        out_shape=jax.ShapeDtypeStruct(o_prev.shape, F32),
        input_output_aliases={4: 0},
        compiler_params=_params(("arbitrary", "arbitrary", "arbitrary")),
        name="attention_window",
    )(q, k, v, sink, o_prev)


def _merge_kernel(x_ref, h_ref, oa_ref, ob_ref, oc_ref, od_ref, mod_ref, wg_ref, wb_ref, wo_ref, n2g_ref,
                  x1_ref, h2_ref):
    hb = h_ref[...]
    outs = (oa_ref, ob_ref, oc_ref, od_ref)
    acc = None
    for i in range(N_BRANCH):
        gt = jax.nn.sigmoid(_dot(hb, wg_ref[:, i * D_MODEL:(i + 1) * D_MODEL]))
        br = _dot(outs[i][...].astype(BF16), wb_ref[i])
        acc = gt * br if acc is None else acc + gt * br
    y = _dot(acc.astype(BF16), wo_ref[...])
    x1 = x_ref[...] + mod_ref[2:3, :] * y
    x1_ref[...] = x1
    h2 = _row_rms(x1, n2g_ref[...]) * (1.0 + mod_ref[4:5, :]) + mod_ref[3:4, :]
    h2_ref[...] = h2.astype(BF16)


def _merge_stage(x, h, oa, ob, oc, od, mod3, wg, wb, wo, n2g, *, tm=256):
    n = x.shape[0]

    def tok(i):
        return (i, 0)

    return pl.pallas_call(
        _merge_kernel,
        grid=(n // tm,),
        in_specs=[
            pl.BlockSpec((tm, D_MODEL), tok),
            pl.BlockSpec((tm, D_MODEL), tok),
            pl.BlockSpec((tm, BRANCH_W), tok),
            pl.BlockSpec((tm, BRANCH_W), tok),
            pl.BlockSpec((tm, BRANCH_W), tok),
            pl.BlockSpec((tm, BRANCH_W), tok),
            pl.BlockSpec((None, 6, D_MODEL), lambda i: ((i * tm) // DEC_SEQ, 0, 0)),
            pl.BlockSpec((D_MODEL, N_BRANCH * D_MODEL), lambda i: (0, 0)),
            pl.BlockSpec((N_BRANCH, BRANCH_W, D_MODEL), lambda i: (0, 0, 0)),
            pl.BlockSpec((D_MODEL, D_MODEL), lambda i: (0, 0)),
            pl.BlockSpec((1, D_MODEL), lambda i: (0, 0)),
        ],
        out_specs=[pl.BlockSpec((tm, D_MODEL), tok), pl.BlockSpec((tm, D_MODEL), tok)],
        out_shape=[jax.ShapeDtypeStruct((n, D_MODEL), F32), jax.ShapeDtypeStruct((n, D_MODEL), BF16)],
        compiler_params=_params(("arbitrary",)),
        name="merge_stage",
    )(x, h, oa, ob, oc, od, mod3, wg, wb, wo, n2g)


def _top_distinct(x, iters):
    vals, cnts = [], []
    for _ in range(iters):
        m = jnp.max(x, axis=0, keepdims=True)
        eq = x == m
        c = jnp.sum(jnp.where(eq, 1.0, 0.0), axis=0, keepdims=True)
        x = jnp.where(eq, EXHAUSTED, x)
        vals.append(m)
        cnts.append(jnp.where(m > EXHAUSTED, c, 0.0))
    return jnp.concatenate(vals, axis=0), jnp.concatenate(cnts, axis=0)


def _pair_threshold(v1, c1, v2, c2, k):
    half = k // 2
    cands = [v1[0:1] + v2]
    mults = [c1[0:1] * c2]
    for i in range(1, half):
        cands.append(v1[i:i + 1] + v2[0:half])
        mults.append(c1[i:i + 1] * c2[0:half])
    cands.append(v1[half:k] + v2[0:1])
    mults.append(c1[half:k] * c2[0:1])
    cand = jnp.concatenate(cands, axis=0)
    mult = jnp.concatenate(mults, axis=0)
    top = None
    cum = tau = z = None
    for _ in range(k):
        m = jnp.max(cand, axis=0, keepdims=True)
        eq = cand == m
        c = jnp.sum(jnp.where(eq, mult, 0.0), axis=0, keepdims=True)
        cand = jnp.where(eq, EXHAUSTED, cand)
        if top is None:
            top, tau, cum, z = m, m, c, c
        else:
            active = cum < k
            tau = jnp.where(active, m, tau)
            z = z + jnp.where(active, c * jnp.exp(m - top), 0.0)
            cum = cum + jnp.where(active, c, 0.0)
    return tau, top, z


def _peer_query_kernel(h2_ref, wqt_ref, sk_ref, thr_ref, p1_ref, q2_ref, p2_ref):
    qt = _nt_dot(wqt_ref[...], h2_ref[...]).astype(BF16)
    for hd in range(PEER_HEADS):
        s1 = _dot(sk_ref[2 * hd], qt[(2 * hd) * PEER_HALF:(2 * hd + 1) * PEER_HALF, :])
        s2 = _dot(sk_ref[2 * hd + 1], qt[(2 * hd + 1) * PEER_HALF:(2 * hd + 2) * PEER_HALF, :])
        v1, c1 = _top_distinct(s1, PEER_TOPK)
        v2, c2 = _top_distinct(s2, PEER_TOPK)
        tau, _, z = _pair_threshold(v1, c1, v2, c2, PEER_TOPK)
        thr = jnp.full(s1.shape, -1.0, F32)
        q2 = jnp.full(s2.shape, float(PEER_TOPK), F32)
        for a in range(PEER_TOPK):
            row = v1[a:a + 1]
            cnt = jnp.sum(jnp.where(row + v2 >= tau, 1.0, 0.0), axis=0, keepdims=True)
            thr = jnp.where(s1 == row, cnt - 1.0, thr)
            q2 = jnp.where(s2 == v2[a:a + 1], float(a), q2)
        thr_ref[hd] = thr
        p1_ref[hd] = jnp.exp(s1 - v1[0:1])
        q2_ref[hd] = q2.astype(BF16)
        p2 = (jnp.exp(s2 - v2[0:1]) * (0.5 / z)).astype(BF16)
        n_lt = p2.shape[1] // 128
        for lt in range(n_lt):
            dst = (lt + 1) % n_lt
            p2_ref[hd, :, dst * 128:(dst + 1) * 128] = p2[:, lt * 128:(lt + 1) * 128]


def _peer_query(h2, wqt, sk, *, tm=512):
    n = h2.shape[0]
    big = jax.ShapeDtypeStruct((PEER_HEADS, PEER_KEYS, n), F32)
    half = jax.ShapeDtypeStruct((PEER_HEADS, PEER_KEYS, n), BF16)
    bspec = pl.BlockSpec((PEER_HEADS, PEER_KEYS, tm), lambda i: (0, 0, i))
    return pl.pallas_call(
        _peer_query_kernel,
        grid=(n // tm,),
        in_specs=[
            pl.BlockSpec((tm, D_MODEL), lambda i: (i, 0)),
            pl.BlockSpec((PEER_HEADS * PEER_QDIM, D_MODEL), lambda i: (0, 0)),
            pl.BlockSpec((2 * PEER_HEADS, PEER_KEYS, PEER_HALF), lambda i: (0, 0, 0)),
        ],
        out_specs=[bspec, bspec, bspec, bspec],
        out_shape=[big, big, half, half],
        compiler_params=_params(("arbitrary",)),
        name="peer_query",
    )(h2, wqt, sk)


def _peer_dense_kernel(h2_ref, x1_ref, mod_ref, u_ref, vt_ref, thr_ref, p1_ref, q2_ref, p2_ref,
                       o_ref, acc_ref, act_ref, p_ref, q2s_ref, p2s_ref, *, e1_per_step):
    step = pl.program_id(1)
    tm = act_ref.shape[1]
    n_lt = tm // 128

    def lane_tile(lt, shift=0):
        lt = (lt + shift) % n_lt
        return slice(lt * 128, (lt + 1) * 128)

    @pl.when(step == 0)
    def _init():
        acc_ref[...] = jnp.zeros_like(acc_ref)
        q2s_ref[...] = q2_ref[...]
        p2s_ref[...] = p2_ref[...]

    a = _nt_dot(u_ref[...], h2_ref[...])
    act = (a + a * lax.erf(a * (2.0 ** -0.5))).astype(BF16)
    for lt in range(n_lt):
        act_ref[:, lane_tile(lt, 2)] = act[:, lane_tile(lt)]
    rows = 16
    for c in range(e1_per_step):
        for lt in range(n_lt):
            lanes = lane_tile(lt)
            n_r = PEER_KEYS // rows
            g = [None] * n_r
            for hd in range(PEER_HEADS):
                thr_b = jnp.broadcast_to(thr_ref[hd, c:c + 1, lanes], (rows, 128)).astype(BF16)
                p1_b = jnp.broadcast_to(p1_ref[hd, c:c + 1, lanes], (rows, 128)).astype(BF16)
                for r in range(n_r):
                    rs = slice(r * rows, (r + 1) * rows)
                    w = jnp.where(q2s_ref[hd, rs, lanes] <= thr_b,
                                  p2s_ref[hd, rs, lane_tile(lt, 1)] * p1_b, jnp.zeros((), BF16))
                    g[r] = w if g[r] is None else g[r] + w
            for r in range(n_r):
                es = slice(c * PEER_KEYS + r * rows, c * PEER_KEYS + (r + 1) * rows)
                p_ref[es, lanes] = g[r] * act_ref[es, lane_tile(lt, 2)]
    acc_ref[...] += _dot(vt_ref[...], p_ref[...])

    @pl.when(step == pl.num_programs(1) - 1)
    def _fin():
        o_ref[...] = x1_ref[...] + mod_ref[5:6, :] * acc_ref[...].T


def _peer_dense(h2, x1, mod3, u, vt, thr, p1, q2, p2, *, tm=512, e1_per_step=8):
    n = h2.shape[0]
    te = e1_per_step * PEER_KEYS
    tok = lambda i, j: (i, 0)
    big = pl.BlockSpec((PEER_HEADS, PEER_KEYS, tm), lambda i, j: (0, 0, i))
    rows = pl.BlockSpec((PEER_HEADS, e1_per_step, tm), lambda i, j: (0, j, i))
    return pl.pallas_call(
        functools.partial(_peer_dense_kernel, e1_per_step=e1_per_step),
        grid=(n // tm, PEER_EXPERTS // te),
        in_specs=[
            pl.BlockSpec((tm, D_MODEL), tok),
            pl.BlockSpec((tm, D_MODEL), tok),
            pl.BlockSpec((None, 6, D_MODEL), lambda i, j: ((i * tm) // DEC_SEQ, 0, 0)),
            pl.BlockSpec((te, D_MODEL), lambda i, j: (j, 0)),
            pl.BlockSpec((D_MODEL, te), lambda i, j: (0, j)),
            rows, rows, big, big,
        ],
        out_specs=pl.BlockSpec((tm, D_MODEL), tok),
        out_shape=jax.ShapeDtypeStruct((n, D_MODEL), F32),
        scratch_shapes=[pltpu.VMEM((D_MODEL, tm), F32), pltpu.VMEM((te, tm), BF16), pltpu.VMEM((te, tm), BF16),
                        pltpu.VMEM((PEER_HEADS, PEER_KEYS, tm), BF16), pltpu.VMEM((PEER_HEADS, PEER_KEYS, tm), BF16)],
        compiler_params=_params(("arbitrary", "arbitrary")),
        name="peer_dense",
    )(h2, x1, mod3, u, vt, thr, p1, q2, p2)


def _block_ones(width, d):
    idx = jnp.arange(width) // d
    return (idx[:, None] == idx[None, :]).astype(BF16)


def _rope_angles(d):
    q4 = d // 4
    inv = ROPE_BASE ** (-jnp.arange(q4, dtype=F32) / q4)
    t = jnp.arange(DEC_SEQ)
    ar = (t // GRID_W).astype(F32)[:, None] * inv
    ac = (t % GRID_W).astype(F32)[:, None] * inv
    return jnp.concatenate([ar, ar, ac, ac], axis=-1)


def _rope_table(d, group, n_groups):
    ang = _rope_angles(d)
    q4 = d // 4
    first = (jnp.arange(d) % (2 * q4)) < q4
    cos = jnp.cos(ang)
    sin = jnp.sin(ang)
    sa = jnp.where(first[None, :], -sin, 0.0)
    sb = jnp.where(first[None, :], 0.0, sin)
    pad = group - d
    cos = jnp.pad(cos, ((0, 0), (pad, 0)), constant_values=1.0)
    sa = jnp.pad(sa, ((0, 0), (pad, 0)))
    sb = jnp.pad(sb, ((0, 0), (pad, 0)))
    return jnp.stack([jnp.tile(cos, (1, n_groups)), jnp.tile(sa, (1, n_groups)), jnp.tile(sb, (1, n_groups))])


def _rot_matrix(d):
    q4 = d // 4
    i = jnp.arange(d)
    first = (i % (2 * q4)) < q4
    src = jnp.where(first, i + q4, i - q4)
    sign = jnp.where(first, -1.0, 1.0)
    return (jnp.zeros((d, d), F32).at[src, i].set(sign)).astype(BF16)


def _pad_row(v, width):
    return jnp.pad(v, (0, width - v.shape[0]))


def kernel(x_prompt, x_sample, c, cache_a_k, cache_a_v, cache_b_k, cache_b_v, cache_c_k, cache_c_v, cache_d_ckv, cache_d_krope, c_ctx, ada_w, ada_b, norm1_g, norm2_g, w_in, a_lam, a_qk_g, a_subln_g, b_sink, b_qk_g, c_qk_g, d_qnorm_g, d_kvnorm_g, w_d_qb, w_d_kvb, d_qk_g, w_branch, w_out, peer_wq, peer_subkeys, peer_u, peer_v):
    x = jnp.concatenate([x_prompt.reshape(N_CTX, D_MODEL), x_sample.reshape(N_LAT, D_MODEL)], axis=0)
    cvec = jnp.concatenate([c_ctx[None, :], c, jnp.zeros((N_MODROWS - 1 - DEC_BATCH, D_MODEL), F32)], axis=0)
    mod_all = _modulation(cvec, ada_w, ada_b)

    e32 = _block_ones(256, A_QK)
    e64 = _block_ones(256, HEAD_DIM)
    e96 = _block_ones(GAIN_W, D_QK)
    tab_a = _rope_table(A_QK, A_QK, 2 * A_HEADS)
    tab_b = _rope_table(HEAD_DIM, HEAD_DIM, B_HEADS)
    tab_d = _rope_table(D_ROPE, D_QK, D_HEADS)
    ang_r = _rope_angles(D_ROPE)
    tab_r = jnp.stack([jnp.cos(ang_r), jnp.sin(ang_r)])
    rot_r = _rot_matrix(D_ROPE)

    states = [[] for _ in range(8)]
    for l in range(DEPTH):
        lam_init = 0.8 - 0.6 * math.exp(-0.3 * l)
        mod3 = mod_all[l, :3].reshape(3, 6, D_MODEL)
        ws = jnp.pad(w_in[l, :, :SMALL_COLS], ((0, 0), (0, SMALL_PAD - SMALL_COLS))).astype(BF16)
        wg = w_in[l, :, SMALL_COLS:].astype(BF16)
        gains = jnp.stack([
            _pad_row(jnp.tile(a_qk_g[l, 0], 2 * A_HEADS), GAIN_W),
            _pad_row(jnp.tile(a_qk_g[l, 1], 2 * A_HEADS), GAIN_W),
            _pad_row(jnp.tile(b_qk_g[l, 0], B_HEADS), GAIN_W),
            _pad_row(jnp.tile(b_qk_g[l, 1], B_KV), GAIN_W),
            _pad_row(jnp.tile(c_qk_g[l, 0], C_HEADS), GAIN_W),
            _pad_row(jnp.tile(c_qk_g[l, 1], C_KV), GAIN_W),
            _pad_row(d_qnorm_g[l], GAIN_W),
            _pad_row(d_kvnorm_g[l], GAIN_W),
            jnp.tile(d_qk_g[l, 0], D_HEADS),
        ] + [jnp.zeros((GAIN_W,), F32)] * (GAIN_ROWS - 9))

        (h, aq, ak, av, bq, bk, bv, cq, ck, cv, dq,
         akt, avt, bkt, bvt, ckt, cvt, dckv, dkr) = _input_stage(
            x, mod3, norm1_g[l][None, :], ws, gains, e32, e64, e96, w_d_qb[l].astype(BF16), tab_a, tab_b, tab_d)

        for lst, s_ in zip(states, (
                akt[:N_CTX].reshape(BATCH, SEQ, A_HEADS, 2, A_QK), avt[:N_CTX].reshape(BATCH, SEQ, A_HEADS, A_V),
                bkt[:N_CTX].reshape(BATCH, SEQ, B_KV, HEAD_DIM), bvt[:N_CTX].reshape(BATCH, SEQ, B_KV, HEAD_DIM),
                ckt[:N_CTX].reshape(BATCH, SEQ, C_KV, HEAD_DIM), cvt[:N_CTX].reshape(BATCH, SEQ, C_KV, HEAD_DIM),
                dckv[:N_CTX].reshape(BATCH, SEQ, D_KV_LORA), dkr[:N_CTX].reshape(BATCH, SEQ, D_ROPE))):
            lst.append(s_)

        ckv_rows = jnp.concatenate([t for b in range(DEC_BATCH) for t in (
            cache_d_ckv[b, l], dckv[N_CTX + b * DEC_SEQ:N_CTX + (b + 1) * DEC_SEQ])] + [dckv[:N_CTX]], axis=0)
        kr_rows = jnp.concatenate([t for b in range(DEC_BATCH) for t in (
            cache_d_krope[b, l], dkr[N_CTX + b * DEC_SEQ:N_CTX + (b + 1) * DEC_SEQ])] + [dkr[:N_CTX]], axis=0)
        kd, vd = _mla_expand(ckv_rows, kr_rows, w_d_kvb[l].astype(BF16), d_qk_g[l, 1][None, :], rot_r, tab_r)

        def lat_kv(new, cache, nh, d):
            cache = jnp.moveaxis(cache.reshape(DEC_BATCH, PAST_LEN, nh, d), 2, 0).astype(BF16)
            new = new[:, N_CTX:].reshape(nh, DEC_BATCH, DEC_SEQ, d)
            return jnp.concatenate([cache, new], axis=2).reshape(nh, DEC_BATCH * KV_LAT, d)

        ak_l = lat_kv(ak, cache_a_k[:, l], 2 * A_HEADS, A_QK)
        av_l = lat_kv(av, cache_a_v[:, l], A_HEADS, A_V)
        bk_l = lat_kv(bk, cache_b_k[:, l], B_KV, HEAD_DIM).reshape(B_KV, DEC_BATCH, KV_LAT, HEAD_DIM)
        bv_l = lat_kv(bv, cache_b_v[:, l], B_KV, HEAD_DIM).reshape(B_KV, DEC_BATCH, KV_LAT, HEAD_DIM)
        ck_l = lat_kv(ck, cache_c_k[:, l], C_KV, HEAD_DIM)
        cv_l = lat_kv(cv, cache_c_v[:, l], C_KV, HEAD_DIM)

        ctx = dict(batch=BATCH, tq_rows=SEQ, tk_rows=SEQ, q_off=0, o_off=0, n_out=N_TOK,
                   lam_init=lam_init, tq=SEQ, tk=SEQ)
        lat = dict(batch=DEC_BATCH, tq_rows=DEC_SEQ, tk_rows=KV_LAT, q_off=N_CTX, k_off=0, o_off=N_CTX, n_out=N_TOK,
                   lam_init=lam_init, tq=512, tk=KV_LAT // 3)
        a_extra = (a_lam[l], a_subln_g[l][None, :])
        oa = _attention(aq, ak, av, a_extra, None, k_off=0, g_q=4, g_k=4, g_v=2, dv=A_V, mode="diff", **ctx)
        oa = _attention(aq, ak_l, av_l, a_extra, oa, g_q=4, g_k=4, g_v=2, dv=A_V, mode="diff", **lat)
        ob = _attention(bq, bk, bv, (b_sink[l],), None, k_off=0, g_q=2, g_k=1, g_v=1, dv=HEAD_DIM, mode="sink",
                        **ctx)
        ob = _window_attention(bq, bk_l, bv_l, b_sink[l], ob, q_off=N_CTX, o_off=N_CTX)
        oc = _attention(cq, ck, cv, (), None, k_off=0, g_q=2, g_k=1, g_v=1, dv=HEAD_DIM, mode="plain", **ctx)
        oc = _attention(cq, ck_l, cv_l, (), oc, g_q=2, g_k=1, g_v=1, dv=HEAD_DIM, mode="plain", **lat)
        od = _attention(dq, kd, vd, (), None, k_off=DEC_BATCH * KV_LAT, g_q=2, g_k=2, g_v=2, dv=D_V,
                        mode="plain", **ctx)
        od = _attention(dq, kd, vd, (), od, g_q=2, g_k=2, g_v=2, dv=D_V, mode="plain", **lat)

        x1, h2 = _merge_stage(x, h, oa, ob, oc, od, mod3, wg, w_branch[l].astype(BF16), w_out[l].astype(BF16),
                              norm2_g[l][None, :])

        thr, p1, q2, p2 = _peer_query(
            h2, peer_wq[l].T.astype(BF16),
            peer_subkeys[l].reshape(2 * PEER_HEADS, PEER_KEYS, PEER_HALF).astype(BF16))
        x = _peer_dense(h2, x1, mod3, peer_u[l].astype(BF16), peer_v[l].T.astype(BF16), thr, p1, q2, p2)

    y_prompt = x[:N_CTX].reshape(BATCH, SEQ, D_MODEL)
    y_sample = x[N_CTX:].reshape(DEC_BATCH, DEC_SEQ, D_MODEL)
    return (y_prompt, y_sample) + tuple(jnp.stack(s_, axis=1) for s_ in states)
```

```python
import functools
import math

import jax
import jax.numpy as jnp
from jax import lax
from jax.experimental import pallas as pl
from jax.experimental.pallas import tpu as pltpu

F32 = jnp.float32
BF16 = jnp.bfloat16

D_MODEL = 1024
BATCH = 16
SEQ = 256
DEPTH = 2
DEC_BATCH = 2
DEC_SEQ = 4096
PAST_LEN = 512
GRID_W = 64
Q_BLOCK = 128
HEAD_DIM = 64
ROPE_BASE = 10000.0
EPS = 1e-6
NEG_INF = -1e30
A_HEADS = 4
A_QK = 32
A_V = 64
B_HEADS = 4
B_KV = 2
WINDOW = 128
C_HEADS = 4
C_KV = 2
D_HEADS = 4
D_Q_LORA = 256
D_KV_LORA = 128
D_NOPE = 64
D_ROPE = 32
D_V = 64
D_QK = D_NOPE + D_ROPE
N_BRANCH = 4
BRANCH_W = 256
PEER_HEADS = 8
PEER_KEYS = 128
PEER_EXPERTS = PEER_KEYS * PEER_KEYS
PEER_QDIM = 256
PEER_HALF = PEER_QDIM // 2
PEER_TOPK = 16

N_CTX = BATCH * SEQ
N_LAT = DEC_BATCH * DEC_SEQ
N_TOK = N_CTX + N_LAT
KV_LAT = PAST_LEN + DEC_SEQ
N_MODROWS = 8
SMALL_COLS = 2208
SMALL_PAD = 2304

VMEM_LIMIT = 56 * 1024 * 1024
EXHAUSTED = -3.0e38


def _params(sem, flags=None):
    return pltpu.CompilerParams(dimension_semantics=sem, vmem_limit_bytes=VMEM_LIMIT, flags=flags)


def _nt_dot(a, b):
    return lax.dot_general(a, b, (((1,), (1,)), ((), ())), preferred_element_type=F32)


def _dot(a, b):
    return jnp.dot(a, b, preferred_element_type=F32)


def _mod_kernel(c_ref, w_ref, b_ref, o_ref):
    c = c_ref[...]
    s = c * jax.nn.sigmoid(c)
    o_ref[...] = jnp.dot(s, w_ref[...], preferred_element_type=F32, precision=lax.Precision.HIGHEST) + b_ref[...]


def _modulation(cvec, ada_w, ada_b):
    depth = ada_w.shape[0]
    ncol = ada_w.shape[2] // D_MODEL
    return pl.pallas_call(
        _mod_kernel,
        grid=(depth, ncol),
        in_specs=[
            pl.BlockSpec((N_MODROWS, D_MODEL), lambda l, j: (0, 0)),
            pl.BlockSpec((None, D_MODEL, D_MODEL), lambda l, j: (l, 0, j)),
            pl.BlockSpec((None, 1, D_MODEL), lambda l, j: (l, 0, j)),
        ],
        out_specs=pl.BlockSpec((None, N_MODROWS, D_MODEL), lambda l, j: (l, 0, j)),
        out_shape=jax.ShapeDtypeStruct((depth, N_MODROWS, ada_w.shape[2]), F32),
        compiler_params=_params(("arbitrary", "arbitrary")),
        name="modulation",
    )(cvec, ada_w, ada_b.reshape(depth, 1, -1))


def _group_rms(x, ones_bd, gain, d):
    sq = x * x
    hi = sq.astype(BF16)
    lo = (sq - hi.astype(F32)).astype(BF16)
    ss = _dot(hi, ones_bd) + _dot(lo, ones_bd)
    return x * lax.rsqrt(ss * (1.0 / d) + EPS) * gain


def _row_rms(x, gain):
    ms = jnp.mean(x * x, axis=-1, keepdims=True)
    return x * lax.rsqrt(ms + EPS) * gain


def _rope(x, tab_ref, q4):
    w = x.shape[-1]
    return x * tab_ref[0] + pltpu.roll(x, w - q4, 1) * tab_ref[1] + pltpu.roll(x, q4, 1) * tab_ref[2]


GAIN_ROWS = 16
GAIN_W = D_HEADS * D_QK


def _in_kernel(x_ref, mod_ref, n1g_ref, ws_ref, gains_ref, e32_ref, e64_ref, e96_ref, wdq_ref,
               ta_ref, tb_ref, td_ref,
               h_ref, aq_ref, ak_ref, av_ref, bq_ref, bk_ref, bv_ref, cq_ref, ck_ref, cv_ref, dq_ref,
               akt_ref, avt_ref, bkt_ref, bvt_ref, ckt_ref, cvt_ref, dckv_ref, dkr_ref, *, tm, n_ctx):
    is_lat = pl.program_id(0) * tm >= n_ctx
    x = x_ref[...]
    h = _row_rms(x, n1g_ref[...]) * (1.0 + mod_ref[1:2, :]) + mod_ref[0:1, :]
    hb = h.astype(BF16)
    h_ref[...] = hb
    p = _dot(hb, ws_ref[...])

    def gain(r, w):
        return gains_ref[r:r + 1, 0:w]

    def heads_out(ref, val, nh, d):
        for i in range(nh):
            ref[i] = val[:, i * d:(i + 1) * d].astype(BF16)

    def maybe_rope(val, tab_ref, q4):
        return jnp.where(is_lat, _rope(val, tab_ref, q4), val)

    aq = maybe_rope(_group_rms(p[:, 0:256], e32_ref[...], gain(0, 256), A_QK), ta_ref, A_QK // 4)
    ak = maybe_rope(_group_rms(p[:, 256:512], e32_ref[...], gain(1, 256), A_QK), ta_ref, A_QK // 4)
    av = p[:, 512:768]
    heads_out(aq_ref, aq * (A_QK ** -0.5), 2 * A_HEADS, A_QK)
    heads_out(ak_ref, ak, 2 * A_HEADS, A_QK)
    heads_out(av_ref, av, A_HEADS, A_V)
    akt_ref[...] = ak
    avt_ref[...] = av
    bq = maybe_rope(_group_rms(p[:, 768:1024], e64_ref[...], gain(2, 256), HEAD_DIM), tb_ref, HEAD_DIM // 4)
    bk = _group_rms(p[:, 1024:1152], e64_ref[0:128, 0:128], gain(3, 128), HEAD_DIM)
    bk = jnp.where(is_lat, bk * tb_ref[0, :, 0:128] + pltpu.roll(bk, 128 - 16, 1) * tb_ref[1, :, 0:128]
                   + pltpu.roll(bk, 16, 1) * tb_ref[2, :, 0:128], bk)
    bv = p[:, 1152:1280]
    heads_out(bq_ref, bq * (HEAD_DIM ** -0.5), B_HEADS, HEAD_DIM)
    heads_out(bk_ref, bk, B_KV, HEAD_DIM)
    heads_out(bv_ref, bv, B_KV, HEAD_DIM)
    bkt_ref[...] = bk
    bvt_ref[...] = bv
    cq = maybe_rope(_group_rms(p[:, 1280:1536], e64_ref[...], gain(4, 256), HEAD_DIM), tb_ref, HEAD_DIM // 4)
    ck = _group_rms(p[:, 1536:1664], e64_ref[0:128, 0:128], gain(5, 128), HEAD_DIM)
    ck = jnp.where(is_lat, ck * tb_ref[0, :, 0:128] + pltpu.roll(ck, 128 - 16, 1) * tb_ref[1, :, 0:128]
                   + pltpu.roll(ck, 16, 1) * tb_ref[2, :, 0:128], ck)
    cv = p[:, 1664:1792]
    heads_out(cq_ref, cq * (HEAD_DIM ** -0.5), C_HEADS, HEAD_DIM)
    heads_out(ck_ref, ck, C_KV, HEAD_DIM)
    heads_out(cv_ref, cv, C_KV, HEAD_DIM)
    ckt_ref[...] = ck
    cvt_ref[...] = cv
    dqa = _row_rms(p[:, 1792:2048], gain(6, 256))
    dq = _dot(dqa.astype(BF16), wdq_ref[...])
    dq = maybe_rope(_group_rms(dq, e96_ref[...], gain(8, GAIN_W), D_QK), td_ref, D_ROPE // 4)
    heads_out(dq_ref, dq * (D_QK ** -0.5), D_HEADS, D_QK)
    dckv_ref[...] = _row_rms(p[:, 2048:2176], gain(7, 128))
    dkr_ref[...] = p[:, 2176:2208]


def _input_stage(x, mod3, n1g, ws, gains, e32, e64, e96, wdq, tab_a, tab_b, tab_d, *, tm=256):
    n = x.shape[0]
    nt = n // tm
    lat0 = N_CTX // tm
    pos_blocks = DEC_SEQ // tm

    def tok(i):
        return (i, 0)

    def pos(i):
        return (0, jnp.maximum(i - lat0, 0) % pos_blocks, 0)

    def const2(i):
        return (0, 0)

    def hm(i):
        return (0, i, 0)

    def hspec(nh, d):
        return pl.BlockSpec((nh, tm, d), hm)

    def hshape(nh, d):
        return jax.ShapeDtypeStruct((nh, n, d), BF16)

    def tspec(w):
        return pl.BlockSpec((tm, w), tok)

    def tshape(w):
        return jax.ShapeDtypeStruct((n, w), F32)

    in_specs = [
        pl.BlockSpec((tm, D_MODEL), tok),
        pl.BlockSpec((None, 6, D_MODEL), lambda i: ((i * tm) // DEC_SEQ, 0, 0)),
        pl.BlockSpec((1, D_MODEL), const2),
        pl.BlockSpec((D_MODEL, SMALL_PAD), const2),
        pl.BlockSpec((GAIN_ROWS, GAIN_W), const2),
        pl.BlockSpec((256, 256), const2),
        pl.BlockSpec((256, 256), const2),
        pl.BlockSpec((GAIN_W, GAIN_W), const2),
        pl.BlockSpec((D_Q_LORA, GAIN_W), const2),
        pl.BlockSpec((3, tm, 256), pos),
        pl.BlockSpec((3, tm, 256), pos),
        pl.BlockSpec((3, tm, GAIN_W), pos),
    ]
    out_specs = [
        pl.BlockSpec((tm, D_MODEL), tok),
        hspec(8, A_QK), hspec(8, A_QK), hspec(4, A_V),
        hspec(4, HEAD_DIM), hspec(2, HEAD_DIM), hspec(2, HEAD_DIM),
        hspec(4, HEAD_DIM), hspec(2, HEAD_DIM), hspec(2, HEAD_DIM),
        hspec(4, D_QK),
        tspec(256), tspec(256), tspec(128), tspec(128), tspec(128), tspec(128), tspec(128), tspec(D_ROPE),
    ]
    out_shape = [
        jax.ShapeDtypeStruct((n, D_MODEL), BF16),
        hshape(8, A_QK), hshape(8, A_QK), hshape(4, A_V),
        hshape(4, HEAD_DIM), hshape(2, HEAD_DIM), hshape(2, HEAD_DIM),
        hshape(4, HEAD_DIM), hshape(2, HEAD_DIM), hshape(2, HEAD_DIM),
        hshape(4, D_QK),
        tshape(256), tshape(256), tshape(128), tshape(128), tshape(128), tshape(128), tshape(128), tshape(D_ROPE),
    ]
    return pl.pallas_call(
        functools.partial(_in_kernel, tm=tm, n_ctx=N_CTX),
        grid=(nt,),
        in_specs=in_specs,
        out_specs=out_specs,
        out_shape=out_shape,
        compiler_params=_params(("arbitrary",)),
        name="input_stage",
    )(x, mod3, n1g, ws, gains, e32, e64, e96, wdq, tab_a, tab_b, tab_d)


def _mla_kernel(ckv_ref, kr_ref, wkvb_ref, g_ref, rot_ref, tab_ref, kd_ref, vd_ref, *, tm, lat_tiles, per_batch_tiles):
    i = pl.program_id(0)
    rotate = jnp.logical_and(i < lat_tiles, i % per_batch_tiles >= PAST_LEN // tm)
    kv = _dot(ckv_ref[...].astype(BF16), wkvb_ref[...])
    kr = kr_ref[...]
    ssr = jnp.sum(kr * kr, axis=-1, keepdims=True)
    z = kr * g_ref[:, D_NOPE:D_QK]
    zh = z.astype(BF16)
    zl = (z - zh.astype(F32)).astype(BF16)
    rot = _dot(zh, rot_ref[...]) + _dot(zl, rot_ref[...])
    rz = jnp.where(rotate, z * tab_ref[0] + rot * tab_ref[1], z)
    g_nope = g_ref[:, 0:D_NOPE]
    for hd in range(D_HEADS):
        kn = kv[:, hd * 128:hd * 128 + D_NOPE]
        r = lax.rsqrt((jnp.sum(kn * kn, axis=-1, keepdims=True) + ssr) * (1.0 / D_QK) + EPS)
        kd_ref[hd] = jnp.concatenate([kn * r * g_nope, rz * r], axis=-1).astype(BF16)
        vd_ref[hd] = kv[:, hd * 128 + D_NOPE:(hd + 1) * 128].astype(BF16)


def _mla_expand(ckv, kr, wkvb, g, rot, tab, *, tm=512):
    rows = ckv.shape[0]
    per_batch_tiles = KV_LAT // tm
    lat_tiles = DEC_BATCH * per_batch_tiles
    past_tiles = PAST_LEN // tm
    pos_blocks = DEC_SEQ // tm

    def pos(i):
        return (0, jnp.maximum(i % per_batch_tiles - past_tiles, 0) % pos_blocks, 0)

    return pl.pallas_call(
        functools.partial(_mla_kernel, tm=tm, lat_tiles=lat_tiles, per_batch_tiles=per_batch_tiles),
        grid=(rows // tm,),
        in_specs=[
            pl.BlockSpec((tm, D_KV_LORA), lambda i: (i, 0)),
            pl.BlockSpec((tm, D_ROPE), lambda i: (i, 0)),
            pl.BlockSpec((D_KV_LORA, D_HEADS * (D_NOPE + D_V)), lambda i: (0, 0)),
            pl.BlockSpec((1, D_QK), lambda i: (0, 0)),
            pl.BlockSpec((D_ROPE, D_ROPE), lambda i: (0, 0)),
            pl.BlockSpec((2, tm, D_ROPE), pos),
        ],
        out_specs=[
            pl.BlockSpec((D_HEADS, tm, D_QK), lambda i: (0, i, 0)),
            pl.BlockSpec((D_HEADS, tm, D_V), lambda i: (0, i, 0)),
        ],
        out_shape=[
            jax.ShapeDtypeStruct((D_HEADS, rows, D_QK), BF16),
            jax.ShapeDtypeStruct((D_HEADS, rows, D_V), BF16),
        ],
        compiler_params=_params(("arbitrary",)),
        name="mla_expand",
    )(ckv, kr, wkvb, g, rot, tab)


def _attn_kernel(*refs, g_q, g_k, g_v, dv, mode, lam_init, has_prev):
    refs = list(refs)
    q_ref, k_ref, v_ref = refs[:3]
    pos = 3
    sink_ref = lam_ref = sub_ref = None
    if mode == "sink":
        sink_ref = refs[pos]
        pos += 1
    elif mode == "diff":
        lam_ref, sub_ref = refs[pos], refs[pos + 1]
        pos += 2
    if has_prev:
        pos += 1
    o_ref = refs[pos]
    grp = pl.program_id(1)

    def probs(g):
        s = _nt_dot(q_ref[g], k_ref[g * g_k // g_q])
        m = jnp.max(s, axis=-1, keepdims=True)
        if mode == "sink":
            sk = sink_ref[grp * g_q + g]
            m = jnp.maximum(m, sk)
        p = jnp.exp(s - m)
        den = jnp.sum(p, axis=-1, keepdims=True)
        if mode == "sink":
            den = den + jnp.exp(sk - m)
        return p, den

    if mode == "diff":
        lp = lam_ref[...]
        lam = (jnp.exp(jnp.sum(lp[0:1] * lp[1:2], axis=-1, keepdims=True))
               - jnp.exp(jnp.sum(lp[2:3] * lp[3:4], axis=-1, keepdims=True)) + lam_init)
        for i in range(g_q // 2):
            p1, den1 = probs(2 * i)
            p2, den2 = probs(2 * i + 1)
            comb = p1 * (1.0 / den1) - p2 * (lam / den2)
            d = _dot(comb.astype(BF16), v_ref[i * g_v // (g_q // 2)])
            o_ref[:, i * dv:(i + 1) * dv] = _row_rms(d, sub_ref[...]) * (1.0 - lam_init)
    else:
        for g in range(g_q):
            p, den = probs(g)
            o_ref[:, g * dv:(g + 1) * dv] = _dot(p.astype(BF16), v_ref[g * g_v // g_q]) * (1.0 / den)


def _attention(q, k, v, extras, o_prev, *, batch, tq_rows, tk_rows, q_off, k_off, o_off, n_out,
               g_q, g_k, g_v, dv, mode, lam_init, tq):
    hq, _, dq = q.shape
    n_groups = hq // g_q
    g_out = g_q // 2 if mode == "diff" else g_q
    nq = tq_rows // tq
    tk = tk_rows
    in_specs = [
        pl.BlockSpec((g_q, tq, dq), lambda b, g, qi: (g, (q_off + b * tq_rows) // tq + qi, 0)),
        pl.BlockSpec((g_k, tk, dq), lambda b, g, qi: (g, (k_off + b * tk_rows) // tk, 0)),
        pl.BlockSpec((g_v, tk, dv), lambda b, g, qi: (g, (k_off + b * tk_rows) // tk, 0)),
    ]
    args = [q, k, v]
    if mode == "sink":
        in_specs.append(pl.BlockSpec(memory_space=pltpu.SMEM))
        args.append(extras[0])
    elif mode == "diff":
        in_specs.append(pl.BlockSpec((4, A_QK), lambda b, g, qi: (0, 0)))
        in_specs.append(pl.BlockSpec((1, dv), lambda b, g, qi: (0, 0)))
        args += list(extras)
    aliases = {}
    if o_prev is not None:
        in_specs.append(pl.BlockSpec(memory_space=pl.ANY))
        aliases = {len(args): 0}
        args.append(o_prev)
    return pl.pallas_call(
        functools.partial(_attn_kernel, g_q=g_q, g_k=g_k, g_v=g_v, dv=dv, mode=mode, lam_init=lam_init,
                          has_prev=o_prev is not None),
        grid=(batch, n_groups, nq),
        in_specs=in_specs,
        out_specs=pl.BlockSpec((tq, g_out * dv), lambda b, g, qi: ((o_off + b * tq_rows) // tq + qi, g)),
        out_shape=jax.ShapeDtypeStruct((n_out, n_groups * g_out * dv), F32),
        input_output_aliases=aliases,
        compiler_params=_params(("arbitrary", "arbitrary", "arbitrary")),
        name="attention_" + mode,
    )(*args)


def _win_kernel(q_ref, k_ref, v_ref, sink_ref, prev_ref, o_ref, *, g_q, dv, tq, seq, past, band):
    del prev_ref
    grp = pl.program_id(1)
    qi = pl.program_id(2)
    start = jnp.clip(qi * tq - WINDOW, 0, seq - band)
    row0 = pl.multiple_of(past + start, WINDOW)
    k_ctx = k_ref[0:past, :]
    v_ctx = v_ref[0:past, :]
    k_band = k_ref[pl.ds(row0, band), :]
    v_band = v_ref[pl.ds(row0, band), :]
    qpos = qi * tq + lax.broadcasted_iota(jnp.int32, (tq, band), 0)
    kpos = start + lax.broadcasted_iota(jnp.int32, (tq, band), 1)
    valid = jnp.abs(kpos - qpos) <= WINDOW
    for g in range(g_q):
        q = q_ref[g]
        s_ctx = _nt_dot(q, k_ctx)
        s_band = jnp.where(valid, _nt_dot(q, k_band), NEG_INF)
        sk = sink_ref[grp * g_q + g]
        m = jnp.maximum(jnp.maximum(jnp.max(s_ctx, axis=-1, keepdims=True),
                                    jnp.max(s_band, axis=-1, keepdims=True)), sk)
        p_ctx = jnp.exp(s_ctx - m)
        p_band = jnp.exp(s_band - m)
        den = (jnp.sum(p_ctx, axis=-1, keepdims=True) + jnp.sum(p_band, axis=-1, keepdims=True)
               + jnp.exp(sk - m))
        num = _dot(p_ctx.astype(BF16), v_ctx) + _dot(p_band.astype(BF16), v_band)
        o_ref[:, g * dv:(g + 1) * dv] = num * (1.0 / den)


def _window_attention(q, k, v, sink, o_prev, *, q_off, o_off, tq=256):
    hq, _, d = q.shape
    hk = k.shape[0]
    g_q = hq // hk
    band = tq + 2 * WINDOW
    nq = DEC_SEQ // tq
    return pl.pallas_call(
        functools.partial(_win_kernel, g_q=g_q, dv=d, tq=tq, seq=DEC_SEQ, past=PAST_LEN, band=band),
        grid=(DEC_BATCH, hk, nq),
        in_specs=[
            pl.BlockSpec((g_q, tq, d), lambda b, g, qi: (g, (q_off + b * DEC_SEQ) // tq + qi, 0)),
            pl.BlockSpec((None, None, KV_LAT, d), lambda b, g, qi: (g, b, 0, 0)),
            pl.BlockSpec((None, None, KV_LAT, d), lambda b, g, qi: (g, b, 0, 0)),
            pl.BlockSpec(memory_space=pltpu.SMEM),
            pl.BlockSpec(memory_space=pl.ANY),
        ],
        out_specs=pl.BlockSpec((tq, g_q * d), lambda b, g, qi: ((o_off + b * DEC_SEQ) // tq + qi, g)),
        out_shape=jax.ShapeDtypeStruct(o_prev.shape, F32),
        input_output_aliases={4: 0},
        compiler_params=_params(("arbitrary", "arbitrary", "arbitrary")),
        name="attention_window",
    )(q, k, v, sink, o_prev)


def _merge_kernel(x_ref, h_ref, oa_ref, ob_ref, oc_ref, od_ref, mod_ref, wg_ref, wb_ref, wo_ref, n2g_ref,
                  x1_ref, h2_ref):
    hb = h_ref[...]
    outs = (oa_ref, ob_ref, oc_ref, od_ref)
    acc = None
    for i in range(N_BRANCH):
        gt = jax.nn.sigmoid(_dot(hb, wg_ref[:, i * D_MODEL:(i + 1) * D_MODEL]))
        br = _dot(outs[i][...].astype(BF16), wb_ref[i])
        acc = gt * br if acc is None else acc + gt * br
    y = _dot(acc.astype(BF16), wo_ref[...])
    x1 = x_ref[...] + mod_ref[2:3, :] * y
    x1_ref[...] = x1
    h2 = _row_rms(x1, n2g_ref[...]) * (1.0 + mod_ref[4:5, :]) + mod_ref[3:4, :]
    h2_ref[...] = h2.astype(BF16)


def _merge_stage(x, h, oa, ob, oc, od, mod3, wg, wb, wo, n2g, *, tm=256):
    n = x.shape[0]

    def tok(i):
        return (i, 0)

    return pl.pallas_call(
        _merge_kernel,
        grid=(n // tm,),
        in_specs=[
            pl.BlockSpec((tm, D_MODEL), tok),
            pl.BlockSpec((tm, D_MODEL), tok),
            pl.BlockSpec((tm, BRANCH_W), tok),
            pl.BlockSpec((tm, BRANCH_W), tok),
            pl.BlockSpec((tm, BRANCH_W), tok),
            pl.BlockSpec((tm, BRANCH_W), tok),
            pl.BlockSpec((None, 6, D_MODEL), lambda i: ((i * tm) // DEC_SEQ, 0, 0)),
            pl.BlockSpec((D_MODEL, N_BRANCH * D_MODEL), lambda i: (0, 0)),
            pl.BlockSpec((N_BRANCH, BRANCH_W, D_MODEL), lambda i: (0, 0, 0)),
            pl.BlockSpec((D_MODEL, D_MODEL), lambda i: (0, 0)),
            pl.BlockSpec((1, D_MODEL), lambda i: (0, 0)),
        ],
        out_specs=[pl.BlockSpec((tm, D_MODEL), tok), pl.BlockSpec((tm, D_MODEL), tok)],
        out_shape=[jax.ShapeDtypeStruct((n, D_MODEL), F32), jax.ShapeDtypeStruct((n, D_MODEL), BF16)],
        compiler_params=_params(("arbitrary",)),
        name="merge_stage",
    )(x, h, oa, ob, oc, od, mod3, wg, wb, wo, n2g)


def _top_distinct(x, iters):
    vals, cnts = [], []
    for _ in range(iters):
        m = jnp.max(x, axis=0, keepdims=True)
        eq = x == m
        c = jnp.sum(jnp.where(eq, 1.0, 0.0), axis=0, keepdims=True)
        x = jnp.where(eq, EXHAUSTED, x)
        vals.append(m)
        cnts.append(jnp.where(m > EXHAUSTED, c, 0.0))
    return jnp.concatenate(vals, axis=0), jnp.concatenate(cnts, axis=0)


def _pair_threshold(v1, c1, v2, c2, k):
    half = k // 2
    cands = [v1[0:1] + v2]
    mults = [c1[0:1] * c2]
    for i in range(1, half):
        cands.append(v1[i:i + 1] + v2[0:half])
        mults.append(c1[i:i + 1] * c2[0:half])
    cands.append(v1[half:k] + v2[0:1])
    mults.append(c1[half:k] * c2[0:1])
    cand = jnp.concatenate(cands, axis=0)
    mult = jnp.concatenate(mults, axis=0)
    top = None
    cum = tau = z = None
    for _ in range(k):
        m = jnp.max(cand, axis=0, keepdims=True)
        eq = cand == m
        c = jnp.sum(jnp.where(eq, mult, 0.0), axis=0, keepdims=True)
        cand = jnp.where(eq, EXHAUSTED, cand)
        if top is None:
            top, tau, cum, z = m, m, c, c
        else:
            active = cum < k
            tau = jnp.where(active, m, tau)
            z = z + jnp.where(active, c * jnp.exp(m - top), 0.0)
            cum = cum + jnp.where(active, c, 0.0)
    return tau, top, z


def _peer_query_kernel(h2_ref, wqt_ref, sk_ref, thr_ref, p1_ref, q2_ref, p2_ref):
    qt = _nt_dot(wqt_ref[...], h2_ref[...]).astype(BF16)
    for hd in range(PEER_HEADS):
        s1 = _dot(sk_ref[2 * hd], qt[(2 * hd) * PEER_HALF:(2 * hd + 1) * PEER_HALF, :])
        s2 = _dot(sk_ref[2 * hd + 1], qt[(2 * hd + 1) * PEER_HALF:(2 * hd + 2) * PEER_HALF, :])
        v1, c1 = _top_distinct(s1, PEER_TOPK)
        v2, c2 = _top_distinct(s2, PEER_TOPK)
        tau, _, z = _pair_threshold(v1, c1, v2, c2, PEER_TOPK)
        thr = jnp.full(s1.shape, -1.0, F32)
        q2 = jnp.full(s2.shape, float(PEER_TOPK), F32)
        for a in range(PEER_TOPK):
            row = v1[a:a + 1]
            cnt = jnp.sum(jnp.where(row + v2 >= tau, 1.0, 0.0), axis=0, keepdims=True)
            thr = jnp.where(s1 == row, cnt - 1.0, thr)
            q2 = jnp.where(s2 == v2[a:a + 1], float(a), q2)
        thr_ref[hd] = thr
        p1_ref[hd] = jnp.exp(s1 - v1[0:1])
        q2_ref[hd] = q2.astype(BF16)
        p2 = (jnp.exp(s2 - v2[0:1]) * (0.5 / z)).astype(BF16)
        n_lt = p2.shape[1] // 128
        for lt in range(n_lt):
            dst = (lt + 1) % n_lt
            p2_ref[hd, :, dst * 128:(dst + 1) * 128] = p2[:, lt * 128:(lt + 1) * 128]


def _peer_query(h2, wqt, sk, *, tm=512):
    n = h2.shape[0]
    big = jax.ShapeDtypeStruct((PEER_HEADS, PEER_KEYS, n), F32)
    half = jax.ShapeDtypeStruct((PEER_HEADS, PEER_KEYS, n), BF16)
    bspec = pl.BlockSpec((PEER_HEADS, PEER_KEYS, tm), lambda i: (0, 0, i))
    return pl.pallas_call(
        _peer_query_kernel,
        grid=(n // tm,),
        in_specs=[
            pl.BlockSpec((tm, D_MODEL), lambda i: (i, 0)),
            pl.BlockSpec((PEER_HEADS * PEER_QDIM, D_MODEL), lambda i: (0, 0)),
            pl.BlockSpec((2 * PEER_HEADS, PEER_KEYS, PEER_HALF), lambda i: (0, 0, 0)),
        ],
        out_specs=[bspec, bspec, bspec, bspec],
        out_shape=[big, big, half, half],
        compiler_params=_params(("arbitrary",)),
        name="peer_query",
    )(h2, wqt, sk)


def _peer_dense_kernel(h2_ref, x1_ref, mod_ref, u_ref, vt_ref, thr_ref, p1_ref, q2_ref, p2_ref,
                       o_ref, acc_ref, act_ref, p_ref, q2s_ref, p2s_ref, *, e1_per_step):
    step = pl.program_id(1)
    tm = act_ref.shape[1]
    n_lt = tm // 128

    def lane_tile(lt, shift=0):
        lt = (lt + shift) % n_lt
        return slice(lt * 128, (lt + 1) * 128)

    @pl.when(step == 0)
    def _init():
        acc_ref[...] = jnp.zeros_like(acc_ref)
        q2s_ref[...] = q2_ref[...]
        p2s_ref[...] = p2_ref[...]

    a = _nt_dot(u_ref[...], h2_ref[...])
    act = (a + a * lax.erf(a * (2.0 ** -0.5))).astype(BF16)
    for lt in range(n_lt):
        act_ref[:, lane_tile(lt, 2)] = act[:, lane_tile(lt)]
    rows = 16
    for c in range(e1_per_step):
        for lt in range(n_lt):
            lanes = lane_tile(lt)
            n_r = PEER_KEYS // rows
            g = [None] * n_r
            for hd in range(PEER_HEADS):
                thr_b = jnp.broadcast_to(thr_ref[hd, c:c + 1, lanes], (rows, 128)).astype(BF16)
                p1_b = jnp.broadcast_to(p1_ref[hd, c:c + 1, lanes], (rows, 128)).astype(BF16)
                for r in range(n_r):
                    rs = slice(r * rows, (r + 1) * rows)
                    w = jnp.where(q2s_ref[hd, rs, lanes] <= thr_b,
                                  p2s_ref[hd, rs, lane_tile(lt, 1)] * p1_b, jnp.zeros((), BF16))
                    g[r] = w if g[r] is None else g[r] + w
            for r in range(n_r):
                es = slice(c * PEER_KEYS + r * rows, c * PEER_KEYS + (r + 1) * rows)
                p_ref[es, lanes] = g[r] * act_ref[es, lane_tile(lt, 2)]
    acc_ref[...] += _dot(vt_ref[...], p_ref[...])

    @pl.when(step == pl.num_programs(1) - 1)
    def _fin():
        o_ref[...] = x1_ref[...] + mod_ref[5:6, :] * acc_ref[...].T


def _peer_dense(h2, x1, mod3, u, vt, thr, p1, q2, p2, *, tm=512, e1_per_step=8):
    n = h2.shape[0]
    te = e1_per_step * PEER_KEYS
    tok = lambda i, j: (i, 0)
    big = pl.BlockSpec((PEER_HEADS, PEER_KEYS, tm), lambda i, j: (0, 0, i))
    rows = pl.BlockSpec((PEER_HEADS, e1_per_step, tm), lambda i, j: (0, j, i))
    return pl.pallas_call(
        functools.partial(_peer_dense_kernel, e1_per_step=e1_per_step),
        grid=(n // tm, PEER_EXPERTS // te),
        in_specs=[
            pl.BlockSpec((tm, D_MODEL), tok),
            pl.BlockSpec((tm, D_MODEL), tok),
            pl.BlockSpec((None, 6, D_MODEL), lambda i, j: ((i * tm) // DEC_SEQ, 0, 0)),
            pl.BlockSpec((te, D_MODEL), lambda i, j: (j, 0)),
            pl.BlockSpec((D_MODEL, te), lambda i, j: (0, j)),
            rows, rows, big, big,
        ],
        out_specs=pl.BlockSpec((tm, D_MODEL), tok),
        out_shape=jax.ShapeDtypeStruct((n, D_MODEL), F32),
        scratch_shapes=[pltpu.VMEM((D_MODEL, tm), F32), pltpu.VMEM((te, tm), BF16), pltpu.VMEM((te, tm), BF16),
                        pltpu.VMEM((PEER_HEADS, PEER_KEYS, tm), BF16), pltpu.VMEM((PEER_HEADS, PEER_KEYS, tm), BF16)],
        compiler_params=_params(("arbitrary", "arbitrary")),
        name="peer_dense",
    )(h2, x1, mod3, u, vt, thr, p1, q2, p2)


def _block_ones(width, d):
    idx = jnp.arange(width) // d
    return (idx[:, None] == idx[None, :]).astype(BF16)


def _rope_angles(d):
    q4 = d // 4
    inv = ROPE_BASE ** (-jnp.arange(q4, dtype=F32) / q4)
    t = jnp.arange(DEC_SEQ)
    ar = (t // GRID_W).astype(F32)[:, None] * inv
    ac = (t % GRID_W).astype(F32)[:, None] * inv
    return jnp.concatenate([ar, ar, ac, ac], axis=-1)


def _rope_table(d, group, n_groups):
    ang = _rope_angles(d)
    q4 = d // 4
    first = (jnp.arange(d) % (2 * q4)) < q4
    cos = jnp.cos(ang)
    sin = jnp.sin(ang)
    sa = jnp.where(first[None, :], -sin, 0.0)
    sb = jnp.where(first[None, :], 0.0, sin)
    pad = group - d
    cos = jnp.pad(cos, ((0, 0), (pad, 0)), constant_values=1.0)
    sa = jnp.pad(sa, ((0, 0), (pad, 0)))
    sb = jnp.pad(sb, ((0, 0), (pad, 0)))
    return jnp.stack([jnp.tile(cos, (1, n_groups)), jnp.tile(sa, (1, n_groups)), jnp.tile(sb, (1, n_groups))])


def _rot_matrix(d):
    q4 = d // 4
    i = jnp.arange(d)
    first = (i % (2 * q4)) < q4
    src = jnp.where(first, i + q4, i - q4)
    sign = jnp.where(first, -1.0, 1.0)
    return (jnp.zeros((d, d), F32).at[src, i].set(sign)).astype(BF16)


def _pad_row(v, width):
    return jnp.pad(v, (0, width - v.shape[0]))


def kernel(x_prompt, x_sample, c, cache_a_k, cache_a_v, cache_b_k, cache_b_v, cache_c_k, cache_c_v, cache_d_ckv, cache_d_krope, c_ctx, ada_w, ada_b, norm1_g, norm2_g, w_in, a_lam, a_qk_g, a_subln_g, b_sink, b_qk_g, c_qk_g, d_qnorm_g, d_kvnorm_g, w_d_qb, w_d_kvb, d_qk_g, w_branch, w_out, peer_wq, peer_subkeys, peer_u, peer_v):
    x = jnp.concatenate([x_prompt.reshape(N_CTX, D_MODEL), x_sample.reshape(N_LAT, D_MODEL)], axis=0)
    cvec = jnp.concatenate([c_ctx[None, :], c, jnp.zeros((N_MODROWS - 1 - DEC_BATCH, D_MODEL), F32)], axis=0)
    mod_all = _modulation(cvec, ada_w, ada_b)

    e32 = _block_ones(256, A_QK)
    e64 = _block_ones(256, HEAD_DIM)
    e96 = _block_ones(GAIN_W, D_QK)
    tab_a = _rope_table(A_QK, A_QK, 2 * A_HEADS)
    tab_b = _rope_table(HEAD_DIM, HEAD_DIM, B_HEADS)
    tab_d = _rope_table(D_ROPE, D_QK, D_HEADS)
    ang_r = _rope_angles(D_ROPE)
    tab_r = jnp.stack([jnp.cos(ang_r), jnp.sin(ang_r)])
    rot_r = _rot_matrix(D_ROPE)

    states = [[] for _ in range(8)]
    for l in range(DEPTH):
        lam_init = 0.8 - 0.6 * math.exp(-0.3 * l)
        mod3 = mod_all[l, :3].reshape(3, 6, D_MODEL)
        ws = jnp.pad(w_in[l, :, :SMALL_COLS], ((0, 0), (0, SMALL_PAD - SMALL_COLS))).astype(BF16)
        wg = w_in[l, :, SMALL_COLS:].astype(BF16)
        gains = jnp.stack([
            _pad_row(jnp.tile(a_qk_g[l, 0], 2 * A_HEADS), GAIN_W),
            _pad_row(jnp.tile(a_qk_g[l, 1], 2 * A_HEADS), GAIN_W),
            _pad_row(jnp.tile(b_qk_g[l, 0], B_HEADS), GAIN_W),
            _pad_row(jnp.tile(b_qk_g[l, 1], B_KV), GAIN_W),
            _pad_row(jnp.tile(c_qk_g[l, 0], C_HEADS), GAIN_W),
            _pad_row(jnp.tile(c_qk_g[l, 1], C_KV), GAIN_W),
            _pad_row(d_qnorm_g[l], GAIN_W),
            _pad_row(d_kvnorm_g[l], GAIN_W),
            jnp.tile(d_qk_g[l, 0], D_HEADS),
        ] + [jnp.zeros((GAIN_W,), F32)] * (GAIN_ROWS - 9))

        (h, aq, ak, av, bq, bk, bv, cq, ck, cv, dq,
         akt, avt, bkt, bvt, ckt, cvt, dckv, dkr) = _input_stage(
            x, mod3, norm1_g[l][None, :], ws, gains, e32, e64, e96, w_d_qb[l].astype(BF16), tab_a, tab_b, tab_d)

        for lst, s_ in zip(states, (
                akt[:N_CTX].reshape(BATCH, SEQ, A_HEADS, 2, A_QK), avt[:N_CTX].reshape(BATCH, SEQ, A_HEADS, A_V),
                bkt[:N_CTX].reshape(BATCH, SEQ, B_KV, HEAD_DIM), bvt[:N_CTX].reshape(BATCH, SEQ, B_KV, HEAD_DIM),
                ckt[:N_CTX].reshape(BATCH, SEQ, C_KV, HEAD_DIM), cvt[:N_CTX].reshape(BATCH, SEQ, C_KV, HEAD_DIM),
                dckv[:N_CTX].reshape(BATCH, SEQ, D_KV_LORA), dkr[:N_CTX].reshape(BATCH, SEQ, D_ROPE))):
            lst.append(s_)

        ckv_rows = jnp.concatenate([t for b in range(DEC_BATCH) for t in (
            cache_d_ckv[b, l], dckv[N_CTX + b * DEC_SEQ:N_CTX + (b + 1) * DEC_SEQ])] + [dckv[:N_CTX]], axis=0)
        kr_rows = jnp.concatenate([t for b in range(DEC_BATCH) for t in (
            cache_d_krope[b, l], dkr[N_CTX + b * DEC_SEQ:N_CTX + (b + 1) * DEC_SEQ])] + [dkr[:N_CTX]], axis=0)
        kd, vd = _mla_expand(ckv_rows, kr_rows, w_d_kvb[l].astype(BF16), d_qk_g[l, 1][None, :], rot_r, tab_r)

        def lat_kv(new, cache, nh, d):
            cache = jnp.moveaxis(cache.reshape(DEC_BATCH, PAST_LEN, nh, d), 2, 0).astype(BF16)
            new = new[:, N_CTX:].reshape(nh, DEC_BATCH, DEC_SEQ, d)
            return jnp.concatenate([cache, new], axis=2).reshape(nh, DEC_BATCH * KV_LAT, d)

        ak_l = lat_kv(ak, cache_a_k[:, l], 2 * A_HEADS, A_QK)
        av_l = lat_kv(av, cache_a_v[:, l], A_HEADS, A_V)
        bk_l = lat_kv(bk, cache_b_k[:, l], B_KV, HEAD_DIM).reshape(B_KV, DEC_BATCH, KV_LAT, HEAD_DIM)
        bv_l = lat_kv(bv, cache_b_v[:, l], B_KV, HEAD_DIM).reshape(B_KV, DEC_BATCH, KV_LAT, HEAD_DIM)
        ck_l = lat_kv(ck, cache_c_k[:, l], C_KV, HEAD_DIM)
        cv_l = lat_kv(cv, cache_c_v[:, l], C_KV, HEAD_DIM)

        ctx = dict(batch=BATCH, tq_rows=SEQ, tk_rows=SEQ, q_off=0, o_off=0, n_out=N_TOK,
                   lam_init=lam_init, tq=SEQ)
        lat = dict(batch=DEC_BATCH, tq_rows=DEC_SEQ, tk_rows=KV_LAT, q_off=N_CTX, k_off=0, o_off=N_CTX, n_out=N_TOK,
                   lam_init=lam_init, tq=256)
        a_extra = (a_lam[l], a_subln_g[l][None, :])
        oa = _attention(aq, ak, av, a_extra, None, k_off=0, g_q=4, g_k=4, g_v=2, dv=A_V, mode="diff", **ctx)
        oa = _attention(aq, ak_l, av_l, a_extra, oa, g_q=4, g_k=4, g_v=2, dv=A_V, mode="diff", **lat)
        ob = _attention(bq, bk, bv, (b_sink[l],), None, k_off=0, g_q=2, g_k=1, g_v=1, dv=HEAD_DIM, mode="sink",
                        **ctx)
        ob = _window_attention(bq, bk_l, bv_l, b_sink[l], ob, q_off=N_CTX, o_off=N_CTX)
        oc = _attention(cq, ck, cv, (), None, k_off=0, g_q=2, g_k=1, g_v=1, dv=HEAD_DIM, mode="plain", **ctx)
        oc = _attention(cq, ck_l, cv_l, (), oc, g_q=2, g_k=1, g_v=1, dv=HEAD_DIM, mode="plain", **lat)
        od = _attention(dq, kd, vd, (), None, k_off=DEC_BATCH * KV_LAT, g_q=2, g_k=2, g_v=2, dv=D_V,
                        mode="plain", **ctx)
        od = _attention(dq, kd, vd, (), od, g_q=2, g_k=2, g_v=2, dv=D_V, mode="plain", **lat)

        x1, h2 = _merge_stage(x, h, oa, ob, oc, od, mod3, wg, w_branch[l].astype(BF16), w_out[l].astype(BF16),
                              norm2_g[l][None, :])

        thr, p1, q2, p2 = _peer_query(
            h2, peer_wq[l].T.astype(BF16),
            peer_subkeys[l].reshape(2 * PEER_HEADS, PEER_KEYS, PEER_HALF).astype(BF16))
        x = _peer_dense(h2, x1, mod3, peer_u[l].astype(BF16), peer_v[l].T.astype(BF16), thr, p1, q2, p2)

    y_prompt = x[:N_CTX].reshape(BATCH, SEQ, D_MODEL)
    y_sample = x[N_CTX:].reshape(DEC_BATCH, DEC_SEQ, D_MODEL)
    return (y_prompt, y_sample) + tuple(jnp.stack(s_, axis=1) for s_ in states)
```

```python
import functools
import math

import jax
import jax.numpy as jnp
from jax import lax
from jax.experimental import pallas as pl
from jax.experimental.pallas import tpu as pltpu

F32 = jnp.float32
BF16 = jnp.bfloat16

D_MODEL = 1024
BATCH = 16
SEQ = 256
DEPTH = 2
DEC_BATCH = 2
DEC_SEQ = 4096
PAST_LEN = 512
GRID_W = 64
Q_BLOCK = 128
HEAD_DIM = 64
ROPE_BASE = 10000.0
EPS = 1e-6
NEG_INF = -1e30
A_HEADS = 4
A_QK = 32
A_V = 64
B_HEADS = 4
B_KV = 2
WINDOW = 128
C_HEADS = 4
C_KV = 2
D_HEADS = 4
D_Q_LORA = 256
D_KV_LORA = 128
D_NOPE = 64
D_ROPE = 32
D_V = 64
D_QK = D_NOPE + D_ROPE
N_BRANCH = 4
BRANCH_W = 256
PEER_HEADS = 8
PEER_KEYS = 128
PEER_EXPERTS = PEER_KEYS * PEER_KEYS
PEER_QDIM = 256
PEER_HALF = PEER_QDIM // 2
PEER_TOPK = 16

N_CTX = BATCH * SEQ
N_LAT = DEC_BATCH * DEC_SEQ
N_TOK = N_CTX + N_LAT
KV_LAT = PAST_LEN + DEC_SEQ
N_MODROWS = 8
SMALL_COLS = 2208
SMALL_PAD = 2304

VMEM_LIMIT = 56 * 1024 * 1024
EXHAUSTED = -3.0e38


def _params(sem, flags=None):
    return pltpu.CompilerParams(dimension_semantics=sem, vmem_limit_bytes=VMEM_LIMIT, flags=flags)


def _nt_dot(a, b):
    return lax.dot_general(a, b, (((1,), (1,)), ((), ())), preferred_element_type=F32)


def _dot(a, b):
    return jnp.dot(a, b, preferred_element_type=F32)


def _mod_kernel(c_ref, w_ref, b_ref, o_ref):
    c = c_ref[...]
    s = c * jax.nn.sigmoid(c)
    o_ref[...] = jnp.dot(s, w_ref[...], preferred_element_type=F32, precision=lax.Precision.HIGHEST) + b_ref[...]


def _modulation(cvec, ada_w, ada_b):
    depth = ada_w.shape[0]
    ncol = ada_w.shape[2] // D_MODEL
    return pl.pallas_call(
        _mod_kernel,
        grid=(depth, ncol),
        in_specs=[
            pl.BlockSpec((N_MODROWS, D_MODEL), lambda l, j: (0, 0)),
            pl.BlockSpec((None, D_MODEL, D_MODEL), lambda l, j: (l, 0, j)),
            pl.BlockSpec((None, 1, D_MODEL), lambda l, j: (l, 0, j)),
        ],
        out_specs=pl.BlockSpec((None, N_MODROWS, D_MODEL), lambda l, j: (l, 0, j)),
        out_shape=jax.ShapeDtypeStruct((depth, N_MODROWS, ada_w.shape[2]), F32),
        compiler_params=_params(("arbitrary", "arbitrary")),
        name="modulation",
    )(cvec, ada_w, ada_b.reshape(depth, 1, -1))


def _group_rms(x, ones_bd, gain, d):
    sq = x * x
    hi = sq.astype(BF16)
    lo = (sq - hi.astype(F32)).astype(BF16)
    ss = _dot(hi, ones_bd) + _dot(lo, ones_bd)
    return x * lax.rsqrt(ss * (1.0 / d) + EPS) * gain


def _row_rms(x, gain):
    ms = jnp.mean(x * x, axis=-1, keepdims=True)
    return x * lax.rsqrt(ms + EPS) * gain


def _rope(x, tab_ref, q4):
    w = x.shape[-1]
    return x * tab_ref[0] + pltpu.roll(x, w - q4, 1) * tab_ref[1] + pltpu.roll(x, q4, 1) * tab_ref[2]


GAIN_ROWS = 16
GAIN_W = D_HEADS * D_QK


def _in_kernel(x_ref, mod_ref, n1g_ref, ws_ref, gains_ref, e32_ref, e64_ref, e96_ref, wdq_ref,
               ta_ref, tb_ref, td_ref,
               h_ref, aq_ref, ak_ref, av_ref, bq_ref, bk_ref, bv_ref, cq_ref, ck_ref, cv_ref, dq_ref,
               akt_ref, avt_ref, bkt_ref, bvt_ref, ckt_ref, cvt_ref, dckv_ref, dkr_ref, *, tm, n_ctx):
    is_lat = pl.program_id(0) * tm >= n_ctx
    x = x_ref[...]
    h = _row_rms(x, n1g_ref[...]) * (1.0 + mod_ref[1:2, :]) + mod_ref[0:1, :]
    hb = h.astype(BF16)
    h_ref[...] = hb
    p = _dot(hb, ws_ref[...])

    def gain(r, w):
        return gains_ref[r:r + 1, 0:w]

    def heads_out(ref, val, nh, d):
        for i in range(nh):
            ref[i] = val[:, i * d:(i + 1) * d].astype(BF16)

    def maybe_rope(val, tab_ref, q4):
        return jnp.where(is_lat, _rope(val, tab_ref, q4), val)

    aq = maybe_rope(_group_rms(p[:, 0:256], e32_ref[...], gain(0, 256), A_QK), ta_ref, A_QK // 4)
    ak = maybe_rope(_group_rms(p[:, 256:512], e32_ref[...], gain(1, 256), A_QK), ta_ref, A_QK // 4)
    av = p[:, 512:768]
    heads_out(aq_ref, aq * (A_QK ** -0.5), 2 * A_HEADS, A_QK)
    heads_out(ak_ref, ak, 2 * A_HEADS, A_QK)
    heads_out(av_ref, av, A_HEADS, A_V)
    akt_ref[...] = ak
    avt_ref[...] = av
    bq = maybe_rope(_group_rms(p[:, 768:1024], e64_ref[...], gain(2, 256), HEAD_DIM), tb_ref, HEAD_DIM // 4)
    bk = _group_rms(p[:, 1024:1152], e64_ref[0:128, 0:128], gain(3, 128), HEAD_DIM)
    bk = jnp.where(is_lat, bk * tb_ref[0, :, 0:128] + pltpu.roll(bk, 128 - 16, 1) * tb_ref[1, :, 0:128]
                   + pltpu.roll(bk, 16, 1) * tb_ref[2, :, 0:128], bk)
    bv = p[:, 1152:1280]
    heads_out(bq_ref, bq * (HEAD_DIM ** -0.5), B_HEADS, HEAD_DIM)
    heads_out(bk_ref, bk, B_KV, HEAD_DIM)
    heads_out(bv_ref, bv, B_KV, HEAD_DIM)
    bkt_ref[...] = bk
    bvt_ref[...] = bv
    cq = maybe_rope(_group_rms(p[:, 1280:1536], e64_ref[...], gain(4, 256), HEAD_DIM), tb_ref, HEAD_DIM // 4)
    ck = _group_rms(p[:, 1536:1664], e64_ref[0:128, 0:128], gain(5, 128), HEAD_DIM)
    ck = jnp.where(is_lat, ck * tb_ref[0, :, 0:128] + pltpu.roll(ck, 128 - 16, 1) * tb_ref[1, :, 0:128]
                   + pltpu.roll(ck, 16, 1) * tb_ref[2, :, 0:128], ck)
    cv = p[:, 1664:1792]
    heads_out(cq_ref, cq * (HEAD_DIM ** -0.5), C_HEADS, HEAD_DIM)
    heads_out(ck_ref, ck, C_KV, HEAD_DIM)
    heads_out(cv_ref, cv, C_KV, HEAD_DIM)
    ckt_ref[...] = ck
    cvt_ref[...] = cv
    dqa = _row_rms(p[:, 1792:2048], gain(6, 256))
    dq = _dot(dqa.astype(BF16), wdq_ref[...])
    dq = maybe_rope(_group_rms(dq, e96_ref[...], gain(8, GAIN_W), D_QK), td_ref, D_ROPE // 4)
    heads_out(dq_ref, dq * (D_QK ** -0.5), D_HEADS, D_QK)
    dckv_ref[...] = _row_rms(p[:, 2048:2176], gain(7, 128))
    dkr_ref[...] = p[:, 2176:2208]


def _input_stage(x, mod3, n1g, ws, gains, e32, e64, e96, wdq, tab_a, tab_b, tab_d, *, tm=256):
    n = x.shape[0]
    nt = n // tm
    lat0 = N_CTX // tm
    pos_blocks = DEC_SEQ // tm

    def tok(i):
        return (i, 0)

    def pos(i):
        return (0, jnp.maximum(i - lat0, 0) % pos_blocks, 0)

    def const2(i):
        return (0, 0)

    def hm(i):
        return (0, i, 0)

    def hspec(nh, d):
        return pl.BlockSpec((nh, tm, d), hm)

    def hshape(nh, d):
        return jax.ShapeDtypeStruct((nh, n, d), BF16)

    def tspec(w):
        return pl.BlockSpec((tm, w), tok)

    def tshape(w):
        return jax.ShapeDtypeStruct((n, w), F32)

    in_specs = [
        pl.BlockSpec((tm, D_MODEL), tok),
        pl.BlockSpec((None, 6, D_MODEL), lambda i: ((i * tm) // DEC_SEQ, 0, 0)),
        pl.BlockSpec((1, D_MODEL), const2),
        pl.BlockSpec((D_MODEL, SMALL_PAD), const2),
        pl.BlockSpec((GAIN_ROWS, GAIN_W), const2),
        pl.BlockSpec((256, 256), const2),
        pl.BlockSpec((256, 256), const2),
        pl.BlockSpec((GAIN_W, GAIN_W), const2),
        pl.BlockSpec((D_Q_LORA, GAIN_W), const2),
        pl.BlockSpec((3, tm, 256), pos),
        pl.BlockSpec((3, tm, 256), pos),
        pl.BlockSpec((3, tm, GAIN_W), pos),
    ]
    out_specs = [
        pl.BlockSpec((tm, D_MODEL), tok),
        hspec(8, A_QK), hspec(8, A_QK), hspec(4, A_V),
        hspec(4, HEAD_DIM), hspec(2, HEAD_DIM), hspec(2, HEAD_DIM),
        hspec(4, HEAD_DIM), hspec(2, HEAD_DIM), hspec(2, HEAD_DIM),
        hspec(4, D_QK),
        tspec(256), tspec(256), tspec(128), tspec(128), tspec(128), tspec(128), tspec(128), tspec(D_ROPE),
    ]
    out_shape = [
        jax.ShapeDtypeStruct((n, D_MODEL), BF16),
        hshape(8, A_QK), hshape(8, A_QK), hshape(4, A_V),
        hshape(4, HEAD_DIM), hshape(2, HEAD_DIM), hshape(2, HEAD_DIM),
        hshape(4, HEAD_DIM), hshape(2, HEAD_DIM), hshape(2, HEAD_DIM),
        hshape(4, D_QK),
        tshape(256), tshape(256), tshape(128), tshape(128), tshape(128), tshape(128), tshape(128), tshape(D_ROPE),
    ]
    return pl.pallas_call(
        functools.partial(_in_kernel, tm=tm, n_ctx=N_CTX),
        grid=(nt,),
        in_specs=in_specs,
        out_specs=out_specs,
        out_shape=out_shape,
        compiler_params=_params(("arbitrary",)),
        name="input_stage",
    )(x, mod3, n1g, ws, gains, e32, e64, e96, wdq, tab_a, tab_b, tab_d)


def _mla_kernel(ckv_ref, kr_ref, wkvb_ref, g_ref, rot_ref, tab_ref, kd_ref, vd_ref, *, tm, lat_tiles, per_batch_tiles):
    i = pl.program_id(0)
    rotate = jnp.logical_and(i < lat_tiles, i % per_batch_tiles >= PAST_LEN // tm)
    kv = _dot(ckv_ref[...].astype(BF16), wkvb_ref[...])
    kr = kr_ref[...]
    ssr = jnp.sum(kr * kr, axis=-1, keepdims=True)
    z = kr * g_ref[:, D_NOPE:D_QK]
    zh = z.astype(BF16)
    zl = (z - zh.astype(F32)).astype(BF16)
    rot = _dot(zh, rot_ref[...]) + _dot(zl, rot_ref[...])
    rz = jnp.where(rotate, z * tab_ref[0] + rot * tab_ref[1], z)
    g_nope = g_ref[:, 0:D_NOPE]
    for hd in range(D_HEADS):
        kn = kv[:, hd * 128:hd * 128 + D_NOPE]
        r = lax.rsqrt((jnp.sum(kn * kn, axis=-1, keepdims=True) + ssr) * (1.0 / D_QK) + EPS)
        kd_ref[hd] = jnp.concatenate([kn * r * g_nope, rz * r], axis=-1).astype(BF16)
        vd_ref[hd] = kv[:, hd * 128 + D_NOPE:(hd + 1) * 128].astype(BF16)


def _mla_expand(ckv, kr, wkvb, g, rot, tab, *, tm=512):
    rows = ckv.shape[0]
    per_batch_tiles = KV_LAT // tm
    lat_tiles = DEC_BATCH * per_batch_tiles
    past_tiles = PAST_LEN // tm
    pos_blocks = DEC_SEQ // tm

    def pos(i):
        return (0, jnp.maximum(i % per_batch_tiles - past_tiles, 0) % pos_blocks, 0)

    return pl.pallas_call(
        functools.partial(_mla_kernel, tm=tm, lat_tiles=lat_tiles, per_batch_tiles=per_batch_tiles),
        grid=(rows // tm,),
        in_specs=[
            pl.BlockSpec((tm, D_KV_LORA), lambda i: (i, 0)),
            pl.BlockSpec((tm, D_ROPE), lambda i: (i, 0)),
            pl.BlockSpec((D_KV_LORA, D_HEADS * (D_NOPE + D_V)), lambda i: (0, 0)),
            pl.BlockSpec((1, D_QK), lambda i: (0, 0)),
            pl.BlockSpec((D_ROPE, D_ROPE), lambda i: (0, 0)),
            pl.BlockSpec((2, tm, D_ROPE), pos),
        ],
        out_specs=[
            pl.BlockSpec((D_HEADS, tm, D_QK), lambda i: (0, i, 0)),
            pl.BlockSpec((D_HEADS, tm, D_V), lambda i: (0, i, 0)),
        ],
        out_shape=[
            jax.ShapeDtypeStruct((D_HEADS, rows, D_QK), BF16),
            jax.ShapeDtypeStruct((D_HEADS, rows, D_V), BF16),
        ],
        compiler_params=_params(("arbitrary",)),
        name="mla_expand",
    )(ckv, kr, wkvb, g, rot, tab)


def _attn_kernel(*refs, g_q, g_k, g_v, dv, mode, lam_init, has_prev):
    refs = list(refs)
    q_ref, k_ref, v_ref = refs[:3]
    pos = 3
    sink_ref = lam_ref = sub_ref = None
    if mode == "sink":
        sink_ref = refs[pos]
        pos += 1
    elif mode == "diff":
        lam_ref, sub_ref = refs[pos], refs[pos + 1]
        pos += 2
    if has_prev:
        pos += 1
    o_ref = refs[pos]
    grp = pl.program_id(1)

    def probs(g):
        s = _nt_dot(q_ref[g], k_ref[g * g_k // g_q])
        m = jnp.max(s, axis=-1, keepdims=True)
        if mode == "sink":
            sk = sink_ref[grp * g_q + g]
            m = jnp.maximum(m, sk)
        p = jnp.exp(s - m)
        den = jnp.sum(p, axis=-1, keepdims=True)
        if mode == "sink":
            den = den + jnp.exp(sk - m)
        return p, den

    if mode == "diff":
        lp = lam_ref[...]
        lam = (jnp.exp(jnp.sum(lp[0:1] * lp[1:2], axis=-1, keepdims=True))
               - jnp.exp(jnp.sum(lp[2:3] * lp[3:4], axis=-1, keepdims=True)) + lam_init)
        for i in range(g_q // 2):
            p1, den1 = probs(2 * i)
            p2, den2 = probs(2 * i + 1)
            comb = p1 * (1.0 / den1) - p2 * (lam / den2)
            d = _dot(comb.astype(BF16), v_ref[i * g_v // (g_q // 2)])
            o_ref[:, i * dv:(i + 1) * dv] = _row_rms(d, sub_ref[...]) * (1.0 - lam_init)
    else:
        for g in range(g_q):
            p, den = probs(g)
            o_ref[:, g * dv:(g + 1) * dv] = _dot(p.astype(BF16), v_ref[g * g_v // g_q]) * (1.0 / den)


def _attention(q, k, v, extras, o_prev, *, batch, tq_rows, tk_rows, q_off, k_off, o_off, n_out,
               g_q, g_k, g_v, dv, mode, lam_init, tq):
    hq, _, dq = q.shape
    n_groups = hq // g_q
    g_out = g_q // 2 if mode == "diff" else g_q
    nq = tq_rows // tq
    tk = tk_rows
    in_specs = [
        pl.BlockSpec((g_q, tq, dq), lambda b, g, qi: (g, (q_off + b * tq_rows) // tq + qi, 0)),
        pl.BlockSpec((g_k, tk, dq), lambda b, g, qi: (g, (k_off + b * tk_rows) // tk, 0)),
        pl.BlockSpec((g_v, tk, dv), lambda b, g, qi: (g, (k_off + b * tk_rows) // tk, 0)),
    ]
    args = [q, k, v]
    if mode == "sink":
        in_specs.append(pl.BlockSpec(memory_space=pltpu.SMEM))
        args.append(extras[0])
    elif mode == "diff":
        in_specs.append(pl.BlockSpec((4, A_QK), lambda b, g, qi: (0, 0)))
        in_specs.append(pl.BlockSpec((1, dv), lambda b, g, qi: (0, 0)))
        args += list(extras)
    aliases = {}
    if o_prev is not None:
        in_specs.append(pl.BlockSpec(memory_space=pl.ANY))
        aliases = {len(args): 0}
        args.append(o_prev)
    return pl.pallas_call(
        functools.partial(_attn_kernel, g_q=g_q, g_k=g_k, g_v=g_v, dv=dv, mode=mode, lam_init=lam_init,
                          has_prev=o_prev is not None),
        grid=(batch, n_groups, nq),
        in_specs=in_specs,
        out_specs=pl.BlockSpec((tq, g_out * dv), lambda b, g, qi: ((o_off + b * tq_rows) // tq + qi, g)),
        out_shape=jax.ShapeDtypeStruct((n_out, n_groups * g_out * dv), F32),
        input_output_aliases=aliases,
        compiler_params=_params(("arbitrary", "arbitrary", "arbitrary")),
        name="attention_" + mode,
    )(*args)


def _win_kernel(q_ref, k_ref, v_ref, sink_ref, prev_ref, o_ref, *, g_q, dv, tq, seq, past, band):
    del prev_ref
    grp = pl.program_id(1)
    qi = pl.program_id(2)
    start = jnp.clip(qi * tq - WINDOW, 0, seq - band)
    row0 = pl.multiple_of(past + start, WINDOW)
    k_ctx = k_ref[0:past, :]
    v_ctx = v_ref[0:past, :]
    k_band = k_ref[pl.ds(row0, band), :]
    v_band = v_ref[pl.ds(row0, band), :]
    qpos = qi * tq + lax.broadcasted_iota(jnp.int32, (tq, band), 0)
    kpos = start + lax.broadcasted_iota(jnp.int32, (tq, band), 1)
    valid = jnp.abs(kpos - qpos) <= WINDOW
    for g in range(g_q):
        q = q_ref[g]
        s_ctx = _nt_dot(q, k_ctx)
        s_band = jnp.where(valid, _nt_dot(q, k_band), NEG_INF)
        sk = sink_ref[grp * g_q + g]
        m = jnp.maximum(jnp.maximum(jnp.max(s_ctx, axis=-1, keepdims=True),
                                    jnp.max(s_band, axis=-1, keepdims=True)), sk)
        p_ctx = jnp.exp(s_ctx - m)
        p_band = jnp.exp(s_band - m)
        den = (jnp.sum(p_ctx, axis=-1, keepdims=True) + jnp.sum(p_band, axis=-1, keepdims=True)
               + jnp.exp(sk - m))
        num = _dot(p_ctx.astype(BF16), v_ctx) + _dot(p_band.astype(BF16), v_band)
        o_ref[:, g * dv:(g + 1) * dv] = num * (1.0 / den)


def _window_attention(q, k, v, sink, o_prev, *, q_off, o_off, tq=256):
    hq, _, d = q.shape
    hk = k.shape[0]
    g_q = hq // hk
    band = tq + 2 * WINDOW
    nq = DEC_SEQ // tq
    return pl.pallas_call(
        functools.partial(_win_kernel, g_q=g_q, dv=d, tq=tq, seq=DEC_SEQ, past=PAST_LEN, band=band),
        grid=(DEC_BATCH, hk, nq),
        in_specs=[
            pl.BlockSpec((g_q, tq, d), lambda b, g, qi: (g, (q_off + b * DEC_SEQ) // tq + qi, 0)),
            pl.BlockSpec((None, None, KV_LAT, d), lambda b, g, qi: (g, b, 0, 0)),
            pl.BlockSpec((None, None, KV_LAT, d), lambda b, g, qi: (g, b, 0, 0)),
            pl.BlockSpec(memory_space=pltpu.SMEM),
            pl.BlockSpec(memory_space=pl.ANY),
        ],
        out_specs=pl.BlockSpec((tq, g_q * d), lambda b, g, qi: ((o_off + b * DEC_SEQ) // tq + qi, g)),
        out_shape=jax.ShapeDtypeStruct(o_prev.shape, F32),
        input_output_aliases={4: 0},
        compiler_params=_params(("arbitrary", "arbitrary", "arbitrary")),
        name="attention_window",
    )(q, k, v, sink, o_prev)


def _merge_kernel(x_ref, h_ref, oa_ref, ob_ref, oc_ref, od_ref, mod_ref, wg_ref, wb_ref, wo_ref, n2g_ref,
                  x1_ref, h2_ref):
    hb = h_ref[...]
    outs = (oa_ref, ob_ref, oc_ref, od_ref)
    acc = None
    for i in range(N_BRANCH):
        gt = jax.nn.sigmoid(_dot(hb, wg_ref[:, i * D_MODEL:(i + 1) * D_MODEL]))
        br = _dot(outs[i][...].astype(BF16), wb_ref[i])
        acc = gt * br if acc is None else acc + gt * br
    y = _dot(acc.astype(BF16), wo_ref[...])
    x1 = x_ref[...] + mod_ref[2:3, :] * y
    x1_ref[...] = x1
    h2 = _row_rms(x1, n2g_ref[...]) * (1.0 + mod_ref[4:5, :]) + mod_ref[3:4, :]
    h2_ref[...] = h2.astype(BF16)


def _merge_stage(x, h, oa, ob, oc, od, mod3, wg, wb, wo, n2g, *, tm=256):
    n = x.shape[0]

    def tok(i):
        return (i, 0)

    return pl.pallas_call(
        _merge_kernel,
        grid=(n // tm,),
        in_specs=[
            pl.BlockSpec((tm, D_MODEL), tok),
            pl.BlockSpec((tm, D_MODEL), tok),
            pl.BlockSpec((tm, BRANCH_W), tok),
            pl.BlockSpec((tm, BRANCH_W), tok),
            pl.BlockSpec((tm, BRANCH_W), tok),
            pl.BlockSpec((tm, BRANCH_W), tok),
            pl.BlockSpec((None, 6, D_MODEL), lambda i: ((i * tm) // DEC_SEQ, 0, 0)),
            pl.BlockSpec((D_MODEL, N_BRANCH * D_MODEL), lambda i: (0, 0)),
            pl.BlockSpec((N_BRANCH, BRANCH_W, D_MODEL), lambda i: (0, 0, 0)),
            pl.BlockSpec((D_MODEL, D_MODEL), lambda i: (0, 0)),
            pl.BlockSpec((1, D_MODEL), lambda i: (0, 0)),
        ],
        out_specs=[pl.BlockSpec((tm, D_MODEL), tok), pl.BlockSpec((tm, D_MODEL), tok)],
        out_shape=[jax.ShapeDtypeStruct((n, D_MODEL), F32), jax.ShapeDtypeStruct((n, D_MODEL), BF16)],
        compiler_params=_params(("arbitrary",)),
        name="merge_stage",
    )(x, h, oa, ob, oc, od, mod3, wg, wb, wo, n2g)


def _oddeven_mergesort_pairs(n):
    pairs = []

    def merge(lo, m, r):
        step = 2 * r
        if step < m:
            merge(lo, m, step)
            merge(lo + r, m, step)
            for i in range(lo + r, lo + m - r, step):
                pairs.append((i, i + r))
        else:
            pairs.append((lo, lo + r))

    def sort(lo, m):
        if m > 1:
            sort(lo, m // 2)
            sort(lo + m // 2, m // 2)
            merge(lo, m, 1)

    sort(0, n)
    return pairs


_SORT_PAIRS = _oddeven_mergesort_pairs(PEER_TOPK)


def _exchange(xs, i, j):
    xs[i], xs[j] = jnp.maximum(xs[i], xs[j]), jnp.minimum(xs[i], xs[j])


def _top16_of_groups(groups):
    n = len(groups)
    xs = list(groups)
    for i, j in _SORT_PAIRS:
        _exchange(xs, i, j)
    for shift in (4, 2, 1):
        xs = [jnp.maximum(xs[k], pltpu.roll(xs[n - 1 - k], shift, 0)) for k in range(n)]
        s = n // 2
        while s >= 1:
            for i in range(n):
                if not i & s:
                    _exchange(xs, i, i + s)
            s //= 2
    return xs


def _sublane_sum(x):
    for shift in (4, 2, 1):
        x = x + pltpu.roll(x, shift, 0)
    return x


def _sublane_spread(vals):
    row = lax.broadcasted_iota(jnp.int32, vals[0].shape, 0)
    out = vals[-1]
    for j in range(len(vals) - 2, -1, -1):
        out = jnp.where(row == j, vals[j], out)
    return out


def _peer_select(g1, g2):
    k = PEER_TOPK
    half = k // 2
    v1 = _top16_of_groups(g1)
    v2 = _top16_of_groups(g2)
    v2lo = _sublane_spread(v2[:half])
    v2hi = _sublane_spread(v2[half:])
    v1hi = _sublane_spread(v1[half:])
    cands = [v1[0] + v2lo, v1[0] + v2hi] + [v1[i] + v2lo for i in range(1, half)] + [v1hi + v2[0]]
    pad = jnp.full(cands[0].shape, EXHAUSTED, F32)
    top = _top16_of_groups(cands + [pad] * (k - len(cands)))
    tau = top[k - 1]
    z = None
    for c in cands:
        e = jnp.where(c >= tau, jnp.exp(c - top[0]), 0.0)
        z = e if z is None else z + e
    z = _sublane_sum(z)
    thr = [jnp.full(g1[0].shape, -1.0, F32) for _ in g1]
    q2 = [jnp.full(g2[0].shape, float(k), F32) for _ in g2]
    for a in range(k - 1, -1, -1):
        cnt = _sublane_sum(jnp.where(v1[a] + v2lo >= tau, 1.0, 0.0) + jnp.where(v1[a] + v2hi >= tau, 1.0, 0.0))
        thr = [jnp.where(x == v1[a], cnt - 1.0, t) for x, t in zip(g1, thr)]
        q2 = [jnp.where(x == v2[a], float(a), q) for x, q in zip(g2, q2)]
    p1 = [jnp.exp(x - v1[0]) for x in g1]
    scale = 0.5 / z
    p2 = [jnp.exp(x - v2[0]) * scale for x in g2]
    return thr, p1, q2, p2


def _peer_query_kernel(h2_ref, wqt_ref, sk_ref, thr_ref, p1_ref, q2_ref, p2_ref):
    qt = _nt_dot(wqt_ref[...], h2_ref[...]).astype(BF16)
    n_lt = qt.shape[1] // 128
    n_grp = PEER_KEYS // 8
    for hd in range(PEER_HEADS):
        s1 = _dot(sk_ref[2 * hd], qt[(2 * hd) * PEER_HALF:(2 * hd + 1) * PEER_HALF, :])
        s2 = _dot(sk_ref[2 * hd + 1], qt[(2 * hd + 1) * PEER_HALF:(2 * hd + 2) * PEER_HALF, :])
        for lt in range(n_lt):
            lanes = slice(lt * 128, (lt + 1) * 128)
            dst = (lt + 1) % n_lt
            rot = slice(dst * 128, (dst + 1) * 128)
            g1 = [s1[8 * i:8 * i + 8, lanes] for i in range(n_grp)]
            g2 = [s2[8 * i:8 * i + 8, lanes] for i in range(n_grp)]
            thr, p1, q2, p2 = _peer_select(g1, g2)
            thr_ref[hd, :, lanes] = jnp.concatenate(thr, axis=0)
            p1_ref[hd, :, lanes] = jnp.concatenate(p1, axis=0)
            q2_ref[hd, :, lanes] = jnp.concatenate(q2, axis=0).astype(BF16)
            p2_ref[hd, :, rot] = jnp.concatenate(p2, axis=0).astype(BF16)


def _peer_query(h2, wqt, sk, *, tm=512):
    n = h2.shape[0]
    big = jax.ShapeDtypeStruct((PEER_HEADS, PEER_KEYS, n), F32)
    half = jax.ShapeDtypeStruct((PEER_HEADS, PEER_KEYS, n), BF16)
    bspec = pl.BlockSpec((PEER_HEADS, PEER_KEYS, tm), lambda i: (0, 0, i))
    return pl.pallas_call(
        _peer_query_kernel,
        grid=(n // tm,),
        in_specs=[
            pl.BlockSpec((tm, D_MODEL), lambda i: (i, 0)),
            pl.BlockSpec((PEER_HEADS * PEER_QDIM, D_MODEL), lambda i: (0, 0)),
            pl.BlockSpec((2 * PEER_HEADS, PEER_KEYS, PEER_HALF), lambda i: (0, 0, 0)),
        ],
        out_specs=[bspec, bspec, bspec, bspec],
        out_shape=[big, big, half, half],
        compiler_params=_params(("arbitrary",)),
        name="peer_query",
    )(h2, wqt, sk)


def _peer_dense_kernel(h2_ref, x1_ref, mod_ref, u_ref, vt_ref, thr_ref, p1_ref, q2_ref, p2_ref,
                       o_ref, acc_ref, act_ref, p_ref, q2s_ref, p2s_ref, *, e1_per_step):
    step = pl.program_id(1)
    tm = act_ref.shape[1]
    n_lt = tm // 128

    def lane_tile(lt, shift=0):
        lt = (lt + shift) % n_lt
        return slice(lt * 128, (lt + 1) * 128)

    @pl.when(step == 0)
    def _init():
        acc_ref[...] = jnp.zeros_like(acc_ref)
        q2s_ref[...] = q2_ref[...]
        p2s_ref[...] = p2_ref[...]

    a = _nt_dot(u_ref[...], h2_ref[...])
    act = (a + a * lax.erf(a * (2.0 ** -0.5))).astype(BF16)
    for lt in range(n_lt):
        act_ref[:, lane_tile(lt, 2)] = act[:, lane_tile(lt)]
    rows = 16
    for c in range(e1_per_step):
        for lt in range(n_lt):
            lanes = lane_tile(lt)
            n_r = PEER_KEYS // rows
            g = [None] * n_r
            for hd in range(PEER_HEADS):
                thr_b = jnp.broadcast_to(thr_ref[hd, c:c + 1, lanes], (rows, 128)).astype(BF16)
                p1_b = jnp.broadcast_to(p1_ref[hd, c:c + 1, lanes], (rows, 128)).astype(BF16)
                for r in range(n_r):
                    rs = slice(r * rows, (r + 1) * rows)
                    w = jnp.where(q2s_ref[hd, rs, lanes] <= thr_b,
                                  p2s_ref[hd, rs, lane_tile(lt, 1)] * p1_b, jnp.zeros((), BF16))
                    g[r] = w if g[r] is None else g[r] + w
            for r in range(n_r):
                es = slice(c * PEER_KEYS + r * rows, c * PEER_KEYS + (r + 1) * rows)
                p_ref[es, lanes] = g[r] * act_ref[es, lane_tile(lt, 2)]
    acc_ref[...] += _dot(vt_ref[...], p_ref[...])

    @pl.when(step == pl.num_programs(1) - 1)
    def _fin():
        o_ref[...] = x1_ref[...] + mod_ref[5:6, :] * acc_ref[...].T


def _peer_dense(h2, x1, mod3, u, vt, thr, p1, q2, p2, *, tm=512, e1_per_step=8):
    n = h2.shape[0]
    te = e1_per_step * PEER_KEYS
    tok = lambda i, j: (i, 0)
    big = pl.BlockSpec((PEER_HEADS, PEER_KEYS, tm), lambda i, j: (0, 0, i))
    rows = pl.BlockSpec((PEER_HEADS, e1_per_step, tm), lambda i, j: (0, j, i))
    return pl.pallas_call(
        functools.partial(_peer_dense_kernel, e1_per_step=e1_per_step),
        grid=(n // tm, PEER_EXPERTS // te),
        in_specs=[
            pl.BlockSpec((tm, D_MODEL), tok),
            pl.BlockSpec((tm, D_MODEL), tok),
            pl.BlockSpec((None, 6, D_MODEL), lambda i, j: ((i * tm) // DEC_SEQ, 0, 0)),
            pl.BlockSpec((te, D_MODEL), lambda i, j: (j, 0)),
            pl.BlockSpec((D_MODEL, te), lambda i, j: (0, j)),
            rows, rows, big, big,
        ],
        out_specs=pl.BlockSpec((tm, D_MODEL), tok),
        out_shape=jax.ShapeDtypeStruct((n, D_MODEL), F32),
        scratch_shapes=[pltpu.VMEM((D_MODEL, tm), F32), pltpu.VMEM((te, tm), BF16), pltpu.VMEM((te, tm), BF16),
                        pltpu.VMEM((PEER_HEADS, PEER_KEYS, tm), BF16), pltpu.VMEM((PEER_HEADS, PEER_KEYS, tm), BF16)],
        compiler_params=_params(("arbitrary", "arbitrary")),
        name="peer_dense",
    )(h2, x1, mod3, u, vt, thr, p1, q2, p2)


def _block_ones(width, d):
    idx = jnp.arange(width) // d
    return (idx[:, None] == idx[None, :]).astype(BF16)


def _rope_angles(d):
    q4 = d // 4
    inv = ROPE_BASE ** (-jnp.arange(q4, dtype=F32) / q4)
    t = jnp.arange(DEC_SEQ)
    ar = (t // GRID_W).astype(F32)[:, None] * inv
    ac = (t % GRID_W).astype(F32)[:, None] * inv
    return jnp.concatenate([ar, ar, ac, ac], axis=-1)


def _rope_table(d, group, n_groups):
    ang = _rope_angles(d)
    q4 = d // 4
    first = (jnp.arange(d) % (2 * q4)) < q4
    cos = jnp.cos(ang)
    sin = jnp.sin(ang)
    sa = jnp.where(first[None, :], -sin, 0.0)
    sb = jnp.where(first[None, :], 0.0, sin)
    pad = group - d
    cos = jnp.pad(cos, ((0, 0), (pad, 0)), constant_values=1.0)
    sa = jnp.pad(sa, ((0, 0), (pad, 0)))
    sb = jnp.pad(sb, ((0, 0), (pad, 0)))
    return jnp.stack([jnp.tile(cos, (1, n_groups)), jnp.tile(sa, (1, n_groups)), jnp.tile(sb, (1, n_groups))])


def _rot_matrix(d):
    q4 = d // 4
    i = jnp.arange(d)
    first = (i % (2 * q4)) < q4
    src = jnp.where(first, i + q4, i - q4)
    sign = jnp.where(first, -1.0, 1.0)
    return (jnp.zeros((d, d), F32).at[src, i].set(sign)).astype(BF16)


def _pad_row(v, width):
    return jnp.pad(v, (0, width - v.shape[0]))


def kernel(x_prompt, x_sample, c, cache_a_k, cache_a_v, cache_b_k, cache_b_v, cache_c_k, cache_c_v, cache_d_ckv, cache_d_krope, c_ctx, ada_w, ada_b, norm1_g, norm2_g, w_in, a_lam, a_qk_g, a_subln_g, b_sink, b_qk_g, c_qk_g, d_qnorm_g, d_kvnorm_g, w_d_qb, w_d_kvb, d_qk_g, w_branch, w_out, peer_wq, peer_subkeys, peer_u, peer_v):
    x = jnp.concatenate([x_prompt.reshape(N_CTX, D_MODEL), x_sample.reshape(N_LAT, D_MODEL)], axis=0)
    cvec = jnp.concatenate([c_ctx[None, :], c, jnp.zeros((N_MODROWS - 1 - DEC_BATCH, D_MODEL), F32)], axis=0)
    mod_all = _modulation(cvec, ada_w, ada_b)

    e32 = _block_ones(256, A_QK)
    e64 = _block_ones(256, HEAD_DIM)
    e96 = _block_ones(GAIN_W, D_QK)
    tab_a = _rope_table(A_QK, A_QK, 2 * A_HEADS)
    tab_b = _rope_table(HEAD_DIM, HEAD_DIM, B_HEADS)
    tab_d = _rope_table(D_ROPE, D_QK, D_HEADS)
    ang_r = _rope_angles(D_ROPE)
    tab_r = jnp.stack([jnp.cos(ang_r), jnp.sin(ang_r)])
    rot_r = _rot_matrix(D_ROPE)

    states = [[] for _ in range(8)]
    for l in range(DEPTH):
        lam_init = 0.8 - 0.6 * math.exp(-0.3 * l)
        mod3 = mod_all[l, :3].reshape(3, 6, D_MODEL)
        ws = jnp.pad(w_in[l, :, :SMALL_COLS], ((0, 0), (0, SMALL_PAD - SMALL_COLS))).astype(BF16)
        wg = w_in[l, :, SMALL_COLS:].astype(BF16)
        gains = jnp.stack([
            _pad_row(jnp.tile(a_qk_g[l, 0], 2 * A_HEADS), GAIN_W),
            _pad_row(jnp.tile(a_qk_g[l, 1], 2 * A_HEADS), GAIN_W),
            _pad_row(jnp.tile(b_qk_g[l, 0], B_HEADS), GAIN_W),
            _pad_row(jnp.tile(b_qk_g[l, 1], B_KV), GAIN_W),
            _pad_row(jnp.tile(c_qk_g[l, 0], C_HEADS), GAIN_W),
            _pad_row(jnp.tile(c_qk_g[l, 1], C_KV), GAIN_W),
            _pad_row(d_qnorm_g[l], GAIN_W),
            _pad_row(d_kvnorm_g[l], GAIN_W),
            jnp.tile(d_qk_g[l, 0], D_HEADS),
        ] + [jnp.zeros((GAIN_W,), F32)] * (GAIN_ROWS - 9))

        (h, aq, ak, av, bq, bk, bv, cq, ck, cv, dq,
         akt, avt, bkt, bvt, ckt, cvt, dckv, dkr) = _input_stage(
            x, mod3, norm1_g[l][None, :], ws, gains, e32, e64, e96, w_d_qb[l].astype(BF16), tab_a, tab_b, tab_d)

        for lst, s_ in zip(states, (
                akt[:N_CTX].reshape(BATCH, SEQ, A_HEADS, 2, A_QK), avt[:N_CTX].reshape(BATCH, SEQ, A_HEADS, A_V),
                bkt[:N_CTX].reshape(BATCH, SEQ, B_KV, HEAD_DIM), bvt[:N_CTX].reshape(BATCH, SEQ, B_KV, HEAD_DIM),
                ckt[:N_CTX].reshape(BATCH, SEQ, C_KV, HEAD_DIM), cvt[:N_CTX].reshape(BATCH, SEQ, C_KV, HEAD_DIM),
                dckv[:N_CTX].reshape(BATCH, SEQ, D_KV_LORA), dkr[:N_CTX].reshape(BATCH, SEQ, D_ROPE))):
            lst.append(s_)

        ckv_rows = jnp.concatenate([t for b in range(DEC_BATCH) for t in (
            cache_d_ckv[b, l], dckv[N_CTX + b * DEC_SEQ:N_CTX + (b + 1) * DEC_SEQ])] + [dckv[:N_CTX]], axis=0)
        kr_rows = jnp.concatenate([t for b in range(DEC_BATCH) for t in (
            cache_d_krope[b, l], dkr[N_CTX + b * DEC_SEQ:N_CTX + (b + 1) * DEC_SEQ])] + [dkr[:N_CTX]], axis=0)
        kd, vd = _mla_expand(ckv_rows, kr_rows, w_d_kvb[l].astype(BF16), d_qk_g[l, 1][None, :], rot_r, tab_r)

        def lat_kv(new, cache, nh, d):
            cache = jnp.moveaxis(cache.reshape(DEC_BATCH, PAST_LEN, nh, d), 2, 0).astype(BF16)
            new = new[:, N_CTX:].reshape(nh, DEC_BATCH, DEC_SEQ, d)
            return jnp.concatenate([cache, new], axis=2).reshape(nh, DEC_BATCH * KV_LAT, d)

        ak_l = lat_kv(ak, cache_a_k[:, l], 2 * A_HEADS, A_QK)
        av_l = lat_kv(av, cache_a_v[:, l], A_HEADS, A_V)
        bk_l = lat_kv(bk, cache_b_k[:, l], B_KV, HEAD_DIM).reshape(B_KV, DEC_BATCH, KV_LAT, HEAD_DIM)
        bv_l = lat_kv(bv, cache_b_v[:, l], B_KV, HEAD_DIM).reshape(B_KV, DEC_BATCH, KV_LAT, HEAD_DIM)
        ck_l = lat_kv(ck, cache_c_k[:, l], C_KV, HEAD_DIM)
        cv_l = lat_kv(cv, cache_c_v[:, l], C_KV, HEAD_DIM)

        ctx = dict(batch=BATCH, tq_rows=SEQ, tk_rows=SEQ, q_off=0, o_off=0, n_out=N_TOK,
                   lam_init=lam_init, tq=SEQ)
        lat = dict(batch=DEC_BATCH, tq_rows=DEC_SEQ, tk_rows=KV_LAT, q_off=N_CTX, k_off=0, o_off=N_CTX, n_out=N_TOK,
                   lam_init=lam_init, tq=256)
        a_extra = (a_lam[l], a_subln_g[l][None, :])
        oa = _attention(aq, ak, av, a_extra, None, k_off=0, g_q=4, g_k=4, g_v=2, dv=A_V, mode="diff", **ctx)
        oa = _attention(aq, ak_l, av_l, a_extra, oa, g_q=4, g_k=4, g_v=2, dv=A_V, mode="diff", **lat)
        ob = _attention(bq, bk, bv, (b_sink[l],), None, k_off=0, g_q=2, g_k=1, g_v=1, dv=HEAD_DIM, mode="sink",
                        **ctx)
        ob = _window_attention(bq, bk_l, bv_l, b_sink[l], ob, q_off=N_CTX, o_off=N_CTX)
        oc = _attention(cq, ck, cv, (), None, k_off=0, g_q=2, g_k=1, g_v=1, dv=HEAD_DIM, mode="plain", **ctx)
        oc = _attention(cq, ck_l, cv_l, (), oc, g_q=2, g_k=1, g_v=1, dv=HEAD_DIM, mode="plain", **lat)
        od = _attention(dq, kd, vd, (), None, k_off=DEC_BATCH * KV_LAT, g_q=2, g_k=2, g_v=2, dv=D_V,
                        mode="plain", **ctx)
        od = _attention(dq, kd, vd, (), od, g_q=2, g_k=2, g_v=2, dv=D_V, mode="plain", **lat)

        x1, h2 = _merge_stage(x, h, oa, ob, oc, od, mod3, wg, w_branch[l].astype(BF16), w_out[l].astype(BF16),
                              norm2_g[l][None, :])

        thr, p1, q2, p2 = _peer_query(
            h2, peer_wq[l].T.astype(BF16),
            peer_subkeys[l].reshape(2 * PEER_HEADS, PEER_KEYS, PEER_HALF).astype(BF16))
        x = _peer_dense(h2, x1, mod3, peer_u[l].astype(BF16), peer_v[l].T.astype(BF16), thr, p1, q2, p2)

    y_prompt = x[:N_CTX].reshape(BATCH, SEQ, D_MODEL)
    y_sample = x[N_CTX:].reshape(DEC_BATCH, DEC_SEQ, D_MODEL)
    return (y_prompt, y_sample) + tuple(jnp.stack(s_, axis=1) for s_ in states)
```

```python
import functools
import math

import jax
import jax.numpy as jnp
from jax import lax
from jax.experimental import pallas as pl
from jax.experimental.pallas import tpu as pltpu

F32 = jnp.float32
BF16 = jnp.bfloat16

D_MODEL = 1024
BATCH = 16
SEQ = 256
DEPTH = 2
DEC_BATCH = 2
DEC_SEQ = 4096
PAST_LEN = 512
GRID_W = 64
Q_BLOCK = 128
HEAD_DIM = 64
ROPE_BASE = 10000.0
EPS = 1e-6
NEG_INF = -1e30
A_HEADS = 4
A_QK = 32
A_V = 64
B_HEADS = 4
B_KV = 2
WINDOW = 128
C_HEADS = 4
C_KV = 2
D_HEADS = 4
D_Q_LORA = 256
D_KV_LORA = 128
D_NOPE = 64
D_ROPE = 32
D_V = 64
D_QK = D_NOPE + D_ROPE
N_BRANCH = 4
BRANCH_W = 256
PEER_HEADS = 8
PEER_KEYS = 128
PEER_EXPERTS = PEER_KEYS * PEER_KEYS
PEER_QDIM = 256
PEER_HALF = PEER_QDIM // 2
PEER_TOPK = 16

N_CTX = BATCH * SEQ
N_LAT = DEC_BATCH * DEC_SEQ
N_TOK = N_CTX + N_LAT
KV_LAT = PAST_LEN + DEC_SEQ
N_MODROWS = 8
SMALL_COLS = 2208
SMALL_PAD = 2304

VMEM_LIMIT = 56 * 1024 * 1024
EXHAUSTED = -3.0e38


def _params(sem, flags=None):
    return pltpu.CompilerParams(dimension_semantics=sem, vmem_limit_bytes=VMEM_LIMIT, flags=flags)


def _nt_dot(a, b):
    return lax.dot_general(a, b, (((1,), (1,)), ((), ())), preferred_element_type=F32)


def _dot(a, b):
    return jnp.dot(a, b, preferred_element_type=F32)


def _mod_kernel(c_ref, w_ref, b_ref, o_ref):
    c = c_ref[...]
    s = c * jax.nn.sigmoid(c)
    o_ref[...] = jnp.dot(s, w_ref[...], preferred_element_type=F32, precision=lax.Precision.HIGHEST) + b_ref[...]


def _modulation(cvec, ada_w, ada_b):
    depth = ada_w.shape[0]
    ncol = ada_w.shape[2] // D_MODEL
    return pl.pallas_call(
        _mod_kernel,
        grid=(depth, ncol),
        in_specs=[
            pl.BlockSpec((N_MODROWS, D_MODEL), lambda l, j: (0, 0)),
            pl.BlockSpec((None, D_MODEL, D_MODEL), lambda l, j: (l, 0, j)),
            pl.BlockSpec((None, 1, D_MODEL), lambda l, j: (l, 0, j)),
        ],
        out_specs=pl.BlockSpec((None, N_MODROWS, D_MODEL), lambda l, j: (l, 0, j)),
        out_shape=jax.ShapeDtypeStruct((depth, N_MODROWS, ada_w.shape[2]), F32),
        compiler_params=_params(("arbitrary", "arbitrary")),
        name="modulation",
    )(cvec, ada_w, ada_b.reshape(depth, 1, -1))


def _group_rms(x, ones_bd, gain, d):
    sq = x * x
    hi = sq.astype(BF16)
    lo = (sq - hi.astype(F32)).astype(BF16)
    ss = _dot(hi, ones_bd) + _dot(lo, ones_bd)
    return x * lax.rsqrt(ss * (1.0 / d) + EPS) * gain


def _row_rms(x, gain):
    ms = jnp.mean(x * x, axis=-1, keepdims=True)
    return x * lax.rsqrt(ms + EPS) * gain


def _rope(x, tab_ref, q4):
    w = x.shape[-1]
    return x * tab_ref[0] + pltpu.roll(x, w - q4, 1) * tab_ref[1] + pltpu.roll(x, q4, 1) * tab_ref[2]


GAIN_ROWS = 16
GAIN_W = D_HEADS * D_QK


def _in_kernel(x_ref, mod_ref, n1g_ref, ws_ref, gains_ref, e32_ref, e64_ref, e96_ref, wdq_ref,
               ta_ref, tb_ref, td_ref,
               h_ref, aq_ref, ak_ref, av_ref, bq_ref, bk_ref, bv_ref, cq_ref, ck_ref, cv_ref, dq_ref,
               akt_ref, avt_ref, bkt_ref, bvt_ref, ckt_ref, cvt_ref, dckv_ref, dkr_ref, *, tm, n_ctx):
    is_lat = pl.program_id(0) * tm >= n_ctx
    x = x_ref[...]
    h = _row_rms(x, n1g_ref[...]) * (1.0 + mod_ref[1:2, :]) + mod_ref[0:1, :]
    hb = h.astype(BF16)
    h_ref[...] = hb
    p = _dot(hb, ws_ref[...])

    def gain(r, w):
        return gains_ref[r:r + 1, 0:w]

    def heads_out(ref, val, nh, d):
        for i in range(nh):
            ref[i] = val[:, i * d:(i + 1) * d].astype(BF16)

    def maybe_rope(val, tab_ref, q4):
        return jnp.where(is_lat, _rope(val, tab_ref, q4), val)

    aq = maybe_rope(_group_rms(p[:, 0:256], e32_ref[...], gain(0, 256), A_QK), ta_ref, A_QK // 4)
    ak = maybe_rope(_group_rms(p[:, 256:512], e32_ref[...], gain(1, 256), A_QK), ta_ref, A_QK // 4)
    av = p[:, 512:768]
    heads_out(aq_ref, aq * (A_QK ** -0.5), 2 * A_HEADS, A_QK)
    heads_out(ak_ref, ak, 2 * A_HEADS, A_QK)
    heads_out(av_ref, av, A_HEADS, A_V)
    akt_ref[...] = ak
    avt_ref[...] = av
    bq = maybe_rope(_group_rms(p[:, 768:1024], e64_ref[...], gain(2, 256), HEAD_DIM), tb_ref, HEAD_DIM // 4)
    bk = _group_rms(p[:, 1024:1152], e64_ref[0:128, 0:128], gain(3, 128), HEAD_DIM)
    bk = jnp.where(is_lat, bk * tb_ref[0, :, 0:128] + pltpu.roll(bk, 128 - 16, 1) * tb_ref[1, :, 0:128]
                   + pltpu.roll(bk, 16, 1) * tb_ref[2, :, 0:128], bk)
    bv = p[:, 1152:1280]
    heads_out(bq_ref, bq * (HEAD_DIM ** -0.5), B_HEADS, HEAD_DIM)
    heads_out(bk_ref, bk, B_KV, HEAD_DIM)
    heads_out(bv_ref, bv, B_KV, HEAD_DIM)
    bkt_ref[...] = bk
    bvt_ref[...] = bv
    cq = maybe_rope(_group_rms(p[:, 1280:1536], e64_ref[...], gain(4, 256), HEAD_DIM), tb_ref, HEAD_DIM // 4)
    ck = _group_rms(p[:, 1536:1664], e64_ref[0:128, 0:128], gain(5, 128), HEAD_DIM)
    ck = jnp.where(is_lat, ck * tb_ref[0, :, 0:128] + pltpu.roll(ck, 128 - 16, 1) * tb_ref[1, :, 0:128]
                   + pltpu.roll(ck, 16, 1) * tb_ref[2, :, 0:128], ck)
    cv = p[:, 1664:1792]
    heads_out(cq_ref, cq * (HEAD_DIM ** -0.5), C_HEADS, HEAD_DIM)
    heads_out(ck_ref, ck, C_KV, HEAD_DIM)
    heads_out(cv_ref, cv, C_KV, HEAD_DIM)
    ckt_ref[...] = ck
    cvt_ref[...] = cv
    dqa = _row_rms(p[:, 1792:2048], gain(6, 256))
    dq = _dot(dqa.astype(BF16), wdq_ref[...])
    dq = maybe_rope(_group_rms(dq, e96_ref[...], gain(8, GAIN_W), D_QK), td_ref, D_ROPE // 4)
    heads_out(dq_ref, dq * (D_QK ** -0.5), D_HEADS, D_QK)
    dckv_ref[...] = _row_rms(p[:, 2048:2176], gain(7, 128))
    dkr_ref[...] = p[:, 2176:2208]


def _input_stage(x, mod3, n1g, ws, gains, e32, e64, e96, wdq, tab_a, tab_b, tab_d, *, tm=256):
    n = x.shape[0]
    nt = n // tm
    lat0 = N_CTX // tm
    pos_blocks = DEC_SEQ // tm

    def tok(i):
        return (i, 0)

    def pos(i):
        return (0, jnp.maximum(i - lat0, 0) % pos_blocks, 0)

    def const2(i):
        return (0, 0)

    def hm(i):
        return (0, i, 0)

    def hspec(nh, d):
        return pl.BlockSpec((nh, tm, d), hm)

    def hshape(nh, d):
        return jax.ShapeDtypeStruct((nh, n, d), BF16)

    def tspec(w):
        return pl.BlockSpec((tm, w), tok)

    def tshape(w):
        return jax.ShapeDtypeStruct((n, w), F32)

    in_specs = [
        pl.BlockSpec((tm, D_MODEL), tok),
        pl.BlockSpec((None, 6, D_MODEL), lambda i: ((i * tm) // DEC_SEQ, 0, 0)),
        pl.BlockSpec((1, D_MODEL), const2),
        pl.BlockSpec((D_MODEL, SMALL_PAD), const2),
        pl.BlockSpec((GAIN_ROWS, GAIN_W), const2),
        pl.BlockSpec((256, 256), const2),
        pl.BlockSpec((256, 256), const2),
        pl.BlockSpec((GAIN_W, GAIN_W), const2),
        pl.BlockSpec((D_Q_LORA, GAIN_W), const2),
        pl.BlockSpec((3, tm, 256), pos),
        pl.BlockSpec((3, tm, 256), pos),
        pl.BlockSpec((3, tm, GAIN_W), pos),
    ]
    out_specs = [
        pl.BlockSpec((tm, D_MODEL), tok),
        hspec(8, A_QK), hspec(8, A_QK), hspec(4, A_V),
        hspec(4, HEAD_DIM), hspec(2, HEAD_DIM), hspec(2, HEAD_DIM),
        hspec(4, HEAD_DIM), hspec(2, HEAD_DIM), hspec(2, HEAD_DIM),
        hspec(4, D_QK),
        tspec(256), tspec(256), tspec(128), tspec(128), tspec(128), tspec(128), tspec(128), tspec(D_ROPE),
    ]
    out_shape = [
        jax.ShapeDtypeStruct((n, D_MODEL), BF16),
        hshape(8, A_QK), hshape(8, A_QK), hshape(4, A_V),
        hshape(4, HEAD_DIM), hshape(2, HEAD_DIM), hshape(2, HEAD_DIM),
        hshape(4, HEAD_DIM), hshape(2, HEAD_DIM), hshape(2, HEAD_DIM),
        hshape(4, D_QK),
        tshape(256), tshape(256), tshape(128), tshape(128), tshape(128), tshape(128), tshape(128), tshape(D_ROPE),
    ]
    return pl.pallas_call(
        functools.partial(_in_kernel, tm=tm, n_ctx=N_CTX),
        grid=(nt,),
        in_specs=in_specs,
        out_specs=out_specs,
        out_shape=out_shape,
        compiler_params=_params(("arbitrary",)),
        name="input_stage",
    )(x, mod3, n1g, ws, gains, e32, e64, e96, wdq, tab_a, tab_b, tab_d)


def _mla_kernel(ckv_ref, kr_ref, wkvb_ref, g_ref, rot_ref, tab_ref, kd_ref, vd_ref, *, tm, lat_tiles, per_batch_tiles):
    i = pl.program_id(0)
    rotate = jnp.logical_and(i < lat_tiles, i % per_batch_tiles >= PAST_LEN // tm)
    kv = _dot(ckv_ref[...].astype(BF16), wkvb_ref[...])
    kr = kr_ref[...]
    ssr = jnp.sum(kr * kr, axis=-1, keepdims=True)
    z = kr * g_ref[:, D_NOPE:D_QK]
    zh = z.astype(BF16)
    zl = (z - zh.astype(F32)).astype(BF16)
    rot = _dot(zh, rot_ref[...]) + _dot(zl, rot_ref[...])
    rz = jnp.where(rotate, z * tab_ref[0] + rot * tab_ref[1], z)
    g_nope = g_ref[:, 0:D_NOPE]
    for hd in range(D_HEADS):
        kn = kv[:, hd * 128:hd * 128 + D_NOPE]
        r = lax.rsqrt((jnp.sum(kn * kn, axis=-1, keepdims=True) + ssr) * (1.0 / D_QK) + EPS)
        kd_ref[hd] = jnp.concatenate([kn * r * g_nope, rz * r], axis=-1).astype(BF16)
        vd_ref[hd] = kv[:, hd * 128 + D_NOPE:(hd + 1) * 128].astype(BF16)


def _mla_expand(ckv, kr, wkvb, g, rot, tab, *, tm=512):
    rows = ckv.shape[0]
    per_batch_tiles = KV_LAT // tm
    lat_tiles = DEC_BATCH * per_batch_tiles
    past_tiles = PAST_LEN // tm
    pos_blocks = DEC_SEQ // tm

    def pos(i):
        return (0, jnp.maximum(i % per_batch_tiles - past_tiles, 0) % pos_blocks, 0)

    return pl.pallas_call(
        functools.partial(_mla_kernel, tm=tm, lat_tiles=lat_tiles, per_batch_tiles=per_batch_tiles),
        grid=(rows // tm,),
        in_specs=[
            pl.BlockSpec((tm, D_KV_LORA), lambda i: (i, 0)),
            pl.BlockSpec((tm, D_ROPE), lambda i: (i, 0)),
            pl.BlockSpec((D_KV_LORA, D_HEADS * (D_NOPE + D_V)), lambda i: (0, 0)),
            pl.BlockSpec((1, D_QK), lambda i: (0, 0)),
            pl.BlockSpec((D_ROPE, D_ROPE), lambda i: (0, 0)),
            pl.BlockSpec((2, tm, D_ROPE), pos),
        ],
        out_specs=[
            pl.BlockSpec((D_HEADS, tm, D_QK), lambda i: (0, i, 0)),
            pl.BlockSpec((D_HEADS, tm, D_V), lambda i: (0, i, 0)),
        ],
        out_shape=[
            jax.ShapeDtypeStruct((D_HEADS, rows, D_QK), BF16),
            jax.ShapeDtypeStruct((D_HEADS, rows, D_V), BF16),
        ],
        compiler_params=_params(("arbitrary",)),
        name="mla_expand",
    )(ckv, kr, wkvb, g, rot, tab)


def _attn_kernel(*refs, g_q, g_k, g_v, dv, mode, lam_init, has_prev):
    refs = list(refs)
    q_ref, k_ref, v_ref = refs[:3]
    pos = 3
    sink_ref = lam_ref = sub_ref = None
    if mode == "sink":
        sink_ref = refs[pos]
        pos += 1
    elif mode == "diff":
        lam_ref, sub_ref = refs[pos], refs[pos + 1]
        pos += 2
    if has_prev:
        pos += 1
    o_ref = refs[pos]
    grp = pl.program_id(1)

    def probs(g):
        s = _nt_dot(q_ref[g], k_ref[g * g_k // g_q])
        m = jnp.max(s, axis=-1, keepdims=True)
        if mode == "sink":
            sk = sink_ref[grp * g_q + g]
            m = jnp.maximum(m, sk)
        p = jnp.exp(s - m)
        den = jnp.sum(p, axis=-1, keepdims=True)
        if mode == "sink":
            den = den + jnp.exp(sk - m)
        return p, den

    if mode == "diff":
        lp = lam_ref[...]
        lam = (jnp.exp(jnp.sum(lp[0:1] * lp[1:2], axis=-1, keepdims=True))
               - jnp.exp(jnp.sum(lp[2:3] * lp[3:4], axis=-1, keepdims=True)) + lam_init)
        for i in range(g_q // 2):
            p1, den1 = probs(2 * i)
            p2, den2 = probs(2 * i + 1)
            comb = p1 * (1.0 / den1) - p2 * (lam / den2)
            d = _dot(comb.astype(BF16), v_ref[i * g_v // (g_q // 2)])
            o_ref[:, i * dv:(i + 1) * dv] = _row_rms(d, sub_ref[...]) * (1.0 - lam_init)
    else:
        for g in range(g_q):
            p, den = probs(g)
            o_ref[:, g * dv:(g + 1) * dv] = _dot(p.astype(BF16), v_ref[g * g_v // g_q]) * (1.0 / den)


def _attention(q, k, v, extras, o_prev, *, batch, tq_rows, tk_rows, q_off, k_off, o_off, n_out,
               g_q, g_k, g_v, dv, mode, lam_init, tq):
    hq, _, dq = q.shape
    n_groups = hq // g_q
    g_out = g_q // 2 if mode == "diff" else g_q
    nq = tq_rows // tq
    tk = tk_rows
    in_specs = [
        pl.BlockSpec((g_q, tq, dq), lambda b, g, qi: (g, (q_off + b * tq_rows) // tq + qi, 0)),
        pl.BlockSpec((g_k, tk, dq), lambda b, g, qi: (g, (k_off + b * tk_rows) // tk, 0)),
        pl.BlockSpec((g_v, tk, dv), lambda b, g, qi: (g, (k_off + b * tk_rows) // tk, 0)),
    ]
    args = [q, k, v]
    if mode == "sink":
        in_specs.append(pl.BlockSpec(memory_space=pltpu.SMEM))
        args.append(extras[0])
    elif mode == "diff":
        in_specs.append(pl.BlockSpec((4, A_QK), lambda b, g, qi: (0, 0)))
        in_specs.append(pl.BlockSpec((1, dv), lambda b, g, qi: (0, 0)))
        args += list(extras)
    aliases = {}
    if o_prev is not None:
        in_specs.append(pl.BlockSpec(memory_space=pl.ANY))
        aliases = {len(args): 0}
        args.append(o_prev)
    return pl.pallas_call(
        functools.partial(_attn_kernel, g_q=g_q, g_k=g_k, g_v=g_v, dv=dv, mode=mode, lam_init=lam_init,
                          has_prev=o_prev is not None),
        grid=(batch, n_groups, nq),
        in_specs=in_specs,
        out_specs=pl.BlockSpec((tq, g_out * dv), lambda b, g, qi: ((o_off + b * tq_rows) // tq + qi, g)),
        out_shape=jax.ShapeDtypeStruct((n_out, n_groups * g_out * dv), F32),
        input_output_aliases=aliases,
        compiler_params=_params(("arbitrary", "arbitrary", "arbitrary")),
        name="attention_" + mode,
    )(*args)


def _win_kernel(q_ref, k_ref, v_ref, sink_ref, prev_ref, o_ref, *, g_q, dv, tq, seq, past, band):
    del prev_ref
    grp = pl.program_id(1)
    qi = pl.program_id(2)
    start = jnp.clip(qi * tq - WINDOW, 0, seq - band)
    row0 = pl.multiple_of(past + start, WINDOW)
    k_ctx = k_ref[0:past, :]
    v_ctx = v_ref[0:past, :]
    k_band = k_ref[pl.ds(row0, band), :]
    v_band = v_ref[pl.ds(row0, band), :]
    qpos = qi * tq + lax.broadcasted_iota(jnp.int32, (tq, band), 0)
    kpos = start + lax.broadcasted_iota(jnp.int32, (tq, band), 1)
    valid = jnp.abs(kpos - qpos) <= WINDOW
    for g in range(g_q):
        q = q_ref[g]
        s_ctx = _nt_dot(q, k_ctx)
        s_band = jnp.where(valid, _nt_dot(q, k_band), NEG_INF)
        sk = sink_ref[grp * g_q + g]
        m = jnp.maximum(jnp.maximum(jnp.max(s_ctx, axis=-1, keepdims=True),
                                    jnp.max(s_band, axis=-1, keepdims=True)), sk)
        p_ctx = jnp.exp(s_ctx - m)
        p_band = jnp.exp(s_band - m)
        den = (jnp.sum(p_ctx, axis=-1, keepdims=True) + jnp.sum(p_band, axis=-1, keepdims=True)
               + jnp.exp(sk - m))
        num = _dot(p_ctx.astype(BF16), v_ctx) + _dot(p_band.astype(BF16), v_band)
        o_ref[:, g * dv:(g + 1) * dv] = num * (1.0 / den)


def _window_attention(q, k, v, sink, o_prev, *, q_off, o_off, tq=256):
    hq, _, d = q.shape
    hk = k.shape[0]
    g_q = hq // hk
    band = tq + 2 * WINDOW
    nq = DEC_SEQ // tq
    return pl.pallas_call(
        functools.partial(_win_kernel, g_q=g_q, dv=d, tq=tq, seq=DEC_SEQ, past=PAST_LEN, band=band),
        grid=(DEC_BATCH, hk, nq),
        in_specs=[
            pl.BlockSpec((g_q, tq, d), lambda b, g, qi: (g, (q_off + b * DEC_SEQ) // tq + qi, 0)),
            pl.BlockSpec((None, None, KV_LAT, d), lambda b, g, qi: (g, b, 0, 0)),
            pl.BlockSpec((None, None, KV_LAT, d), lambda b, g, qi: (g, b, 0, 0)),
            pl.BlockSpec(memory_space=pltpu.SMEM),
            pl.BlockSpec(memory_space=pl.ANY),
        ],
        out_specs=pl.BlockSpec((tq, g_q * d), lambda b, g, qi: ((o_off + b * DEC_SEQ) // tq + qi, g)),
        out_shape=jax.ShapeDtypeStruct(o_prev.shape, F32),
        input_output_aliases={4: 0},
        compiler_params=_params(("arbitrary", "arbitrary", "arbitrary")),
        name="attention_window",
    )(q, k, v, sink, o_prev)


def _merge_kernel(x_ref, h_ref, oa_ref, ob_ref, oc_ref, od_ref, mod_ref, wg_ref, wb_ref, wo_ref, n2g_ref,
                  x1_ref, h2_ref):
    hb = h_ref[...]
    outs = (oa_ref, ob_ref, oc_ref, od_ref)
    acc = None
    for i in range(N_BRANCH):
        gt = jax.nn.sigmoid(_dot(hb, wg_ref[:, i * D_MODEL:(i + 1) * D_MODEL]))
        br = _dot(outs[i][...].astype(BF16), wb_ref[i])
        acc = gt * br if acc is None else acc + gt * br
    y = _dot(acc.astype(BF16), wo_ref[...])
    x1 = x_ref[...] + mod_ref[2:3, :] * y
    x1_ref[...] = x1
    h2 = _row_rms(x1, n2g_ref[...]) * (1.0 + mod_ref[4:5, :]) + mod_ref[3:4, :]
    h2_ref[...] = h2.astype(BF16)


def _merge_stage(x, h, oa, ob, oc, od, mod3, wg, wb, wo, n2g, *, tm=256):
    n = x.shape[0]

    def tok(i):
        return (i, 0)

    return pl.pallas_call(
        _merge_kernel,
        grid=(n // tm,),
        in_specs=[
            pl.BlockSpec((tm, D_MODEL), tok),
            pl.BlockSpec((tm, D_MODEL), tok),
            pl.BlockSpec((tm, BRANCH_W), tok),
            pl.BlockSpec((tm, BRANCH_W), tok),
            pl.BlockSpec((tm, BRANCH_W), tok),
            pl.BlockSpec((tm, BRANCH_W), tok),
            pl.BlockSpec((None, 6, D_MODEL), lambda i: ((i * tm) // DEC_SEQ, 0, 0)),
            pl.BlockSpec((D_MODEL, N_BRANCH * D_MODEL), lambda i: (0, 0)),
            pl.BlockSpec((N_BRANCH, BRANCH_W, D_MODEL), lambda i: (0, 0, 0)),
            pl.BlockSpec((D_MODEL, D_MODEL), lambda i: (0, 0)),
            pl.BlockSpec((1, D_MODEL), lambda i: (0, 0)),
        ],
        out_specs=[pl.BlockSpec((tm, D_MODEL), tok), pl.BlockSpec((tm, D_MODEL), tok)],
        out_shape=[jax.ShapeDtypeStruct((n, D_MODEL), F32), jax.ShapeDtypeStruct((n, D_MODEL), BF16)],
        compiler_params=_params(("arbitrary",)),
        name="merge_stage",
    )(x, h, oa, ob, oc, od, mod3, wg, wb, wo, n2g)


def _oddeven_mergesort_pairs(n):
    pairs = []

    def merge(lo, m, r):
        step = 2 * r
        if step < m:
            merge(lo, m, step)
            merge(lo + r, m, step)
            for i in range(lo + r, lo + m - r, step):
                pairs.append((i, i + r))
        else:
            pairs.append((lo, lo + r))

    def sort(lo, m):
        if m > 1:
            sort(lo, m // 2)
            sort(lo + m // 2, m // 2)
            merge(lo, m, 1)

    sort(0, n)
    return pairs


_SORT_PAIRS = _oddeven_mergesort_pairs(PEER_TOPK)


def _exchange(xs, i, j):
    xs[i], xs[j] = jnp.maximum(xs[i], xs[j]), jnp.minimum(xs[i], xs[j])


def _top16_of_groups(groups):
    n = len(groups)
    xs = list(groups)
    for i, j in _SORT_PAIRS:
        _exchange(xs, i, j)
    for shift in (4, 2, 1):
        xs = [jnp.maximum(xs[k], pltpu.roll(xs[n - 1 - k], shift, 0)) for k in range(n)]
        s = n // 2
        while s >= 1:
            for i in range(n):
                if not i & s:
                    _exchange(xs, i, i + s)
            s //= 2
    return xs


def _sublane_sum(x):
    for shift in (4, 2, 1):
        x = x + pltpu.roll(x, shift, 0)
    return x


def _sublane_spread(vals):
    row = lax.broadcasted_iota(jnp.int32, vals[0].shape, 0)
    out = vals[-1]
    for j in range(len(vals) - 2, -1, -1):
        out = jnp.where(row == j, vals[j], out)
    return out


def _peer_select(g1, g2):
    k = PEER_TOPK
    half = k // 2
    v1 = _top16_of_groups(g1)
    v2 = _top16_of_groups(g2)
    v2lo = _sublane_spread(v2[:half])
    v2hi = _sublane_spread(v2[half:])
    v1hi = _sublane_spread(v1[half:])
    cands = [v1[0] + v2lo, v1[0] + v2hi] + [v1[i] + v2lo for i in range(1, half)] + [v1hi + v2[0]]
    pad = jnp.full(cands[0].shape, EXHAUSTED, F32)
    top = _top16_of_groups(cands + [pad] * (k - len(cands)))
    tau = top[k - 1]
    z = None
    for c in cands:
        e = jnp.where(c >= tau, jnp.exp(c - top[0]), 0.0)
        z = e if z is None else z + e
    z = _sublane_sum(z)
    thr = [jnp.full(g1[0].shape, -1.0, F32) for _ in g1]
    q2 = [jnp.full(g2[0].shape, float(k), F32) for _ in g2]
    for a in range(k - 1, -1, -1):
        cnt = _sublane_sum(jnp.where(v1[a] + v2lo >= tau, 1.0, 0.0) + jnp.where(v1[a] + v2hi >= tau, 1.0, 0.0))
        thr = [jnp.where(x == v1[a], cnt - 1.0, t) for x, t in zip(g1, thr)]
        q2 = [jnp.where(x == v2[a], float(a), q) for x, q in zip(g2, q2)]
    p1 = [jnp.exp(x - v1[0]) for x in g1]
    scale = 0.5 / z
    p2 = [jnp.exp(x - v2[0]) * scale for x in g2]
    return thr, p1, q2, p2


def _peer_query_kernel(h2_ref, wqt_ref, sk_ref, thr_ref, p1_ref, q2_ref, p2_ref):
    qt = _nt_dot(wqt_ref[...], h2_ref[...]).astype(BF16)
    n_lt = qt.shape[1] // 128
    n_grp = PEER_KEYS // 8
    for hd in range(PEER_HEADS):
        s1 = _dot(sk_ref[2 * hd], qt[(2 * hd) * PEER_HALF:(2 * hd + 1) * PEER_HALF, :])
        s2 = _dot(sk_ref[2 * hd + 1], qt[(2 * hd + 1) * PEER_HALF:(2 * hd + 2) * PEER_HALF, :])
        for lt in range(n_lt):
            lanes = slice(lt * 128, (lt + 1) * 128)
            dst = (lt + 1) % n_lt
            rot = slice(dst * 128, (dst + 1) * 128)
            g1 = [s1[8 * i:8 * i + 8, lanes] for i in range(n_grp)]
            g2 = [s2[8 * i:8 * i + 8, lanes] for i in range(n_grp)]
            thr, p1, q2, p2 = _peer_select(g1, g2)
            thr_ref[hd, :, lanes] = jnp.concatenate(thr, axis=0)
            p1_ref[hd, :, lanes] = jnp.concatenate(p1, axis=0)
            q2_ref[hd, :, lanes] = jnp.concatenate(q2, axis=0).astype(BF16)
            p2_ref[hd, :, rot] = jnp.concatenate(p2, axis=0).astype(BF16)


def _peer_query(h2, wqt, sk, *, tm=512):
    n = h2.shape[0]
    big = jax.ShapeDtypeStruct((PEER_HEADS, PEER_KEYS, n), F32)
    half = jax.ShapeDtypeStruct((PEER_HEADS, PEER_KEYS, n), BF16)
    bspec = pl.BlockSpec((PEER_HEADS, PEER_KEYS, tm), lambda i: (0, 0, i))
    return pl.pallas_call(
        _peer_query_kernel,
        grid=(n // tm,),
        in_specs=[
            pl.BlockSpec((tm, D_MODEL), lambda i: (i, 0)),
            pl.BlockSpec((PEER_HEADS * PEER_QDIM, D_MODEL), lambda i: (0, 0)),
            pl.BlockSpec((2 * PEER_HEADS, PEER_KEYS, PEER_HALF), lambda i: (0, 0, 0)),
        ],
        out_specs=[bspec, bspec, bspec, bspec],
        out_shape=[big, big, half, half],
        compiler_params=_params(("arbitrary",)),
        name="peer_query",
    )(h2, wqt, sk)


def _peer_dense_kernel(h2_ref, x1_ref, mod_ref, u_ref, vt_ref, thr_ref, p1_ref, q2_ref, p2_ref,
                       o_ref, acc_ref, g_ref, p_ref, q2s_ref, p2s_ref, *, e1_per_step):
    step = pl.program_id(1)
    tm = g_ref.shape[1]
    n_lt = tm // 128
    rows = 16
    n_r = PEER_KEYS // rows

    def gate_pass():
        for c in range(e1_per_step):
            for lt in range(n_lt):
                lanes = slice(lt * 128, (lt + 1) * 128)
                rot = slice(((lt + 1) % n_lt) * 128, ((lt + 1) % n_lt + 1) * 128)
                g = [None] * n_r
                for hd in range(PEER_HEADS):
                    thr_b = jnp.broadcast_to(thr_ref[hd, c:c + 1, lanes], (rows, 128)).astype(BF16)
                    p1_b = jnp.broadcast_to(p1_ref[hd, c:c + 1, lanes], (rows, 128)).astype(BF16)
                    for r in range(n_r):
                        rs = slice(r * rows, (r + 1) * rows)
                        w = jnp.where(q2s_ref[hd, rs, lanes] <= thr_b, p2s_ref[hd, rs, rot] * p1_b,
                                      jnp.zeros((), BF16))
                        g[r] = w if g[r] is None else g[r] + w
                for r in range(n_r):
                    g_ref[c * PEER_KEYS + r * rows:c * PEER_KEYS + (r + 1) * rows, lanes] = g[r]

    @pl.when(step == 0)
    def _init():
        acc_ref[...] = jnp.zeros_like(acc_ref)
        q2s_ref[...] = q2_ref[...]
        p2s_ref[...] = p2_ref[...]

    pl.when(step >= 0)(gate_pass)

    a = _nt_dot(u_ref[...], h2_ref[...])
    act = (a + a * lax.erf(a * (2.0 ** -0.5))).astype(BF16)
    p_ref[...] = g_ref[...] * act
    acc_ref[...] += _dot(vt_ref[...], p_ref[...])

    @pl.when(step == pl.num_programs(1) - 1)
    def _fin():
        o_ref[...] = x1_ref[...] + mod_ref[5:6, :] * acc_ref[...].T


def _peer_dense(h2, x1, mod3, u, vt, thr, p1, q2, p2, *, tm=512, e1_per_step=8):
    n = h2.shape[0]
    te = e1_per_step * PEER_KEYS
    tok = lambda i, j: (i, 0)
    big = pl.BlockSpec((PEER_HEADS, PEER_KEYS, tm), lambda i, j: (0, 0, i))
    rows = pl.BlockSpec((PEER_HEADS, e1_per_step, tm), lambda i, j: (0, j, i))
    return pl.pallas_call(
        functools.partial(_peer_dense_kernel, e1_per_step=e1_per_step),
        grid=(n // tm, PEER_EXPERTS // te),
        in_specs=[
            pl.BlockSpec((tm, D_MODEL), tok),
            pl.BlockSpec((tm, D_MODEL), tok),
            pl.BlockSpec((None, 6, D_MODEL), lambda i, j: ((i * tm) // DEC_SEQ, 0, 0)),
            pl.BlockSpec((te, D_MODEL), lambda i, j: (j, 0)),
            pl.BlockSpec((D_MODEL, te), lambda i, j: (0, j)),
            rows, rows, big, big,
        ],
        out_specs=pl.BlockSpec((tm, D_MODEL), tok),
        out_shape=jax.ShapeDtypeStruct((n, D_MODEL), F32),
        scratch_shapes=[pltpu.VMEM((D_MODEL, tm), F32), pltpu.VMEM((te, tm), BF16), pltpu.VMEM((te, tm), BF16),
                        pltpu.VMEM((PEER_HEADS, PEER_KEYS, tm), BF16), pltpu.VMEM((PEER_HEADS, PEER_KEYS, tm), BF16)],
        compiler_params=_params(("arbitrary", "arbitrary")),
        name="peer_dense",
    )(h2, x1, mod3, u, vt, thr, p1, q2, p2)


def _block_ones(width, d):
    idx = jnp.arange(width) // d
    return (idx[:, None] == idx[None, :]).astype(BF16)


def _rope_angles(d):
    q4 = d // 4
    inv = ROPE_BASE ** (-jnp.arange(q4, dtype=F32) / q4)
    t = jnp.arange(DEC_SEQ)
    ar = (t // GRID_W).astype(F32)[:, None] * inv
    ac = (t % GRID_W).astype(F32)[:, None] * inv
    return jnp.concatenate([ar, ar, ac, ac], axis=-1)


def _rope_table(d, group, n_groups):
    ang = _rope_angles(d)
    q4 = d // 4
    first = (jnp.arange(d) % (2 * q4)) < q4
    cos = jnp.cos(ang)
    sin = jnp.sin(ang)
    sa = jnp.where(first[None, :], -sin, 0.0)
    sb = jnp.where(first[None, :], 0.0, sin)
    pad = group - d
    cos = jnp.pad(cos, ((0, 0), (pad, 0)), constant_values=1.0)
    sa = jnp.pad(sa, ((0, 0), (pad, 0)))
    sb = jnp.pad(sb, ((0, 0), (pad, 0)))
    return jnp.stack([jnp.tile(cos, (1, n_groups)), jnp.tile(sa, (1, n_groups)), jnp.tile(sb, (1, n_groups))])


def _rot_matrix(d):
    q4 = d // 4
    i = jnp.arange(d)
    first = (i % (2 * q4)) < q4
    src = jnp.where(first, i + q4, i - q4)
    sign = jnp.where(first, -1.0, 1.0)
    return (jnp.zeros((d, d), F32).at[src, i].set(sign)).astype(BF16)


def _pad_row(v, width):
    return jnp.pad(v, (0, width - v.shape[0]))


def kernel(x_prompt, x_sample, c, cache_a_k, cache_a_v, cache_b_k, cache_b_v, cache_c_k, cache_c_v, cache_d_ckv, cache_d_krope, c_ctx, ada_w, ada_b, norm1_g, norm2_g, w_in, a_lam, a_qk_g, a_subln_g, b_sink, b_qk_g, c_qk_g, d_qnorm_g, d_kvnorm_g, w_d_qb, w_d_kvb, d_qk_g, w_branch, w_out, peer_wq, peer_subkeys, peer_u, peer_v):
    x = jnp.concatenate([x_prompt.reshape(N_CTX, D_MODEL), x_sample.reshape(N_LAT, D_MODEL)], axis=0)
    cvec = jnp.concatenate([c_ctx[None, :], c, jnp.zeros((N_MODROWS - 1 - DEC_BATCH, D_MODEL), F32)], axis=0)
    mod_all = _modulation(cvec, ada_w, ada_b)

    e32 = _block_ones(256, A_QK)
    e64 = _block_ones(256, HEAD_DIM)
    e96 = _block_ones(GAIN_W, D_QK)
    tab_a = _rope_table(A_QK, A_QK, 2 * A_HEADS)
    tab_b = _rope_table(HEAD_DIM, HEAD_DIM, B_HEADS)
    tab_d = _rope_table(D_ROPE, D_QK, D_HEADS)
    ang_r = _rope_angles(D_ROPE)
    tab_r = jnp.stack([jnp.cos(ang_r), jnp.sin(ang_r)])
    rot_r = _rot_matrix(D_ROPE)

    states = [[] for _ in range(8)]
    for l in range(DEPTH):
        lam_init = 0.8 - 0.6 * math.exp(-0.3 * l)
        mod3 = mod_all[l, :3].reshape(3, 6, D_MODEL)
        ws = jnp.pad(w_in[l, :, :SMALL_COLS], ((0, 0), (0, SMALL_PAD - SMALL_COLS))).astype(BF16)
        wg = w_in[l, :, SMALL_COLS:].astype(BF16)
        gains = jnp.stack([
            _pad_row(jnp.tile(a_qk_g[l, 0], 2 * A_HEADS), GAIN_W),
            _pad_row(jnp.tile(a_qk_g[l, 1], 2 * A_HEADS), GAIN_W),
            _pad_row(jnp.tile(b_qk_g[l, 0], B_HEADS), GAIN_W),
            _pad_row(jnp.tile(b_qk_g[l, 1], B_KV), GAIN_W),
            _pad_row(jnp.tile(c_qk_g[l, 0], C_HEADS), GAIN_W),
            _pad_row(jnp.tile(c_qk_g[l, 1], C_KV), GAIN_W),
            _pad_row(d_qnorm_g[l], GAIN_W),
            _pad_row(d_kvnorm_g[l], GAIN_W),
            jnp.tile(d_qk_g[l, 0], D_HEADS),
        ] + [jnp.zeros((GAIN_W,), F32)] * (GAIN_ROWS - 9))

        (h, aq, ak, av, bq, bk, bv, cq, ck, cv, dq,
         akt, avt, bkt, bvt, ckt, cvt, dckv, dkr) = _input_stage(
            x, mod3, norm1_g[l][None, :], ws, gains, e32, e64, e96, w_d_qb[l].astype(BF16), tab_a, tab_b, tab_d)

        for lst, s_ in zip(states, (
                akt[:N_CTX].reshape(BATCH, SEQ, A_HEADS, 2, A_QK), avt[:N_CTX].reshape(BATCH, SEQ, A_HEADS, A_V),
                bkt[:N_CTX].reshape(BATCH, SEQ, B_KV, HEAD_DIM), bvt[:N_CTX].reshape(BATCH, SEQ, B_KV, HEAD_DIM),
                ckt[:N_CTX].reshape(BATCH, SEQ, C_KV, HEAD_DIM), cvt[:N_CTX].reshape(BATCH, SEQ, C_KV, HEAD_DIM),
                dckv[:N_CTX].reshape(BATCH, SEQ, D_KV_LORA), dkr[:N_CTX].reshape(BATCH, SEQ, D_ROPE))):
            lst.append(s_)

        ckv_rows = jnp.concatenate([t for b in range(DEC_BATCH) for t in (
            cache_d_ckv[b, l], dckv[N_CTX + b * DEC_SEQ:N_CTX + (b + 1) * DEC_SEQ])] + [dckv[:N_CTX]], axis=0)
        kr_rows = jnp.concatenate([t for b in range(DEC_BATCH) for t in (
            cache_d_krope[b, l], dkr[N_CTX + b * DEC_SEQ:N_CTX + (b + 1) * DEC_SEQ])] + [dkr[:N_CTX]], axis=0)
        kd, vd = _mla_expand(ckv_rows, kr_rows, w_d_kvb[l].astype(BF16), d_qk_g[l, 1][None, :], rot_r, tab_r)

        def lat_kv(new, cache, nh, d):
            cache = jnp.moveaxis(cache.reshape(DEC_BATCH, PAST_LEN, nh, d), 2, 0).astype(BF16)
            new = new[:, N_CTX:].reshape(nh, DEC_BATCH, DEC_SEQ, d)
            return jnp.concatenate([cache, new], axis=2).reshape(nh, DEC_BATCH * KV_LAT, d)

        ak_l = lat_kv(ak, cache_a_k[:, l], 2 * A_HEADS, A_QK)
        av_l = lat_kv(av, cache_a_v[:, l], A_HEADS, A_V)
        bk_l = lat_kv(bk, cache_b_k[:, l], B_KV, HEAD_DIM).reshape(B_KV, DEC_BATCH, KV_LAT, HEAD_DIM)
        bv_l = lat_kv(bv, cache_b_v[:, l], B_KV, HEAD_DIM).reshape(B_KV, DEC_BATCH, KV_LAT, HEAD_DIM)
        ck_l = lat_kv(ck, cache_c_k[:, l], C_KV, HEAD_DIM)
        cv_l = lat_kv(cv, cache_c_v[:, l], C_KV, HEAD_DIM)

        ctx = dict(batch=BATCH, tq_rows=SEQ, tk_rows=SEQ, q_off=0, o_off=0, n_out=N_TOK,
                   lam_init=lam_init, tq=SEQ)
        lat = dict(batch=DEC_BATCH, tq_rows=DEC_SEQ, tk_rows=KV_LAT, q_off=N_CTX, k_off=0, o_off=N_CTX, n_out=N_TOK,
                   lam_init=lam_init, tq=256)
        a_extra = (a_lam[l], a_subln_g[l][None, :])
        blank = jnp.zeros((N_TOK, BRANCH_W), F32)
        oa = _attention(aq, ak, av, a_extra, blank, k_off=0, g_q=4, g_k=4, g_v=2, dv=A_V, mode="diff", **ctx)
        oa = _attention(aq, ak_l, av_l, a_extra, oa, g_q=4, g_k=4, g_v=2, dv=A_V, mode="diff", **lat)
        ob = _attention(bq, bk, bv, (b_sink[l],), blank, k_off=0, g_q=2, g_k=1, g_v=1, dv=HEAD_DIM, mode="sink",
                        **ctx)
        ob = _window_attention(bq, bk_l, bv_l, b_sink[l], ob, q_off=N_CTX, o_off=N_CTX)
        oc = _attention(cq, ck, cv, (), blank, k_off=0, g_q=2, g_k=1, g_v=1, dv=HEAD_DIM, mode="plain", **ctx)
        oc = _attention(cq, ck_l, cv_l, (), oc, g_q=2, g_k=1, g_v=1, dv=HEAD_DIM, mode="plain", **lat)
        od = _attention(dq, kd, vd, (), blank, k_off=DEC_BATCH * KV_LAT, g_q=2, g_k=2, g_v=2, dv=D_V,
                        mode="plain", **ctx)
        od = _attention(dq, kd, vd, (), od, g_q=2, g_k=2, g_v=2, dv=D_V, mode="plain", **lat)

        x1, h2 = _merge_stage(x, h, oa, ob, oc, od, mod3, wg, w_branch[l].astype(BF16), w_out[l].astype(BF16),
                              norm2_g[l][None, :])

        thr, p1, q2, p2 = _peer_query(
            h2, peer_wq[l].T.astype(BF16),
            peer_subkeys[l].reshape(2 * PEER_HEADS, PEER_KEYS, PEER_HALF).astype(BF16))
        x = _peer_dense(h2, x1, mod3, peer_u[l].astype(BF16), peer_v[l].T.astype(BF16), thr, p1, q2, p2)

    y_prompt = x[:N_CTX].reshape(BATCH, SEQ, D_MODEL)
    y_sample = x[N_CTX:].reshape(DEC_BATCH, DEC_SEQ, D_MODEL)
    return (y_prompt, y_sample) + tuple(jnp.stack(s_, axis=1) for s_ in states)
```

```python
import functools
import math

import jax
import jax.numpy as jnp
from jax import lax
from jax.experimental import pallas as pl
from jax.experimental.pallas import tpu as pltpu

F32 = jnp.float32
BF16 = jnp.bfloat16

D_MODEL = 1024
BATCH = 16
SEQ = 256
DEPTH = 2
DEC_BATCH = 2
DEC_SEQ = 4096
PAST_LEN = 512
GRID_W = 64
Q_BLOCK = 128
HEAD_DIM = 64
ROPE_BASE = 10000.0
EPS = 1e-6
NEG_INF = -1e30
A_HEADS = 4
A_QK = 32
A_V = 64
B_HEADS = 4
B_KV = 2
WINDOW = 128
C_HEADS = 4
C_KV = 2
D_HEADS = 4
D_Q_LORA = 256
D_KV_LORA = 128
D_NOPE = 64
D_ROPE = 32
D_V = 64
D_QK = D_NOPE + D_ROPE
N_BRANCH = 4
BRANCH_W = 256
PEER_HEADS = 8
PEER_KEYS = 128
PEER_EXPERTS = PEER_KEYS * PEER_KEYS
PEER_QDIM = 256
PEER_HALF = PEER_QDIM // 2
PEER_TOPK = 16

N_CTX = BATCH * SEQ
N_LAT = DEC_BATCH * DEC_SEQ
N_TOK = N_CTX + N_LAT
KV_LAT = PAST_LEN + DEC_SEQ
N_MODROWS = 8
SMALL_COLS = 2208
SMALL_PAD = 2304

VMEM_LIMIT = 56 * 1024 * 1024
EXHAUSTED = -3.0e38
LOG2E = math.log2(math.e)


def _params(sem, flags=None):
    return pltpu.CompilerParams(dimension_semantics=sem, vmem_limit_bytes=VMEM_LIMIT, flags=flags)


def _nt_dot(a, b):
    return lax.dot_general(a, b, (((1,), (1,)), ((), ())), preferred_element_type=F32)


def _dot(a, b):
    return jnp.dot(a, b, preferred_element_type=F32)


def _mod_kernel(c_ref, w_ref, b_ref, o_ref):
    c = c_ref[...]
    s = c * jax.nn.sigmoid(c)
    o_ref[...] = jnp.dot(s, w_ref[...], preferred_element_type=F32, precision=lax.Precision.HIGHEST) + b_ref[...]


def _modulation(cvec, ada_w, ada_b):
    depth = ada_w.shape[0]
    ncol = ada_w.shape[2] // D_MODEL
    return pl.pallas_call(
        _mod_kernel,
        grid=(depth, ncol),
        in_specs=[
            pl.BlockSpec((N_MODROWS, D_MODEL), lambda l, j: (0, 0)),
            pl.BlockSpec((None, D_MODEL, D_MODEL), lambda l, j: (l, 0, j)),
            pl.BlockSpec((None, 1, D_MODEL), lambda l, j: (l, 0, j)),
        ],
        out_specs=pl.BlockSpec((None, N_MODROWS, D_MODEL), lambda l, j: (l, 0, j)),
        out_shape=jax.ShapeDtypeStruct((depth, N_MODROWS, ada_w.shape[2]), F32),
        compiler_params=_params(("arbitrary", "arbitrary")),
        name="modulation",
    )(cvec, ada_w, ada_b.reshape(depth, 1, -1))


def _group_rms(x, ones_bd, gain, d):
    sq = x * x
    hi = sq.astype(BF16)
    lo = (sq - hi.astype(F32)).astype(BF16)
    ss = _dot(hi, ones_bd) + _dot(lo, ones_bd)
    return x * lax.rsqrt(ss * (1.0 / d) + EPS) * gain


def _row_rms(x, gain):
    ms = jnp.mean(x * x, axis=-1, keepdims=True)
    return x * lax.rsqrt(ms + EPS) * gain


def _rope(x, tab_ref, q4):
    w = x.shape[-1]
    return x * tab_ref[0] + pltpu.roll(x, w - q4, 1) * tab_ref[1] + pltpu.roll(x, q4, 1) * tab_ref[2]


GAIN_ROWS = 16
GAIN_W = D_HEADS * D_QK


def _in_kernel(x_ref, mod_ref, n1g_ref, ws_ref, gains_ref, e32_ref, e64_ref, e96_ref, wdq_ref,
               ta_ref, tb_ref, td_ref,
               h_ref, aq_ref, ak_ref, av_ref, bq_ref, bk_ref, bv_ref, cq_ref, ck_ref, cv_ref, dq_ref,
               akt_ref, avt_ref, bkt_ref, bvt_ref, ckt_ref, cvt_ref, dckv_ref, dkr_ref, *, tm, n_ctx):
    is_lat = pl.program_id(0) * tm >= n_ctx
    x = x_ref[...]
    h = _row_rms(x, n1g_ref[...]) * (1.0 + mod_ref[1:2, :]) + mod_ref[0:1, :]
    hb = h.astype(BF16)
    h_ref[...] = hb
    p = _dot(hb, ws_ref[...])

    def gain(r, w):
        return gains_ref[r:r + 1, 0:w]

    def heads_out(ref, val, nh, d):
        for i in range(nh):
            ref[i] = val[:, i * d:(i + 1) * d].astype(BF16)

    def maybe_rope(val, tab_ref, q4):
        return jnp.where(is_lat, _rope(val, tab_ref, q4), val)

    aq = maybe_rope(_group_rms(p[:, 0:256], e32_ref[...], gain(0, 256), A_QK), ta_ref, A_QK // 4)
    ak = maybe_rope(_group_rms(p[:, 256:512], e32_ref[...], gain(1, 256), A_QK), ta_ref, A_QK // 4)
    av = p[:, 512:768]
    heads_out(aq_ref, aq * (A_QK ** -0.5 * LOG2E), 2 * A_HEADS, A_QK)
    heads_out(ak_ref, ak, 2 * A_HEADS, A_QK)
    heads_out(av_ref, av, A_HEADS, A_V)
    akt_ref[...] = ak
    avt_ref[...] = av
    bq = maybe_rope(_group_rms(p[:, 768:1024], e64_ref[...], gain(2, 256), HEAD_DIM), tb_ref, HEAD_DIM // 4)
    bk = _group_rms(p[:, 1024:1152], e64_ref[0:128, 0:128], gain(3, 128), HEAD_DIM)
    bk = jnp.where(is_lat, bk * tb_ref[0, :, 0:128] + pltpu.roll(bk, 128 - 16, 1) * tb_ref[1, :, 0:128]
                   + pltpu.roll(bk, 16, 1) * tb_ref[2, :, 0:128], bk)
    bv = p[:, 1152:1280]
    heads_out(bq_ref, bq * (HEAD_DIM ** -0.5 * LOG2E), B_HEADS, HEAD_DIM)
    heads_out(bk_ref, bk, B_KV, HEAD_DIM)
    heads_out(bv_ref, bv, B_KV, HEAD_DIM)
    bkt_ref[...] = bk
    bvt_ref[...] = bv
    cq = maybe_rope(_group_rms(p[:, 1280:1536], e64_ref[...], gain(4, 256), HEAD_DIM), tb_ref, HEAD_DIM // 4)
    ck = _group_rms(p[:, 1536:1664], e64_ref[0:128, 0:128], gain(5, 128), HEAD_DIM)
    ck = jnp.where(is_lat, ck * tb_ref[0, :, 0:128] + pltpu.roll(ck, 128 - 16, 1) * tb_ref[1, :, 0:128]
                   + pltpu.roll(ck, 16, 1) * tb_ref[2, :, 0:128], ck)
    cv = p[:, 1664:1792]
    heads_out(cq_ref, cq * (HEAD_DIM ** -0.5 * LOG2E), C_HEADS, HEAD_DIM)
    heads_out(ck_ref, ck, C_KV, HEAD_DIM)
    heads_out(cv_ref, cv, C_KV, HEAD_DIM)
    ckt_ref[...] = ck
    cvt_ref[...] = cv
    dqa = _row_rms(p[:, 1792:2048], gain(6, 256))
    dq = _dot(dqa.astype(BF16), wdq_ref[...])
    dq = maybe_rope(_group_rms(dq, e96_ref[...], gain(8, GAIN_W), D_QK), td_ref, D_ROPE // 4)
    heads_out(dq_ref, dq * (D_QK ** -0.5 * LOG2E), D_HEADS, D_QK)
    dckv_ref[...] = _row_rms(p[:, 2048:2176], gain(7, 128))
    dkr_ref[...] = p[:, 2176:2208]


def _input_stage(x, mod3, n1g, ws, gains, e32, e64, e96, wdq, tab_a, tab_b, tab_d, *, tm=256):
    n = x.shape[0]
    nt = n // tm
    lat0 = N_CTX // tm
    pos_blocks = DEC_SEQ // tm

    def tok(i):
        return (i, 0)

    def pos(i):
        return (0, jnp.maximum(i - lat0, 0) % pos_blocks, 0)

    def const2(i):
        return (0, 0)

    def hm(i):
        return (0, i, 0)

    def hspec(nh, d):
        return pl.BlockSpec((nh, tm, d), hm)

    def hshape(nh, d):
        return jax.ShapeDtypeStruct((nh, n, d), BF16)

    def tspec(w):
        return pl.BlockSpec((tm, w), tok)

    def tshape(w):
        return jax.ShapeDtypeStruct((n, w), F32)

    in_specs = [
        pl.BlockSpec((tm, D_MODEL), tok),
        pl.BlockSpec((None, 6, D_MODEL), lambda i: ((i * tm) // DEC_SEQ, 0, 0)),
        pl.BlockSpec((1, D_MODEL), const2),
        pl.BlockSpec((D_MODEL, SMALL_PAD), const2),
        pl.BlockSpec((GAIN_ROWS, GAIN_W), const2),
        pl.BlockSpec((256, 256), const2),
        pl.BlockSpec((256, 256), const2),
        pl.BlockSpec((GAIN_W, GAIN_W), const2),
        pl.BlockSpec((D_Q_LORA, GAIN_W), const2),
        pl.BlockSpec((3, tm, 256), pos),
        pl.BlockSpec((3, tm, 256), pos),
        pl.BlockSpec((3, tm, GAIN_W), pos),
    ]
    out_specs = [
        pl.BlockSpec((tm, D_MODEL), tok),
        hspec(8, A_QK), hspec(8, A_QK), hspec(4, A_V),
        hspec(4, HEAD_DIM), hspec(2, HEAD_DIM), hspec(2, HEAD_DIM),
        hspec(4, HEAD_DIM), hspec(2, HEAD_DIM), hspec(2, HEAD_DIM),
        hspec(4, D_QK),
        tspec(256), tspec(256), tspec(128), tspec(128), tspec(128), tspec(128), tspec(128), tspec(D_ROPE),
    ]
    out_shape = [
        jax.ShapeDtypeStruct((n, D_MODEL), BF16),
        hshape(8, A_QK), hshape(8, A_QK), hshape(4, A_V),
        hshape(4, HEAD_DIM), hshape(2, HEAD_DIM), hshape(2, HEAD_DIM),
        hshape(4, HEAD_DIM), hshape(2, HEAD_DIM), hshape(2, HEAD_DIM),
        hshape(4, D_QK),
        tshape(256), tshape(256), tshape(128), tshape(128), tshape(128), tshape(128), tshape(128), tshape(D_ROPE),
    ]
    return pl.pallas_call(
        functools.partial(_in_kernel, tm=tm, n_ctx=N_CTX),
        grid=(nt,),
        in_specs=in_specs,
        out_specs=out_specs,
        out_shape=out_shape,
        compiler_params=_params(("arbitrary",)),
        name="input_stage",
    )(x, mod3, n1g, ws, gains, e32, e64, e96, wdq, tab_a, tab_b, tab_d)


def _mla_kernel(ckv_ref, kr_ref, wkvb_ref, g_ref, rot_ref, tab_ref, kd_ref, vd_ref, *, tm, lat_tiles, per_batch_tiles):
    i = pl.program_id(0)
    rotate = jnp.logical_and(i < lat_tiles, i % per_batch_tiles >= PAST_LEN // tm)
    kv = _dot(ckv_ref[...].astype(BF16), wkvb_ref[...])
    kr = kr_ref[...]
    ssr = jnp.sum(kr * kr, axis=-1, keepdims=True)
    z = kr * g_ref[:, D_NOPE:D_QK]
    zh = z.astype(BF16)
    zl = (z - zh.astype(F32)).astype(BF16)
    rot = _dot(zh, rot_ref[...]) + _dot(zl, rot_ref[...])
    rz = jnp.where(rotate, z * tab_ref[0] + rot * tab_ref[1], z)
    g_nope = g_ref[:, 0:D_NOPE]
    for hd in range(D_HEADS):
        kn = kv[:, hd * 128:hd * 128 + D_NOPE]
        r = lax.rsqrt((jnp.sum(kn * kn, axis=-1, keepdims=True) + ssr) * (1.0 / D_QK) + EPS)
        kd_ref[hd] = jnp.concatenate([kn * r * g_nope, rz * r], axis=-1).astype(BF16)
        vd_ref[hd] = kv[:, hd * 128 + D_NOPE:(hd + 1) * 128].astype(BF16)


def _mla_expand(ckv, kr, wkvb, g, rot, tab, *, tm=512):
    rows = ckv.shape[0]
    per_batch_tiles = KV_LAT // tm
    lat_tiles = DEC_BATCH * per_batch_tiles
    past_tiles = PAST_LEN // tm
    pos_blocks = DEC_SEQ // tm

    def pos(i):
        return (0, jnp.maximum(i % per_batch_tiles - past_tiles, 0) % pos_blocks, 0)

    return pl.pallas_call(
        functools.partial(_mla_kernel, tm=tm, lat_tiles=lat_tiles, per_batch_tiles=per_batch_tiles),
        grid=(rows // tm,),
        in_specs=[
            pl.BlockSpec((tm, D_KV_LORA), lambda i: (i, 0)),
            pl.BlockSpec((tm, D_ROPE), lambda i: (i, 0)),
            pl.BlockSpec((D_KV_LORA, D_HEADS * (D_NOPE + D_V)), lambda i: (0, 0)),
            pl.BlockSpec((1, D_QK), lambda i: (0, 0)),
            pl.BlockSpec((D_ROPE, D_ROPE), lambda i: (0, 0)),
            pl.BlockSpec((2, tm, D_ROPE), pos),
        ],
        out_specs=[
            pl.BlockSpec((D_HEADS, tm, D_QK), lambda i: (0, i, 0)),
            pl.BlockSpec((D_HEADS, tm, D_V), lambda i: (0, i, 0)),
        ],
        out_shape=[
            jax.ShapeDtypeStruct((D_HEADS, rows, D_QK), BF16),
            jax.ShapeDtypeStruct((D_HEADS, rows, D_V), BF16),
        ],
        compiler_params=_params(("arbitrary",)),
        name="mla_expand",
    )(ckv, kr, wkvb, g, rot, tab)


def _attn_kernel(*refs, g_q, g_k, g_v, dv, mode, lam_init, has_cache):
    refs = list(refs)
    q_ref, k_ref, v_ref = refs[:3]
    pos = 3
    kc_ref = vc_ref = sink_ref = lam_ref = sub_ref = None
    if has_cache:
        kc_ref, vc_ref = refs[pos], refs[pos + 1]
        pos += 2
    if mode == "sink":
        sink_ref = refs[pos]
        pos += 1
    elif mode == "diff":
        lam_ref, sub_ref = refs[pos], refs[pos + 1]
        pos += 2
    o_ref = refs[pos + 1]
    grp = pl.program_id(1)

    def probs(g):
        q = q_ref[g]
        parts = [_nt_dot(q, k_ref[g * g_k // g_q])]
        if has_cache:
            parts.append(_nt_dot(q, kc_ref[g * g_k // g_q]))
        m = jnp.max(parts[0], axis=-1, keepdims=True)
        for s in parts[1:]:
            m = jnp.maximum(m, jnp.max(s, axis=-1, keepdims=True))
        if mode == "sink":
            sk = sink_ref[grp * g_q + g] * LOG2E
            m = jnp.maximum(m, sk)
        ps = [jnp.exp2(s - m) for s in parts]
        den = jnp.sum(ps[0], axis=-1, keepdims=True)
        for p in ps[1:]:
            den = den + jnp.sum(p, axis=-1, keepdims=True)
        if mode == "sink":
            den = den + jnp.exp2(sk - m)
        return ps, den

    def weighted(ps, hv):
        out = _dot(ps[0].astype(BF16), v_ref[hv])
        if has_cache:
            out = out + _dot(ps[1].astype(BF16), vc_ref[hv])
        return out

    if mode == "diff":
        lp = lam_ref[...]
        lam = (jnp.exp(jnp.sum(lp[0:1] * lp[1:2], axis=-1, keepdims=True))
               - jnp.exp(jnp.sum(lp[2:3] * lp[3:4], axis=-1, keepdims=True)) + lam_init)
        for i in range(g_q // 2):
            ps1, den1 = probs(2 * i)
            ps2, den2 = probs(2 * i + 1)
            w1 = 1.0 / den1
            w2 = lam / den2
            comb = [a * w1 - b * w2 for a, b in zip(ps1, ps2)]
            d = weighted(comb, i * g_v // (g_q // 2))
            o_ref[:, i * dv:(i + 1) * dv] = _row_rms(d, sub_ref[...]) * (1.0 - lam_init)
    else:
        for g in range(g_q):
            ps, den = probs(g)
            o_ref[:, g * dv:(g + 1) * dv] = weighted(ps, g * g_v // g_q) * (1.0 / den)


def _attention(q, k, v, cache, extras, o_prev, *, batch, tq_rows, tk_rows, q_off, k_off, o_off,
               g_q, g_k, g_v, dv, mode, lam_init, tq):
    hq, _, dq = q.shape
    n_groups = hq // g_q
    g_out = g_q // 2 if mode == "diff" else g_q
    nq = tq_rows // tq
    tk = tk_rows
    in_specs = [
        pl.BlockSpec((g_q, tq, dq), lambda b, g, qi: (g, (q_off + b * tq_rows) // tq + qi, 0)),
        pl.BlockSpec((g_k, tk, dq), lambda b, g, qi: (g, (k_off + b * tk_rows) // tk, 0)),
        pl.BlockSpec((g_v, tk, dv), lambda b, g, qi: (g, (k_off + b * tk_rows) // tk, 0)),
    ]
    args = [q, k, v]
    if cache is not None:
        in_specs.append(pl.BlockSpec((g_k, PAST_LEN, dq), lambda b, g, qi: (g, b, 0)))
        in_specs.append(pl.BlockSpec((g_v, PAST_LEN, dv), lambda b, g, qi: (g, b, 0)))
        args += list(cache)
    if mode == "sink":
        in_specs.append(pl.BlockSpec(memory_space=pltpu.SMEM))
        args.append(extras[0])
    elif mode == "diff":
        in_specs.append(pl.BlockSpec((4, A_QK), lambda b, g, qi: (0, 0)))
        in_specs.append(pl.BlockSpec((1, dv), lambda b, g, qi: (0, 0)))
        args += list(extras)
    in_specs.append(pl.BlockSpec(memory_space=pl.ANY))
    aliases = {len(args): 0}
    args.append(o_prev)
    return pl.pallas_call(
        functools.partial(_attn_kernel, g_q=g_q, g_k=g_k, g_v=g_v, dv=dv, mode=mode, lam_init=lam_init,
                          has_cache=cache is not None),
        grid=(batch, n_groups, nq),
        in_specs=in_specs,
        out_specs=pl.BlockSpec((tq, g_out * dv), lambda b, g, qi: ((o_off + b * tq_rows) // tq + qi, g)),
        out_shape=jax.ShapeDtypeStruct(o_prev.shape, F32),
        input_output_aliases=aliases,
        compiler_params=_params(("arbitrary", "arbitrary", "arbitrary")),
        name="attention_" + mode,
    )(*args)


def _win_kernel(q_ref, k_ref, v_ref, kc_ref, vc_ref, sink_ref, prev_ref, o_ref, *, g_q, dv, tq, seq, band):
    del prev_ref
    grp = pl.program_id(1)
    qi = pl.program_id(2)
    start = pl.multiple_of(jnp.clip(qi * tq - WINDOW, 0, seq - band), WINDOW)
    k_ctx = kc_ref[...]
    v_ctx = vc_ref[...]
    k_band = k_ref[pl.ds(start, band), :]
    v_band = v_ref[pl.ds(start, band), :]
    qpos = qi * tq + lax.broadcasted_iota(jnp.int32, (tq, band), 0)
    kpos = start + lax.broadcasted_iota(jnp.int32, (tq, band), 1)
    valid = jnp.abs(kpos - qpos) <= WINDOW
    for g in range(g_q):
        q = q_ref[g]
        s_ctx = _nt_dot(q, k_ctx)
        s_band = jnp.where(valid, _nt_dot(q, k_band), NEG_INF)
        sk = sink_ref[grp * g_q + g] * LOG2E
        m = jnp.maximum(jnp.maximum(jnp.max(s_ctx, axis=-1, keepdims=True),
                                    jnp.max(s_band, axis=-1, keepdims=True)), sk)
        p_ctx = jnp.exp2(s_ctx - m)
        p_band = jnp.exp2(s_band - m)
        den = (jnp.sum(p_ctx, axis=-1, keepdims=True) + jnp.sum(p_band, axis=-1, keepdims=True)
               + jnp.exp2(sk - m))
        num = _dot(p_ctx.astype(BF16), v_ctx) + _dot(p_band.astype(BF16), v_band)
        o_ref[:, g * dv:(g + 1) * dv] = num * (1.0 / den)


def _window_attention(q, k, v, kc, vc, sink, o_prev, *, q_off, k_off, o_off, tq=256):
    hq, _, d = q.shape
    hk = k.shape[0]
    g_q = hq // hk
    band = tq + 2 * WINDOW
    nq = DEC_SEQ // tq
    return pl.pallas_call(
        functools.partial(_win_kernel, g_q=g_q, dv=d, tq=tq, seq=DEC_SEQ, band=band),
        grid=(DEC_BATCH, hk, nq),
        in_specs=[
            pl.BlockSpec((g_q, tq, d), lambda b, g, qi: (g, (q_off + b * DEC_SEQ) // tq + qi, 0)),
            pl.BlockSpec((None, DEC_SEQ, d), lambda b, g, qi: (g, k_off // DEC_SEQ + b, 0)),
            pl.BlockSpec((None, DEC_SEQ, d), lambda b, g, qi: (g, k_off // DEC_SEQ + b, 0)),
            pl.BlockSpec((None, PAST_LEN, d), lambda b, g, qi: (g, b, 0)),
            pl.BlockSpec((None, PAST_LEN, d), lambda b, g, qi: (g, b, 0)),
            pl.BlockSpec(memory_space=pltpu.SMEM),
            pl.BlockSpec(memory_space=pl.ANY),
        ],
        out_specs=pl.BlockSpec((tq, g_q * d), lambda b, g, qi: ((o_off + b * DEC_SEQ) // tq + qi, g)),
        out_shape=jax.ShapeDtypeStruct(o_prev.shape, F32),
        input_output_aliases={6: 0},
        compiler_params=_params(("arbitrary", "arbitrary", "arbitrary")),
        name="attention_window",
    )(q, k, v, kc, vc, sink, o_prev)


def _merge_kernel(x_ref, h_ref, oa_ref, ob_ref, oc_ref, od_ref, mod_ref, wg_ref, wb_ref, wo_ref, n2g_ref,
                  x1_ref, h2_ref):
    hb = h_ref[...]
    outs = (oa_ref, ob_ref, oc_ref, od_ref)
    acc = None
    for i in range(N_BRANCH):
        gt = jax.nn.sigmoid(_dot(hb, wg_ref[:, i * D_MODEL:(i + 1) * D_MODEL]))
        br = _dot(outs[i][...].astype(BF16), wb_ref[i])
        acc = gt * br if acc is None else acc + gt * br
    y = _dot(acc.astype(BF16), wo_ref[...])
    x1 = x_ref[...] + mod_ref[2:3, :] * y
    x1_ref[...] = x1
    h2 = _row_rms(x1, n2g_ref[...]) * (1.0 + mod_ref[4:5, :]) + mod_ref[3:4, :]
    h2_ref[...] = h2.astype(BF16)


def _merge_stage(x, h, oa, ob, oc, od, mod3, wg, wb, wo, n2g, *, tm=256):
    n = x.shape[0]

    def tok(i):
        return (i, 0)

    return pl.pallas_call(
        _merge_kernel,
        grid=(n // tm,),
        in_specs=[
            pl.BlockSpec((tm, D_MODEL), tok),
            pl.BlockSpec((tm, D_MODEL), tok),
            pl.BlockSpec((tm, BRANCH_W), tok),
            pl.BlockSpec((tm, BRANCH_W), tok),
            pl.BlockSpec((tm, BRANCH_W), tok),
            pl.BlockSpec((tm, BRANCH_W), tok),
            pl.BlockSpec((None, 6, D_MODEL), lambda i: ((i * tm) // DEC_SEQ, 0, 0)),
            pl.BlockSpec((D_MODEL, N_BRANCH * D_MODEL), lambda i: (0, 0)),
            pl.BlockSpec((N_BRANCH, BRANCH_W, D_MODEL), lambda i: (0, 0, 0)),
            pl.BlockSpec((D_MODEL, D_MODEL), lambda i: (0, 0)),
            pl.BlockSpec((1, D_MODEL), lambda i: (0, 0)),
        ],
        out_specs=[pl.BlockSpec((tm, D_MODEL), tok), pl.BlockSpec((tm, D_MODEL), tok)],
        out_shape=[jax.ShapeDtypeStruct((n, D_MODEL), F32), jax.ShapeDtypeStruct((n, D_MODEL), BF16)],
        compiler_params=_params(("arbitrary",)),
        name="merge_stage",
    )(x, h, oa, ob, oc, od, mod3, wg, wb, wo, n2g)


def _oddeven_mergesort_pairs(n):
    pairs = []

    def merge(lo, m, r):
        step = 2 * r
        if step < m:
            merge(lo, m, step)
            merge(lo + r, m, step)
            for i in range(lo + r, lo + m - r, step):
                pairs.append((i, i + r))
        else:
            pairs.append((lo, lo + r))

    def sort(lo, m):
        if m > 1:
            sort(lo, m // 2)
            sort(lo + m // 2, m // 2)
            merge(lo, m, 1)

    sort(0, n)
    return pairs


_SORT_PAIRS = _oddeven_mergesort_pairs(PEER_TOPK)


def _exchange(xs, i, j):
    xs[i], xs[j] = jnp.maximum(xs[i], xs[j]), jnp.minimum(xs[i], xs[j])


def _top16_of_groups(groups):
    n = len(groups)
    xs = list(groups)
    for i, j in _SORT_PAIRS:
        _exchange(xs, i, j)
    for shift in (4, 2, 1):
        xs = [jnp.maximum(xs[k], pltpu.roll(xs[n - 1 - k], shift, 0)) for k in range(n)]
        s = n // 2
        while s >= 1:
            for i in range(n):
                if not i & s:
                    _exchange(xs, i, i + s)
            s //= 2
    return xs


def _sublane_sum(x):
    for shift in (4, 2, 1):
        x = x + pltpu.roll(x, shift, 0)
    return x


def _sublane_spread(vals):
    row = lax.broadcasted_iota(jnp.int32, vals[0].shape, 0)
    out = vals[-1]
    for j in range(len(vals) - 2, -1, -1):
        out = jnp.where(row == j, vals[j], out)
    return out


def _peer_select(g1, g2):
    k = PEER_TOPK
    half = k // 2
    v1 = _top16_of_groups(g1)
    v2 = _top16_of_groups(g2)
    v2lo = _sublane_spread(v2[:half])
    v2hi = _sublane_spread(v2[half:])
    v1hi = _sublane_spread(v1[half:])
    cands = [v1[0] + v2lo, v1[0] + v2hi] + [v1[i] + v2lo for i in range(1, half)] + [v1hi + v2[0]]
    pad = jnp.full(cands[0].shape, EXHAUSTED, F32)
    top = _top16_of_groups(cands + [pad] * (k - len(cands)))
    tau = top[k - 1]
    z = None
    for c in cands:
        e = jnp.where(c >= tau, jnp.exp(c - top[0]), 0.0)
        z = e if z is None else z + e
    z = _sublane_sum(z)
    thr = [jnp.full(g1[0].shape, -1.0, F32) for _ in g1]
    q2 = [jnp.full(g2[0].shape, float(k), F32) for _ in g2]
    for a in range(k - 1, -1, -1):
        cnt = _sublane_sum(jnp.where(v1[a] + v2lo >= tau, 1.0, 0.0) + jnp.where(v1[a] + v2hi >= tau, 1.0, 0.0))
        thr = [jnp.where(x == v1[a], cnt - 1.0, t) for x, t in zip(g1, thr)]
        q2 = [jnp.where(x == v2[a], float(a), q) for x, q in zip(g2, q2)]
    p1 = [jnp.exp(x - v1[0]) for x in g1]
    scale = 0.5 / z
    p2 = [jnp.exp(x - v2[0]) * scale for x in g2]
    return thr, p1, q2, p2


def _peer_query_kernel(h2_ref, wqt_ref, sk_ref, thr_ref, p1_ref, q2_ref, p2_ref):
    qt = _nt_dot(wqt_ref[...], h2_ref[...]).astype(BF16)
    n_lt = qt.shape[1] // 128
    n_grp = PEER_KEYS // 8
    for hd in range(PEER_HEADS):
        s1 = _dot(sk_ref[2 * hd], qt[(2 * hd) * PEER_HALF:(2 * hd + 1) * PEER_HALF, :])
        s2 = _dot(sk_ref[2 * hd + 1], qt[(2 * hd + 1) * PEER_HALF:(2 * hd + 2) * PEER_HALF, :])
        for lt in range(n_lt):
            lanes = slice(lt * 128, (lt + 1) * 128)
            dst = (lt + 1) % n_lt
            rot = slice(dst * 128, (dst + 1) * 128)
            g1 = [s1[8 * i:8 * i + 8, lanes] for i in range(n_grp)]
            g2 = [s2[8 * i:8 * i + 8, lanes] for i in range(n_grp)]
            thr, p1, q2, p2 = _peer_select(g1, g2)
            thr_ref[hd, :, lanes] = jnp.concatenate(thr, axis=0)
            p1_ref[hd, :, lanes] = jnp.concatenate(p1, axis=0)
            q2_ref[hd, :, lanes] = jnp.concatenate(q2, axis=0).astype(BF16)
            p2_ref[hd, :, rot] = jnp.concatenate(p2, axis=0).astype(BF16)


def _peer_query(h2, wqt, sk, *, tm=512):
    n = h2.shape[0]
    big = jax.ShapeDtypeStruct((PEER_HEADS, PEER_KEYS, n), F32)
    half = jax.ShapeDtypeStruct((PEER_HEADS, PEER_KEYS, n), BF16)
    bspec = pl.BlockSpec((PEER_HEADS, PEER_KEYS, tm), lambda i: (0, 0, i))
    return pl.pallas_call(
        _peer_query_kernel,
        grid=(n // tm,),
        in_specs=[
            pl.BlockSpec((tm, D_MODEL), lambda i: (i, 0)),
            pl.BlockSpec((PEER_HEADS * PEER_QDIM, D_MODEL), lambda i: (0, 0)),
            pl.BlockSpec((2 * PEER_HEADS, PEER_KEYS, PEER_HALF), lambda i: (0, 0, 0)),
        ],
        out_specs=[bspec, bspec, bspec, bspec],
        out_shape=[big, big, half, half],
        compiler_params=_params(("arbitrary",)),
        name="peer_query",
    )(h2, wqt, sk)


def _peer_dense_kernel(h2_ref, x1_ref, mod_ref, u_ref, vt_ref, thr_ref, p1_ref, q2_ref, p2_ref,
                       o_ref, acc_ref, g_ref, p_ref, q2s_ref, p2s_ref, *, e1_per_step):
    step = pl.program_id(1)
    tm = g_ref.shape[1]
    n_lt = tm // 128
    rows = 16
    n_r = PEER_KEYS // rows

    def gate_pass():
        for c in range(e1_per_step):
            for lt in range(n_lt):
                lanes = slice(lt * 128, (lt + 1) * 128)
                rot = slice(((lt + 1) % n_lt) * 128, ((lt + 1) % n_lt + 1) * 128)
                g = [None] * n_r
                for hd in range(PEER_HEADS):
                    thr_b = jnp.broadcast_to(thr_ref[hd, c:c + 1, lanes], (rows, 128)).astype(BF16)
                    p1_b = jnp.broadcast_to(p1_ref[hd, c:c + 1, lanes], (rows, 128)).astype(BF16)
                    for r in range(n_r):
                        rs = slice(r * rows, (r + 1) * rows)
                        w = jnp.where(q2s_ref[hd, rs, lanes] <= thr_b, p2s_ref[hd, rs, rot] * p1_b,
                                      jnp.zeros((), BF16))
                        g[r] = w if g[r] is None else g[r] + w
                for r in range(n_r):
                    g_ref[c * PEER_KEYS + r * rows:c * PEER_KEYS + (r + 1) * rows, lanes] = g[r]

    @pl.when(step == 0)
    def _init():
        acc_ref[...] = jnp.zeros_like(acc_ref)
        q2s_ref[...] = q2_ref[...]
        p2s_ref[...] = p2_ref[...]

    pl.when(step >= 0)(gate_pass)

    a = _nt_dot(u_ref[...], h2_ref[...])
    act = (a + a * lax.erf(a * (2.0 ** -0.5))).astype(BF16)
    p_ref[...] = g_ref[...] * act
    acc_ref[...] += _dot(vt_ref[...], p_ref[...])

    @pl.when(step == pl.num_programs(1) - 1)
    def _fin():
        o_ref[...] = x1_ref[...] + mod_ref[5:6, :] * acc_ref[...].T


def _peer_dense(h2, x1, mod3, u, vt, thr, p1, q2, p2, *, layer, tm=512, e1_per_step=8):
    n = h2.shape[0]
    te = e1_per_step * PEER_KEYS
    tok = lambda i, j: (i, 0)
    big = pl.BlockSpec((PEER_HEADS, PEER_KEYS, tm), lambda i, j: (0, 0, i))
    rows = pl.BlockSpec((PEER_HEADS, e1_per_step, tm), lambda i, j: (0, j, i))
    return pl.pallas_call(
        functools.partial(_peer_dense_kernel, e1_per_step=e1_per_step),
        grid=(n // tm, PEER_EXPERTS // te),
        in_specs=[
            pl.BlockSpec((tm, D_MODEL), tok),
            pl.BlockSpec((tm, D_MODEL), tok),
            pl.BlockSpec((None, 6, D_MODEL), lambda i, j: ((i * tm) // DEC_SEQ, 0, 0)),
            pl.BlockSpec((None, te, D_MODEL), lambda i, j: (layer, j, 0)),
            pl.BlockSpec((None, D_MODEL, te), lambda i, j: (layer, 0, j)),
            rows, rows, big, big,
        ],
        out_specs=pl.BlockSpec((tm, D_MODEL), tok),
        out_shape=jax.ShapeDtypeStruct((n, D_MODEL), F32),
        scratch_shapes=[pltpu.VMEM((D_MODEL, tm), F32), pltpu.VMEM((te, tm), BF16), pltpu.VMEM((te, tm), BF16),
                        pltpu.VMEM((PEER_HEADS, PEER_KEYS, tm), BF16), pltpu.VMEM((PEER_HEADS, PEER_KEYS, tm), BF16)],
        compiler_params=_params(("arbitrary", "arbitrary")),
        name="peer_dense",
    )(h2, x1, mod3, u, vt, thr, p1, q2, p2)


def _block_ones(width, d):
    idx = jnp.arange(width) // d
    return (idx[:, None] == idx[None, :]).astype(BF16)


def _rope_angles(d):
    q4 = d // 4
    inv = ROPE_BASE ** (-jnp.arange(q4, dtype=F32) / q4)
    t = jnp.arange(DEC_SEQ)
    ar = (t // GRID_W).astype(F32)[:, None] * inv
    ac = (t % GRID_W).astype(F32)[:, None] * inv
    return jnp.concatenate([ar, ar, ac, ac], axis=-1)


def _rope_table(d, group, n_groups):
    ang = _rope_angles(d)
    q4 = d // 4
    first = (jnp.arange(d) % (2 * q4)) < q4
    cos = jnp.cos(ang)
    sin = jnp.sin(ang)
    sa = jnp.where(first[None, :], -sin, 0.0)
    sb = jnp.where(first[None, :], 0.0, sin)
    pad = group - d
    cos = jnp.pad(cos, ((0, 0), (pad, 0)), constant_values=1.0)
    sa = jnp.pad(sa, ((0, 0), (pad, 0)))
    sb = jnp.pad(sb, ((0, 0), (pad, 0)))
    return jnp.stack([jnp.tile(cos, (1, n_groups)), jnp.tile(sa, (1, n_groups)), jnp.tile(sb, (1, n_groups))])


def _rot_matrix(d):
    q4 = d // 4
    i = jnp.arange(d)
    first = (i % (2 * q4)) < q4
    src = jnp.where(first, i + q4, i - q4)
    sign = jnp.where(first, -1.0, 1.0)
    return (jnp.zeros((d, d), F32).at[src, i].set(sign)).astype(BF16)


def _pad_row(v, width):
    return jnp.pad(v, (0, width - v.shape[0]))


def kernel(x_prompt, x_sample, c, cache_a_k, cache_a_v, cache_b_k, cache_b_v, cache_c_k, cache_c_v, cache_d_ckv, cache_d_krope, c_ctx, ada_w, ada_b, norm1_g, norm2_g, w_in, a_lam, a_qk_g, a_subln_g, b_sink, b_qk_g, c_qk_g, d_qnorm_g, d_kvnorm_g, w_d_qb, w_d_kvb, d_qk_g, w_branch, w_out, peer_wq, peer_subkeys, peer_u, peer_v):
    x = jnp.concatenate([x_prompt.reshape(N_CTX, D_MODEL), x_sample.reshape(N_LAT, D_MODEL)], axis=0)
    cvec = jnp.concatenate([c_ctx[None, :], c, jnp.zeros((N_MODROWS - 1 - DEC_BATCH, D_MODEL), F32)], axis=0)
    mod_all = _modulation(cvec, ada_w, ada_b)

    e32 = _block_ones(256, A_QK)
    e64 = _block_ones(256, HEAD_DIM)
    e96 = _block_ones(GAIN_W, D_QK)
    tab_a = _rope_table(A_QK, A_QK, 2 * A_HEADS)
    tab_b = _rope_table(HEAD_DIM, HEAD_DIM, B_HEADS)
    tab_d = _rope_table(D_ROPE, D_QK, D_HEADS)
    ang_r = _rope_angles(D_ROPE)
    tab_r = jnp.stack([jnp.cos(ang_r), jnp.sin(ang_r)])
    rot_r = _rot_matrix(D_ROPE)

    u_all = peer_u.astype(BF16)
    vt_all = jnp.swapaxes(peer_v, 1, 2).astype(BF16)
    states = [[] for _ in range(8)]
    for l in range(DEPTH):
        lam_init = 0.8 - 0.6 * math.exp(-0.3 * l)
        mod3 = mod_all[l, :3].reshape(3, 6, D_MODEL)
        ws = jnp.pad(w_in[l, :, :SMALL_COLS], ((0, 0), (0, SMALL_PAD - SMALL_COLS))).astype(BF16)
        wg = w_in[l, :, SMALL_COLS:].astype(BF16)
        gains = jnp.stack([
            _pad_row(jnp.tile(a_qk_g[l, 0], 2 * A_HEADS), GAIN_W),
            _pad_row(jnp.tile(a_qk_g[l, 1], 2 * A_HEADS), GAIN_W),
            _pad_row(jnp.tile(b_qk_g[l, 0], B_HEADS), GAIN_W),
            _pad_row(jnp.tile(b_qk_g[l, 1], B_KV), GAIN_W),
            _pad_row(jnp.tile(c_qk_g[l, 0], C_HEADS), GAIN_W),
            _pad_row(jnp.tile(c_qk_g[l, 1], C_KV), GAIN_W),
            _pad_row(d_qnorm_g[l], GAIN_W),
            _pad_row(d_kvnorm_g[l], GAIN_W),
            jnp.tile(d_qk_g[l, 0], D_HEADS),
        ] + [jnp.zeros((GAIN_W,), F32)] * (GAIN_ROWS - 9))

        (h, aq, ak, av, bq, bk, bv, cq, ck, cv, dq,
         akt, avt, bkt, bvt, ckt, cvt, dckv, dkr) = _input_stage(
            x, mod3, norm1_g[l][None, :], ws, gains, e32, e64, e96, w_d_qb[l].astype(BF16), tab_a, tab_b, tab_d)

        for lst, s_ in zip(states, (
                akt[:N_CTX].reshape(BATCH, SEQ, A_HEADS, 2, A_QK), avt[:N_CTX].reshape(BATCH, SEQ, A_HEADS, A_V),
                bkt[:N_CTX].reshape(BATCH, SEQ, B_KV, HEAD_DIM), bvt[:N_CTX].reshape(BATCH, SEQ, B_KV, HEAD_DIM),
                ckt[:N_CTX].reshape(BATCH, SEQ, C_KV, HEAD_DIM), cvt[:N_CTX].reshape(BATCH, SEQ, C_KV, HEAD_DIM),
                dckv[:N_CTX].reshape(BATCH, SEQ, D_KV_LORA), dkr[:N_CTX].reshape(BATCH, SEQ, D_ROPE))):
            lst.append(s_)

        ckv_rows = jnp.concatenate([t for b in range(DEC_BATCH) for t in (
            cache_d_ckv[b, l], dckv[N_CTX + b * DEC_SEQ:N_CTX + (b + 1) * DEC_SEQ])] + [dckv[:N_CTX]], axis=0)
        kr_rows = jnp.concatenate([t for b in range(DEC_BATCH) for t in (
            cache_d_krope[b, l], dkr[N_CTX + b * DEC_SEQ:N_CTX + (b + 1) * DEC_SEQ])] + [dkr[:N_CTX]], axis=0)
        kd, vd = _mla_expand(ckv_rows, kr_rows, w_d_kvb[l].astype(BF16), d_qk_g[l, 1][None, :], rot_r, tab_r)

        def cached(cache, nh, d):
            cache = jnp.moveaxis(cache.reshape(DEC_BATCH, PAST_LEN, nh, d), 2, 0)
            return cache.reshape(nh, DEC_BATCH * PAST_LEN, d).astype(BF16)

        a_cache = (cached(cache_a_k[:, l], 2 * A_HEADS, A_QK), cached(cache_a_v[:, l], A_HEADS, A_V))
        b_cache = (cached(cache_b_k[:, l], B_KV, HEAD_DIM), cached(cache_b_v[:, l], B_KV, HEAD_DIM))
        c_cache = (cached(cache_c_k[:, l], C_KV, HEAD_DIM), cached(cache_c_v[:, l], C_KV, HEAD_DIM))

        ctx = dict(batch=BATCH, tq_rows=SEQ, tk_rows=SEQ, q_off=0, o_off=0, lam_init=lam_init, tq=SEQ)
        lat = dict(batch=DEC_BATCH, tq_rows=DEC_SEQ, tk_rows=DEC_SEQ, q_off=N_CTX, k_off=N_CTX, o_off=N_CTX,
                   lam_init=lam_init, tq=256)
        a_extra = (a_lam[l], a_subln_g[l][None, :])
        blank = jnp.zeros((N_TOK, BRANCH_W), F32)
        oa = _attention(aq, ak, av, None, a_extra, blank, k_off=0, g_q=4, g_k=4, g_v=2, dv=A_V, mode="diff", **ctx)
        oa = _attention(aq, ak, av, a_cache, a_extra, oa, g_q=4, g_k=4, g_v=2, dv=A_V, mode="diff", **lat)
        ob = _attention(bq, bk, bv, None, (b_sink[l],), blank, k_off=0, g_q=2, g_k=1, g_v=1, dv=HEAD_DIM,
                        mode="sink", **ctx)
        ob = _window_attention(bq, bk, bv, *b_cache, b_sink[l], ob, q_off=N_CTX, k_off=N_CTX, o_off=N_CTX)
        oc = _attention(cq, ck, cv, None, (), blank, k_off=0, g_q=2, g_k=1, g_v=1, dv=HEAD_DIM, mode="plain",
                        **ctx)
        oc = _attention(cq, ck, cv, c_cache, (), oc, g_q=2, g_k=1, g_v=1, dv=HEAD_DIM, mode="plain", **lat)
        od = _attention(dq, kd, vd, None, (), blank, k_off=DEC_BATCH * KV_LAT, g_q=2, g_k=2, g_v=2, dv=D_V,
                        mode="plain", **ctx)
        lat_d = dict(lat, tk_rows=KV_LAT, k_off=0)
        od = _attention(dq, kd, vd, None, (), od, g_q=2, g_k=2, g_v=2, dv=D_V, mode="plain", **lat_d)

        x1, h2 = _merge_stage(x, h, oa, ob, oc, od, mod3, wg, w_branch[l].astype(BF16), w_out[l].astype(BF16),
                              norm2_g[l][None, :])

        thr, p1, q2, p2 = _peer_query(
            h2, peer_wq[l].T.astype(BF16),
            peer_subkeys[l].reshape(2 * PEER_HEADS, PEER_KEYS, PEER_HALF).astype(BF16))
        x = _peer_dense(h2, x1, mod3, u_all, vt_all, thr, p1, q2, p2, layer=l)

    y_prompt = x[:N_CTX].reshape(BATCH, SEQ, D_MODEL)
    y_sample = x[N_CTX:].reshape(DEC_BATCH, DEC_SEQ, D_MODEL)
    return (y_prompt, y_sample) + tuple(jnp.stack(s_, axis=1) for s_ in states)
```

```python
import functools
import math

import jax
import jax.numpy as jnp
from jax import lax
from jax.experimental import pallas as pl
from jax.experimental.pallas import tpu as pltpu

F32 = jnp.float32
BF16 = jnp.bfloat16

D_MODEL = 1024
BATCH = 16
SEQ = 256
DEPTH = 2
DEC_BATCH = 2
DEC_SEQ = 4096
PAST_LEN = 512
GRID_W = 64
Q_BLOCK = 128
HEAD_DIM = 64
ROPE_BASE = 10000.0
EPS = 1e-6
NEG_INF = -1e30
A_HEADS = 4
A_QK = 32
A_V = 64
B_HEADS = 4
B_KV = 2
WINDOW = 128
C_HEADS = 4
C_KV = 2
D_HEADS = 4
D_Q_LORA = 256
D_KV_LORA = 128
D_NOPE = 64
D_ROPE = 32
D_V = 64
D_QK = D_NOPE + D_ROPE
N_BRANCH = 4
BRANCH_W = 256
PEER_HEADS = 8
PEER_KEYS = 128
PEER_EXPERTS = PEER_KEYS * PEER_KEYS
PEER_QDIM = 256
PEER_HALF = PEER_QDIM // 2
PEER_TOPK = 16

N_CTX = BATCH * SEQ
N_LAT = DEC_BATCH * DEC_SEQ
N_TOK = N_CTX + N_LAT
KV_LAT = PAST_LEN + DEC_SEQ
N_MODROWS = 8
SMALL_COLS = 2208
SMALL_PAD = 2304

VMEM_LIMIT = 56 * 1024 * 1024
EXHAUSTED = -3.0e38
LOG2E = math.log2(math.e)


def _params(sem, flags=None):
    return pltpu.CompilerParams(dimension_semantics=sem, vmem_limit_bytes=VMEM_LIMIT, flags=flags)


def _nt_dot(a, b):
    return lax.dot_general(a, b, (((1,), (1,)), ((), ())), preferred_element_type=F32)


def _dot(a, b):
    return jnp.dot(a, b, preferred_element_type=F32)


def _mod_kernel(c_ref, w_ref, b_ref, o_ref):
    c = c_ref[...]
    s = c * jax.nn.sigmoid(c)
    o_ref[...] = jnp.dot(s, w_ref[...], preferred_element_type=F32, precision=lax.Precision.HIGHEST) + b_ref[...]


def _modulation(cvec, ada_w, ada_b):
    depth = ada_w.shape[0]
    ncol = ada_w.shape[2] // D_MODEL
    return pl.pallas_call(
        _mod_kernel,
        grid=(depth, ncol),
        in_specs=[
            pl.BlockSpec((N_MODROWS, D_MODEL), lambda l, j: (0, 0)),
            pl.BlockSpec((None, D_MODEL, D_MODEL), lambda l, j: (l, 0, j)),
            pl.BlockSpec((None, 1, D_MODEL), lambda l, j: (l, 0, j)),
        ],
        out_specs=pl.BlockSpec((None, N_MODROWS, D_MODEL), lambda l, j: (l, 0, j)),
        out_shape=jax.ShapeDtypeStruct((depth, N_MODROWS, ada_w.shape[2]), F32),
        compiler_params=_params(("arbitrary", "arbitrary")),
        name="modulation",
    )(cvec, ada_w, ada_b.reshape(depth, 1, -1))


def _group_rms(x, ones_bd, gain, d):
    sq = x * x
    hi = sq.astype(BF16)
    lo = (sq - hi.astype(F32)).astype(BF16)
    ss = _dot(hi, ones_bd) + _dot(lo, ones_bd)
    return x * lax.rsqrt(ss * (1.0 / d) + EPS) * gain


def _row_rms(x, gain):
    ms = jnp.mean(x * x, axis=-1, keepdims=True)
    return x * lax.rsqrt(ms + EPS) * gain


def _rope(x, tab_ref, q4):
    w = x.shape[-1]
    return x * tab_ref[0] + pltpu.roll(x, w - q4, 1) * tab_ref[1] + pltpu.roll(x, q4, 1) * tab_ref[2]


GAIN_ROWS = 16
GAIN_W = D_HEADS * D_QK


def _in_kernel(x_ref, mod_ref, n1g_ref, ws_ref, gains_ref, e32_ref, e64_ref, e96_ref, wdq_ref,
               ta_ref, tb_ref, td_ref,
               h_ref, aq_ref, ak_ref, av_ref, bq_ref, bk_ref, bv_ref, cq_ref, ck_ref, cv_ref, dq_ref,
               akt_ref, avt_ref, bkt_ref, bvt_ref, ckt_ref, cvt_ref, dckv_ref, dkr_ref, *, tm, n_ctx):
    is_lat = pl.program_id(0) * tm >= n_ctx
    x = x_ref[...]
    h = _row_rms(x, n1g_ref[...]) * (1.0 + mod_ref[1:2, :]) + mod_ref[0:1, :]
    hb = h.astype(BF16)
    h_ref[...] = hb
    p = _dot(hb, ws_ref[...])

    def gain(r, w):
        return gains_ref[r:r + 1, 0:w]

    def heads_out(ref, val, nh, d):
        for i in range(nh):
            ref[i] = val[:, i * d:(i + 1) * d].astype(BF16)

    def maybe_rope(val, tab_ref, q4):
        return jnp.where(is_lat, _rope(val, tab_ref, q4), val)

    aq = maybe_rope(_group_rms(p[:, 0:256], e32_ref[...], gain(0, 256), A_QK), ta_ref, A_QK // 4)
    ak = maybe_rope(_group_rms(p[:, 256:512], e32_ref[...], gain(1, 256), A_QK), ta_ref, A_QK // 4)
    av = p[:, 512:768]
    heads_out(aq_ref, aq * (A_QK ** -0.5 * LOG2E), 2 * A_HEADS, A_QK)
    heads_out(ak_ref, ak, 2 * A_HEADS, A_QK)
    heads_out(av_ref, av, A_HEADS, A_V)
    akt_ref[...] = ak
    avt_ref[...] = av
    bq = maybe_rope(_group_rms(p[:, 768:1024], e64_ref[...], gain(2, 256), HEAD_DIM), tb_ref, HEAD_DIM // 4)
    bk = _group_rms(p[:, 1024:1152], e64_ref[0:128, 0:128], gain(3, 128), HEAD_DIM)
    bk = jnp.where(is_lat, bk * tb_ref[0, :, 0:128] + pltpu.roll(bk, 128 - 16, 1) * tb_ref[1, :, 0:128]
                   + pltpu.roll(bk, 16, 1) * tb_ref[2, :, 0:128], bk)
    bv = p[:, 1152:1280]
    heads_out(bq_ref, bq * (HEAD_DIM ** -0.5 * LOG2E), B_HEADS, HEAD_DIM)
    heads_out(bk_ref, bk, B_KV, HEAD_DIM)
    heads_out(bv_ref, bv, B_KV, HEAD_DIM)
    bkt_ref[...] = bk
    bvt_ref[...] = bv
    cq = maybe_rope(_group_rms(p[:, 1280:1536], e64_ref[...], gain(4, 256), HEAD_DIM), tb_ref, HEAD_DIM // 4)
    ck = _group_rms(p[:, 1536:1664], e64_ref[0:128, 0:128], gain(5, 128), HEAD_DIM)
    ck = jnp.where(is_lat, ck * tb_ref[0, :, 0:128] + pltpu.roll(ck, 128 - 16, 1) * tb_ref[1, :, 0:128]
                   + pltpu.roll(ck, 16, 1) * tb_ref[2, :, 0:128], ck)
    cv = p[:, 1664:1792]
    heads_out(cq_ref, cq * (HEAD_DIM ** -0.5 * LOG2E), C_HEADS, HEAD_DIM)
    heads_out(ck_ref, ck, C_KV, HEAD_DIM)
    heads_out(cv_ref, cv, C_KV, HEAD_DIM)
    ckt_ref[...] = ck
    cvt_ref[...] = cv
    dqa = _row_rms(p[:, 1792:2048], gain(6, 256))
    dq = _dot(dqa.astype(BF16), wdq_ref[...])
    dq = maybe_rope(_group_rms(dq, e96_ref[...], gain(8, GAIN_W), D_QK), td_ref, D_ROPE // 4)
    heads_out(dq_ref, dq * (D_QK ** -0.5 * LOG2E), D_HEADS, D_QK)
    dckv_ref[...] = _row_rms(p[:, 2048:2176], gain(7, 128))
    dkr_ref[...] = p[:, 2176:2208]


def _input_stage(x, mod3, n1g, ws, gains, e32, e64, e96, wdq, tab_a, tab_b, tab_d, *, tm=256):
    n = x.shape[0]
    nt = n // tm
    lat0 = N_CTX // tm
    pos_blocks = DEC_SEQ // tm

    def tok(i):
        return (i, 0)

    def pos(i):
        return (0, jnp.maximum(i - lat0, 0) % pos_blocks, 0)

    def const2(i):
        return (0, 0)

    def hm(i):
        return (0, i, 0)

    def hspec(nh, d):
        return pl.BlockSpec((nh, tm, d), hm)

    def hshape(nh, d):
        return jax.ShapeDtypeStruct((nh, n, d), BF16)

    def tspec(w):
        return pl.BlockSpec((tm, w), tok)

    def tshape(w):
        return jax.ShapeDtypeStruct((n, w), F32)

    in_specs = [
        pl.BlockSpec((tm, D_MODEL), tok),
        pl.BlockSpec((None, 6, D_MODEL), lambda i: ((i * tm) // DEC_SEQ, 0, 0)),
        pl.BlockSpec((1, D_MODEL), const2),
        pl.BlockSpec((D_MODEL, SMALL_PAD), const2),
        pl.BlockSpec((GAIN_ROWS, GAIN_W), const2),
        pl.BlockSpec((256, 256), const2),
        pl.BlockSpec((256, 256), const2),
        pl.BlockSpec((GAIN_W, GAIN_W), const2),
        pl.BlockSpec((D_Q_LORA, GAIN_W), const2),
        pl.BlockSpec((3, tm, 256), pos),
        pl.BlockSpec((3, tm, 256), pos),
        pl.BlockSpec((3, tm, GAIN_W), pos),
    ]
    out_specs = [
        pl.BlockSpec((tm, D_MODEL), tok),
        hspec(8, A_QK), hspec(8, A_QK), hspec(4, A_V),
        hspec(4, HEAD_DIM), hspec(2, HEAD_DIM), hspec(2, HEAD_DIM),
        hspec(4, HEAD_DIM), hspec(2, HEAD_DIM), hspec(2, HEAD_DIM),
        hspec(4, D_QK),
        tspec(256), tspec(256), tspec(128), tspec(128), tspec(128), tspec(128), tspec(128), tspec(D_ROPE),
    ]
    out_shape = [
        jax.ShapeDtypeStruct((n, D_MODEL), BF16),
        hshape(8, A_QK), hshape(8, A_QK), hshape(4, A_V),
        hshape(4, HEAD_DIM), hshape(2, HEAD_DIM), hshape(2, HEAD_DIM),
        hshape(4, HEAD_DIM), hshape(2, HEAD_DIM), hshape(2, HEAD_DIM),
        hshape(4, D_QK),
        tshape(256), tshape(256), tshape(128), tshape(128), tshape(128), tshape(128), tshape(128), tshape(D_ROPE),
    ]
    return pl.pallas_call(
        functools.partial(_in_kernel, tm=tm, n_ctx=N_CTX),
        grid=(nt,),
        in_specs=in_specs,
        out_specs=out_specs,
        out_shape=out_shape,
        compiler_params=_params(("arbitrary",)),
        name="input_stage",
    )(x, mod3, n1g, ws, gains, e32, e64, e96, wdq, tab_a, tab_b, tab_d)


def _mla_kernel(ckv_ref, kr_ref, wkvb_ref, g_ref, rot_ref, tab_ref, kd_ref, vd_ref, *, tm, lat_tiles, per_batch_tiles):
    i = pl.program_id(0)
    rotate = jnp.logical_and(i < lat_tiles, i % per_batch_tiles >= PAST_LEN // tm)
    kv = _dot(ckv_ref[...].astype(BF16), wkvb_ref[...])
    kr = kr_ref[...]
    ssr = jnp.sum(kr * kr, axis=-1, keepdims=True)
    z = kr * g_ref[:, D_NOPE:D_QK]
    zh = z.astype(BF16)
    zl = (z - zh.astype(F32)).astype(BF16)
    rot = _dot(zh, rot_ref[...]) + _dot(zl, rot_ref[...])
    rz = jnp.where(rotate, z * tab_ref[0] + rot * tab_ref[1], z)
    g_nope = g_ref[:, 0:D_NOPE]
    for hd in range(D_HEADS):
        kn = kv[:, hd * 128:hd * 128 + D_NOPE]
        r = lax.rsqrt((jnp.sum(kn * kn, axis=-1, keepdims=True) + ssr) * (1.0 / D_QK) + EPS)
        kd_ref[hd] = jnp.concatenate([kn * r * g_nope, rz * r], axis=-1).astype(BF16)
        vd_ref[hd] = kv[:, hd * 128 + D_NOPE:(hd + 1) * 128].astype(BF16)


def _mla_expand(ckv, kr, wkvb, g, rot, tab, *, tm=512):
    rows = ckv.shape[0]
    per_batch_tiles = KV_LAT // tm
    lat_tiles = DEC_BATCH * per_batch_tiles
    past_tiles = PAST_LEN // tm
    pos_blocks = DEC_SEQ // tm

    def pos(i):
        return (0, jnp.maximum(i % per_batch_tiles - past_tiles, 0) % pos_blocks, 0)

    return pl.pallas_call(
        functools.partial(_mla_kernel, tm=tm, lat_tiles=lat_tiles, per_batch_tiles=per_batch_tiles),
        grid=(rows // tm,),
        in_specs=[
            pl.BlockSpec((tm, D_KV_LORA), lambda i: (i, 0)),
            pl.BlockSpec((tm, D_ROPE), lambda i: (i, 0)),
            pl.BlockSpec((D_KV_LORA, D_HEADS * (D_NOPE + D_V)), lambda i: (0, 0)),
            pl.BlockSpec((1, D_QK), lambda i: (0, 0)),
            pl.BlockSpec((D_ROPE, D_ROPE), lambda i: (0, 0)),
            pl.BlockSpec((2, tm, D_ROPE), pos),
        ],
        out_specs=[
            pl.BlockSpec((D_HEADS, tm, D_QK), lambda i: (0, i, 0)),
            pl.BlockSpec((D_HEADS, tm, D_V), lambda i: (0, i, 0)),
        ],
        out_shape=[
            jax.ShapeDtypeStruct((D_HEADS, rows, D_QK), BF16),
            jax.ShapeDtypeStruct((D_HEADS, rows, D_V), BF16),
        ],
        compiler_params=_params(("arbitrary",)),
        name="mla_expand",
    )(ckv, kr, wkvb, g, rot, tab)


def _attn_kernel(*refs, g_q, g_k, g_v, dv, mode, lam_init, has_cache):
    refs = list(refs)
    q_ref, k_ref, v_ref = refs[:3]
    pos = 3
    sink_ref = lam_ref = sub_ref = None
    if has_cache:
        kc_ref, vc_ref = refs[pos], refs[pos + 1]
        pos += 2
    if mode == "sink":
        sink_ref = refs[pos]
        pos += 1
    elif mode == "diff":
        lam_ref, sub_ref = refs[pos], refs[pos + 1]
        pos += 2
    o_ref = refs[pos + 1]
    grp = pl.program_id(1)

    if has_cache:
        kall_ref, vall_ref = refs[pos + 2], refs[pos + 3]
        past = kc_ref.shape[1]

        @pl.when(pl.program_id(2) == 0)
        def _join():
            kall_ref[:, 0:past, :] = kc_ref[...]
            kall_ref[:, past:, :] = k_ref[...]
            vall_ref[:, 0:past, :] = vc_ref[...]
            vall_ref[:, past:, :] = v_ref[...]

        k_ref, v_ref = kall_ref, vall_ref

    def probs(g):
        s = _nt_dot(q_ref[g], k_ref[g * g_k // g_q])
        m = jnp.max(s, axis=-1, keepdims=True)
        if mode == "sink":
            sk = sink_ref[grp * g_q + g] * LOG2E
            m = jnp.maximum(m, sk)
        p = jnp.exp2(s - m)
        den = jnp.sum(p, axis=-1, keepdims=True)
        if mode == "sink":
            den = den + jnp.exp2(sk - m)
        return p, den

    if mode == "diff":
        lp = lam_ref[...]
        lam = (jnp.exp(jnp.sum(lp[0:1] * lp[1:2], axis=-1, keepdims=True))
               - jnp.exp(jnp.sum(lp[2:3] * lp[3:4], axis=-1, keepdims=True)) + lam_init)
        for i in range(g_q // 2):
            p1, den1 = probs(2 * i)
            p2, den2 = probs(2 * i + 1)
            comb = p1 * (1.0 / den1) - p2 * (lam / den2)
            d = _dot(comb.astype(BF16), v_ref[i * g_v // (g_q // 2)])
            o_ref[:, i * dv:(i + 1) * dv] = _row_rms(d, sub_ref[...]) * (1.0 - lam_init)
    else:
        for g in range(g_q):
            p, den = probs(g)
            o_ref[:, g * dv:(g + 1) * dv] = _dot(p.astype(BF16), v_ref[g * g_v // g_q]) * (1.0 / den)


def _attention(q, k, v, cache, extras, o_prev, *, batch, tq_rows, tk_rows, q_off, k_off, o_off,
               g_q, g_k, g_v, dv, mode, lam_init, tq):
    hq, _, dq = q.shape
    n_groups = hq // g_q
    g_out = g_q // 2 if mode == "diff" else g_q
    nq = tq_rows // tq
    tk = tk_rows
    in_specs = [
        pl.BlockSpec((g_q, tq, dq), lambda b, g, qi: (g, (q_off + b * tq_rows) // tq + qi, 0)),
        pl.BlockSpec((g_k, tk, dq), lambda b, g, qi: (g, (k_off + b * tk_rows) // tk, 0)),
        pl.BlockSpec((g_v, tk, dv), lambda b, g, qi: (g, (k_off + b * tk_rows) // tk, 0)),
    ]
    args = [q, k, v]
    if cache is not None:
        in_specs.append(pl.BlockSpec((g_k, PAST_LEN, dq), lambda b, g, qi: (g, b, 0)))
        in_specs.append(pl.BlockSpec((g_v, PAST_LEN, dv), lambda b, g, qi: (g, b, 0)))
        args += list(cache)
    if mode == "sink":
        in_specs.append(pl.BlockSpec(memory_space=pltpu.SMEM))
        args.append(extras[0])
    elif mode == "diff":
        in_specs.append(pl.BlockSpec((4, A_QK), lambda b, g, qi: (0, 0)))
        in_specs.append(pl.BlockSpec((1, dv), lambda b, g, qi: (0, 0)))
        args += list(extras)
    in_specs.append(pl.BlockSpec(memory_space=pl.ANY))
    aliases = {len(args): 0}
    args.append(o_prev)
    return pl.pallas_call(
        functools.partial(_attn_kernel, g_q=g_q, g_k=g_k, g_v=g_v, dv=dv, mode=mode, lam_init=lam_init,
                          has_cache=cache is not None),
        grid=(batch, n_groups, nq),
        in_specs=in_specs,
        out_specs=pl.BlockSpec((tq, g_out * dv), lambda b, g, qi: ((o_off + b * tq_rows) // tq + qi, g)),
        out_shape=jax.ShapeDtypeStruct(o_prev.shape, F32),
        scratch_shapes=([pltpu.VMEM((g_k, PAST_LEN + tk, dq), BF16), pltpu.VMEM((g_v, PAST_LEN + tk, dv), BF16)]
                        if cache is not None else []),
        input_output_aliases=aliases,
        compiler_params=_params(("arbitrary", "arbitrary", "arbitrary")),
        name="attention_" + mode,
    )(*args)


def _win_kernel(q_ref, k_ref, v_ref, kc_ref, vc_ref, sink_ref, prev_ref, o_ref, *, g_q, dv, tq, seq, band):
    del prev_ref
    grp = pl.program_id(1)
    qi = pl.program_id(2)
    start = pl.multiple_of(jnp.clip(qi * tq - WINDOW, 0, seq - band), WINDOW)
    k_ctx = kc_ref[...]
    v_ctx = vc_ref[...]
    k_band = k_ref[pl.ds(start, band), :]
    v_band = v_ref[pl.ds(start, band), :]
    qpos = qi * tq + lax.broadcasted_iota(jnp.int32, (tq, band), 0)
    kpos = start + lax.broadcasted_iota(jnp.int32, (tq, band), 1)
    valid = jnp.abs(kpos - qpos) <= WINDOW
    for g in range(g_q):
        q = q_ref[g]
        s_ctx = _nt_dot(q, k_ctx)
        s_band = jnp.where(valid, _nt_dot(q, k_band), NEG_INF)
        sk = sink_ref[grp * g_q + g] * LOG2E
        m = jnp.maximum(jnp.maximum(jnp.max(s_ctx, axis=-1, keepdims=True),
                                    jnp.max(s_band, axis=-1, keepdims=True)), sk)
        p_ctx = jnp.exp2(s_ctx - m)
        p_band = jnp.exp2(s_band - m)
        den = (jnp.sum(p_ctx, axis=-1, keepdims=True) + jnp.sum(p_band, axis=-1, keepdims=True)
               + jnp.exp2(sk - m))
        num = _dot(p_ctx.astype(BF16), v_ctx) + _dot(p_band.astype(BF16), v_band)
        o_ref[:, g * dv:(g + 1) * dv] = num * (1.0 / den)


def _window_attention(q, k, v, kc, vc, sink, o_prev, *, q_off, k_off, o_off, tq=256):
    hq, _, d = q.shape
    hk = k.shape[0]
    g_q = hq // hk
    band = tq + 2 * WINDOW
    nq = DEC_SEQ // tq
    return pl.pallas_call(
        functools.partial(_win_kernel, g_q=g_q, dv=d, tq=tq, seq=DEC_SEQ, band=band),
        grid=(DEC_BATCH, hk, nq),
        in_specs=[
            pl.BlockSpec((g_q, tq, d), lambda b, g, qi: (g, (q_off + b * DEC_SEQ) // tq + qi, 0)),
            pl.BlockSpec((None, DEC_SEQ, d), lambda b, g, qi: (g, k_off // DEC_SEQ + b, 0)),
            pl.BlockSpec((None, DEC_SEQ, d), lambda b, g, qi: (g, k_off // DEC_SEQ + b, 0)),
            pl.BlockSpec((None, PAST_LEN, d), lambda b, g, qi: (g, b, 0)),
            pl.BlockSpec((None, PAST_LEN, d), lambda b, g, qi: (g, b, 0)),
            pl.BlockSpec(memory_space=pltpu.SMEM),
            pl.BlockSpec(memory_space=pl.ANY),
        ],
        out_specs=pl.BlockSpec((tq, g_q * d), lambda b, g, qi: ((o_off + b * DEC_SEQ) // tq + qi, g)),
        out_shape=jax.ShapeDtypeStruct(o_prev.shape, F32),
        input_output_aliases={6: 0},
        compiler_params=_params(("arbitrary", "arbitrary", "arbitrary")),
        name="attention_window",
    )(q, k, v, kc, vc, sink, o_prev)


def _merge_kernel(x_ref, h_ref, oa_ref, ob_ref, oc_ref, od_ref, mod_ref, wg_ref, wb_ref, wo_ref, n2g_ref,
                  x1_ref, h2_ref):
    hb = h_ref[...]
    outs = (oa_ref, ob_ref, oc_ref, od_ref)
    acc = None
    for i in range(N_BRANCH):
        gt = jax.nn.sigmoid(_dot(hb, wg_ref[:, i * D_MODEL:(i + 1) * D_MODEL]))
        br = _dot(outs[i][...].astype(BF16), wb_ref[i])
        acc = gt * br if acc is None else acc + gt * br
    y = _dot(acc.astype(BF16), wo_ref[...])
    x1 = x_ref[...] + mod_ref[2:3, :] * y
    x1_ref[...] = x1
    h2 = _row_rms(x1, n2g_ref[...]) * (1.0 + mod_ref[4:5, :]) + mod_ref[3:4, :]
    h2_ref[...] = h2.astype(BF16)


def _merge_stage(x, h, oa, ob, oc, od, mod3, wg, wb, wo, n2g, *, tm=256):
    n = x.shape[0]

    def tok(i):
        return (i, 0)

    return pl.pallas_call(
        _merge_kernel,
        grid=(n // tm,),
        in_specs=[
            pl.BlockSpec((tm, D_MODEL), tok),
            pl.BlockSpec((tm, D_MODEL), tok),
            pl.BlockSpec((tm, BRANCH_W), tok),
            pl.BlockSpec((tm, BRANCH_W), tok),
            pl.BlockSpec((tm, BRANCH_W), tok),
            pl.BlockSpec((tm, BRANCH_W), tok),
            pl.BlockSpec((None, 6, D_MODEL), lambda i: ((i * tm) // DEC_SEQ, 0, 0)),
            pl.BlockSpec((D_MODEL, N_BRANCH * D_MODEL), lambda i: (0, 0)),
            pl.BlockSpec((N_BRANCH, BRANCH_W, D_MODEL), lambda i: (0, 0, 0)),
            pl.BlockSpec((D_MODEL, D_MODEL), lambda i: (0, 0)),
            pl.BlockSpec((1, D_MODEL), lambda i: (0, 0)),
        ],
        out_specs=[pl.BlockSpec((tm, D_MODEL), tok), pl.BlockSpec((tm, D_MODEL), tok)],
        out_shape=[jax.ShapeDtypeStruct((n, D_MODEL), F32), jax.ShapeDtypeStruct((n, D_MODEL), BF16)],
        compiler_params=_params(("arbitrary",)),
        name="merge_stage",
    )(x, h, oa, ob, oc, od, mod3, wg, wb, wo, n2g)


def _oddeven_mergesort_pairs(n):
    pairs = []

    def merge(lo, m, r):
        step = 2 * r
        if step < m:
            merge(lo, m, step)
            merge(lo + r, m, step)
            for i in range(lo + r, lo + m - r, step):
                pairs.append((i, i + r))
        else:
            pairs.append((lo, lo + r))

    def sort(lo, m):
        if m > 1:
            sort(lo, m // 2)
            sort(lo + m // 2, m // 2)
            merge(lo, m, 1)

    sort(0, n)
    return pairs


_SORT_PAIRS = _oddeven_mergesort_pairs(PEER_TOPK)


def _exchange(xs, i, j):
    xs[i], xs[j] = jnp.maximum(xs[i], xs[j]), jnp.minimum(xs[i], xs[j])


def _top16_of_groups(groups):
    n = len(groups)
    xs = list(groups)
    for i, j in _SORT_PAIRS:
        _exchange(xs, i, j)
    for shift in (4, 2, 1):
        xs = [jnp.maximum(xs[k], pltpu.roll(xs[n - 1 - k], shift, 0)) for k in range(n)]
        s = n // 2
        while s >= 1:
            for i in range(n):
                if not i & s:
                    _exchange(xs, i, i + s)
            s //= 2
    return xs


def _sublane_sum(x):
    for shift in (4, 2, 1):
        x = x + pltpu.roll(x, shift, 0)
    return x


def _sublane_spread(vals):
    row = lax.broadcasted_iota(jnp.int32, vals[0].shape, 0)
    out = vals[-1]
    for j in range(len(vals) - 2, -1, -1):
        out = jnp.where(row == j, vals[j], out)
    return out


def _peer_select(g1, g2):
    k = PEER_TOPK
    half = k // 2
    v1 = _top16_of_groups(g1)
    v2 = _top16_of_groups(g2)
    v2lo = _sublane_spread(v2[:half])
    v2hi = _sublane_spread(v2[half:])
    v1hi = _sublane_spread(v1[half:])
    cands = [v1[0] + v2lo, v1[0] + v2hi] + [v1[i] + v2lo for i in range(1, half)] + [v1hi + v2[0]]
    pad = jnp.full(cands[0].shape, EXHAUSTED, F32)
    top = _top16_of_groups(cands + [pad] * (k - len(cands)))
    tau = top[k - 1]
    z = None
    for c in cands:
        e = jnp.where(c >= tau, jnp.exp(c - top[0]), 0.0)
        z = e if z is None else z + e
    z = _sublane_sum(z)
    thr = [jnp.full(g1[0].shape, -1.0, F32) for _ in g1]
    q2 = [jnp.full(g2[0].shape, float(k), F32) for _ in g2]
    for a in range(k - 1, -1, -1):
        cnt = _sublane_sum(jnp.where(v1[a] + v2lo >= tau, 1.0, 0.0) + jnp.where(v1[a] + v2hi >= tau, 1.0, 0.0))
        thr = [jnp.where(x == v1[a], cnt - 1.0, t) for x, t in zip(g1, thr)]
        q2 = [jnp.where(x == v2[a], float(a), q) for x, q in zip(g2, q2)]
    p1 = [jnp.exp(x - v1[0]) for x in g1]
    scale = 0.5 / z
    p2 = [jnp.exp(x - v2[0]) * scale for x in g2]
    return thr, p1, q2, p2


def _peer_query_kernel(h2_ref, wqt_ref, sk_ref, thr_ref, p1_ref, q2_ref, p2_ref):
    qt = _nt_dot(wqt_ref[...], h2_ref[...]).astype(BF16)
    n_lt = qt.shape[1] // 128
    n_grp = PEER_KEYS // 8
    for hd in range(PEER_HEADS):
        s1 = _dot(sk_ref[2 * hd], qt[(2 * hd) * PEER_HALF:(2 * hd + 1) * PEER_HALF, :])
        s2 = _dot(sk_ref[2 * hd + 1], qt[(2 * hd + 1) * PEER_HALF:(2 * hd + 2) * PEER_HALF, :])
        for lt in range(n_lt):
            lanes = slice(lt * 128, (lt + 1) * 128)
            dst = (lt + 1) % n_lt
            rot = slice(dst * 128, (dst + 1) * 128)
            g1 = [s1[8 * i:8 * i + 8, lanes] for i in range(n_grp)]
            g2 = [s2[8 * i:8 * i + 8, lanes] for i in range(n_grp)]
            thr, p1, q2, p2 = _peer_select(g1, g2)
            thr_ref[hd, :, lanes] = jnp.concatenate(thr, axis=0)
            p1_ref[hd, :, lanes] = jnp.concatenate(p1, axis=0)
            q2_ref[hd, :, lanes] = jnp.concatenate(q2, axis=0).astype(BF16)
            p2_ref[hd, :, rot] = jnp.concatenate(p2, axis=0).astype(BF16)


def _peer_query(h2, wqt, sk, *, tm=512):
    n = h2.shape[0]
    big = jax.ShapeDtypeStruct((PEER_HEADS, PEER_KEYS, n), F32)
    half = jax.ShapeDtypeStruct((PEER_HEADS, PEER_KEYS, n), BF16)
    bspec = pl.BlockSpec((PEER_HEADS, PEER_KEYS, tm), lambda i: (0, 0, i))
    return pl.pallas_call(
        _peer_query_kernel,
        grid=(n // tm,),
        in_specs=[
            pl.BlockSpec((tm, D_MODEL), lambda i: (i, 0)),
            pl.BlockSpec((PEER_HEADS * PEER_QDIM, D_MODEL), lambda i: (0, 0)),
            pl.BlockSpec((2 * PEER_HEADS, PEER_KEYS, PEER_HALF), lambda i: (0, 0, 0)),
        ],
        out_specs=[bspec, bspec, bspec, bspec],
        out_shape=[big, big, half, half],
        compiler_params=_params(("arbitrary",)),
        name="peer_query",
    )(h2, wqt, sk)


def _peer_dense_kernel(h2_ref, x1_ref, mod_ref, u_ref, vt_ref, thr_ref, p1_ref, q2_ref, p2_ref,
                       o_ref, acc_ref, g_ref, p_ref, q2s_ref, p2s_ref, *, e1_per_step):
    step = pl.program_id(1)
    tm = g_ref.shape[1]
    n_lt = tm // 128
    rows = 16
    n_r = PEER_KEYS // rows

    def gate_pass():
        for c in range(e1_per_step):
            for lt in range(n_lt):
                lanes = slice(lt * 128, (lt + 1) * 128)
                rot = slice(((lt + 1) % n_lt) * 128, ((lt + 1) % n_lt + 1) * 128)
                g = [None] * n_r
                for hd in range(PEER_HEADS):
                    thr_b = jnp.broadcast_to(thr_ref[hd, c:c + 1, lanes], (rows, 128)).astype(BF16)
                    p1_b = jnp.broadcast_to(p1_ref[hd, c:c + 1, lanes], (rows, 128)).astype(BF16)
                    for r in range(n_r):
                        rs = slice(r * rows, (r + 1) * rows)
                        w = jnp.where(q2s_ref[hd, rs, lanes] <= thr_b, p2s_ref[hd, rs, rot] * p1_b,
                                      jnp.zeros((), BF16))
                        g[r] = w if g[r] is None else g[r] + w
                for r in range(n_r):
                    g_ref[c * PEER_KEYS + r * rows:c * PEER_KEYS + (r + 1) * rows, lanes] = g[r]

    @pl.when(step == 0)
    def _init():
        acc_ref[...] = jnp.zeros_like(acc_ref)
        q2s_ref[...] = q2_ref[...]
        p2s_ref[...] = p2_ref[...]

    pl.when(step >= 0)(gate_pass)

    a = _nt_dot(u_ref[...], h2_ref[...])
    act = (a + a * lax.erf(a * (2.0 ** -0.5))).astype(BF16)
    p_ref[...] = g_ref[...] * act
    acc_ref[...] += _dot(vt_ref[...], p_ref[...])

    @pl.when(step == pl.num_programs(1) - 1)
    def _fin():
        o_ref[...] = x1_ref[...] + mod_ref[5:6, :] * acc_ref[...].T


def _peer_dense(h2, x1, mod3, u, vt, thr, p1, q2, p2, *, layer, tm=512, e1_per_step=8):
    n = h2.shape[0]
    te = e1_per_step * PEER_KEYS
    tok = lambda i, j: (i, 0)
    big = pl.BlockSpec((PEER_HEADS, PEER_KEYS, tm), lambda i, j: (0, 0, i))
    rows = pl.BlockSpec((PEER_HEADS, e1_per_step, tm), lambda i, j: (0, j, i))
    return pl.pallas_call(
        functools.partial(_peer_dense_kernel, e1_per_step=e1_per_step),
        grid=(n // tm, PEER_EXPERTS // te),
        in_specs=[
            pl.BlockSpec((tm, D_MODEL), tok),
            pl.BlockSpec((tm, D_MODEL), tok),
            pl.BlockSpec((None, 6, D_MODEL), lambda i, j: ((i * tm) // DEC_SEQ, 0, 0)),
            pl.BlockSpec((None, te, D_MODEL), lambda i, j: (layer, j, 0)),
            pl.BlockSpec((None, D_MODEL, te), lambda i, j: (layer, 0, j)),
            rows, rows, big, big,
        ],
        out_specs=pl.BlockSpec((tm, D_MODEL), tok),
        out_shape=jax.ShapeDtypeStruct((n, D_MODEL), F32),
        scratch_shapes=[pltpu.VMEM((D_MODEL, tm), F32), pltpu.VMEM((te, tm), BF16), pltpu.VMEM((te, tm), BF16),
                        pltpu.VMEM((PEER_HEADS, PEER_KEYS, tm), BF16), pltpu.VMEM((PEER_HEADS, PEER_KEYS, tm), BF16)],
        compiler_params=_params(("arbitrary", "arbitrary")),
        name="peer_dense",
    )(h2, x1, mod3, u, vt, thr, p1, q2, p2)


def _block_ones(width, d):
    idx = jnp.arange(width) // d
    return (idx[:, None] == idx[None, :]).astype(BF16)


def _rope_angles(d):
    q4 = d // 4
    inv = ROPE_BASE ** (-jnp.arange(q4, dtype=F32) / q4)
    t = jnp.arange(DEC_SEQ)
    ar = (t // GRID_W).astype(F32)[:, None] * inv
    ac = (t % GRID_W).astype(F32)[:, None] * inv
    return jnp.concatenate([ar, ar, ac, ac], axis=-1)


def _rope_table(d, group, n_groups):
    ang = _rope_angles(d)
    q4 = d // 4
    first = (jnp.arange(d) % (2 * q4)) < q4
    cos = jnp.cos(ang)
    sin = jnp.sin(ang)
    sa = jnp.where(first[None, :], -sin, 0.0)
    sb = jnp.where(first[None, :], 0.0, sin)
    pad = group - d
    cos = jnp.pad(cos, ((0, 0), (pad, 0)), constant_values=1.0)
    sa = jnp.pad(sa, ((0, 0), (pad, 0)))
    sb = jnp.pad(sb, ((0, 0), (pad, 0)))
    return jnp.stack([jnp.tile(cos, (1, n_groups)), jnp.tile(sa, (1, n_groups)), jnp.tile(sb, (1, n_groups))])


def _rot_matrix(d):
    q4 = d // 4
    i = jnp.arange(d)
    first = (i % (2 * q4)) < q4
    src = jnp.where(first, i + q4, i - q4)
    sign = jnp.where(first, -1.0, 1.0)
    return (jnp.zeros((d, d), F32).at[src, i].set(sign)).astype(BF16)


def _pad_row(v, width):
    return jnp.pad(v, (0, width - v.shape[0]))


def kernel(x_prompt, x_sample, c, cache_a_k, cache_a_v, cache_b_k, cache_b_v, cache_c_k, cache_c_v, cache_d_ckv, cache_d_krope, c_ctx, ada_w, ada_b, norm1_g, norm2_g, w_in, a_lam, a_qk_g, a_subln_g, b_sink, b_qk_g, c_qk_g, d_qnorm_g, d_kvnorm_g, w_d_qb, w_d_kvb, d_qk_g, w_branch, w_out, peer_wq, peer_subkeys, peer_u, peer_v):
    x = jnp.concatenate([x_prompt.reshape(N_CTX, D_MODEL), x_sample.reshape(N_LAT, D_MODEL)], axis=0)
    cvec = jnp.concatenate([c_ctx[None, :], c, jnp.zeros((N_MODROWS - 1 - DEC_BATCH, D_MODEL), F32)], axis=0)
    mod_all = _modulation(cvec, ada_w, ada_b)

    e32 = _block_ones(256, A_QK)
    e64 = _block_ones(256, HEAD_DIM)
    e96 = _block_ones(GAIN_W, D_QK)
    tab_a = _rope_table(A_QK, A_QK, 2 * A_HEADS)
    tab_b = _rope_table(HEAD_DIM, HEAD_DIM, B_HEADS)
    tab_d = _rope_table(D_ROPE, D_QK, D_HEADS)
    ang_r = _rope_angles(D_ROPE)
    tab_r = jnp.stack([jnp.cos(ang_r), jnp.sin(ang_r)])
    rot_r = _rot_matrix(D_ROPE)

    u_all = peer_u.astype(BF16)
    vt_all = jnp.swapaxes(peer_v, 1, 2).astype(BF16)
    states = [[] for _ in range(8)]
    for l in range(DEPTH):
        lam_init = 0.8 - 0.6 * math.exp(-0.3 * l)
        mod3 = mod_all[l, :3].reshape(3, 6, D_MODEL)
        ws = jnp.pad(w_in[l, :, :SMALL_COLS], ((0, 0), (0, SMALL_PAD - SMALL_COLS))).astype(BF16)
        wg = w_in[l, :, SMALL_COLS:].astype(BF16)
        gains = jnp.stack([
            _pad_row(jnp.tile(a_qk_g[l, 0], 2 * A_HEADS), GAIN_W),
            _pad_row(jnp.tile(a_qk_g[l, 1], 2 * A_HEADS), GAIN_W),
            _pad_row(jnp.tile(b_qk_g[l, 0], B_HEADS), GAIN_W),
            _pad_row(jnp.tile(b_qk_g[l, 1], B_KV), GAIN_W),
            _pad_row(jnp.tile(c_qk_g[l, 0], C_HEADS), GAIN_W),
            _pad_row(jnp.tile(c_qk_g[l, 1], C_KV), GAIN_W),
            _pad_row(d_qnorm_g[l], GAIN_W),
            _pad_row(d_kvnorm_g[l], GAIN_W),
            jnp.tile(d_qk_g[l, 0], D_HEADS),
        ] + [jnp.zeros((GAIN_W,), F32)] * (GAIN_ROWS - 9))

        (h, aq, ak, av, bq, bk, bv, cq, ck, cv, dq,
         akt, avt, bkt, bvt, ckt, cvt, dckv, dkr) = _input_stage(
            x, mod3, norm1_g[l][None, :], ws, gains, e32, e64, e96, w_d_qb[l].astype(BF16), tab_a, tab_b, tab_d)

        for lst, s_ in zip(states, (
                akt[:N_CTX].reshape(BATCH, SEQ, A_HEADS, 2, A_QK), avt[:N_CTX].reshape(BATCH, SEQ, A_HEADS, A_V),
                bkt[:N_CTX].reshape(BATCH, SEQ, B_KV, HEAD_DIM), bvt[:N_CTX].reshape(BATCH, SEQ, B_KV, HEAD_DIM),
                ckt[:N_CTX].reshape(BATCH, SEQ, C_KV, HEAD_DIM), cvt[:N_CTX].reshape(BATCH, SEQ, C_KV, HEAD_DIM),
                dckv[:N_CTX].reshape(BATCH, SEQ, D_KV_LORA), dkr[:N_CTX].reshape(BATCH, SEQ, D_ROPE))):
            lst.append(s_)

        ckv_rows = jnp.concatenate([t for b in range(DEC_BATCH) for t in (
            cache_d_ckv[b, l], dckv[N_CTX + b * DEC_SEQ:N_CTX + (b + 1) * DEC_SEQ])] + [dckv[:N_CTX]], axis=0)
        kr_rows = jnp.concatenate([t for b in range(DEC_BATCH) for t in (
            cache_d_krope[b, l], dkr[N_CTX + b * DEC_SEQ:N_CTX + (b + 1) * DEC_SEQ])] + [dkr[:N_CTX]], axis=0)
        kd, vd = _mla_expand(ckv_rows, kr_rows, w_d_kvb[l].astype(BF16), d_qk_g[l, 1][None, :], rot_r, tab_r)

        def cached(cache, nh, d):
            cache = jnp.moveaxis(cache.reshape(DEC_BATCH, PAST_LEN, nh, d), 2, 0)
            return cache.reshape(nh, DEC_BATCH * PAST_LEN, d).astype(BF16)

        a_cache = (cached(cache_a_k[:, l], 2 * A_HEADS, A_QK), cached(cache_a_v[:, l], A_HEADS, A_V))
        b_cache = (cached(cache_b_k[:, l], B_KV, HEAD_DIM), cached(cache_b_v[:, l], B_KV, HEAD_DIM))
        c_cache = (cached(cache_c_k[:, l], C_KV, HEAD_DIM), cached(cache_c_v[:, l], C_KV, HEAD_DIM))

        ctx = dict(batch=BATCH, tq_rows=SEQ, tk_rows=SEQ, q_off=0, o_off=0, lam_init=lam_init, tq=SEQ)
        lat = dict(batch=DEC_BATCH, tq_rows=DEC_SEQ, tk_rows=DEC_SEQ, q_off=N_CTX, k_off=N_CTX, o_off=N_CTX,
                   lam_init=lam_init, tq=256)
        a_extra = (a_lam[l], a_subln_g[l][None, :])
        blank = jnp.zeros((N_TOK, BRANCH_W), F32)
        oa = _attention(aq, ak, av, None, a_extra, blank, k_off=0, g_q=4, g_k=4, g_v=2, dv=A_V, mode="diff", **ctx)
        oa = _attention(aq, ak, av, a_cache, a_extra, oa, g_q=4, g_k=4, g_v=2, dv=A_V, mode="diff", **lat)
        ob = _attention(bq, bk, bv, None, (b_sink[l],), blank, k_off=0, g_q=2, g_k=1, g_v=1, dv=HEAD_DIM,
                        mode="sink", **ctx)
        ob = _window_attention(bq, bk, bv, *b_cache, b_sink[l], ob, q_off=N_CTX, k_off=N_CTX, o_off=N_CTX)
        oc = _attention(cq, ck, cv, None, (), blank, k_off=0, g_q=2, g_k=1, g_v=1, dv=HEAD_DIM, mode="plain",
                        **ctx)
        oc = _attention(cq, ck, cv, c_cache, (), oc, g_q=2, g_k=1, g_v=1, dv=HEAD_DIM, mode="plain", **lat)
        od = _attention(dq, kd, vd, None, (), blank, k_off=DEC_BATCH * KV_LAT, g_q=2, g_k=2, g_v=2, dv=D_V,
                        mode="plain", **ctx)
        lat_d = dict(lat, tk_rows=KV_LAT, k_off=0)
        od = _attention(dq, kd, vd, None, (), od, g_q=2, g_k=2, g_v=2, dv=D_V, mode="plain", **lat_d)

        x1, h2 = _merge_stage(x, h, oa, ob, oc, od, mod3, wg, w_branch[l].astype(BF16), w_out[l].astype(BF16),
                              norm2_g[l][None, :])

        thr, p1, q2, p2 = _peer_query(
            h2, peer_wq[l].T.astype(BF16),
            peer_subkeys[l].reshape(2 * PEER_HEADS, PEER_KEYS, PEER_HALF).astype(BF16))
        x = _peer_dense(h2, x1, mod3, u_all, vt_all, thr, p1, q2, p2, layer=l)

    y_prompt = x[:N_CTX].reshape(BATCH, SEQ, D_MODEL)
    y_sample = x[N_CTX:].reshape(DEC_BATCH, DEC_SEQ, D_MODEL)
    return (y_prompt, y_sample) + tuple(jnp.stack(s_, axis=1) for s_ in states)
```

```python
import functools
import math

import jax
import jax.numpy as jnp
from jax import lax
from jax.experimental import pallas as pl
from jax.experimental.pallas import tpu as pltpu

F32 = jnp.float32
BF16 = jnp.bfloat16

D_MODEL = 1024
BATCH = 16
SEQ = 256
DEPTH = 2
DEC_BATCH = 2
DEC_SEQ = 4096
PAST_LEN = 512
GRID_W = 64
Q_BLOCK = 128
HEAD_DIM = 64
ROPE_BASE = 10000.0
EPS = 1e-6
NEG_INF = -1e30
A_HEADS = 4
A_QK = 32
A_V = 64
B_HEADS = 4
B_KV = 2
WINDOW = 128
C_HEADS = 4
C_KV = 2
D_HEADS = 4
D_Q_LORA = 256
D_KV_LORA = 128
D_NOPE = 64
D_ROPE = 32
D_V = 64
D_QK = D_NOPE + D_ROPE
N_BRANCH = 4
BRANCH_W = 256
PEER_HEADS = 8
PEER_KEYS = 128
PEER_EXPERTS = PEER_KEYS * PEER_KEYS
PEER_QDIM = 256
PEER_HALF = PEER_QDIM // 2
PEER_TOPK = 16

N_CTX = BATCH * SEQ
N_LAT = DEC_BATCH * DEC_SEQ
N_TOK = N_CTX + N_LAT
KV_LAT = PAST_LEN + DEC_SEQ
N_MODROWS = 8
SMALL_COLS = 2208
SMALL_PAD = 2304

VMEM_LIMIT = 56 * 1024 * 1024
EXHAUSTED = -3.0e38
LOG2E = math.log2(math.e)


def _params(sem, flags=None):
    return pltpu.CompilerParams(dimension_semantics=sem, vmem_limit_bytes=VMEM_LIMIT, flags=flags)


def _nt_dot(a, b):
    return lax.dot_general(a, b, (((1,), (1,)), ((), ())), preferred_element_type=F32)


def _dot(a, b):
    return jnp.dot(a, b, preferred_element_type=F32)


def _mod_kernel(c_ref, w_ref, b_ref, o_ref):
    c = c_ref[...]
    s = c * jax.nn.sigmoid(c)
    o_ref[...] = jnp.dot(s, w_ref[...], preferred_element_type=F32, precision=lax.Precision.HIGHEST) + b_ref[...]


def _modulation(cvec, ada_w, ada_b):
    depth = ada_w.shape[0]
    ncol = ada_w.shape[2] // D_MODEL
    return pl.pallas_call(
        _mod_kernel,
        grid=(depth, ncol),
        in_specs=[
            pl.BlockSpec((N_MODROWS, D_MODEL), lambda l, j: (0, 0)),
            pl.BlockSpec((None, D_MODEL, D_MODEL), lambda l, j: (l, 0, j)),
            pl.BlockSpec((None, 1, D_MODEL), lambda l, j: (l, 0, j)),
        ],
        out_specs=pl.BlockSpec((None, N_MODROWS, D_MODEL), lambda l, j: (l, 0, j)),
        out_shape=jax.ShapeDtypeStruct((depth, N_MODROWS, ada_w.shape[2]), F32),
        compiler_params=_params(("arbitrary", "arbitrary")),
        name="modulation",
    )(cvec, ada_w, ada_b.reshape(depth, 1, -1))


def _group_rms(x, ones_bd, gain, d):
    sq = x * x
    hi = sq.astype(BF16)
    lo = (sq - hi.astype(F32)).astype(BF16)
    ss = _dot(hi, ones_bd) + _dot(lo, ones_bd)
    return x * lax.rsqrt(ss * (1.0 / d) + EPS) * gain


def _row_rms(x, gain):
    ms = jnp.mean(x * x, axis=-1, keepdims=True)
    return x * lax.rsqrt(ms + EPS) * gain


def _rope(x, tab_ref, q4):
    w = x.shape[-1]
    return x * tab_ref[0] + pltpu.roll(x, w - q4, 1) * tab_ref[1] + pltpu.roll(x, q4, 1) * tab_ref[2]


GAIN_ROWS = 16
GAIN_W = D_HEADS * D_QK


def _in_kernel(x_ref, mod_ref, n1g_ref, ws_ref, gains_ref, e32_ref, e64_ref, e96_ref, wdq_ref,
               ta_ref, tb_ref, td_ref,
               h_ref, aq_ref, ak_ref, av_ref, bq_ref, bk_ref, bv_ref, cq_ref, ck_ref, cv_ref, dq_ref,
               akt_ref, avt_ref, bkt_ref, bvt_ref, ckt_ref, cvt_ref, dckv_ref, dkr_ref, *, tm, n_ctx):
    is_lat = pl.program_id(0) * tm >= n_ctx
    x = x_ref[...]
    h = _row_rms(x, n1g_ref[...]) * (1.0 + mod_ref[1:2, :]) + mod_ref[0:1, :]
    hb = h.astype(BF16)
    h_ref[...] = hb
    p = _dot(hb, ws_ref[...])

    def gain(r, w):
        return gains_ref[r:r + 1, 0:w]

    def heads_out(ref, val, nh, d):
        for i in range(nh):
            ref[i] = val[:, i * d:(i + 1) * d].astype(BF16)

    def maybe_rope(val, tab_ref, q4):
        return jnp.where(is_lat, _rope(val, tab_ref, q4), val)

    aq = maybe_rope(_group_rms(p[:, 0:256], e32_ref[...], gain(0, 256), A_QK), ta_ref, A_QK // 4)
    ak = maybe_rope(_group_rms(p[:, 256:512], e32_ref[...], gain(1, 256), A_QK), ta_ref, A_QK // 4)
    av = p[:, 512:768]
    heads_out(aq_ref, aq * (A_QK ** -0.5 * LOG2E), 2 * A_HEADS, A_QK)
    heads_out(ak_ref, ak, 2 * A_HEADS, A_QK)
    heads_out(av_ref, av, A_HEADS, A_V)
    akt_ref[...] = ak
    avt_ref[...] = av
    bq = maybe_rope(_group_rms(p[:, 768:1024], e64_ref[...], gain(2, 256), HEAD_DIM), tb_ref, HEAD_DIM // 4)
    bk = _group_rms(p[:, 1024:1152], e64_ref[0:128, 0:128], gain(3, 128), HEAD_DIM)
    bk = jnp.where(is_lat, bk * tb_ref[0, :, 0:128] + pltpu.roll(bk, 128 - 16, 1) * tb_ref[1, :, 0:128]
                   + pltpu.roll(bk, 16, 1) * tb_ref[2, :, 0:128], bk)
    bv = p[:, 1152:1280]
    heads_out(bq_ref, bq * (HEAD_DIM ** -0.5 * LOG2E), B_HEADS, HEAD_DIM)
    heads_out(bk_ref, bk, B_KV, HEAD_DIM)
    heads_out(bv_ref, bv, B_KV, HEAD_DIM)
    bkt_ref[...] = bk
    bvt_ref[...] = bv
    cq = maybe_rope(_group_rms(p[:, 1280:1536], e64_ref[...], gain(4, 256), HEAD_DIM), tb_ref, HEAD_DIM // 4)
    ck = _group_rms(p[:, 1536:1664], e64_ref[0:128, 0:128], gain(5, 128), HEAD_DIM)
    ck = jnp.where(is_lat, ck * tb_ref[0, :, 0:128] + pltpu.roll(ck, 128 - 16, 1) * tb_ref[1, :, 0:128]
                   + pltpu.roll(ck, 16, 1) * tb_ref[2, :, 0:128], ck)
    cv = p[:, 1664:1792]
    heads_out(cq_ref, cq * (HEAD_DIM ** -0.5 * LOG2E), C_HEADS, HEAD_DIM)
    heads_out(ck_ref, ck, C_KV, HEAD_DIM)
    heads_out(cv_ref, cv, C_KV, HEAD_DIM)
    ckt_ref[...] = ck
    cvt_ref[...] = cv
    dqa = _row_rms(p[:, 1792:2048], gain(6, 256))
    dq = _dot(dqa.astype(BF16), wdq_ref[...])
    dq = maybe_rope(_group_rms(dq, e96_ref[...], gain(8, GAIN_W), D_QK), td_ref, D_ROPE // 4)
    heads_out(dq_ref, dq * (D_QK ** -0.5 * LOG2E), D_HEADS, D_QK)
    dckv_ref[...] = _row_rms(p[:, 2048:2176], gain(7, 128))
    dkr_ref[...] = p[:, 2176:2208]


def _input_stage(x, mod3, n1g, ws, gains, e32, e64, e96, wdq, tab_a, tab_b, tab_d, *, tm=256):
    n = x.shape[0]
    nt = n // tm
    lat0 = N_CTX // tm
    pos_blocks = DEC_SEQ // tm

    def tok(i):
        return (i, 0)

    def pos(i):
        return (0, jnp.maximum(i - lat0, 0) % pos_blocks, 0)

    def const2(i):
        return (0, 0)

    def hm(i):
        return (0, i, 0)

    def hspec(nh, d):
        return pl.BlockSpec((nh, tm, d), hm)

    def hshape(nh, d):
        return jax.ShapeDtypeStruct((nh, n, d), BF16)

    def tspec(w):
        return pl.BlockSpec((tm, w), tok)

    def tshape(w):
        return jax.ShapeDtypeStruct((n, w), F32)

    in_specs = [
        pl.BlockSpec((tm, D_MODEL), tok),
        pl.BlockSpec((None, 6, D_MODEL), lambda i: ((i * tm) // DEC_SEQ, 0, 0)),
        pl.BlockSpec((1, D_MODEL), const2),
        pl.BlockSpec((D_MODEL, SMALL_PAD), const2),
        pl.BlockSpec((GAIN_ROWS, GAIN_W), const2),
        pl.BlockSpec((256, 256), const2),
        pl.BlockSpec((256, 256), const2),
        pl.BlockSpec((GAIN_W, GAIN_W), const2),
        pl.BlockSpec((D_Q_LORA, GAIN_W), const2),
        pl.BlockSpec((3, tm, 256), pos),
        pl.BlockSpec((3, tm, 256), pos),
        pl.BlockSpec((3, tm, GAIN_W), pos),
    ]
    out_specs = [
        pl.BlockSpec((tm, D_MODEL), tok),
        hspec(8, A_QK), hspec(8, A_QK), hspec(4, A_V),
        hspec(4, HEAD_DIM), hspec(2, HEAD_DIM), hspec(2, HEAD_DIM),
        hspec(4, HEAD_DIM), hspec(2, HEAD_DIM), hspec(2, HEAD_DIM),
        hspec(4, D_QK),
        tspec(256), tspec(256), tspec(128), tspec(128), tspec(128), tspec(128), tspec(128), tspec(D_ROPE),
    ]
    out_shape = [
        jax.ShapeDtypeStruct((n, D_MODEL), BF16),
        hshape(8, A_QK), hshape(8, A_QK), hshape(4, A_V),
        hshape(4, HEAD_DIM), hshape(2, HEAD_DIM), hshape(2, HEAD_DIM),
        hshape(4, HEAD_DIM), hshape(2, HEAD_DIM), hshape(2, HEAD_DIM),
        hshape(4, D_QK),
        tshape(256), tshape(256), tshape(128), tshape(128), tshape(128), tshape(128), tshape(128), tshape(D_ROPE),
    ]
    return pl.pallas_call(
        functools.partial(_in_kernel, tm=tm, n_ctx=N_CTX),
        grid=(nt,),
        in_specs=in_specs,
        out_specs=out_specs,
        out_shape=out_shape,
        compiler_params=_params(("arbitrary",)),
        name="input_stage",
    )(x, mod3, n1g, ws, gains, e32, e64, e96, wdq, tab_a, tab_b, tab_d)


def _mla_kernel(ckv_ref, kr_ref, wkvb_ref, g_ref, rot_ref, tab_ref, kd_ref, vd_ref, *, tm, lat_tiles, per_batch_tiles):
    i = pl.program_id(0)
    rotate = jnp.logical_and(i < lat_tiles, i % per_batch_tiles >= PAST_LEN // tm)
    kv = _dot(ckv_ref[...].astype(BF16), wkvb_ref[...])
    kr = kr_ref[...]
    ssr = jnp.sum(kr * kr, axis=-1, keepdims=True)
    z = kr * g_ref[:, D_NOPE:D_QK]
    zh = z.astype(BF16)
    zl = (z - zh.astype(F32)).astype(BF16)
    rot = _dot(zh, rot_ref[...]) + _dot(zl, rot_ref[...])
    rz = jnp.where(rotate, z * tab_ref[0] + rot * tab_ref[1], z)
    g_nope = g_ref[:, 0:D_NOPE]
    for hd in range(D_HEADS):
        kn = kv[:, hd * 128:hd * 128 + D_NOPE]
        r = lax.rsqrt((jnp.sum(kn * kn, axis=-1, keepdims=True) + ssr) * (1.0 / D_QK) + EPS)
        kd_ref[hd] = jnp.concatenate([kn * r * g_nope, rz * r], axis=-1).astype(BF16)
        vd_ref[hd] = kv[:, hd * 128 + D_NOPE:(hd + 1) * 128].astype(BF16)


def _mla_expand(ckv, kr, wkvb, g, rot, tab, *, tm=512):
    rows = ckv.shape[0]
    per_batch_tiles = KV_LAT // tm
    lat_tiles = DEC_BATCH * per_batch_tiles
    past_tiles = PAST_LEN // tm
    pos_blocks = DEC_SEQ // tm

    def pos(i):
        return (0, jnp.maximum(i % per_batch_tiles - past_tiles, 0) % pos_blocks, 0)

    return pl.pallas_call(
        functools.partial(_mla_kernel, tm=tm, lat_tiles=lat_tiles, per_batch_tiles=per_batch_tiles),
        grid=(rows // tm,),
        in_specs=[
            pl.BlockSpec((tm, D_KV_LORA), lambda i: (i, 0)),
            pl.BlockSpec((tm, D_ROPE), lambda i: (i, 0)),
            pl.BlockSpec((D_KV_LORA, D_HEADS * (D_NOPE + D_V)), lambda i: (0, 0)),
            pl.BlockSpec((1, D_QK), lambda i: (0, 0)),
            pl.BlockSpec((D_ROPE, D_ROPE), lambda i: (0, 0)),
            pl.BlockSpec((2, tm, D_ROPE), pos),
        ],
        out_specs=[
            pl.BlockSpec((D_HEADS, tm, D_QK), lambda i: (0, i, 0)),
            pl.BlockSpec((D_HEADS, tm, D_V), lambda i: (0, i, 0)),
        ],
        out_shape=[
            jax.ShapeDtypeStruct((D_HEADS, rows, D_QK), BF16),
            jax.ShapeDtypeStruct((D_HEADS, rows, D_V), BF16),
        ],
        compiler_params=_params(("arbitrary",)),
        name="mla_expand",
    )(ckv, kr, wkvb, g, rot, tab)


def _attn_kernel(*refs, g_q, g_k, g_v, dv, mode, lam_init, has_cache):
    refs = list(refs)
    q_ref, k_ref, v_ref = refs[:3]
    pos = 3
    sink_ref = lam_ref = sub_ref = None
    if has_cache:
        kc_ref, vc_ref = refs[pos], refs[pos + 1]
        pos += 2
    if mode == "sink":
        sink_ref = refs[pos]
        pos += 1
    elif mode == "diff":
        lam_ref, sub_ref = refs[pos], refs[pos + 1]
        pos += 2
    o_ref = refs[pos + 1]
    grp = pl.program_id(1)

    if has_cache:
        kall_ref, vall_ref = refs[pos + 2], refs[pos + 3]
        past = kc_ref.shape[1]

        @pl.when(pl.program_id(2) == 0)
        def _join():
            kall_ref[:, 0:past, :] = kc_ref[...]
            kall_ref[:, past:, :] = k_ref[...]
            vall_ref[:, 0:past, :] = vc_ref[...]
            vall_ref[:, past:, :] = v_ref[...]

        k_ref, v_ref = kall_ref, vall_ref

    def probs(g):
        s = _nt_dot(q_ref[g], k_ref[g * g_k // g_q])
        m = jnp.max(s, axis=-1, keepdims=True)
        if mode == "sink":
            sk = sink_ref[grp * g_q + g] * LOG2E
            m = jnp.maximum(m, sk)
        p = jnp.exp2(s - m)
        den = jnp.sum(p, axis=-1, keepdims=True)
        if mode == "sink":
            den = den + jnp.exp2(sk - m)
        return p, den

    if mode == "diff":
        lp = lam_ref[...]
        lam = (jnp.exp(jnp.sum(lp[0:1] * lp[1:2], axis=-1, keepdims=True))
               - jnp.exp(jnp.sum(lp[2:3] * lp[3:4], axis=-1, keepdims=True)) + lam_init)
        for i in range(g_q // 2):
            p1, den1 = probs(2 * i)
            p2, den2 = probs(2 * i + 1)
            comb = p1 * (1.0 / den1) - p2 * (lam / den2)
            d = _dot(comb.astype(BF16), v_ref[i * g_v // (g_q // 2)])
            o_ref[:, i * dv:(i + 1) * dv] = _row_rms(d, sub_ref[...]) * (1.0 - lam_init)
    else:
        for g in range(g_q):
            p, den = probs(g)
            o_ref[:, g * dv:(g + 1) * dv] = _dot(p.astype(BF16), v_ref[g * g_v // g_q]) * (1.0 / den)


def _attention(q, k, v, cache, extras, o_prev, *, batch, tq_rows, tk_rows, q_off, k_off, o_off,
               g_q, g_k, g_v, dv, mode, lam_init, tq):
    hq, _, dq = q.shape
    n_groups = hq // g_q
    g_out = g_q // 2 if mode == "diff" else g_q
    nq = tq_rows // tq
    tk = tk_rows
    in_specs = [
        pl.BlockSpec((g_q, tq, dq), lambda b, g, qi: (g, (q_off + b * tq_rows) // tq + qi, 0)),
        pl.BlockSpec((g_k, tk, dq), lambda b, g, qi: (g, (k_off + b * tk_rows) // tk, 0)),
        pl.BlockSpec((g_v, tk, dv), lambda b, g, qi: (g, (k_off + b * tk_rows) // tk, 0)),
    ]
    args = [q, k, v]
    if cache is not None:
        in_specs.append(pl.BlockSpec((g_k, PAST_LEN, dq), lambda b, g, qi: (g, b, 0)))
        in_specs.append(pl.BlockSpec((g_v, PAST_LEN, dv), lambda b, g, qi: (g, b, 0)))
        args += list(cache)
    if mode == "sink":
        in_specs.append(pl.BlockSpec(memory_space=pltpu.SMEM))
        args.append(extras[0])
    elif mode == "diff":
        in_specs.append(pl.BlockSpec((4, A_QK), lambda b, g, qi: (0, 0)))
        in_specs.append(pl.BlockSpec((1, dv), lambda b, g, qi: (0, 0)))
        args += list(extras)
    in_specs.append(pl.BlockSpec(memory_space=pl.ANY))
    aliases = {len(args): 0}
    args.append(o_prev)
    return pl.pallas_call(
        functools.partial(_attn_kernel, g_q=g_q, g_k=g_k, g_v=g_v, dv=dv, mode=mode, lam_init=lam_init,
                          has_cache=cache is not None),
        grid=(batch, n_groups, nq),
        in_specs=in_specs,
        out_specs=pl.BlockSpec((tq, g_out * dv), lambda b, g, qi: ((o_off + b * tq_rows) // tq + qi, g)),
        out_shape=jax.ShapeDtypeStruct(o_prev.shape, F32),
        scratch_shapes=([pltpu.VMEM((g_k, PAST_LEN + tk, dq), BF16), pltpu.VMEM((g_v, PAST_LEN + tk, dv), BF16)]
                        if cache is not None else []),
        input_output_aliases=aliases,
        compiler_params=_params(("arbitrary", "arbitrary", "arbitrary")),
        name="attention_" + mode,
    )(*args)


def _win_kernel(q_ref, k_ref, v_ref, kc_ref, vc_ref, sink_ref, prev_ref, o_ref, *, g_q, dv, tq, seq, band):
    del prev_ref
    grp = pl.program_id(1)
    qi = pl.program_id(2)
    start = pl.multiple_of(jnp.clip(qi * tq - WINDOW, 0, seq - band), WINDOW)
    k_ctx = kc_ref[...]
    v_ctx = vc_ref[...]
    k_band = k_ref[pl.ds(start, band), :]
    v_band = v_ref[pl.ds(start, band), :]
    qpos = qi * tq + lax.broadcasted_iota(jnp.int32, (tq, band), 0)
    kpos = start + lax.broadcasted_iota(jnp.int32, (tq, band), 1)
    valid = jnp.abs(kpos - qpos) <= WINDOW
    for g in range(g_q):
        q = q_ref[g]
        s_ctx = _nt_dot(q, k_ctx)
        s_band = jnp.where(valid, _nt_dot(q, k_band), NEG_INF)
        sk = sink_ref[grp * g_q + g] * LOG2E
        m = jnp.maximum(jnp.maximum(jnp.max(s_ctx, axis=-1, keepdims=True),
                                    jnp.max(s_band, axis=-1, keepdims=True)), sk)
        p_ctx = jnp.exp2(s_ctx - m)
        p_band = jnp.exp2(s_band - m)
        den = (jnp.sum(p_ctx, axis=-1, keepdims=True) + jnp.sum(p_band, axis=-1, keepdims=True)
               + jnp.exp2(sk - m))
        num = _dot(p_ctx.astype(BF16), v_ctx) + _dot(p_band.astype(BF16), v_band)
        o_ref[:, g * dv:(g + 1) * dv] = num * (1.0 / den)


def _window_attention(q, k, v, kc, vc, sink, o_prev, *, q_off, k_off, o_off, tq=256):
    hq, _, d = q.shape
    hk = k.shape[0]
    g_q = hq // hk
    band = tq + 2 * WINDOW
    nq = DEC_SEQ // tq
    return pl.pallas_call(
        functools.partial(_win_kernel, g_q=g_q, dv=d, tq=tq, seq=DEC_SEQ, band=band),
        grid=(DEC_BATCH, hk, nq),
        in_specs=[
            pl.BlockSpec((g_q, tq, d), lambda b, g, qi: (g, (q_off + b * DEC_SEQ) // tq + qi, 0)),
            pl.BlockSpec((None, DEC_SEQ, d), lambda b, g, qi: (g, k_off // DEC_SEQ + b, 0)),
            pl.BlockSpec((None, DEC_SEQ, d), lambda b, g, qi: (g, k_off // DEC_SEQ + b, 0)),
            pl.BlockSpec((None, PAST_LEN, d), lambda b, g, qi: (g, b, 0)),
            pl.BlockSpec((None, PAST_LEN, d), lambda b, g, qi: (g, b, 0)),
            pl.BlockSpec(memory_space=pltpu.SMEM),
            pl.BlockSpec(memory_space=pl.ANY),
        ],
        out_specs=pl.BlockSpec((tq, g_q * d), lambda b, g, qi: ((o_off + b * DEC_SEQ) // tq + qi, g)),
        out_shape=jax.ShapeDtypeStruct(o_prev.shape, F32),
        input_output_aliases={6: 0},
        compiler_params=_params(("arbitrary", "arbitrary", "arbitrary")),
        name="attention_window",
    )(q, k, v, kc, vc, sink, o_prev)


def _merge_kernel(x_ref, h_ref, oa_ref, ob_ref, oc_ref, od_ref, mod_ref, wg_ref, wb_ref, wo_ref, n2g_ref,
                  x1_ref, h2_ref):
    hb = h_ref[...]
    outs = (oa_ref, ob_ref, oc_ref, od_ref)
    acc = None
    for i in range(N_BRANCH):
        gt = jax.nn.sigmoid(_dot(hb, wg_ref[:, i * D_MODEL:(i + 1) * D_MODEL]))
        br = _dot(outs[i][...].astype(BF16), wb_ref[i])
        acc = gt * br if acc is None else acc + gt * br
    y = _dot(acc.astype(BF16), wo_ref[...])
    x1 = x_ref[...] + mod_ref[2:3, :] * y
    x1_ref[...] = x1
    h2 = _row_rms(x1, n2g_ref[...]) * (1.0 + mod_ref[4:5, :]) + mod_ref[3:4, :]
    h2_ref[...] = h2.astype(BF16)


def _merge_stage(x, h, oa, ob, oc, od, mod3, wg, wb, wo, n2g, *, tm=256):
    n = x.shape[0]

    def tok(i):
        return (i, 0)

    return pl.pallas_call(
        _merge_kernel,
        grid=(n // tm,),
        in_specs=[
            pl.BlockSpec((tm, D_MODEL), tok),
            pl.BlockSpec((tm, D_MODEL), tok),
            pl.BlockSpec((tm, BRANCH_W), tok),
            pl.BlockSpec((tm, BRANCH_W), tok),
            pl.BlockSpec((tm, BRANCH_W), tok),
            pl.BlockSpec((tm, BRANCH_W), tok),
            pl.BlockSpec((None, 6, D_MODEL), lambda i: ((i * tm) // DEC_SEQ, 0, 0)),
            pl.BlockSpec((D_MODEL, N_BRANCH * D_MODEL), lambda i: (0, 0)),
            pl.BlockSpec((N_BRANCH, BRANCH_W, D_MODEL), lambda i: (0, 0, 0)),
            pl.BlockSpec((D_MODEL, D_MODEL), lambda i: (0, 0)),
            pl.BlockSpec((1, D_MODEL), lambda i: (0, 0)),
        ],
        out_specs=[pl.BlockSpec((tm, D_MODEL), tok), pl.BlockSpec((tm, D_MODEL), tok)],
        out_shape=[jax.ShapeDtypeStruct((n, D_MODEL), F32), jax.ShapeDtypeStruct((n, D_MODEL), BF16)],
        compiler_params=_params(("arbitrary",)),
        name="merge_stage",
    )(x, h, oa, ob, oc, od, mod3, wg, wb, wo, n2g)


def _oddeven_mergesort_pairs(n):
    pairs = []

    def merge(lo, m, r):
        step = 2 * r
        if step < m:
            merge(lo, m, step)
            merge(lo + r, m, step)
            for i in range(lo + r, lo + m - r, step):
                pairs.append((i, i + r))
        else:
            pairs.append((lo, lo + r))

    def sort(lo, m):
        if m > 1:
            sort(lo, m // 2)
            sort(lo + m // 2, m // 2)
            merge(lo, m, 1)

    sort(0, n)
    return pairs


_SORT_PAIRS = _oddeven_mergesort_pairs(PEER_TOPK)


def _exchange(xs, i, j):
    xs[i], xs[j] = jnp.maximum(xs[i], xs[j]), jnp.minimum(xs[i], xs[j])


def _top16_of_groups(groups):
    n = len(groups)
    xs = list(groups)
    for i, j in _SORT_PAIRS:
        _exchange(xs, i, j)
    for shift in (4, 2, 1):
        xs = [jnp.maximum(xs[k], pltpu.roll(xs[n - 1 - k], shift, 0)) for k in range(n)]
        s = n // 2
        while s >= 1:
            for i in range(n):
                if not i & s:
                    _exchange(xs, i, i + s)
            s //= 2
    return xs


def _sublane_sum(x):
    for shift in (4, 2, 1):
        x = x + pltpu.roll(x, shift, 0)
    return x


def _sublane_spread(vals):
    row = lax.broadcasted_iota(jnp.int32, vals[0].shape, 0)
    out = vals[-1]
    for j in range(len(vals) - 2, -1, -1):
        out = jnp.where(row == j, vals[j], out)
    return out


def _peer_select(g1, g2):
    k = PEER_TOPK
    half = k // 2
    v1 = _top16_of_groups(g1)
    v2 = _top16_of_groups(g2)
    v2lo = _sublane_spread(v2[:half])
    v2hi = _sublane_spread(v2[half:])
    v1hi = _sublane_spread(v1[half:])
    cands = [v1[0] + v2lo, v1[0] + v2hi] + [v1[i] + v2lo for i in range(1, half)] + [v1hi + v2[0]]
    pad = jnp.full(cands[0].shape, EXHAUSTED, F32)
    top = _top16_of_groups(cands + [pad] * (k - len(cands)))
    tau = top[k - 1]
    z = None
    for c in cands:
        e = jnp.where(c >= tau, jnp.exp(c - top[0]), 0.0)
        z = e if z is None else z + e
    z = _sublane_sum(z)
    thr = [jnp.full(g1[0].shape, -1.0, F32) for _ in g1]
    q2 = [jnp.full(g2[0].shape, float(k), F32) for _ in g2]
    for a in range(k - 1, -1, -1):
        cnt = _sublane_sum(jnp.where(v1[a] + v2lo >= tau, 1.0, 0.0) + jnp.where(v1[a] + v2hi >= tau, 1.0, 0.0))
        thr = [jnp.where(x == v1[a], cnt - 1.0, t) for x, t in zip(g1, thr)]
        q2 = [jnp.where(x == v2[a], float(a), q) for x, q in zip(g2, q2)]
    p1 = [jnp.exp(x - v1[0]) for x in g1]
    scale = 0.5 / z
    p2 = [jnp.exp(x - v2[0]) * scale for x in g2]
    return thr, p1, q2, p2


def _peer_query_kernel(h2_ref, wqt_ref, sk_ref, thr_ref, p1_ref, q2_ref, p2_ref):
    qt = _nt_dot(wqt_ref[...], h2_ref[...]).astype(BF16)
    n_lt = qt.shape[1] // 128
    n_grp = PEER_KEYS // 8
    for hd in range(PEER_HEADS):
        s1 = _dot(sk_ref[2 * hd], qt[(2 * hd) * PEER_HALF:(2 * hd + 1) * PEER_HALF, :])
        s2 = _dot(sk_ref[2 * hd + 1], qt[(2 * hd + 1) * PEER_HALF:(2 * hd + 2) * PEER_HALF, :])
        for lt in range(n_lt):
            lanes = slice(lt * 128, (lt + 1) * 128)
            dst = (lt + 1) % n_lt
            rot = slice(dst * 128, (dst + 1) * 128)
            g1 = [s1[8 * i:8 * i + 8, lanes] for i in range(n_grp)]
            g2 = [s2[8 * i:8 * i + 8, lanes] for i in range(n_grp)]
            thr, p1, q2, p2 = _peer_select(g1, g2)
            thr_ref[hd, :, lanes] = jnp.concatenate(thr, axis=0)
            p1_ref[hd, :, lanes] = jnp.concatenate(p1, axis=0)
            q2_ref[hd, :, lanes] = jnp.concatenate(q2, axis=0).astype(BF16)
            p2_ref[hd, :, rot] = jnp.concatenate(p2, axis=0).astype(BF16)


def _peer_query(h2, wqt, sk, *, tm=512):
    n = h2.shape[0]
    big = jax.ShapeDtypeStruct((PEER_HEADS, PEER_KEYS, n), F32)
    half = jax.ShapeDtypeStruct((PEER_HEADS, PEER_KEYS, n), BF16)
    bspec = pl.BlockSpec((PEER_HEADS, PEER_KEYS, tm), lambda i: (0, 0, i))
    return pl.pallas_call(
        _peer_query_kernel,
        grid=(n // tm,),
        in_specs=[
            pl.BlockSpec((tm, D_MODEL), lambda i: (i, 0)),
            pl.BlockSpec((PEER_HEADS * PEER_QDIM, D_MODEL), lambda i: (0, 0)),
            pl.BlockSpec((2 * PEER_HEADS, PEER_KEYS, PEER_HALF), lambda i: (0, 0, 0)),
        ],
        out_specs=[bspec, bspec, bspec, bspec],
        out_shape=[big, big, half, half],
        compiler_params=_params(("arbitrary",)),
        name="peer_query",
    )(h2, wqt, sk)


def _peer_dense_kernel(h2_ref, x1_ref, mod_ref, u_ref, vt_ref, thr_ref, p1_ref, q2_ref, p2_ref,
                       o_ref, acc_ref, g_ref, p_ref, q2s_ref, p2s_ref, *, e1_per_step):
    step = pl.program_id(1)
    tm = g_ref.shape[1]
    n_lt = tm // 128
    rows = 16
    n_r = PEER_KEYS // rows

    def gate_pass():
        cb, rb = 4, 4
        for lt in range(n_lt):
            lanes = slice(lt * 128, (lt + 1) * 128)
            rot = slice(((lt + 1) % n_lt) * 128, ((lt + 1) % n_lt + 1) * 128)
            for c0 in range(0, e1_per_step, cb):
                for r0 in range(0, n_r, rb):
                    g = [[None] * rb for _ in range(cb)]
                    for hd in range(PEER_HEADS):
                        q2v = [q2s_ref[hd, (r0 + i) * rows:(r0 + i + 1) * rows, lanes] for i in range(rb)]
                        p2v = [p2s_ref[hd, (r0 + i) * rows:(r0 + i + 1) * rows, rot] for i in range(rb)]
                        for ci in range(cb):
                            c = c0 + ci
                            thr_b = jnp.broadcast_to(thr_ref[hd, c:c + 1, lanes], (rows, 128)).astype(BF16)
                            p1_b = jnp.broadcast_to(p1_ref[hd, c:c + 1, lanes], (rows, 128)).astype(BF16)
                            for i in range(rb):
                                w = jnp.where(q2v[i] <= thr_b, p2v[i] * p1_b, jnp.zeros((), BF16))
                                g[ci][i] = w if g[ci][i] is None else g[ci][i] + w
                    for ci in range(cb):
                        for i in range(rb):
                            row = (c0 + ci) * PEER_KEYS + (r0 + i) * rows
                            g_ref[row:row + rows, lanes] = g[ci][i]

    @pl.when(step == 0)
    def _init():
        acc_ref[...] = jnp.zeros_like(acc_ref)
        q2s_ref[...] = q2_ref[...]
        p2s_ref[...] = p2_ref[...]

    pl.when(step >= 0)(gate_pass)

    a = _nt_dot(u_ref[...], h2_ref[...])
    act = (a + a * lax.erf(a * (2.0 ** -0.5))).astype(BF16)
    p_ref[...] = g_ref[...] * act
    acc_ref[...] += _dot(vt_ref[...], p_ref[...])

    @pl.when(step == pl.num_programs(1) - 1)
    def _fin():
        o_ref[...] = x1_ref[...] + mod_ref[5:6, :] * acc_ref[...].T


def _peer_dense(h2, x1, mod3, u, vt, thr, p1, q2, p2, *, layer, tm=512, e1_per_step=16):
    n = h2.shape[0]
    te = e1_per_step * PEER_KEYS
    tok = lambda i, j: (i, 0)
    big = pl.BlockSpec((PEER_HEADS, PEER_KEYS, tm), lambda i, j: (0, 0, i))
    rows = pl.BlockSpec((PEER_HEADS, e1_per_step, tm), lambda i, j: (0, j, i))
    return pl.pallas_call(
        functools.partial(_peer_dense_kernel, e1_per_step=e1_per_step),
        grid=(n // tm, PEER_EXPERTS // te),
        in_specs=[
            pl.BlockSpec((tm, D_MODEL), tok),
            pl.BlockSpec((tm, D_MODEL), tok),
            pl.BlockSpec((None, 6, D_MODEL), lambda i, j: ((i * tm) // DEC_SEQ, 0, 0)),
            pl.BlockSpec((None, te, D_MODEL), lambda i, j: (layer, j, 0)),
            pl.BlockSpec((None, D_MODEL, te), lambda i, j: (layer, 0, j)),
            rows, rows, big, big,
        ],
        out_specs=pl.BlockSpec((tm, D_MODEL), tok),
        out_shape=jax.ShapeDtypeStruct((n, D_MODEL), F32),
        scratch_shapes=[pltpu.VMEM((D_MODEL, tm), F32), pltpu.VMEM((te, tm), BF16), pltpu.VMEM((te, tm), BF16),
                        pltpu.VMEM((PEER_HEADS, PEER_KEYS, tm), BF16), pltpu.VMEM((PEER_HEADS, PEER_KEYS, tm), BF16)],
        compiler_params=_params(("arbitrary", "arbitrary")),
        name="peer_dense",
    )(h2, x1, mod3, u, vt, thr, p1, q2, p2)


def _block_ones(width, d):
    idx = jnp.arange(width) // d
    return (idx[:, None] == idx[None, :]).astype(BF16)


def _rope_angles(d):
    q4 = d // 4
    inv = ROPE_BASE ** (-jnp.arange(q4, dtype=F32) / q4)
    t = jnp.arange(DEC_SEQ)
    ar = (t // GRID_W).astype(F32)[:, None] * inv
    ac = (t % GRID_W).astype(F32)[:, None] * inv
    return jnp.concatenate([ar, ar, ac, ac], axis=-1)


def _rope_table(d, group, n_groups):
    ang = _rope_angles(d)
    q4 = d // 4
    first = (jnp.arange(d) % (2 * q4)) < q4
    cos = jnp.cos(ang)
    sin = jnp.sin(ang)
    sa = jnp.where(first[None, :], -sin, 0.0)
    sb = jnp.where(first[None, :], 0.0, sin)
    pad = group - d
    cos = jnp.pad(cos, ((0, 0), (pad, 0)), constant_values=1.0)
    sa = jnp.pad(sa, ((0, 0), (pad, 0)))
    sb = jnp.pad(sb, ((0, 0), (pad, 0)))
    return jnp.stack([jnp.tile(cos, (1, n_groups)), jnp.tile(sa, (1, n_groups)), jnp.tile(sb, (1, n_groups))])


def _rot_matrix(d):
    q4 = d // 4
    i = jnp.arange(d)
    first = (i % (2 * q4)) < q4
    src = jnp.where(first, i + q4, i - q4)
    sign = jnp.where(first, -1.0, 1.0)
    return (jnp.zeros((d, d), F32).at[src, i].set(sign)).astype(BF16)


def _pad_row(v, width):
    return jnp.pad(v, (0, width - v.shape[0]))


def kernel(x_prompt, x_sample, c, cache_a_k, cache_a_v, cache_b_k, cache_b_v, cache_c_k, cache_c_v, cache_d_ckv, cache_d_krope, c_ctx, ada_w, ada_b, norm1_g, norm2_g, w_in, a_lam, a_qk_g, a_subln_g, b_sink, b_qk_g, c_qk_g, d_qnorm_g, d_kvnorm_g, w_d_qb, w_d_kvb, d_qk_g, w_branch, w_out, peer_wq, peer_subkeys, peer_u, peer_v):
    x = jnp.concatenate([x_prompt.reshape(N_CTX, D_MODEL), x_sample.reshape(N_LAT, D_MODEL)], axis=0)
    cvec = jnp.concatenate([c_ctx[None, :], c, jnp.zeros((N_MODROWS - 1 - DEC_BATCH, D_MODEL), F32)], axis=0)
    mod_all = _modulation(cvec, ada_w, ada_b)

    e32 = _block_ones(256, A_QK)
    e64 = _block_ones(256, HEAD_DIM)
    e96 = _block_ones(GAIN_W, D_QK)
    tab_a = _rope_table(A_QK, A_QK, 2 * A_HEADS)
    tab_b = _rope_table(HEAD_DIM, HEAD_DIM, B_HEADS)
    tab_d = _rope_table(D_ROPE, D_QK, D_HEADS)
    ang_r = _rope_angles(D_ROPE)
    tab_r = jnp.stack([jnp.cos(ang_r), jnp.sin(ang_r)])
    rot_r = _rot_matrix(D_ROPE)

    u_all = peer_u.astype(BF16)
    vt_all = jnp.swapaxes(peer_v, 1, 2).astype(BF16)
    states = [[] for _ in range(8)]
    for l in range(DEPTH):
        lam_init = 0.8 - 0.6 * math.exp(-0.3 * l)
        mod3 = mod_all[l, :3].reshape(3, 6, D_MODEL)
        ws = jnp.pad(w_in[l, :, :SMALL_COLS], ((0, 0), (0, SMALL_PAD - SMALL_COLS))).astype(BF16)
        wg = w_in[l, :, SMALL_COLS:].astype(BF16)
        gains = jnp.stack([
            _pad_row(jnp.tile(a_qk_g[l, 0], 2 * A_HEADS), GAIN_W),
            _pad_row(jnp.tile(a_qk_g[l, 1], 2 * A_HEADS), GAIN_W),
            _pad_row(jnp.tile(b_qk_g[l, 0], B_HEADS), GAIN_W),
            _pad_row(jnp.tile(b_qk_g[l, 1], B_KV), GAIN_W),
            _pad_row(jnp.tile(c_qk_g[l, 0], C_HEADS), GAIN_W),
            _pad_row(jnp.tile(c_qk_g[l, 1], C_KV), GAIN_W),
            _pad_row(d_qnorm_g[l], GAIN_W),
            _pad_row(d_kvnorm_g[l], GAIN_W),
            jnp.tile(d_qk_g[l, 0], D_HEADS),
        ] + [jnp.zeros((GAIN_W,), F32)] * (GAIN_ROWS - 9))

        (h, aq, ak, av, bq, bk, bv, cq, ck, cv, dq,
         akt, avt, bkt, bvt, ckt, cvt, dckv, dkr) = _input_stage(
            x, mod3, norm1_g[l][None, :], ws, gains, e32, e64, e96, w_d_qb[l].astype(BF16), tab_a, tab_b, tab_d)

        for lst, s_ in zip(states, (
                akt[:N_CTX].reshape(BATCH, SEQ, A_HEADS, 2, A_QK), avt[:N_CTX].reshape(BATCH, SEQ, A_HEADS, A_V),
                bkt[:N_CTX].reshape(BATCH, SEQ, B_KV, HEAD_DIM), bvt[:N_CTX].reshape(BATCH, SEQ, B_KV, HEAD_DIM),
                ckt[:N_CTX].reshape(BATCH, SEQ, C_KV, HEAD_DIM), cvt[:N_CTX].reshape(BATCH, SEQ, C_KV, HEAD_DIM),
                dckv[:N_CTX].reshape(BATCH, SEQ, D_KV_LORA), dkr[:N_CTX].reshape(BATCH, SEQ, D_ROPE))):
            lst.append(s_)

        ckv_rows = jnp.concatenate([t for b in range(DEC_BATCH) for t in (
            cache_d_ckv[b, l], dckv[N_CTX + b * DEC_SEQ:N_CTX + (b + 1) * DEC_SEQ])] + [dckv[:N_CTX]], axis=0)
        kr_rows = jnp.concatenate([t for b in range(DEC_BATCH) for t in (
            cache_d_krope[b, l], dkr[N_CTX + b * DEC_SEQ:N_CTX + (b + 1) * DEC_SEQ])] + [dkr[:N_CTX]], axis=0)
        kd, vd = _mla_expand(ckv_rows, kr_rows, w_d_kvb[l].astype(BF16), d_qk_g[l, 1][None, :], rot_r, tab_r)

        def cached(cache, nh, d):
            cache = jnp.moveaxis(cache.reshape(DEC_BATCH, PAST_LEN, nh, d), 2, 0)
            return cache.reshape(nh, DEC_BATCH * PAST_LEN, d).astype(BF16)

        a_cache = (cached(cache_a_k[:, l], 2 * A_HEADS, A_QK), cached(cache_a_v[:, l], A_HEADS, A_V))
        b_cache = (cached(cache_b_k[:, l], B_KV, HEAD_DIM), cached(cache_b_v[:, l], B_KV, HEAD_DIM))
        c_cache = (cached(cache_c_k[:, l], C_KV, HEAD_DIM), cached(cache_c_v[:, l], C_KV, HEAD_DIM))

        ctx = dict(batch=BATCH, tq_rows=SEQ, tk_rows=SEQ, q_off=0, o_off=0, lam_init=lam_init, tq=SEQ)
        lat = dict(batch=DEC_BATCH, tq_rows=DEC_SEQ, tk_rows=DEC_SEQ, q_off=N_CTX, k_off=N_CTX, o_off=N_CTX,
                   lam_init=lam_init, tq=256)
        a_extra = (a_lam[l], a_subln_g[l][None, :])
        blank = jnp.zeros((N_TOK, BRANCH_W), F32)
        oa = _attention(aq, ak, av, None, a_extra, blank, k_off=0, g_q=4, g_k=4, g_v=2, dv=A_V, mode="diff", **ctx)
        oa = _attention(aq, ak, av, a_cache, a_extra, oa, g_q=4, g_k=4, g_v=2, dv=A_V, mode="diff", **lat)
        ob = _attention(bq, bk, bv, None, (b_sink[l],), blank, k_off=0, g_q=2, g_k=1, g_v=1, dv=HEAD_DIM,
                        mode="sink", **ctx)
        ob = _window_attention(bq, bk, bv, *b_cache, b_sink[l], ob, q_off=N_CTX, k_off=N_CTX, o_off=N_CTX)
        oc = _attention(cq, ck, cv, None, (), blank, k_off=0, g_q=2, g_k=1, g_v=1, dv=HEAD_DIM, mode="plain",
                        **ctx)
        oc = _attention(cq, ck, cv, c_cache, (), oc, g_q=2, g_k=1, g_v=1, dv=HEAD_DIM, mode="plain", **lat)
        od = _attention(dq, kd, vd, None, (), blank, k_off=DEC_BATCH * KV_LAT, g_q=2, g_k=2, g_v=2, dv=D_V,
                        mode="plain", **ctx)
        lat_d = dict(lat, tk_rows=KV_LAT, k_off=0)
        od = _attention(dq, kd, vd, None, (), od, g_q=2, g_k=2, g_v=2, dv=D_V, mode="plain", **lat_d)

        x1, h2 = _merge_stage(x, h, oa, ob, oc, od, mod3, wg, w_branch[l].astype(BF16), w_out[l].astype(BF16),
                              norm2_g[l][None, :])

        thr, p1, q2, p2 = _peer_query(
            h2, peer_wq[l].T.astype(BF16),
            peer_subkeys[l].reshape(2 * PEER_HEADS, PEER_KEYS, PEER_HALF).astype(BF16))
        x = _peer_dense(h2, x1, mod3, u_all, vt_all, thr, p1, q2, p2, layer=l)

    y_prompt = x[:N_CTX].reshape(BATCH, SEQ, D_MODEL)
    y_sample = x[N_CTX:].reshape(DEC_BATCH, DEC_SEQ, D_MODEL)
    return (y_prompt, y_sample) + tuple(jnp.stack(s_, axis=1) for s_ in states)
```

```python
import functools
import math

import jax
import jax.numpy as jnp
from jax import lax
from jax.experimental import pallas as pl
from jax.experimental.pallas import tpu as pltpu

F32 = jnp.float32
BF16 = jnp.bfloat16

D_MODEL = 1024
BATCH = 16
SEQ = 256
DEPTH = 2
DEC_BATCH = 2
DEC_SEQ = 4096
PAST_LEN = 512
GRID_W = 64
Q_BLOCK = 128
HEAD_DIM = 64
ROPE_BASE = 10000.0
EPS = 1e-6
NEG_INF = -1e30
A_HEADS = 4
A_QK = 32
A_V = 64
B_HEADS = 4
B_KV = 2
WINDOW = 128
C_HEADS = 4
C_KV = 2
D_HEADS = 4
D_Q_LORA = 256
D_KV_LORA = 128
D_NOPE = 64
D_ROPE = 32
D_V = 64
D_QK = D_NOPE + D_ROPE
N_BRANCH = 4
BRANCH_W = 256
PEER_HEADS = 8
PEER_KEYS = 128
PEER_EXPERTS = PEER_KEYS * PEER_KEYS
PEER_QDIM = 256
PEER_HALF = PEER_QDIM // 2
PEER_TOPK = 16

N_CTX = BATCH * SEQ
N_LAT = DEC_BATCH * DEC_SEQ
N_TOK = N_CTX + N_LAT
KV_LAT = PAST_LEN + DEC_SEQ
N_MODROWS = 8
SMALL_COLS = 2208
SMALL_PAD = 2304

VMEM_LIMIT = 56 * 1024 * 1024
EXHAUSTED = -3.0e38
LOG2E = math.log2(math.e)


def _params(sem, flags=None):
    return pltpu.CompilerParams(dimension_semantics=sem, vmem_limit_bytes=VMEM_LIMIT, flags=flags)


def _nt_dot(a, b):
    return lax.dot_general(a, b, (((1,), (1,)), ((), ())), preferred_element_type=F32)


def _dot(a, b):
    return jnp.dot(a, b, preferred_element_type=F32)


def _mod_kernel(c_ref, w_ref, b_ref, o_ref):
    c = c_ref[...]
    s = c * jax.nn.sigmoid(c)
    o_ref[...] = jnp.dot(s, w_ref[...], preferred_element_type=F32, precision=lax.Precision.HIGHEST) + b_ref[...]


def _modulation(cvec, ada_w, ada_b):
    depth = ada_w.shape[0]
    ncol = ada_w.shape[2] // D_MODEL
    return pl.pallas_call(
        _mod_kernel,
        grid=(depth, ncol),
        in_specs=[
            pl.BlockSpec((N_MODROWS, D_MODEL), lambda l, j: (0, 0)),
            pl.BlockSpec((None, D_MODEL, D_MODEL), lambda l, j: (l, 0, j)),
            pl.BlockSpec((None, 1, D_MODEL), lambda l, j: (l, 0, j)),
        ],
        out_specs=pl.BlockSpec((None, N_MODROWS, D_MODEL), lambda l, j: (l, 0, j)),
        out_shape=jax.ShapeDtypeStruct((depth, N_MODROWS, ada_w.shape[2]), F32),
        compiler_params=_params(("arbitrary", "arbitrary")),
        name="modulation",
    )(cvec, ada_w, ada_b.reshape(depth, 1, -1))


def _group_rms(x, ones_bd, gain, d):
    sq = x * x
    hi = sq.astype(BF16)
    lo = (sq - hi.astype(F32)).astype(BF16)
    ss = _dot(hi, ones_bd) + _dot(lo, ones_bd)
    return x * lax.rsqrt(ss * (1.0 / d) + EPS) * gain


def _row_rms(x, gain):
    ms = jnp.mean(x * x, axis=-1, keepdims=True)
    return x * lax.rsqrt(ms + EPS) * gain


def _rope(x, tab_ref, q4):
    w = x.shape[-1]
    return x * tab_ref[0] + pltpu.roll(x, w - q4, 1) * tab_ref[1] + pltpu.roll(x, q4, 1) * tab_ref[2]


GAIN_ROWS = 16
GAIN_W = D_HEADS * D_QK


def _in_kernel(x_ref, mod_ref, n1g_ref, ws_ref, gains_ref, e32_ref, e64_ref, e96_ref, wdq_ref,
               ta_ref, tb_ref, td_ref,
               h_ref, aq_ref, ak_ref, av_ref, bq_ref, bk_ref, bv_ref, cq_ref, ck_ref, cv_ref, dq_ref,
               akt_ref, avt_ref, bkt_ref, bvt_ref, ckt_ref, cvt_ref, dckv_ref, dkr_ref, *, tm, n_ctx):
    is_lat = pl.program_id(0) * tm >= n_ctx
    x = x_ref[...]
    h = _row_rms(x, n1g_ref[...]) * (1.0 + mod_ref[1:2, :]) + mod_ref[0:1, :]
    hb = h.astype(BF16)
    h_ref[...] = hb
    p = _dot(hb, ws_ref[...])

    def gain(r, w):
        return gains_ref[r:r + 1, 0:w]

    def heads_out(ref, val, nh, d):
        for i in range(nh):
            ref[i] = val[:, i * d:(i + 1) * d].astype(BF16)

    def maybe_rope(val, tab_ref, q4):
        return jnp.where(is_lat, _rope(val, tab_ref, q4), val)

    aq = maybe_rope(_group_rms(p[:, 0:256], e32_ref[...], gain(0, 256), A_QK), ta_ref, A_QK // 4)
    ak = maybe_rope(_group_rms(p[:, 256:512], e32_ref[...], gain(1, 256), A_QK), ta_ref, A_QK // 4)
    av = p[:, 512:768]
    heads_out(aq_ref, aq * (A_QK ** -0.5 * LOG2E), 2 * A_HEADS, A_QK)
    heads_out(ak_ref, ak, 2 * A_HEADS, A_QK)
    heads_out(av_ref, av, A_HEADS, A_V)
    akt_ref[...] = ak
    avt_ref[...] = av
    bq = maybe_rope(_group_rms(p[:, 768:1024], e64_ref[...], gain(2, 256), HEAD_DIM), tb_ref, HEAD_DIM // 4)
    bk = _group_rms(p[:, 1024:1152], e64_ref[0:128, 0:128], gain(3, 128), HEAD_DIM)
    bk = jnp.where(is_lat, bk * tb_ref[0, :, 0:128] + pltpu.roll(bk, 128 - 16, 1) * tb_ref[1, :, 0:128]
                   + pltpu.roll(bk, 16, 1) * tb_ref[2, :, 0:128], bk)
    bv = p[:, 1152:1280]
    heads_out(bq_ref, bq * (HEAD_DIM ** -0.5 * LOG2E), B_HEADS, HEAD_DIM)
    heads_out(bk_ref, bk, B_KV, HEAD_DIM)
    heads_out(bv_ref, bv, B_KV, HEAD_DIM)
    bkt_ref[...] = bk
    bvt_ref[...] = bv
    cq = maybe_rope(_group_rms(p[:, 1280:1536], e64_ref[...], gain(4, 256), HEAD_DIM), tb_ref, HEAD_DIM // 4)
    ck = _group_rms(p[:, 1536:1664], e64_ref[0:128, 0:128], gain(5, 128), HEAD_DIM)
    ck = jnp.where(is_lat, ck * tb_ref[0, :, 0:128] + pltpu.roll(ck, 128 - 16, 1) * tb_ref[1, :, 0:128]
                   + pltpu.roll(ck, 16, 1) * tb_ref[2, :, 0:128], ck)
    cv = p[:, 1664:1792]
    heads_out(cq_ref, cq * (HEAD_DIM ** -0.5 * LOG2E), C_HEADS, HEAD_DIM)
    heads_out(ck_ref, ck, C_KV, HEAD_DIM)
    heads_out(cv_ref, cv, C_KV, HEAD_DIM)
    ckt_ref[...] = ck
    cvt_ref[...] = cv
    dqa = _row_rms(p[:, 1792:2048], gain(6, 256))
    dq = _dot(dqa.astype(BF16), wdq_ref[...])
    dq = maybe_rope(_group_rms(dq, e96_ref[...], gain(8, GAIN_W), D_QK), td_ref, D_ROPE // 4)
    heads_out(dq_ref, dq * (D_QK ** -0.5 * LOG2E), D_HEADS, D_QK)
    dckv_ref[...] = _row_rms(p[:, 2048:2176], gain(7, 128))
    dkr_ref[...] = p[:, 2176:2208]


def _input_stage(x, mod3, n1g, ws, gains, e32, e64, e96, wdq, tab_a, tab_b, tab_d, *, tm=256):
    n = x.shape[0]
    nt = n // tm
    lat0 = N_CTX // tm
    pos_blocks = DEC_SEQ // tm

    def tok(i):
        return (i, 0)

    def pos(i):
        return (0, jnp.maximum(i - lat0, 0) % pos_blocks, 0)

    def const2(i):
        return (0, 0)

    def hm(i):
        return (0, i, 0)

    def hspec(nh, d):
        return pl.BlockSpec((nh, tm, d), hm)

    def hshape(nh, d):
        return jax.ShapeDtypeStruct((nh, n, d), BF16)

    def tspec(w):
        return pl.BlockSpec((tm, w), tok)

    def tshape(w):
        return jax.ShapeDtypeStruct((n, w), F32)

    in_specs = [
        pl.BlockSpec((tm, D_MODEL), tok),
        pl.BlockSpec((None, 6, D_MODEL), lambda i: ((i * tm) // DEC_SEQ, 0, 0)),
        pl.BlockSpec((1, D_MODEL), const2),
        pl.BlockSpec((D_MODEL, SMALL_PAD), const2),
        pl.BlockSpec((GAIN_ROWS, GAIN_W), const2),
        pl.BlockSpec((256, 256), const2),
        pl.BlockSpec((256, 256), const2),
        pl.BlockSpec((GAIN_W, GAIN_W), const2),
        pl.BlockSpec((D_Q_LORA, GAIN_W), const2),
        pl.BlockSpec((3, tm, 256), pos),
        pl.BlockSpec((3, tm, 256), pos),
        pl.BlockSpec((3, tm, GAIN_W), pos),
    ]
    out_specs = [
        pl.BlockSpec((tm, D_MODEL), tok),
        hspec(8, A_QK), hspec(8, A_QK), hspec(4, A_V),
        hspec(4, HEAD_DIM), hspec(2, HEAD_DIM), hspec(2, HEAD_DIM),
        hspec(4, HEAD_DIM), hspec(2, HEAD_DIM), hspec(2, HEAD_DIM),
        hspec(4, D_QK),
        tspec(256), tspec(256), tspec(128), tspec(128), tspec(128), tspec(128), tspec(128), tspec(D_ROPE),
    ]
    out_shape = [
        jax.ShapeDtypeStruct((n, D_MODEL), BF16),
        hshape(8, A_QK), hshape(8, A_QK), hshape(4, A_V),
        hshape(4, HEAD_DIM), hshape(2, HEAD_DIM), hshape(2, HEAD_DIM),
        hshape(4, HEAD_DIM), hshape(2, HEAD_DIM), hshape(2, HEAD_DIM),
        hshape(4, D_QK),
        tshape(256), tshape(256), tshape(128), tshape(128), tshape(128), tshape(128), tshape(128), tshape(D_ROPE),
    ]
    return pl.pallas_call(
        functools.partial(_in_kernel, tm=tm, n_ctx=N_CTX),
        grid=(nt,),
        in_specs=in_specs,
        out_specs=out_specs,
        out_shape=out_shape,
        compiler_params=_params(("arbitrary",)),
        name="input_stage",
    )(x, mod3, n1g, ws, gains, e32, e64, e96, wdq, tab_a, tab_b, tab_d)


def _mla_kernel(ckv_ref, kr_ref, wkvb_ref, g_ref, rot_ref, tab_ref, kd_ref, vd_ref, *, tm, lat_tiles, per_batch_tiles):
    i = pl.program_id(0)
    rotate = jnp.logical_and(i < lat_tiles, i % per_batch_tiles >= PAST_LEN // tm)
    kv = _dot(ckv_ref[...].astype(BF16), wkvb_ref[...])
    kr = kr_ref[...]
    ssr = jnp.sum(kr * kr, axis=-1, keepdims=True)
    z = kr * g_ref[:, D_NOPE:D_QK]
    zh = z.astype(BF16)
    zl = (z - zh.astype(F32)).astype(BF16)
    rot = _dot(zh, rot_ref[...]) + _dot(zl, rot_ref[...])
    rz = jnp.where(rotate, z * tab_ref[0] + rot * tab_ref[1], z)
    g_nope = g_ref[:, 0:D_NOPE]
    for hd in range(D_HEADS):
        kn = kv[:, hd * 128:hd * 128 + D_NOPE]
        r = lax.rsqrt((jnp.sum(kn * kn, axis=-1, keepdims=True) + ssr) * (1.0 / D_QK) + EPS)
        kd_ref[hd] = jnp.concatenate([kn * r * g_nope, rz * r], axis=-1).astype(BF16)
        vd_ref[hd] = kv[:, hd * 128 + D_NOPE:(hd + 1) * 128].astype(BF16)


def _mla_expand(ckv, kr, wkvb, g, rot, tab, *, tm=512):
    rows = ckv.shape[0]
    per_batch_tiles = KV_LAT // tm
    lat_tiles = DEC_BATCH * per_batch_tiles
    past_tiles = PAST_LEN // tm
    pos_blocks = DEC_SEQ // tm

    def pos(i):
        return (0, jnp.maximum(i % per_batch_tiles - past_tiles, 0) % pos_blocks, 0)

    return pl.pallas_call(
        functools.partial(_mla_kernel, tm=tm, lat_tiles=lat_tiles, per_batch_tiles=per_batch_tiles),
        grid=(rows // tm,),
        in_specs=[
            pl.BlockSpec((tm, D_KV_LORA), lambda i: (i, 0)),
            pl.BlockSpec((tm, D_ROPE), lambda i: (i, 0)),
            pl.BlockSpec((D_KV_LORA, D_HEADS * (D_NOPE + D_V)), lambda i: (0, 0)),
            pl.BlockSpec((1, D_QK), lambda i: (0, 0)),
            pl.BlockSpec((D_ROPE, D_ROPE), lambda i: (0, 0)),
            pl.BlockSpec((2, tm, D_ROPE), pos),
        ],
        out_specs=[
            pl.BlockSpec((D_HEADS, tm, D_QK), lambda i: (0, i, 0)),
            pl.BlockSpec((D_HEADS, tm, D_V), lambda i: (0, i, 0)),
        ],
        out_shape=[
            jax.ShapeDtypeStruct((D_HEADS, rows, D_QK), BF16),
            jax.ShapeDtypeStruct((D_HEADS, rows, D_V), BF16),
        ],
        compiler_params=_params(("arbitrary",)),
        name="mla_expand",
    )(ckv, kr, wkvb, g, rot, tab)


def _attn_kernel(*refs, g_q, g_k, g_v, dv, mode, lam_init, has_cache):
    refs = list(refs)
    q_ref, k_ref, v_ref = refs[:3]
    pos = 3
    sink_ref = lam_ref = sub_ref = None
    if has_cache:
        kc_ref, vc_ref = refs[pos], refs[pos + 1]
        pos += 2
    if mode == "sink":
        sink_ref = refs[pos]
        pos += 1
    elif mode == "diff":
        lam_ref, sub_ref = refs[pos], refs[pos + 1]
        pos += 2
    o_ref = refs[pos + 1]
    grp = pl.program_id(1)

    if has_cache:
        kall_ref, vall_ref = refs[pos + 2], refs[pos + 3]
        past = kc_ref.shape[1]

        @pl.when(pl.program_id(2) == 0)
        def _join():
            kall_ref[:, 0:past, :] = kc_ref[...]
            kall_ref[:, past:, :] = k_ref[...]
            vall_ref[:, 0:past, :] = vc_ref[...]
            vall_ref[:, past:, :] = v_ref[...]

        k_ref, v_ref = kall_ref, vall_ref

    def probs(g):
        s = _nt_dot(q_ref[g], k_ref[g * g_k // g_q])
        m = jnp.max(s, axis=-1, keepdims=True)
        if mode == "sink":
            sk = sink_ref[grp * g_q + g] * LOG2E
            m = jnp.maximum(m, sk)
        p = jnp.exp2(s - m)
        den = jnp.sum(p, axis=-1, keepdims=True)
        if mode == "sink":
            den = den + jnp.exp2(sk - m)
        return p, den

    if mode == "diff":
        lp = lam_ref[...]
        lam = (jnp.exp(jnp.sum(lp[0:1] * lp[1:2], axis=-1, keepdims=True))
               - jnp.exp(jnp.sum(lp[2:3] * lp[3:4], axis=-1, keepdims=True)) + lam_init)
        for i in range(g_q // 2):
            p1, den1 = probs(2 * i)
            p2, den2 = probs(2 * i + 1)
            comb = p1 * (1.0 / den1) - p2 * (lam / den2)
            d = _dot(comb.astype(BF16), v_ref[i * g_v // (g_q // 2)])
            o_ref[:, i * dv:(i + 1) * dv] = _row_rms(d, sub_ref[...]) * (1.0 - lam_init)
    else:
        for g in range(g_q):
            p, den = probs(g)
            o_ref[:, g * dv:(g + 1) * dv] = _dot(p.astype(BF16), v_ref[g * g_v // g_q]) * (1.0 / den)


def _attention(q, k, v, cache, extras, o_prev, *, batch, tq_rows, tk_rows, q_off, k_off, o_off,
               g_q, g_k, g_v, dv, mode, lam_init, tq):
    hq, _, dq = q.shape
    n_groups = hq // g_q
    g_out = g_q // 2 if mode == "diff" else g_q
    nq = tq_rows // tq
    tk = tk_rows
    in_specs = [
        pl.BlockSpec((g_q, tq, dq), lambda b, g, qi: (g, (q_off + b * tq_rows) // tq + qi, 0)),
        pl.BlockSpec((g_k, tk, dq), lambda b, g, qi: (g, (k_off + b * tk_rows) // tk, 0)),
        pl.BlockSpec((g_v, tk, dv), lambda b, g, qi: (g, (k_off + b * tk_rows) // tk, 0)),
    ]
    args = [q, k, v]
    if cache is not None:
        in_specs.append(pl.BlockSpec((g_k, PAST_LEN, dq), lambda b, g, qi: (g, b, 0)))
        in_specs.append(pl.BlockSpec((g_v, PAST_LEN, dv), lambda b, g, qi: (g, b, 0)))
        args += list(cache)
    if mode == "sink":
        in_specs.append(pl.BlockSpec(memory_space=pltpu.SMEM))
        args.append(extras[0])
    elif mode == "diff":
        in_specs.append(pl.BlockSpec((4, A_QK), lambda b, g, qi: (0, 0)))
        in_specs.append(pl.BlockSpec((1, dv), lambda b, g, qi: (0, 0)))
        args += list(extras)
    in_specs.append(pl.BlockSpec(memory_space=pl.ANY))
    aliases = {len(args): 0}
    args.append(o_prev)
    return pl.pallas_call(
        functools.partial(_attn_kernel, g_q=g_q, g_k=g_k, g_v=g_v, dv=dv, mode=mode, lam_init=lam_init,
                          has_cache=cache is not None),
        grid=(batch, n_groups, nq),
        in_specs=in_specs,
        out_specs=pl.BlockSpec((tq, g_out * dv), lambda b, g, qi: ((o_off + b * tq_rows) // tq + qi, g)),
        out_shape=jax.ShapeDtypeStruct(o_prev.shape, F32),
        scratch_shapes=([pltpu.VMEM((g_k, PAST_LEN + tk, dq), BF16), pltpu.VMEM((g_v, PAST_LEN + tk, dv), BF16)]
                        if cache is not None else []),
        input_output_aliases=aliases,
        compiler_params=_params(("arbitrary", "arbitrary", "arbitrary")),
        name="attention_" + mode,
    )(*args)


def _win_kernel(q_ref, k_ref, v_ref, kc_ref, vc_ref, sink_ref, prev_ref, o_ref, *, g_q, dv, tq, seq, band):
    del prev_ref
    grp = pl.program_id(1)
    qi = pl.program_id(2)
    start = pl.multiple_of(jnp.clip(qi * tq - WINDOW, 0, seq - band), WINDOW)
    k_ctx = kc_ref[...]
    v_ctx = vc_ref[...]
    k_band = k_ref[pl.ds(start, band), :]
    v_band = v_ref[pl.ds(start, band), :]
    qpos = qi * tq + lax.broadcasted_iota(jnp.int32, (tq, band), 0)
    kpos = start + lax.broadcasted_iota(jnp.int32, (tq, band), 1)
    valid = jnp.abs(kpos - qpos) <= WINDOW
    for g in range(g_q):
        q = q_ref[g]
        s_ctx = _nt_dot(q, k_ctx)
        s_band = jnp.where(valid, _nt_dot(q, k_band), NEG_INF)
        sk = sink_ref[grp * g_q + g] * LOG2E
        m = jnp.maximum(jnp.maximum(jnp.max(s_ctx, axis=-1, keepdims=True),
                                    jnp.max(s_band, axis=-1, keepdims=True)), sk)
        p_ctx = jnp.exp2(s_ctx - m)
        p_band = jnp.exp2(s_band - m)
        den = (jnp.sum(p_ctx, axis=-1, keepdims=True) + jnp.sum(p_band, axis=-1, keepdims=True)
               + jnp.exp2(sk - m))
        num = _dot(p_ctx.astype(BF16), v_ctx) + _dot(p_band.astype(BF16), v_band)
        o_ref[:, g * dv:(g + 1) * dv] = num * (1.0 / den)


def _window_attention(q, k, v, kc, vc, sink, o_prev, *, q_off, k_off, o_off, tq=256):
    hq, _, d = q.shape
    hk = k.shape[0]
    g_q = hq // hk
    band = tq + 2 * WINDOW
    nq = DEC_SEQ // tq
    return pl.pallas_call(
        functools.partial(_win_kernel, g_q=g_q, dv=d, tq=tq, seq=DEC_SEQ, band=band),
        grid=(DEC_BATCH, hk, nq),
        in_specs=[
            pl.BlockSpec((g_q, tq, d), lambda b, g, qi: (g, (q_off + b * DEC_SEQ) // tq + qi, 0)),
            pl.BlockSpec((None, DEC_SEQ, d), lambda b, g, qi: (g, k_off // DEC_SEQ + b, 0)),
            pl.BlockSpec((None, DEC_SEQ, d), lambda b, g, qi: (g, k_off // DEC_SEQ + b, 0)),
            pl.BlockSpec((None, PAST_LEN, d), lambda b, g, qi: (g, b, 0)),
            pl.BlockSpec((None, PAST_LEN, d), lambda b, g, qi: (g, b, 0)),
            pl.BlockSpec(memory_space=pltpu.SMEM),
            pl.BlockSpec(memory_space=pl.ANY),
        ],
        out_specs=pl.BlockSpec((tq, g_q * d), lambda b, g, qi: ((o_off + b * DEC_SEQ) // tq + qi, g)),
        out_shape=jax.ShapeDtypeStruct(o_prev.shape, F32),
        input_output_aliases={6: 0},
        compiler_params=_params(("arbitrary", "arbitrary", "arbitrary")),
        name="attention_window",
    )(q, k, v, kc, vc, sink, o_prev)


def _merge_kernel(x_ref, h_ref, oa_ref, ob_ref, oc_ref, od_ref, mod_ref, wg_ref, wb_ref, wo_ref, n2g_ref,
                  x1_ref, h2_ref):
    hb = h_ref[...]
    outs = (oa_ref, ob_ref, oc_ref, od_ref)
    acc = None
    for i in range(N_BRANCH):
        gt = jax.nn.sigmoid(_dot(hb, wg_ref[:, i * D_MODEL:(i + 1) * D_MODEL]))
        br = _dot(outs[i][...].astype(BF16), wb_ref[i])
        acc = gt * br if acc is None else acc + gt * br
    y = _dot(acc.astype(BF16), wo_ref[...])
    x1 = x_ref[...] + mod_ref[2:3, :] * y
    x1_ref[...] = x1
    h2 = _row_rms(x1, n2g_ref[...]) * (1.0 + mod_ref[4:5, :]) + mod_ref[3:4, :]
    h2_ref[...] = h2.astype(BF16)


def _merge_stage(x, h, oa, ob, oc, od, mod3, wg, wb, wo, n2g, *, tm=256):
    n = x.shape[0]

    def tok(i):
        return (i, 0)

    return pl.pallas_call(
        _merge_kernel,
        grid=(n // tm,),
        in_specs=[
            pl.BlockSpec((tm, D_MODEL), tok),
            pl.BlockSpec((tm, D_MODEL), tok),
            pl.BlockSpec((tm, BRANCH_W), tok),
            pl.BlockSpec((tm, BRANCH_W), tok),
            pl.BlockSpec((tm, BRANCH_W), tok),
            pl.BlockSpec((tm, BRANCH_W), tok),
            pl.BlockSpec((None, 6, D_MODEL), lambda i: ((i * tm) // DEC_SEQ, 0, 0)),
            pl.BlockSpec((D_MODEL, N_BRANCH * D_MODEL), lambda i: (0, 0)),
            pl.BlockSpec((N_BRANCH, BRANCH_W, D_MODEL), lambda i: (0, 0, 0)),
            pl.BlockSpec((D_MODEL, D_MODEL), lambda i: (0, 0)),
            pl.BlockSpec((1, D_MODEL), lambda i: (0, 0)),
        ],
        out_specs=[pl.BlockSpec((tm, D_MODEL), tok), pl.BlockSpec((tm, D_MODEL), tok)],
        out_shape=[jax.ShapeDtypeStruct((n, D_MODEL), F32), jax.ShapeDtypeStruct((n, D_MODEL), BF16)],
        compiler_params=_params(("arbitrary",)),
        name="merge_stage",
    )(x, h, oa, ob, oc, od, mod3, wg, wb, wo, n2g)


def _oddeven_mergesort_pairs(n):
    pairs = []

    def merge(lo, m, r):
        step = 2 * r
        if step < m:
            merge(lo, m, step)
            merge(lo + r, m, step)
            for i in range(lo + r, lo + m - r, step):
                pairs.append((i, i + r))
        else:
            pairs.append((lo, lo + r))

    def sort(lo, m):
        if m > 1:
            sort(lo, m // 2)
            sort(lo + m // 2, m // 2)
            merge(lo, m, 1)

    sort(0, n)
    return pairs


_SORT_PAIRS = _oddeven_mergesort_pairs(PEER_TOPK)


def _exchange(xs, i, j):
    xs[i], xs[j] = jnp.maximum(xs[i], xs[j]), jnp.minimum(xs[i], xs[j])


def _top16_of_groups(groups):
    n = len(groups)
    xs = list(groups)
    for i, j in _SORT_PAIRS:
        _exchange(xs, i, j)
    for shift in (4, 2, 1):
        xs = [jnp.maximum(xs[k], pltpu.roll(xs[n - 1 - k], shift, 0)) for k in range(n)]
        s = n // 2
        while s >= 1:
            for i in range(n):
                if not i & s:
                    _exchange(xs, i, i + s)
            s //= 2
    return xs


def _sublane_sum(x):
    for shift in (4, 2, 1):
        x = x + pltpu.roll(x, shift, 0)
    return x


def _sublane_spread(vals):
    row = lax.broadcasted_iota(jnp.int32, vals[0].shape, 0)
    out = vals[-1]
    for j in range(len(vals) - 2, -1, -1):
        out = jnp.where(row == j, vals[j], out)
    return out


def _peer_select(g1, g2):
    k = PEER_TOPK
    half = k // 2
    v1 = _top16_of_groups(g1)
    v2 = _top16_of_groups(g2)
    v2lo = _sublane_spread(v2[:half])
    v2hi = _sublane_spread(v2[half:])
    v1hi = _sublane_spread(v1[half:])
    cands = [v1[0] + v2lo, v1[0] + v2hi] + [v1[i] + v2lo for i in range(1, half)] + [v1hi + v2[0]]
    pad = jnp.full(cands[0].shape, EXHAUSTED, F32)
    top = _top16_of_groups(cands + [pad] * (k - len(cands)))
    tau = top[k - 1]
    z = None
    for c in cands:
        e = jnp.where(c >= tau, jnp.exp(c - top[0]), 0.0)
        z = e if z is None else z + e
    z = _sublane_sum(z)
    thr = [jnp.full(g1[0].shape, -1.0, F32) for _ in g1]
    q2 = [jnp.full(g2[0].shape, float(k), F32) for _ in g2]
    for a in range(k - 1, -1, -1):
        cnt = _sublane_sum(jnp.where(v1[a] + v2lo >= tau, 1.0, 0.0) + jnp.where(v1[a] + v2hi >= tau, 1.0, 0.0))
        thr = [jnp.where(x == v1[a], cnt - 1.0, t) for x, t in zip(g1, thr)]
        q2 = [jnp.where(x == v2[a], float(a), q) for x, q in zip(g2, q2)]
    p1 = [jnp.exp(x - v1[0]) for x in g1]
    scale = 0.5 / z
    p2 = [jnp.exp(x - v2[0]) * scale for x in g2]
    return thr, p1, q2, p2


def _peer_query_kernel(h2_ref, wqt_ref, sk_ref, thr_ref, p1_ref, q2_ref, p2_ref):
    qt = _nt_dot(wqt_ref[...], h2_ref[...]).astype(BF16)
    n_lt = qt.shape[1] // 128
    n_grp = PEER_KEYS // 8
    for hd in range(PEER_HEADS):
        s1 = _dot(sk_ref[2 * hd], qt[(2 * hd) * PEER_HALF:(2 * hd + 1) * PEER_HALF, :])
        s2 = _dot(sk_ref[2 * hd + 1], qt[(2 * hd + 1) * PEER_HALF:(2 * hd + 2) * PEER_HALF, :])
        for lt in range(n_lt):
            lanes = slice(lt * 128, (lt + 1) * 128)
            dst = (lt + 1) % n_lt
            rot = slice(dst * 128, (dst + 1) * 128)
            g1 = [s1[8 * i:8 * i + 8, lanes] for i in range(n_grp)]
            g2 = [s2[8 * i:8 * i + 8, lanes] for i in range(n_grp)]
            thr, p1, q2, p2 = _peer_select(g1, g2)
            thr_ref[hd, :, lanes] = jnp.concatenate(thr, axis=0)
            p1_ref[hd, :, lanes] = jnp.concatenate(p1, axis=0)
            q2_ref[hd, :, lanes] = jnp.concatenate(q2, axis=0).astype(BF16)
            p2_ref[hd, :, rot] = jnp.concatenate(p2, axis=0).astype(BF16)


def _peer_query(h2, wqt, sk, *, tm=512):
    n = h2.shape[0]
    big = jax.ShapeDtypeStruct((PEER_HEADS, PEER_KEYS, n), F32)
    half = jax.ShapeDtypeStruct((PEER_HEADS, PEER_KEYS, n), BF16)
    bspec = pl.BlockSpec((PEER_HEADS, PEER_KEYS, tm), lambda i: (0, 0, i))
    return pl.pallas_call(
        _peer_query_kernel,
        grid=(n // tm,),
        in_specs=[
            pl.BlockSpec((tm, D_MODEL), lambda i: (i, 0)),
            pl.BlockSpec((PEER_HEADS * PEER_QDIM, D_MODEL), lambda i: (0, 0)),
            pl.BlockSpec((2 * PEER_HEADS, PEER_KEYS, PEER_HALF), lambda i: (0, 0, 0)),
        ],
        out_specs=[bspec, bspec, bspec, bspec],
        out_shape=[big, big, half, half],
        compiler_params=_params(("arbitrary",)),
        name="peer_query",
    )(h2, wqt, sk)


def _peer_dense_kernel(h2_ref, x1_ref, mod_ref, u_ref, vt_ref, thr_ref, p1_ref, q2_ref, p2_ref,
                       o_ref, acc_ref, g_ref, p_ref, q2s_ref, p2s_ref, *, e1_per_step):
    step = pl.program_id(1)
    tm = g_ref.shape[1]
    n_lt = tm // 128
    rows = 16
    n_r = PEER_KEYS // rows

    def gate_pass():
        for c in range(e1_per_step):
            for lt in range(n_lt):
                lanes = slice(lt * 128, (lt + 1) * 128)
                rot = slice(((lt + 1) % n_lt) * 128, ((lt + 1) % n_lt + 1) * 128)
                g = [None] * n_r
                for hd in range(PEER_HEADS):
                    thr_b = jnp.broadcast_to(thr_ref[hd, c:c + 1, lanes], (rows, 128)).astype(BF16)
                    p1_b = jnp.broadcast_to(p1_ref[hd, c:c + 1, lanes], (rows, 128)).astype(BF16)
                    for r in range(n_r):
                        rs = slice(r * rows, (r + 1) * rows)
                        w = jnp.where(q2s_ref[hd, rs, lanes] <= thr_b, p2s_ref[hd, rs, rot] * p1_b,
                                      jnp.zeros((), BF16))
                        g[r] = w if g[r] is None else g[r] + w
                for r in range(n_r):
                    g_ref[c * PEER_KEYS + r * rows:c * PEER_KEYS + (r + 1) * rows, lanes] = g[r]

    @pl.when(step == 0)
    def _init():
        acc_ref[...] = jnp.zeros_like(acc_ref)
        q2s_ref[...] = q2_ref[...]
        p2s_ref[...] = p2_ref[...]

    pl.when(step >= 0)(gate_pass)

    a = _nt_dot(u_ref[...], h2_ref[...])
    act = (a + a * lax.erf(a * (2.0 ** -0.5))).astype(BF16)
    p_ref[...] = g_ref[...] * act
    acc_ref[...] += _dot(vt_ref[...], p_ref[...])

    @pl.when(step == pl.num_programs(1) - 1)
    def _fin():
        o_ref[...] = x1_ref[...] + mod_ref[5:6, :] * acc_ref[...].T


def _peer_dense(h2, x1, mod3, u, vt, thr, p1, q2, p2, *, layer, tm=512, e1_per_step=16):
    n = h2.shape[0]
    te = e1_per_step * PEER_KEYS
    tok = lambda i, j: (i, 0)
    big = pl.BlockSpec((PEER_HEADS, PEER_KEYS, tm), lambda i, j: (0, 0, i))
    rows = pl.BlockSpec((PEER_HEADS, e1_per_step, tm), lambda i, j: (0, j, i))
    return pl.pallas_call(
        functools.partial(_peer_dense_kernel, e1_per_step=e1_per_step),
        grid=(n // tm, PEER_EXPERTS // te),
        in_specs=[
            pl.BlockSpec((tm, D_MODEL), tok),
            pl.BlockSpec((tm, D_MODEL), tok),
            pl.BlockSpec((None, 6, D_MODEL), lambda i, j: ((i * tm) // DEC_SEQ, 0, 0)),
            pl.BlockSpec((None, te, D_MODEL), lambda i, j: (layer, j, 0)),
            pl.BlockSpec((None, D_MODEL, te), lambda i, j: (layer, 0, j)),
            rows, rows, big, big,
        ],
        out_specs=pl.BlockSpec((tm, D_MODEL), tok),
        out_shape=jax.ShapeDtypeStruct((n, D_MODEL), F32),
        scratch_shapes=[pltpu.VMEM((D_MODEL, tm), F32), pltpu.VMEM((te, tm), BF16), pltpu.VMEM((te, tm), BF16),
                        pltpu.VMEM((PEER_HEADS, PEER_KEYS, tm), BF16), pltpu.VMEM((PEER_HEADS, PEER_KEYS, tm), BF16)],
        compiler_params=_params(("arbitrary", "arbitrary")),
        name="peer_dense",
    )(h2, x1, mod3, u, vt, thr, p1, q2, p2)


def _block_ones(width, d):
    idx = jnp.arange(width) // d
    return (idx[:, None] == idx[None, :]).astype(BF16)


def _rope_angles(d):
    q4 = d // 4
    inv = ROPE_BASE ** (-jnp.arange(q4, dtype=F32) / q4)
    t = jnp.arange(DEC_SEQ)
    ar = (t // GRID_W).astype(F32)[:, None] * inv
    ac = (t % GRID_W).astype(F32)[:, None] * inv
    return jnp.concatenate([ar, ar, ac, ac], axis=-1)


def _rope_table(d, group, n_groups):
    ang = _rope_angles(d)
    q4 = d // 4
    first = (jnp.arange(d) % (2 * q4)) < q4
    cos = jnp.cos(ang)
    sin = jnp.sin(ang)
    sa = jnp.where(first[None, :], -sin, 0.0)
    sb = jnp.where(first[None, :], 0.0, sin)
    pad = group - d
    cos = jnp.pad(cos, ((0, 0), (pad, 0)), constant_values=1.0)
    sa = jnp.pad(sa, ((0, 0), (pad, 0)))
    sb = jnp.pad(sb, ((0, 0), (pad, 0)))
    return jnp.stack([jnp.tile(cos, (1, n_groups)), jnp.tile(sa, (1, n_groups)), jnp.tile(sb, (1, n_groups))])


def _rot_matrix(d):
    q4 = d // 4
    i = jnp.arange(d)
    first = (i % (2 * q4)) < q4
    src = jnp.where(first, i + q4, i - q4)
    sign = jnp.where(first, -1.0, 1.0)
    return (jnp.zeros((d, d), F32).at[src, i].set(sign)).astype(BF16)


def _pad_row(v, width):
    return jnp.pad(v, (0, width - v.shape[0]))


def kernel(x_prompt, x_sample, c, cache_a_k, cache_a_v, cache_b_k, cache_b_v, cache_c_k, cache_c_v, cache_d_ckv, cache_d_krope, c_ctx, ada_w, ada_b, norm1_g, norm2_g, w_in, a_lam, a_qk_g, a_subln_g, b_sink, b_qk_g, c_qk_g, d_qnorm_g, d_kvnorm_g, w_d_qb, w_d_kvb, d_qk_g, w_branch, w_out, peer_wq, peer_subkeys, peer_u, peer_v):
    x = jnp.concatenate([x_prompt.reshape(N_CTX, D_MODEL), x_sample.reshape(N_LAT, D_MODEL)], axis=0)
    cvec = jnp.concatenate([c_ctx[None, :], c, jnp.zeros((N_MODROWS - 1 - DEC_BATCH, D_MODEL), F32)], axis=0)
    mod_all = _modulation(cvec, ada_w, ada_b)

    e32 = _block_ones(256, A_QK)
    e64 = _block_ones(256, HEAD_DIM)
    e96 = _block_ones(GAIN_W, D_QK)
    tab_a = _rope_table(A_QK, A_QK, 2 * A_HEADS)
    tab_b = _rope_table(HEAD_DIM, HEAD_DIM, B_HEADS)
    tab_d = _rope_table(D_ROPE, D_QK, D_HEADS)
    ang_r = _rope_angles(D_ROPE)
    tab_r = jnp.stack([jnp.cos(ang_r), jnp.sin(ang_r)])
    rot_r = _rot_matrix(D_ROPE)

    u_all = peer_u.astype(BF16)
    vt_all = jnp.swapaxes(peer_v, 1, 2).astype(BF16)
    states = [[] for _ in range(8)]
    for l in range(DEPTH):
        lam_init = 0.8 - 0.6 * math.exp(-0.3 * l)
        mod3 = mod_all[l, :3].reshape(3, 6, D_MODEL)
        ws = jnp.pad(w_in[l, :, :SMALL_COLS], ((0, 0), (0, SMALL_PAD - SMALL_COLS))).astype(BF16)
        wg = w_in[l, :, SMALL_COLS:].astype(BF16)
        gains = jnp.stack([
            _pad_row(jnp.tile(a_qk_g[l, 0], 2 * A_HEADS), GAIN_W),
            _pad_row(jnp.tile(a_qk_g[l, 1], 2 * A_HEADS), GAIN_W),
            _pad_row(jnp.tile(b_qk_g[l, 0], B_HEADS), GAIN_W),
            _pad_row(jnp.tile(b_qk_g[l, 1], B_KV), GAIN_W),
            _pad_row(jnp.tile(c_qk_g[l, 0], C_HEADS), GAIN_W),
            _pad_row(jnp.tile(c_qk_g[l, 1], C_KV), GAIN_W),
            _pad_row(d_qnorm_g[l], GAIN_W),
            _pad_row(d_kvnorm_g[l], GAIN_W),
            jnp.tile(d_qk_g[l, 0], D_HEADS),
        ] + [jnp.zeros((GAIN_W,), F32)] * (GAIN_ROWS - 9))

        (h, aq, ak, av, bq, bk, bv, cq, ck, cv, dq,
         akt, avt, bkt, bvt, ckt, cvt, dckv, dkr) = _input_stage(
            x, mod3, norm1_g[l][None, :], ws, gains, e32, e64, e96, w_d_qb[l].astype(BF16), tab_a, tab_b, tab_d,
            tm=512)

        for lst, s_ in zip(states, (
                akt[:N_CTX].reshape(BATCH, SEQ, A_HEADS, 2, A_QK), avt[:N_CTX].reshape(BATCH, SEQ, A_HEADS, A_V),
                bkt[:N_CTX].reshape(BATCH, SEQ, B_KV, HEAD_DIM), bvt[:N_CTX].reshape(BATCH, SEQ, B_KV, HEAD_DIM),
                ckt[:N_CTX].reshape(BATCH, SEQ, C_KV, HEAD_DIM), cvt[:N_CTX].reshape(BATCH, SEQ, C_KV, HEAD_DIM),
                dckv[:N_CTX].reshape(BATCH, SEQ, D_KV_LORA), dkr[:N_CTX].reshape(BATCH, SEQ, D_ROPE))):
            lst.append(s_)

        ckv_rows = jnp.concatenate([t for b in range(DEC_BATCH) for t in (
            cache_d_ckv[b, l], dckv[N_CTX + b * DEC_SEQ:N_CTX + (b + 1) * DEC_SEQ])] + [dckv[:N_CTX]], axis=0)
        kr_rows = jnp.concatenate([t for b in range(DEC_BATCH) for t in (
            cache_d_krope[b, l], dkr[N_CTX + b * DEC_SEQ:N_CTX + (b + 1) * DEC_SEQ])] + [dkr[:N_CTX]], axis=0)
        kd, vd = _mla_expand(ckv_rows, kr_rows, w_d_kvb[l].astype(BF16), d_qk_g[l, 1][None, :], rot_r, tab_r)

        def cached(cache, nh, d):
            cache = jnp.moveaxis(cache.reshape(DEC_BATCH, PAST_LEN, nh, d), 2, 0)
            return cache.reshape(nh, DEC_BATCH * PAST_LEN, d).astype(BF16)

        a_cache = (cached(cache_a_k[:, l], 2 * A_HEADS, A_QK), cached(cache_a_v[:, l], A_HEADS, A_V))
        b_cache = (cached(cache_b_k[:, l], B_KV, HEAD_DIM), cached(cache_b_v[:, l], B_KV, HEAD_DIM))
        c_cache = (cached(cache_c_k[:, l], C_KV, HEAD_DIM), cached(cache_c_v[:, l], C_KV, HEAD_DIM))

        ctx = dict(batch=BATCH, tq_rows=SEQ, tk_rows=SEQ, q_off=0, o_off=0, lam_init=lam_init, tq=SEQ)
        lat = dict(batch=DEC_BATCH, tq_rows=DEC_SEQ, tk_rows=DEC_SEQ, q_off=N_CTX, k_off=N_CTX, o_off=N_CTX,
                   lam_init=lam_init, tq=256)
        a_extra = (a_lam[l], a_subln_g[l][None, :])
        blank = jnp.zeros((N_TOK, BRANCH_W), F32)
        oa = _attention(aq, ak, av, None, a_extra, blank, k_off=0, g_q=8, g_k=8, g_v=4, dv=A_V, mode="diff", **ctx)
        oa = _attention(aq, ak, av, a_cache, a_extra, oa, g_q=4, g_k=4, g_v=2, dv=A_V, mode="diff", **lat)
        ob = _attention(bq, bk, bv, None, (b_sink[l],), blank, k_off=0, g_q=4, g_k=2, g_v=2, dv=HEAD_DIM,
                        mode="sink", **ctx)
        ob = _window_attention(bq, bk, bv, *b_cache, b_sink[l], ob, q_off=N_CTX, k_off=N_CTX, o_off=N_CTX)
        oc = _attention(cq, ck, cv, None, (), blank, k_off=0, g_q=4, g_k=2, g_v=2, dv=HEAD_DIM, mode="plain",
                        **ctx)
        oc = _attention(cq, ck, cv, c_cache, (), oc, g_q=2, g_k=1, g_v=1, dv=HEAD_DIM, mode="plain", **lat)
        od = _attention(dq, kd, vd, None, (), blank, k_off=DEC_BATCH * KV_LAT, g_q=4, g_k=4, g_v=4, dv=D_V,
                        mode="plain", **ctx)
        lat_d = dict(lat, tk_rows=KV_LAT, k_off=0)
        od = _attention(dq, kd, vd, None, (), od, g_q=2, g_k=2, g_v=2, dv=D_V, mode="plain", **lat_d)

        x1, h2 = _merge_stage(x, h, oa, ob, oc, od, mod3, wg, w_branch[l].astype(BF16), w_out[l].astype(BF16),
                              norm2_g[l][None, :])

        thr, p1, q2, p2 = _peer_query(
            h2, peer_wq[l].T.astype(BF16),
            peer_subkeys[l].reshape(2 * PEER_HEADS, PEER_KEYS, PEER_HALF).astype(BF16))
        x = _peer_dense(h2, x1, mod3, u_all, vt_all, thr, p1, q2, p2, layer=l)

    y_prompt = x[:N_CTX].reshape(BATCH, SEQ, D_MODEL)
    y_sample = x[N_CTX:].reshape(DEC_BATCH, DEC_SEQ, D_MODEL)
    return (y_prompt, y_sample) + tuple(jnp.stack(s_, axis=1) for s_ in states)
```

```python
import functools
import math

import jax
import jax.numpy as jnp
from jax import lax
from jax.experimental import pallas as pl
from jax.experimental.pallas import tpu as pltpu

F32 = jnp.float32
BF16 = jnp.bfloat16

D_MODEL = 1024
BATCH = 16
SEQ = 256
DEPTH = 2
DEC_BATCH = 2
DEC_SEQ = 4096
PAST_LEN = 512
GRID_W = 64
Q_BLOCK = 128
HEAD_DIM = 64
ROPE_BASE = 10000.0
EPS = 1e-6
NEG_INF = -1e30
A_HEADS = 4
A_QK = 32
A_V = 64
B_HEADS = 4
B_KV = 2
WINDOW = 128
C_HEADS = 4
C_KV = 2
D_HEADS = 4
D_Q_LORA = 256
D_KV_LORA = 128
D_NOPE = 64
D_ROPE = 32
D_V = 64
D_QK = D_NOPE + D_ROPE
N_BRANCH = 4
BRANCH_W = 256
PEER_HEADS = 8
PEER_KEYS = 128
PEER_EXPERTS = PEER_KEYS * PEER_KEYS
PEER_QDIM = 256
PEER_HALF = PEER_QDIM // 2
PEER_TOPK = 16

N_CTX = BATCH * SEQ
N_LAT = DEC_BATCH * DEC_SEQ
N_TOK = N_CTX + N_LAT
KV_LAT = PAST_LEN + DEC_SEQ
N_MODROWS = 8
SMALL_COLS = 2208
SMALL_PAD = 2304

VMEM_LIMIT = 56 * 1024 * 1024
EXHAUSTED = -3.0e38
LOG2E = math.log2(math.e)


def _params(sem, flags=None):
    return pltpu.CompilerParams(dimension_semantics=sem, vmem_limit_bytes=VMEM_LIMIT, flags=flags)


def _nt_dot(a, b):
    return lax.dot_general(a, b, (((1,), (1,)), ((), ())), preferred_element_type=F32)


def _dot(a, b):
    return jnp.dot(a, b, preferred_element_type=F32)


def _mod_kernel(c_ref, w_ref, b_ref, o_ref):
    c = c_ref[...]
    s = c * jax.nn.sigmoid(c)
    o_ref[...] = jnp.dot(s, w_ref[...], preferred_element_type=F32, precision=lax.Precision.HIGHEST) + b_ref[...]


def _modulation(cvec, ada_w, ada_b):
    depth = ada_w.shape[0]
    ncol = ada_w.shape[2] // D_MODEL
    return pl.pallas_call(
        _mod_kernel,
        grid=(depth, ncol),
        in_specs=[
            pl.BlockSpec((N_MODROWS, D_MODEL), lambda l, j: (0, 0)),
            pl.BlockSpec((None, D_MODEL, D_MODEL), lambda l, j: (l, 0, j)),
            pl.BlockSpec((None, 1, D_MODEL), lambda l, j: (l, 0, j)),
        ],
        out_specs=pl.BlockSpec((None, N_MODROWS, D_MODEL), lambda l, j: (l, 0, j)),
        out_shape=jax.ShapeDtypeStruct((depth, N_MODROWS, ada_w.shape[2]), F32),
        compiler_params=_params(("arbitrary", "arbitrary")),
        name="modulation",
    )(cvec, ada_w, ada_b.reshape(depth, 1, -1))


def _group_rms(x, ones_bd, gain, d):
    sq = x * x
    hi = sq.astype(BF16)
    lo = (sq - hi.astype(F32)).astype(BF16)
    ss = _dot(hi, ones_bd) + _dot(lo, ones_bd)
    return x * lax.rsqrt(ss * (1.0 / d) + EPS) * gain


def _row_rms(x, gain):
    ms = jnp.mean(x * x, axis=-1, keepdims=True)
    return x * lax.rsqrt(ms + EPS) * gain


def _rope(x, tab_ref, q4):
    w = x.shape[-1]
    return x * tab_ref[0] + pltpu.roll(x, w - q4, 1) * tab_ref[1] + pltpu.roll(x, q4, 1) * tab_ref[2]


GAIN_ROWS = 16
GAIN_W = D_HEADS * D_QK


def _in_kernel(x_ref, mod_ref, n1g_ref, ws_ref, gains_ref, e32_ref, e64_ref, e96_ref, wdq_ref,
               ta_ref, tb_ref, td_ref,
               h_ref, aq_ref, ak_ref, av_ref, bq_ref, bk_ref, bv_ref, cq_ref, ck_ref, cv_ref, dq_ref,
               akt_ref, avt_ref, bkt_ref, bvt_ref, ckt_ref, cvt_ref, dckv_ref, dkr_ref, *, tm, n_ctx):
    is_lat = pl.program_id(0) * tm >= n_ctx
    x = x_ref[...]
    h = _row_rms(x, n1g_ref[...]) * (1.0 + mod_ref[1:2, :]) + mod_ref[0:1, :]
    hb = h.astype(BF16)
    h_ref[...] = hb
    p = _dot(hb, ws_ref[...])

    def gain(r, w):
        return gains_ref[r:r + 1, 0:w]

    def heads_out(ref, val, nh, d):
        for i in range(nh):
            ref[i] = val[:, i * d:(i + 1) * d].astype(BF16)

    def maybe_rope(val, tab_ref, q4):
        return jnp.where(is_lat, _rope(val, tab_ref, q4), val)

    aq = maybe_rope(_group_rms(p[:, 0:256], e32_ref[...], gain(0, 256), A_QK), ta_ref, A_QK // 4)
    ak = maybe_rope(_group_rms(p[:, 256:512], e32_ref[...], gain(1, 256), A_QK), ta_ref, A_QK // 4)
    av = p[:, 512:768]
    heads_out(aq_ref, aq * (A_QK ** -0.5 * LOG2E), 2 * A_HEADS, A_QK)
    heads_out(ak_ref, ak, 2 * A_HEADS, A_QK)
    heads_out(av_ref, av, A_HEADS, A_V)
    akt_ref[...] = ak
    avt_ref[...] = av
    bq = maybe_rope(_group_rms(p[:, 768:1024], e64_ref[...], gain(2, 256), HEAD_DIM), tb_ref, HEAD_DIM // 4)
    bk = _group_rms(p[:, 1024:1152], e64_ref[0:128, 0:128], gain(3, 128), HEAD_DIM)
    bk = jnp.where(is_lat, bk * tb_ref[0, :, 0:128] + pltpu.roll(bk, 128 - 16, 1) * tb_ref[1, :, 0:128]
                   + pltpu.roll(bk, 16, 1) * tb_ref[2, :, 0:128], bk)
    bv = p[:, 1152:1280]
    heads_out(bq_ref, bq * (HEAD_DIM ** -0.5 * LOG2E), B_HEADS, HEAD_DIM)
    heads_out(bk_ref, bk, B_KV, HEAD_DIM)
    heads_out(bv_ref, bv, B_KV, HEAD_DIM)
    bkt_ref[...] = bk
    bvt_ref[...] = bv
    cq = maybe_rope(_group_rms(p[:, 1280:1536], e64_ref[...], gain(4, 256), HEAD_DIM), tb_ref, HEAD_DIM // 4)
    ck = _group_rms(p[:, 1536:1664], e64_ref[0:128, 0:128], gain(5, 128), HEAD_DIM)
    ck = jnp.where(is_lat, ck * tb_ref[0, :, 0:128] + pltpu.roll(ck, 128 - 16, 1) * tb_ref[1, :, 0:128]
                   + pltpu.roll(ck, 16, 1) * tb_ref[2, :, 0:128], ck)
    cv = p[:, 1664:1792]
    heads_out(cq_ref, cq * (HEAD_DIM ** -0.5 * LOG2E), C_HEADS, HEAD_DIM)
    heads_out(ck_ref, ck, C_KV, HEAD_DIM)
    heads_out(cv_ref, cv, C_KV, HEAD_DIM)
    ckt_ref[...] = ck
    cvt_ref[...] = cv
    dqa = _row_rms(p[:, 1792:2048], gain(6, 256))
    dq = _dot(dqa.astype(BF16), wdq_ref[...])
    dq = maybe_rope(_group_rms(dq, e96_ref[...], gain(8, GAIN_W), D_QK), td_ref, D_ROPE // 4)
    heads_out(dq_ref, dq * (D_QK ** -0.5 * LOG2E), D_HEADS, D_QK)
    dckv_ref[...] = _row_rms(p[:, 2048:2176], gain(7, 128))
    dkr_ref[...] = p[:, 2176:2208]


def _input_stage(x, mod3, n1g, ws, gains, e32, e64, e96, wdq, tab_a, tab_b, tab_d, *, tm=256):
    n = x.shape[0]
    nt = n // tm
    lat0 = N_CTX // tm
    pos_blocks = DEC_SEQ // tm

    def tok(i):
        return (i, 0)

    def pos(i):
        return (0, jnp.maximum(i - lat0, 0) % pos_blocks, 0)

    def const2(i):
        return (0, 0)

    def hm(i):
        return (0, i, 0)

    def hspec(nh, d):
        return pl.BlockSpec((nh, tm, d), hm)

    def hshape(nh, d):
        return jax.ShapeDtypeStruct((nh, n, d), BF16)

    def tspec(w):
        return pl.BlockSpec((tm, w), tok)

    def tshape(w):
        return jax.ShapeDtypeStruct((n, w), F32)

    in_specs = [
        pl.BlockSpec((tm, D_MODEL), tok),
        pl.BlockSpec((None, 6, D_MODEL), lambda i: ((i * tm) // DEC_SEQ, 0, 0)),
        pl.BlockSpec((1, D_MODEL), const2),
        pl.BlockSpec((D_MODEL, SMALL_PAD), const2),
        pl.BlockSpec((GAIN_ROWS, GAIN_W), const2),
        pl.BlockSpec((256, 256), const2),
        pl.BlockSpec((256, 256), const2),
        pl.BlockSpec((GAIN_W, GAIN_W), const2),
        pl.BlockSpec((D_Q_LORA, GAIN_W), const2),
        pl.BlockSpec((3, tm, 256), pos),
        pl.BlockSpec((3, tm, 256), pos),
        pl.BlockSpec((3, tm, GAIN_W), pos),
    ]
    out_specs = [
        pl.BlockSpec((tm, D_MODEL), tok),
        hspec(8, A_QK), hspec(8, A_QK), hspec(4, A_V),
        hspec(4, HEAD_DIM), hspec(2, HEAD_DIM), hspec(2, HEAD_DIM),
        hspec(4, HEAD_DIM), hspec(2, HEAD_DIM), hspec(2, HEAD_DIM),
        hspec(4, D_QK),
        tspec(256), tspec(256), tspec(128), tspec(128), tspec(128), tspec(128), tspec(128), tspec(D_ROPE),
    ]
    out_shape = [
        jax.ShapeDtypeStruct((n, D_MODEL), BF16),
        hshape(8, A_QK), hshape(8, A_QK), hshape(4, A_V),
        hshape(4, HEAD_DIM), hshape(2, HEAD_DIM), hshape(2, HEAD_DIM),
        hshape(4, HEAD_DIM), hshape(2, HEAD_DIM), hshape(2, HEAD_DIM),
        hshape(4, D_QK),
        tshape(256), tshape(256), tshape(128), tshape(128), tshape(128), tshape(128), tshape(128), tshape(D_ROPE),
    ]
    return pl.pallas_call(
        functools.partial(_in_kernel, tm=tm, n_ctx=N_CTX),
        grid=(nt,),
        in_specs=in_specs,
        out_specs=out_specs,
        out_shape=out_shape,
        compiler_params=_params(("arbitrary",)),
        name="input_stage",
    )(x, mod3, n1g, ws, gains, e32, e64, e96, wdq, tab_a, tab_b, tab_d)


def _mla_kernel(ckv_ref, kr_ref, wkvb_ref, g_ref, rot_ref, tab_ref, kd_ref, vd_ref, *, tm, lat_tiles, per_batch_tiles):
    i = pl.program_id(0)
    rotate = jnp.logical_and(i < lat_tiles, i % per_batch_tiles >= PAST_LEN // tm)
    kv = _dot(ckv_ref[...].astype(BF16), wkvb_ref[...])
    kr = kr_ref[...]
    ssr = jnp.sum(kr * kr, axis=-1, keepdims=True)
    z = kr * g_ref[:, D_NOPE:D_QK]
    zh = z.astype(BF16)
    zl = (z - zh.astype(F32)).astype(BF16)
    rot = _dot(zh, rot_ref[...]) + _dot(zl, rot_ref[...])
    rz = jnp.where(rotate, z * tab_ref[0] + rot * tab_ref[1], z)
    g_nope = g_ref[:, 0:D_NOPE]
    for hd in range(D_HEADS):
        kn = kv[:, hd * 128:hd * 128 + D_NOPE]
        r = lax.rsqrt((jnp.sum(kn * kn, axis=-1, keepdims=True) + ssr) * (1.0 / D_QK) + EPS)
        kd_ref[hd] = jnp.concatenate([kn * r * g_nope, rz * r], axis=-1).astype(BF16)
        vd_ref[hd] = kv[:, hd * 128 + D_NOPE:(hd + 1) * 128].astype(BF16)


def _mla_expand(ckv, kr, wkvb, g, rot, tab, *, tm=512):
    rows = ckv.shape[0]
    per_batch_tiles = KV_LAT // tm
    lat_tiles = DEC_BATCH * per_batch_tiles
    past_tiles = PAST_LEN // tm
    pos_blocks = DEC_SEQ // tm

    def pos(i):
        return (0, jnp.maximum(i % per_batch_tiles - past_tiles, 0) % pos_blocks, 0)

    return pl.pallas_call(
        functools.partial(_mla_kernel, tm=tm, lat_tiles=lat_tiles, per_batch_tiles=per_batch_tiles),
        grid=(rows // tm,),
        in_specs=[
            pl.BlockSpec((tm, D_KV_LORA), lambda i: (i, 0)),
            pl.BlockSpec((tm, D_ROPE), lambda i: (i, 0)),
            pl.BlockSpec((D_KV_LORA, D_HEADS * (D_NOPE + D_V)), lambda i: (0, 0)),
            pl.BlockSpec((1, D_QK), lambda i: (0, 0)),
            pl.BlockSpec((D_ROPE, D_ROPE), lambda i: (0, 0)),
            pl.BlockSpec((2, tm, D_ROPE), pos),
        ],
        out_specs=[
            pl.BlockSpec((D_HEADS, tm, D_QK), lambda i: (0, i, 0)),
            pl.BlockSpec((D_HEADS, tm, D_V), lambda i: (0, i, 0)),
        ],
        out_shape=[
            jax.ShapeDtypeStruct((D_HEADS, rows, D_QK), BF16),
            jax.ShapeDtypeStruct((D_HEADS, rows, D_V), BF16),
        ],
        compiler_params=_params(("arbitrary",)),
        name="mla_expand",
    )(ckv, kr, wkvb, g, rot, tab)


def _attn_kernel(*refs, g_q, g_k, g_v, dv, mode, lam_init, has_cache):
    refs = list(refs)
    q_ref, k_ref, v_ref = refs[:3]
    pos = 3
    sink_ref = lam_ref = sub_ref = None
    if has_cache:
        kc_ref, vc_ref = refs[pos], refs[pos + 1]
        pos += 2
    if mode == "sink":
        sink_ref = refs[pos]
        pos += 1
    elif mode == "diff":
        lam_ref, sub_ref = refs[pos], refs[pos + 1]
        pos += 2
    o_ref = refs[pos + 1]
    grp = pl.program_id(1)

    if has_cache:
        kall_ref, vall_ref = refs[pos + 2], refs[pos + 3]
        past = kc_ref.shape[1]

        @pl.when(pl.program_id(2) == 0)
        def _join():
            kall_ref[:, 0:past, :] = kc_ref[...]
            kall_ref[:, past:, :] = k_ref[...]
            vall_ref[:, 0:past, :] = vc_ref[...]
            vall_ref[:, past:, :] = v_ref[...]

        k_ref, v_ref = kall_ref, vall_ref

    def probs(g):
        s = _nt_dot(q_ref[g], k_ref[g * g_k // g_q])
        m = jnp.max(s, axis=-1, keepdims=True)
        if mode == "sink":
            sk = sink_ref[grp * g_q + g] * LOG2E
            m = jnp.maximum(m, sk)
        p = jnp.exp2(s - m)
        den = jnp.sum(p, axis=-1, keepdims=True)
        if mode == "sink":
            den = den + jnp.exp2(sk - m)
        return p, den

    if mode == "diff":
        lp = lam_ref[...]
        lam = (jnp.exp(jnp.sum(lp[0:1] * lp[1:2], axis=-1, keepdims=True))
               - jnp.exp(jnp.sum(lp[2:3] * lp[3:4], axis=-1, keepdims=True)) + lam_init)
        for i in range(g_q // 2):
            p1, den1 = probs(2 * i)
            p2, den2 = probs(2 * i + 1)
            w1 = 1.0 / den1
            comb = p1 - p2 * (lam * den1 / den2)
            d = _dot(comb.astype(BF16), v_ref[i * g_v // (g_q // 2)]) * w1
            o_ref[:, i * dv:(i + 1) * dv] = _row_rms(d, sub_ref[...]) * (1.0 - lam_init)
    else:
        for g in range(g_q):
            p, den = probs(g)
            o_ref[:, g * dv:(g + 1) * dv] = _dot(p.astype(BF16), v_ref[g * g_v // g_q]) * (1.0 / den)


def _attention(q, k, v, cache, extras, o_prev, *, batch, tq_rows, tk_rows, q_off, k_off, o_off,
               g_q, g_k, g_v, dv, mode, lam_init, tq):
    hq, _, dq = q.shape
    n_groups = hq // g_q
    g_out = g_q // 2 if mode == "diff" else g_q
    nq = tq_rows // tq
    tk = tk_rows
    in_specs = [
        pl.BlockSpec((g_q, tq, dq), lambda b, g, qi: (g, (q_off + b * tq_rows) // tq + qi, 0)),
        pl.BlockSpec((g_k, tk, dq), lambda b, g, qi: (g, (k_off + b * tk_rows) // tk, 0)),
        pl.BlockSpec((g_v, tk, dv), lambda b, g, qi: (g, (k_off + b * tk_rows) // tk, 0)),
    ]
    args = [q, k, v]
    if cache is not None:
        in_specs.append(pl.BlockSpec((g_k, PAST_LEN, dq), lambda b, g, qi: (g, b, 0)))
        in_specs.append(pl.BlockSpec((g_v, PAST_LEN, dv), lambda b, g, qi: (g, b, 0)))
        args += list(cache)
    if mode == "sink":
        in_specs.append(pl.BlockSpec(memory_space=pltpu.SMEM))
        args.append(extras[0])
    elif mode == "diff":
        in_specs.append(pl.BlockSpec((4, A_QK), lambda b, g, qi: (0, 0)))
        in_specs.append(pl.BlockSpec((1, dv), lambda b, g, qi: (0, 0)))
        args += list(extras)
    in_specs.append(pl.BlockSpec(memory_space=pl.ANY))
    aliases = {len(args): 0}
    args.append(o_prev)
    return pl.pallas_call(
        functools.partial(_attn_kernel, g_q=g_q, g_k=g_k, g_v=g_v, dv=dv, mode=mode, lam_init=lam_init,
                          has_cache=cache is not None),
        grid=(batch, n_groups, nq),
        in_specs=in_specs,
        out_specs=pl.BlockSpec((tq, g_out * dv), lambda b, g, qi: ((o_off + b * tq_rows) // tq + qi, g)),
        out_shape=jax.ShapeDtypeStruct(o_prev.shape, F32),
        scratch_shapes=([pltpu.VMEM((g_k, PAST_LEN + tk, dq), BF16), pltpu.VMEM((g_v, PAST_LEN + tk, dv), BF16)]
                        if cache is not None else []),
        input_output_aliases=aliases,
        compiler_params=_params(("arbitrary", "arbitrary", "arbitrary")),
        name="attention_" + mode,
    )(*args)


def _win_kernel(q_ref, k_ref, v_ref, kc_ref, vc_ref, sink_ref, prev_ref, o_ref, *, g_q, dv, tq, seq, band):
    del prev_ref
    grp = pl.program_id(1)
    qi = pl.program_id(2)
    start = pl.multiple_of(jnp.clip(qi * tq - WINDOW, 0, seq - band), WINDOW)
    k_ctx = kc_ref[...]
    v_ctx = vc_ref[...]
    k_band = k_ref[pl.ds(start, band), :]
    v_band = v_ref[pl.ds(start, band), :]
    qpos = qi * tq + lax.broadcasted_iota(jnp.int32, (tq, band), 0)
    kpos = start + lax.broadcasted_iota(jnp.int32, (tq, band), 1)
    valid = jnp.abs(kpos - qpos) <= WINDOW
    for g in range(g_q):
        q = q_ref[g]
        s_ctx = _nt_dot(q, k_ctx)
        s_band = jnp.where(valid, _nt_dot(q, k_band), NEG_INF)
        sk = sink_ref[grp * g_q + g] * LOG2E
        m = jnp.maximum(jnp.maximum(jnp.max(s_ctx, axis=-1, keepdims=True),
                                    jnp.max(s_band, axis=-1, keepdims=True)), sk)
        p_ctx = jnp.exp2(s_ctx - m)
        p_band = jnp.exp2(s_band - m)
        den = (jnp.sum(p_ctx, axis=-1, keepdims=True) + jnp.sum(p_band, axis=-1, keepdims=True)
               + jnp.exp2(sk - m))
        num = _dot(p_ctx.astype(BF16), v_ctx) + _dot(p_band.astype(BF16), v_band)
        o_ref[:, g * dv:(g + 1) * dv] = num * (1.0 / den)


def _window_attention(q, k, v, kc, vc, sink, o_prev, *, q_off, k_off, o_off, tq=256):
    hq, _, d = q.shape
    hk = k.shape[0]
    g_q = hq // hk
    band = tq + 2 * WINDOW
    nq = DEC_SEQ // tq
    return pl.pallas_call(
        functools.partial(_win_kernel, g_q=g_q, dv=d, tq=tq, seq=DEC_SEQ, band=band),
        grid=(DEC_BATCH, hk, nq),
        in_specs=[
            pl.BlockSpec((g_q, tq, d), lambda b, g, qi: (g, (q_off + b * DEC_SEQ) // tq + qi, 0)),
            pl.BlockSpec((None, DEC_SEQ, d), lambda b, g, qi: (g, k_off // DEC_SEQ + b, 0)),
            pl.BlockSpec((None, DEC_SEQ, d), lambda b, g, qi: (g, k_off // DEC_SEQ + b, 0)),
            pl.BlockSpec((None, PAST_LEN, d), lambda b, g, qi: (g, b, 0)),
            pl.BlockSpec((None, PAST_LEN, d), lambda b, g, qi: (g, b, 0)),
            pl.BlockSpec(memory_space=pltpu.SMEM),
            pl.BlockSpec(memory_space=pl.ANY),
        ],
        out_specs=pl.BlockSpec((tq, g_q * d), lambda b, g, qi: ((o_off + b * DEC_SEQ) // tq + qi, g)),
        out_shape=jax.ShapeDtypeStruct(o_prev.shape, F32),
        input_output_aliases={6: 0},
        compiler_params=_params(("arbitrary", "arbitrary", "arbitrary")),
        name="attention_window",
    )(q, k, v, kc, vc, sink, o_prev)


def _merge_kernel(x_ref, h_ref, oa_ref, ob_ref, oc_ref, od_ref, mod_ref, wg_ref, wb_ref, wo_ref, n2g_ref,
                  x1_ref, h2_ref):
    hb = h_ref[...]
    outs = (oa_ref, ob_ref, oc_ref, od_ref)
    acc = None
    for i in range(N_BRANCH):
        gt = jax.nn.sigmoid(_dot(hb, wg_ref[:, i * D_MODEL:(i + 1) * D_MODEL]))
        br = _dot(outs[i][...].astype(BF16), wb_ref[i])
        acc = gt * br if acc is None else acc + gt * br
    y = _dot(acc.astype(BF16), wo_ref[...])
    x1 = x_ref[...] + mod_ref[2:3, :] * y
    x1_ref[...] = x1
    h2 = _row_rms(x1, n2g_ref[...]) * (1.0 + mod_ref[4:5, :]) + mod_ref[3:4, :]
    h2_ref[...] = h2.astype(BF16)


def _merge_stage(x, h, oa, ob, oc, od, mod3, wg, wb, wo, n2g, *, tm=256):
    n = x.shape[0]

    def tok(i):
        return (i, 0)

    return pl.pallas_call(
        _merge_kernel,
        grid=(n // tm,),
        in_specs=[
            pl.BlockSpec((tm, D_MODEL), tok),
            pl.BlockSpec((tm, D_MODEL), tok),
            pl.BlockSpec((tm, BRANCH_W), tok),
            pl.BlockSpec((tm, BRANCH_W), tok),
            pl.BlockSpec((tm, BRANCH_W), tok),
            pl.BlockSpec((tm, BRANCH_W), tok),
            pl.BlockSpec((None, 6, D_MODEL), lambda i: ((i * tm) // DEC_SEQ, 0, 0)),
            pl.BlockSpec((D_MODEL, N_BRANCH * D_MODEL), lambda i: (0, 0)),
            pl.BlockSpec((N_BRANCH, BRANCH_W, D_MODEL), lambda i: (0, 0, 0)),
            pl.BlockSpec((D_MODEL, D_MODEL), lambda i: (0, 0)),
            pl.BlockSpec((1, D_MODEL), lambda i: (0, 0)),
        ],
        out_specs=[pl.BlockSpec((tm, D_MODEL), tok), pl.BlockSpec((tm, D_MODEL), tok)],
        out_shape=[jax.ShapeDtypeStruct((n, D_MODEL), F32), jax.ShapeDtypeStruct((n, D_MODEL), BF16)],
        compiler_params=_params(("arbitrary",)),
        name="merge_stage",
    )(x, h, oa, ob, oc, od, mod3, wg, wb, wo, n2g)


def _oddeven_mergesort_pairs(n):
    pairs = []

    def merge(lo, m, r):
        step = 2 * r
        if step < m:
            merge(lo, m, step)
            merge(lo + r, m, step)
            for i in range(lo + r, lo + m - r, step):
                pairs.append((i, i + r))
        else:
            pairs.append((lo, lo + r))

    def sort(lo, m):
        if m > 1:
            sort(lo, m // 2)
            sort(lo + m // 2, m // 2)
            merge(lo, m, 1)

    sort(0, n)
    return pairs


_SORT_PAIRS = _oddeven_mergesort_pairs(PEER_TOPK)


def _exchange(xs, i, j):
    xs[i], xs[j] = jnp.maximum(xs[i], xs[j]), jnp.minimum(xs[i], xs[j])


def _top16_of_groups(groups):
    n = len(groups)
    xs = list(groups)
    for i, j in _SORT_PAIRS:
        _exchange(xs, i, j)
    for shift in (4, 2, 1):
        xs = [jnp.maximum(xs[k], pltpu.roll(xs[n - 1 - k], shift, 0)) for k in range(n)]
        s = n // 2
        while s >= 1:
            for i in range(n):
                if not i & s:
                    _exchange(xs, i, i + s)
            s //= 2
    return xs


def _sublane_sum(x):
    for shift in (4, 2, 1):
        x = x + pltpu.roll(x, shift, 0)
    return x


def _sublane_spread(vals):
    row = lax.broadcasted_iota(jnp.int32, vals[0].shape, 0)
    out = vals[-1]
    for j in range(len(vals) - 2, -1, -1):
        out = jnp.where(row == j, vals[j], out)
    return out


def _peer_select(g1, g2):
    k = PEER_TOPK
    half = k // 2
    v1 = _top16_of_groups(g1)
    v2 = _top16_of_groups(g2)
    v2lo = _sublane_spread(v2[:half])
    v2hi = _sublane_spread(v2[half:])
    v1hi = _sublane_spread(v1[half:])
    cands = [v1[0] + v2lo, v1[0] + v2hi] + [v1[i] + v2lo for i in range(1, half)] + [v1hi + v2[0]]
    pad = jnp.full(cands[0].shape, EXHAUSTED, F32)
    top = _top16_of_groups(cands + [pad] * (k - len(cands)))
    tau = top[k - 1]
    z = None
    for c in cands:
        e = jnp.where(c >= tau, jnp.exp(c - top[0]), 0.0)
        z = e if z is None else z + e
    z = _sublane_sum(z)
    thr = [jnp.full(g1[0].shape, -1.0, F32) for _ in g1]
    q2 = [jnp.full(g2[0].shape, float(k), F32) for _ in g2]
    for a in range(k - 1, -1, -1):
        cnt = _sublane_sum(jnp.where(v1[a] + v2lo >= tau, 1.0, 0.0) + jnp.where(v1[a] + v2hi >= tau, 1.0, 0.0))
        thr = [jnp.where(x == v1[a], cnt - 1.0, t) for x, t in zip(g1, thr)]
        q2 = [jnp.where(x == v2[a], float(a), q) for x, q in zip(g2, q2)]
    p1 = [jnp.exp(x - v1[0]) for x in g1]
    scale = 0.5 / z
    p2 = [jnp.exp(x - v2[0]) * scale for x in g2]
    return thr, p1, q2, p2


def _peer_query_kernel(h2_ref, wqt_ref, sk_ref, thr_ref, p1_ref, q2_ref, p2_ref):
    qt = _nt_dot(wqt_ref[...], h2_ref[...]).astype(BF16)
    n_lt = qt.shape[1] // 128
    n_grp = PEER_KEYS // 8
    for hd in range(PEER_HEADS):
        s1 = _dot(sk_ref[2 * hd], qt[(2 * hd) * PEER_HALF:(2 * hd + 1) * PEER_HALF, :])
        s2 = _dot(sk_ref[2 * hd + 1], qt[(2 * hd + 1) * PEER_HALF:(2 * hd + 2) * PEER_HALF, :])
        for lt in range(n_lt):
            lanes = slice(lt * 128, (lt + 1) * 128)
            dst = (lt + 1) % n_lt
            rot = slice(dst * 128, (dst + 1) * 128)
            g1 = [s1[8 * i:8 * i + 8, lanes] for i in range(n_grp)]
            g2 = [s2[8 * i:8 * i + 8, lanes] for i in range(n_grp)]
            thr, p1, q2, p2 = _peer_select(g1, g2)
            thr_ref[hd, :, lanes] = jnp.concatenate(thr, axis=0)
            p1_ref[hd, :, lanes] = jnp.concatenate(p1, axis=0)
            q2_ref[hd, :, lanes] = jnp.concatenate(q2, axis=0).astype(BF16)
            p2_ref[hd, :, rot] = jnp.concatenate(p2, axis=0).astype(BF16)


def _peer_query(h2, wqt, sk, *, tm=512):
    n = h2.shape[0]
    big = jax.ShapeDtypeStruct((PEER_HEADS, PEER_KEYS, n), F32)
    half = jax.ShapeDtypeStruct((PEER_HEADS, PEER_KEYS, n), BF16)
    bspec = pl.BlockSpec((PEER_HEADS, PEER_KEYS, tm), lambda i: (0, 0, i))
    return pl.pallas_call(
        _peer_query_kernel,
        grid=(n // tm,),
        in_specs=[
            pl.BlockSpec((tm, D_MODEL), lambda i: (i, 0)),
            pl.BlockSpec((PEER_HEADS * PEER_QDIM, D_MODEL), lambda i: (0, 0)),
            pl.BlockSpec((2 * PEER_HEADS, PEER_KEYS, PEER_HALF), lambda i: (0, 0, 0)),
        ],
        out_specs=[bspec, bspec, bspec, bspec],
        out_shape=[big, big, half, half],
        compiler_params=_params(("arbitrary",)),
        name="peer_query",
    )(h2, wqt, sk)


def _peer_dense_kernel(h2_ref, x1_ref, mod_ref, u_ref, vt_ref, thr_ref, p1_ref, q2_ref, p2_ref,
                       o_ref, acc_ref, g_ref, p_ref, q2s_ref, p2s_ref, *, e1_per_step):
    step = pl.program_id(1)
    tm = g_ref.shape[1]
    n_lt = tm // 128
    rows = 16
    n_r = PEER_KEYS // rows

    def gate_pass():
        for c in range(e1_per_step):
            for lt in range(n_lt):
                lanes = slice(lt * 128, (lt + 1) * 128)
                rot = slice(((lt + 1) % n_lt) * 128, ((lt + 1) % n_lt + 1) * 128)
                g = [None] * n_r
                for hd in range(PEER_HEADS):
                    thr_b = jnp.broadcast_to(thr_ref[hd, c:c + 1, lanes], (rows, 128)).astype(BF16)
                    p1_b = jnp.broadcast_to(p1_ref[hd, c:c + 1, lanes], (rows, 128)).astype(BF16)
                    for r in range(n_r):
                        rs = slice(r * rows, (r + 1) * rows)
                        w = jnp.where(q2s_ref[hd, rs, lanes] <= thr_b, p2s_ref[hd, rs, rot] * p1_b,
                                      jnp.zeros((), BF16))
                        g[r] = w if g[r] is None else g[r] + w
                for r in range(n_r):
                    g_ref[c * PEER_KEYS + r * rows:c * PEER_KEYS + (r + 1) * rows, lanes] = g[r]

    @pl.when(step == 0)
    def _init():
        acc_ref[...] = jnp.zeros_like(acc_ref)
        q2s_ref[...] = q2_ref[...]
        p2s_ref[...] = p2_ref[...]

    pl.when(step >= 0)(gate_pass)

    a = _nt_dot(u_ref[...], h2_ref[...])
    act = (a + a * lax.erf(a * (2.0 ** -0.5))).astype(BF16)
    p_ref[...] = g_ref[...] * act
    acc_ref[...] += _dot(vt_ref[...], p_ref[...])

    @pl.when(step == pl.num_programs(1) - 1)
    def _fin():
        o_ref[...] = x1_ref[...] + mod_ref[5:6, :] * acc_ref[...].T


def _peer_dense(h2, x1, mod3, u, vt, thr, p1, q2, p2, *, layer, tm=512, e1_per_step=16):
    n = h2.shape[0]
    te = e1_per_step * PEER_KEYS
    tok = lambda i, j: (i, 0)
    big = pl.BlockSpec((PEER_HEADS, PEER_KEYS, tm), lambda i, j: (0, 0, i))
    rows = pl.BlockSpec((PEER_HEADS, e1_per_step, tm), lambda i, j: (0, j, i))
    return pl.pallas_call(
        functools.partial(_peer_dense_kernel, e1_per_step=e1_per_step),
        grid=(n // tm, PEER_EXPERTS // te),
        in_specs=[
            pl.BlockSpec((tm, D_MODEL), tok),
            pl.BlockSpec((tm, D_MODEL), tok),
            pl.BlockSpec((None, 6, D_MODEL), lambda i, j: ((i * tm) // DEC_SEQ, 0, 0)),
            pl.BlockSpec((None, te, D_MODEL), lambda i, j: (layer, j, 0)),
            pl.BlockSpec((None, D_MODEL, te), lambda i, j: (layer, 0, j)),
            rows, rows, big, big,
        ],
        out_specs=pl.BlockSpec((tm, D_MODEL), tok),
        out_shape=jax.ShapeDtypeStruct((n, D_MODEL), F32),
        scratch_shapes=[pltpu.VMEM((D_MODEL, tm), F32), pltpu.VMEM((te, tm), BF16), pltpu.VMEM((te, tm), BF16),
                        pltpu.VMEM((PEER_HEADS, PEER_KEYS, tm), BF16), pltpu.VMEM((PEER_HEADS, PEER_KEYS, tm), BF16)],
        compiler_params=_params(("arbitrary", "arbitrary")),
        name="peer_dense",
    )(h2, x1, mod3, u, vt, thr, p1, q2, p2)


def _block_ones(width, d):
    idx = jnp.arange(width) // d
    return (idx[:, None] == idx[None, :]).astype(BF16)


def _rope_angles(d):
    q4 = d // 4
    inv = ROPE_BASE ** (-jnp.arange(q4, dtype=F32) / q4)
    t = jnp.arange(DEC_SEQ)
    ar = (t // GRID_W).astype(F32)[:, None] * inv
    ac = (t % GRID_W).astype(F32)[:, None] * inv
    return jnp.concatenate([ar, ar, ac, ac], axis=-1)


def _rope_table(d, group, n_groups):
    ang = _rope_angles(d)
    q4 = d // 4
    first = (jnp.arange(d) % (2 * q4)) < q4
    cos = jnp.cos(ang)
    sin = jnp.sin(ang)
    sa = jnp.where(first[None, :], -sin, 0.0)
    sb = jnp.where(first[None, :], 0.0, sin)
    pad = group - d
    cos = jnp.pad(cos, ((0, 0), (pad, 0)), constant_values=1.0)
    sa = jnp.pad(sa, ((0, 0), (pad, 0)))
    sb = jnp.pad(sb, ((0, 0), (pad, 0)))
    return jnp.stack([jnp.tile(cos, (1, n_groups)), jnp.tile(sa, (1, n_groups)), jnp.tile(sb, (1, n_groups))])


def _rot_matrix(d):
    q4 = d // 4
    i = jnp.arange(d)
    first = (i % (2 * q4)) < q4
    src = jnp.where(first, i + q4, i - q4)
    sign = jnp.where(first, -1.0, 1.0)
    return (jnp.zeros((d, d), F32).at[src, i].set(sign)).astype(BF16)


def _pad_row(v, width):
    return jnp.pad(v, (0, width - v.shape[0]))


def kernel(x_prompt, x_sample, c, cache_a_k, cache_a_v, cache_b_k, cache_b_v, cache_c_k, cache_c_v, cache_d_ckv, cache_d_krope, c_ctx, ada_w, ada_b, norm1_g, norm2_g, w_in, a_lam, a_qk_g, a_subln_g, b_sink, b_qk_g, c_qk_g, d_qnorm_g, d_kvnorm_g, w_d_qb, w_d_kvb, d_qk_g, w_branch, w_out, peer_wq, peer_subkeys, peer_u, peer_v):
    x = jnp.concatenate([x_prompt.reshape(N_CTX, D_MODEL), x_sample.reshape(N_LAT, D_MODEL)], axis=0)
    cvec = jnp.concatenate([c_ctx[None, :], c, jnp.zeros((N_MODROWS - 1 - DEC_BATCH, D_MODEL), F32)], axis=0)
    mod_all = _modulation(cvec, ada_w, ada_b)

    e32 = _block_ones(256, A_QK)
    e64 = _block_ones(256, HEAD_DIM)
    e96 = _block_ones(GAIN_W, D_QK)
    tab_a = _rope_table(A_QK, A_QK, 2 * A_HEADS)
    tab_b = _rope_table(HEAD_DIM, HEAD_DIM, B_HEADS)
    tab_d = _rope_table(D_ROPE, D_QK, D_HEADS)
    ang_r = _rope_angles(D_ROPE)
    tab_r = jnp.stack([jnp.cos(ang_r), jnp.sin(ang_r)])
    rot_r = _rot_matrix(D_ROPE)

    u_all = peer_u.astype(BF16)
    vt_all = jnp.swapaxes(peer_v, 1, 2).astype(BF16)
    states = [[] for _ in range(8)]
    for l in range(DEPTH):
        lam_init = 0.8 - 0.6 * math.exp(-0.3 * l)
        mod3 = mod_all[l, :3].reshape(3, 6, D_MODEL)
        ws = jnp.pad(w_in[l, :, :SMALL_COLS], ((0, 0), (0, SMALL_PAD - SMALL_COLS))).astype(BF16)
        wg = w_in[l, :, SMALL_COLS:].astype(BF16)
        gains = jnp.stack([
            _pad_row(jnp.tile(a_qk_g[l, 0], 2 * A_HEADS), GAIN_W),
            _pad_row(jnp.tile(a_qk_g[l, 1], 2 * A_HEADS), GAIN_W),
            _pad_row(jnp.tile(b_qk_g[l, 0], B_HEADS), GAIN_W),
            _pad_row(jnp.tile(b_qk_g[l, 1], B_KV), GAIN_W),
            _pad_row(jnp.tile(c_qk_g[l, 0], C_HEADS), GAIN_W),
            _pad_row(jnp.tile(c_qk_g[l, 1], C_KV), GAIN_W),
            _pad_row(d_qnorm_g[l], GAIN_W),
            _pad_row(d_kvnorm_g[l], GAIN_W),
            jnp.tile(d_qk_g[l, 0], D_HEADS),
        ] + [jnp.zeros((GAIN_W,), F32)] * (GAIN_ROWS - 9))

        (h, aq, ak, av, bq, bk, bv, cq, ck, cv, dq,
         akt, avt, bkt, bvt, ckt, cvt, dckv, dkr) = _input_stage(
            x, mod3, norm1_g[l][None, :], ws, gains, e32, e64, e96, w_d_qb[l].astype(BF16), tab_a, tab_b, tab_d,
            tm=512)

        for lst, s_ in zip(states, (
                akt[:N_CTX].reshape(BATCH, SEQ, A_HEADS, 2, A_QK), avt[:N_CTX].reshape(BATCH, SEQ, A_HEADS, A_V),
                bkt[:N_CTX].reshape(BATCH, SEQ, B_KV, HEAD_DIM), bvt[:N_CTX].reshape(BATCH, SEQ, B_KV, HEAD_DIM),
                ckt[:N_CTX].reshape(BATCH, SEQ, C_KV, HEAD_DIM), cvt[:N_CTX].reshape(BATCH, SEQ, C_KV, HEAD_DIM),
                dckv[:N_CTX].reshape(BATCH, SEQ, D_KV_LORA), dkr[:N_CTX].reshape(BATCH, SEQ, D_ROPE))):
            lst.append(s_)

        ckv_rows = jnp.concatenate([t for b in range(DEC_BATCH) for t in (
            cache_d_ckv[b, l], dckv[N_CTX + b * DEC_SEQ:N_CTX + (b + 1) * DEC_SEQ])] + [dckv[:N_CTX]], axis=0)
        kr_rows = jnp.concatenate([t for b in range(DEC_BATCH) for t in (
            cache_d_krope[b, l], dkr[N_CTX + b * DEC_SEQ:N_CTX + (b + 1) * DEC_SEQ])] + [dkr[:N_CTX]], axis=0)
        kd, vd = _mla_expand(ckv_rows, kr_rows, w_d_kvb[l].astype(BF16), d_qk_g[l, 1][None, :], rot_r, tab_r)

        def cached(cache, nh, d):
            cache = jnp.moveaxis(cache.reshape(DEC_BATCH, PAST_LEN, nh, d), 2, 0)
            return cache.reshape(nh, DEC_BATCH * PAST_LEN, d).astype(BF16)

        a_cache = (cached(cache_a_k[:, l], 2 * A_HEADS, A_QK), cached(cache_a_v[:, l], A_HEADS, A_V))
        b_cache = (cached(cache_b_k[:, l], B_KV, HEAD_DIM), cached(cache_b_v[:, l], B_KV, HEAD_DIM))
        c_cache = (cached(cache_c_k[:, l], C_KV, HEAD_DIM), cached(cache_c_v[:, l], C_KV, HEAD_DIM))

        ctx = dict(batch=BATCH, tq_rows=SEQ, tk_rows=SEQ, q_off=0, o_off=0, lam_init=lam_init, tq=SEQ)
        lat = dict(batch=DEC_BATCH, tq_rows=DEC_SEQ, tk_rows=DEC_SEQ, q_off=N_CTX, k_off=N_CTX, o_off=N_CTX,
                   lam_init=lam_init, tq=256)
        a_extra = (a_lam[l], a_subln_g[l][None, :])
        blank = jnp.zeros((N_TOK, BRANCH_W), F32)
        oa = _attention(aq, ak, av, None, a_extra, blank, k_off=0, g_q=8, g_k=8, g_v=4, dv=A_V, mode="diff", **ctx)
        oa = _attention(aq, ak, av, a_cache, a_extra, oa, g_q=4, g_k=4, g_v=2, dv=A_V, mode="diff", **lat)
        ob = _attention(bq, bk, bv, None, (b_sink[l],), blank, k_off=0, g_q=4, g_k=2, g_v=2, dv=HEAD_DIM,
                        mode="sink", **ctx)
        ob = _window_attention(bq, bk, bv, *b_cache, b_sink[l], ob, q_off=N_CTX, k_off=N_CTX, o_off=N_CTX)
        oc = _attention(cq, ck, cv, None, (), blank, k_off=0, g_q=4, g_k=2, g_v=2, dv=HEAD_DIM, mode="plain",
                        **ctx)
        oc = _attention(cq, ck, cv, c_cache, (), oc, g_q=2, g_k=1, g_v=1, dv=HEAD_DIM, mode="plain", **lat)
        od = _attention(dq, kd, vd, None, (), blank, k_off=DEC_BATCH * KV_LAT, g_q=4, g_k=4, g_v=4, dv=D_V,
                        mode="plain", **ctx)
        lat_d = dict(lat, tk_rows=KV_LAT, k_off=0)
        od = _attention(dq, kd, vd, None, (), od, g_q=2, g_k=2, g_v=2, dv=D_V, mode="plain", **lat_d)

        x1, h2 = _merge_stage(x, h, oa, ob, oc, od, mod3, wg, w_branch[l].astype(BF16), w_out[l].astype(BF16),
                              norm2_g[l][None, :])

        thr, p1, q2, p2 = _peer_query(
            h2, peer_wq[l].T.astype(BF16),
            peer_subkeys[l].reshape(2 * PEER_HEADS, PEER_KEYS, PEER_HALF).astype(BF16))
        x = _peer_dense(h2, x1, mod3, u_all, vt_all, thr, p1, q2, p2, layer=l)

    y_prompt = x[:N_CTX].reshape(BATCH, SEQ, D_MODEL)
    y_sample = x[N_CTX:].reshape(DEC_BATCH, DEC_SEQ, D_MODEL)
    return (y_prompt, y_sample) + tuple(jnp.stack(s_, axis=1) for s_ in states)
```

```python
import functools
import math

import jax
import jax.numpy as jnp
from jax import lax
from jax.experimental import pallas as pl
from jax.experimental.pallas import tpu as pltpu

F32 = jnp.float32
BF16 = jnp.bfloat16

D_MODEL = 1024
BATCH = 16
SEQ = 256
DEPTH = 2
DEC_BATCH = 2
DEC_SEQ = 4096
PAST_LEN = 512
GRID_W = 64
Q_BLOCK = 128
HEAD_DIM = 64
ROPE_BASE = 10000.0
EPS = 1e-6
NEG_INF = -1e30
A_HEADS = 4
A_QK = 32
A_V = 64
B_HEADS = 4
B_KV = 2
WINDOW = 128
C_HEADS = 4
C_KV = 2
D_HEADS = 4
D_Q_LORA = 256
D_KV_LORA = 128
D_NOPE = 64
D_ROPE = 32
D_V = 64
D_QK = D_NOPE + D_ROPE
N_BRANCH = 4
BRANCH_W = 256
PEER_HEADS = 8
PEER_KEYS = 128
PEER_EXPERTS = PEER_KEYS * PEER_KEYS
PEER_QDIM = 256
PEER_HALF = PEER_QDIM // 2
PEER_TOPK = 16

N_CTX = BATCH * SEQ
N_LAT = DEC_BATCH * DEC_SEQ
N_TOK = N_CTX + N_LAT
KV_LAT = PAST_LEN + DEC_SEQ
N_MODROWS = 8
SMALL_COLS = 2208
SMALL_PAD = 2304

VMEM_LIMIT = 56 * 1024 * 1024
EXHAUSTED = -3.0e38
LOG2E = math.log2(math.e)


def _params(sem, flags=None):
    return pltpu.CompilerParams(dimension_semantics=sem, vmem_limit_bytes=VMEM_LIMIT, flags=flags)


def _nt_dot(a, b):
    return lax.dot_general(a, b, (((1,), (1,)), ((), ())), preferred_element_type=F32)


def _dot(a, b):
    return jnp.dot(a, b, preferred_element_type=F32)


def _mod_kernel(c_ref, w_ref, b_ref, o_ref):
    c = c_ref[...]
    s = c * jax.nn.sigmoid(c)
    o_ref[...] = jnp.dot(s, w_ref[...], preferred_element_type=F32, precision=lax.Precision.HIGHEST) + b_ref[...]


def _modulation(cvec, ada_w, ada_b):
    depth = ada_w.shape[0]
    ncol = ada_w.shape[2] // D_MODEL
    return pl.pallas_call(
        _mod_kernel,
        grid=(depth, ncol),
        in_specs=[
            pl.BlockSpec((N_MODROWS, D_MODEL), lambda l, j: (0, 0)),
            pl.BlockSpec((None, D_MODEL, D_MODEL), lambda l, j: (l, 0, j)),
            pl.BlockSpec((None, 1, D_MODEL), lambda l, j: (l, 0, j)),
        ],
        out_specs=pl.BlockSpec((None, N_MODROWS, D_MODEL), lambda l, j: (l, 0, j)),
        out_shape=jax.ShapeDtypeStruct((depth, N_MODROWS, ada_w.shape[2]), F32),
        compiler_params=_params(("arbitrary", "arbitrary")),
        name="modulation",
    )(cvec, ada_w, ada_b.reshape(depth, 1, -1))


def _group_rms(x, ones_bd, gain, d):
    sq = x * x
    hi = sq.astype(BF16)
    lo = (sq - hi.astype(F32)).astype(BF16)
    ss = _dot(hi, ones_bd) + _dot(lo, ones_bd)
    return x * lax.rsqrt(ss * (1.0 / d) + EPS) * gain


def _row_rms(x, gain):
    ms = jnp.mean(x * x, axis=-1, keepdims=True)
    return x * lax.rsqrt(ms + EPS) * gain


def _rope(x, tab_ref, q4):
    w = x.shape[-1]
    return x * tab_ref[0] + pltpu.roll(x, w - q4, 1) * tab_ref[1] + pltpu.roll(x, q4, 1) * tab_ref[2]


GAIN_ROWS = 16
GAIN_W = D_HEADS * D_QK


def _in_kernel(x_ref, mod_ref, n1g_ref, ws_ref, gains_ref, e32_ref, e64_ref, e96_ref, wdq_ref,
               ta_ref, tb_ref, td_ref,
               h_ref, aq_ref, ak_ref, av_ref, bq_ref, bk_ref, bv_ref, cq_ref, ck_ref, cv_ref, dq_ref,
               akt_ref, avt_ref, bkt_ref, bvt_ref, ckt_ref, cvt_ref, dckv_ref, dkr_ref, *, tm, n_ctx):
    is_lat = pl.program_id(0) * tm >= n_ctx
    x = x_ref[...]
    h = _row_rms(x, n1g_ref[...]) * (1.0 + mod_ref[1:2, :]) + mod_ref[0:1, :]
    hb = h.astype(BF16)
    h_ref[...] = hb
    p = _dot(hb, ws_ref[...])

    def gain(r, w):
        return gains_ref[r:r + 1, 0:w]

    def heads_out(ref, val, nh, d):
        for i in range(nh):
            ref[i] = val[:, i * d:(i + 1) * d].astype(BF16)

    def maybe_rope(val, tab_ref, q4):
        return jnp.where(is_lat, _rope(val, tab_ref, q4), val)

    aq = maybe_rope(_group_rms(p[:, 0:256], e32_ref[...], gain(0, 256), A_QK), ta_ref, A_QK // 4)
    ak = maybe_rope(_group_rms(p[:, 256:512], e32_ref[...], gain(1, 256), A_QK), ta_ref, A_QK // 4)
    av = p[:, 512:768]
    heads_out(aq_ref, aq * (A_QK ** -0.5 * LOG2E), 2 * A_HEADS, A_QK)
    heads_out(ak_ref, ak, 2 * A_HEADS, A_QK)
    heads_out(av_ref, av, A_HEADS, A_V)
    akt_ref[...] = ak
    avt_ref[...] = av
    bq = maybe_rope(_group_rms(p[:, 768:1024], e64_ref[...], gain(2, 256), HEAD_DIM), tb_ref, HEAD_DIM // 4)
    bk = _group_rms(p[:, 1024:1152], e64_ref[0:128, 0:128], gain(3, 128), HEAD_DIM)
    bk = jnp.where(is_lat, bk * tb_ref[0, :, 0:128] + pltpu.roll(bk, 128 - 16, 1) * tb_ref[1, :, 0:128]
                   + pltpu.roll(bk, 16, 1) * tb_ref[2, :, 0:128], bk)
    bv = p[:, 1152:1280]
    heads_out(bq_ref, bq * (HEAD_DIM ** -0.5 * LOG2E), B_HEADS, HEAD_DIM)
    heads_out(bk_ref, bk, B_KV, HEAD_DIM)
    heads_out(bv_ref, bv, B_KV, HEAD_DIM)
    bkt_ref[...] = bk
    bvt_ref[...] = bv
    cq = maybe_rope(_group_rms(p[:, 1280:1536], e64_ref[...], gain(4, 256), HEAD_DIM), tb_ref, HEAD_DIM // 4)
    ck = _group_rms(p[:, 1536:1664], e64_ref[0:128, 0:128], gain(5, 128), HEAD_DIM)
    ck = jnp.where(is_lat, ck * tb_ref[0, :, 0:128] + pltpu.roll(ck, 128 - 16, 1) * tb_ref[1, :, 0:128]
                   + pltpu.roll(ck, 16, 1) * tb_ref[2, :, 0:128], ck)
    cv = p[:, 1664:1792]
    heads_out(cq_ref, cq * (HEAD_DIM ** -0.5 * LOG2E), C_HEADS, HEAD_DIM)
    heads_out(ck_ref, ck, C_KV, HEAD_DIM)
    heads_out(cv_ref, cv, C_KV, HEAD_DIM)
    ckt_ref[...] = ck
    cvt_ref[...] = cv
    dqa = _row_rms(p[:, 1792:2048], gain(6, 256))
    dq = _dot(dqa.astype(BF16), wdq_ref[...])
    dq = maybe_rope(_group_rms(dq, e96_ref[...], gain(8, GAIN_W), D_QK), td_ref, D_ROPE // 4)
    heads_out(dq_ref, dq * (D_QK ** -0.5 * LOG2E), D_HEADS, D_QK)
    dckv_ref[...] = _row_rms(p[:, 2048:2176], gain(7, 128))
    dkr_ref[...] = p[:, 2176:2208]


def _input_stage(x, mod3, n1g, ws, gains, e32, e64, e96, wdq, tab_a, tab_b, tab_d, *, tm=256):
    n = x.shape[0]
    nt = n // tm
    lat0 = N_CTX // tm
    pos_blocks = DEC_SEQ // tm

    def tok(i):
        return (i, 0)

    def pos(i):
        return (0, jnp.maximum(i - lat0, 0) % pos_blocks, 0)

    def const2(i):
        return (0, 0)

    def hm(i):
        return (0, i, 0)

    def hspec(nh, d):
        return pl.BlockSpec((nh, tm, d), hm)

    def hshape(nh, d):
        return jax.ShapeDtypeStruct((nh, n, d), BF16)

    def tspec(w):
        return pl.BlockSpec((tm, w), tok)

    def tshape(w):
        return jax.ShapeDtypeStruct((n, w), F32)

    in_specs = [
        pl.BlockSpec((tm, D_MODEL), tok),
        pl.BlockSpec((None, 6, D_MODEL), lambda i: ((i * tm) // DEC_SEQ, 0, 0)),
        pl.BlockSpec((1, D_MODEL), const2),
        pl.BlockSpec((D_MODEL, SMALL_PAD), const2),
        pl.BlockSpec((GAIN_ROWS, GAIN_W), const2),
        pl.BlockSpec((256, 256), const2),
        pl.BlockSpec((256, 256), const2),
        pl.BlockSpec((GAIN_W, GAIN_W), const2),
        pl.BlockSpec((D_Q_LORA, GAIN_W), const2),
        pl.BlockSpec((3, tm, 256), pos),
        pl.BlockSpec((3, tm, 256), pos),
        pl.BlockSpec((3, tm, GAIN_W), pos),
    ]
    out_specs = [
        pl.BlockSpec((tm, D_MODEL), tok),
        hspec(8, A_QK), hspec(8, A_QK), hspec(4, A_V),
        hspec(4, HEAD_DIM), hspec(2, HEAD_DIM), hspec(2, HEAD_DIM),
        hspec(4, HEAD_DIM), hspec(2, HEAD_DIM), hspec(2, HEAD_DIM),
        hspec(4, D_QK),
        tspec(256), tspec(256), tspec(128), tspec(128), tspec(128), tspec(128), tspec(128), tspec(D_ROPE),
    ]
    out_shape = [
        jax.ShapeDtypeStruct((n, D_MODEL), BF16),
        hshape(8, A_QK), hshape(8, A_QK), hshape(4, A_V),
        hshape(4, HEAD_DIM), hshape(2, HEAD_DIM), hshape(2, HEAD_DIM),
        hshape(4, HEAD_DIM), hshape(2, HEAD_DIM), hshape(2, HEAD_DIM),
        hshape(4, D_QK),
        tshape(256), tshape(256), tshape(128), tshape(128), tshape(128), tshape(128), tshape(128), tshape(D_ROPE),
    ]
    return pl.pallas_call(
        functools.partial(_in_kernel, tm=tm, n_ctx=N_CTX),
        grid=(nt,),
        in_specs=in_specs,
        out_specs=out_specs,
        out_shape=out_shape,
        compiler_params=_params(("arbitrary",)),
        name="input_stage",
    )(x, mod3, n1g, ws, gains, e32, e64, e96, wdq, tab_a, tab_b, tab_d)


def _mla_kernel(ckv_ref, kr_ref, wkvb_ref, g_ref, rot_ref, tab_ref, kd_ref, vd_ref, *, tm, lat_tiles, per_batch_tiles):
    i = pl.program_id(0)
    rotate = jnp.logical_and(i < lat_tiles, i % per_batch_tiles >= PAST_LEN // tm)
    kv = _dot(ckv_ref[...].astype(BF16), wkvb_ref[...])
    kr = kr_ref[...]
    ssr = jnp.sum(kr * kr, axis=-1, keepdims=True)
    z = kr * g_ref[:, D_NOPE:D_QK]
    zh = z.astype(BF16)
    zl = (z - zh.astype(F32)).astype(BF16)
    rot = _dot(zh, rot_ref[...]) + _dot(zl, rot_ref[...])
    rz = jnp.where(rotate, z * tab_ref[0] + rot * tab_ref[1], z)
    g_nope = g_ref[:, 0:D_NOPE]
    for hd in range(D_HEADS):
        kn = kv[:, hd * 128:hd * 128 + D_NOPE]
        r = lax.rsqrt((jnp.sum(kn * kn, axis=-1, keepdims=True) + ssr) * (1.0 / D_QK) + EPS)
        kd_ref[hd] = jnp.concatenate([kn * r * g_nope, rz * r], axis=-1).astype(BF16)
        vd_ref[hd] = kv[:, hd * 128 + D_NOPE:(hd + 1) * 128].astype(BF16)


def _mla_expand(ckv, kr, wkvb, g, rot, tab, *, tm=512):
    rows = ckv.shape[0]
    per_batch_tiles = KV_LAT // tm
    lat_tiles = DEC_BATCH * per_batch_tiles
    past_tiles = PAST_LEN // tm
    pos_blocks = DEC_SEQ // tm

    def pos(i):
        return (0, jnp.maximum(i % per_batch_tiles - past_tiles, 0) % pos_blocks, 0)

    return pl.pallas_call(
        functools.partial(_mla_kernel, tm=tm, lat_tiles=lat_tiles, per_batch_tiles=per_batch_tiles),
        grid=(rows // tm,),
        in_specs=[
            pl.BlockSpec((tm, D_KV_LORA), lambda i: (i, 0)),
            pl.BlockSpec((tm, D_ROPE), lambda i: (i, 0)),
            pl.BlockSpec((D_KV_LORA, D_HEADS * (D_NOPE + D_V)), lambda i: (0, 0)),
            pl.BlockSpec((1, D_QK), lambda i: (0, 0)),
            pl.BlockSpec((D_ROPE, D_ROPE), lambda i: (0, 0)),
            pl.BlockSpec((2, tm, D_ROPE), pos),
        ],
        out_specs=[
            pl.BlockSpec((D_HEADS, tm, D_QK), lambda i: (0, i, 0)),
            pl.BlockSpec((D_HEADS, tm, D_V), lambda i: (0, i, 0)),
        ],
        out_shape=[
            jax.ShapeDtypeStruct((D_HEADS, rows, D_QK), BF16),
            jax.ShapeDtypeStruct((D_HEADS, rows, D_V), BF16),
        ],
        compiler_params=_params(("arbitrary",)),
        name="mla_expand",
    )(ckv, kr, wkvb, g, rot, tab)


def _attn_kernel(*refs, g_q, g_k, g_v, dv, mode, lam_init, has_cache):
    refs = list(refs)
    q_ref, k_ref, v_ref = refs[:3]
    pos = 3
    sink_ref = lam_ref = sub_ref = None
    if has_cache:
        kc_ref, vc_ref = refs[pos], refs[pos + 1]
        pos += 2
    if mode == "sink":
        sink_ref = refs[pos]
        pos += 1
    elif mode == "diff":
        lam_ref, sub_ref = refs[pos], refs[pos + 1]
        pos += 2
    o_ref = refs[pos]
    grp = pl.program_id(1)

    if has_cache:
        kall_ref, vall_ref = refs[pos + 1], refs[pos + 2]
        past = kc_ref.shape[1]

        @pl.when(pl.program_id(2) == 0)
        def _join():
            kall_ref[:, 0:past, :] = kc_ref[...]
            kall_ref[:, past:, :] = k_ref[...]
            vall_ref[:, 0:past, :] = vc_ref[...]
            vall_ref[:, past:, :] = v_ref[...]

        k_ref, v_ref = kall_ref, vall_ref

    def probs(g):
        s = _nt_dot(q_ref[g], k_ref[g * g_k // g_q])
        m = jnp.max(s, axis=-1, keepdims=True)
        if mode == "sink":
            sk = sink_ref[grp * g_q + g] * LOG2E
            m = jnp.maximum(m, sk)
        p = jnp.exp2(s - m)
        den = jnp.sum(p, axis=-1, keepdims=True)
        if mode == "sink":
            den = den + jnp.exp2(sk - m)
        return p, den

    if mode == "diff":
        lp = lam_ref[...]
        lam = (jnp.exp(jnp.sum(lp[0:1] * lp[1:2], axis=-1, keepdims=True))
               - jnp.exp(jnp.sum(lp[2:3] * lp[3:4], axis=-1, keepdims=True)) + lam_init)
        for i in range(g_q // 2):
            p1, den1 = probs(2 * i)
            p2, den2 = probs(2 * i + 1)
            w1 = 1.0 / den1
            comb = p1 - p2 * (lam * den1 / den2)
            d = _dot(comb.astype(BF16), v_ref[i * g_v // (g_q // 2)]) * w1
            o_ref[:, i * dv:(i + 1) * dv] = _row_rms(d, sub_ref[...]) * (1.0 - lam_init)
    else:
        for g in range(g_q):
            p, den = probs(g)
            o_ref[:, g * dv:(g + 1) * dv] = _dot(p.astype(BF16), v_ref[g * g_v // g_q]) * (1.0 / den)


def _attention(q, k, v, cache, extras, *, batch, tq_rows, tk_rows, q_off, k_off,
               g_q, g_k, g_v, dv, mode, lam_init, tq):
    hq, _, dq = q.shape
    n_groups = hq // g_q
    g_out = g_q // 2 if mode == "diff" else g_q
    nq = tq_rows // tq
    tk = tk_rows
    in_specs = [
        pl.BlockSpec((g_q, tq, dq), lambda b, g, qi: (g, (q_off + b * tq_rows) // tq + qi, 0)),
        pl.BlockSpec((g_k, tk, dq), lambda b, g, qi: (g, (k_off + b * tk_rows) // tk, 0)),
        pl.BlockSpec((g_v, tk, dv), lambda b, g, qi: (g, (k_off + b * tk_rows) // tk, 0)),
    ]
    args = [q, k, v]
    if cache is not None:
        in_specs.append(pl.BlockSpec((g_k, PAST_LEN, dq), lambda b, g, qi: (g, b, 0)))
        in_specs.append(pl.BlockSpec((g_v, PAST_LEN, dv), lambda b, g, qi: (g, b, 0)))
        args += list(cache)
    if mode == "sink":
        in_specs.append(pl.BlockSpec(memory_space=pltpu.SMEM))
        args.append(extras[0])
    elif mode == "diff":
        in_specs.append(pl.BlockSpec((4, A_QK), lambda b, g, qi: (0, 0)))
        in_specs.append(pl.BlockSpec((1, dv), lambda b, g, qi: (0, 0)))
        args += list(extras)
    return pl.pallas_call(
        functools.partial(_attn_kernel, g_q=g_q, g_k=g_k, g_v=g_v, dv=dv, mode=mode, lam_init=lam_init,
                          has_cache=cache is not None),
        grid=(batch, n_groups, nq),
        in_specs=in_specs,
        out_specs=pl.BlockSpec((tq, g_out * dv), lambda b, g, qi: (b * nq + qi, g)),
        out_shape=jax.ShapeDtypeStruct((batch * tq_rows, n_groups * g_out * dv), F32),
        scratch_shapes=([pltpu.VMEM((g_k, PAST_LEN + tk, dq), BF16), pltpu.VMEM((g_v, PAST_LEN + tk, dv), BF16)]
                        if cache is not None else []),
        compiler_params=_params(("arbitrary", "arbitrary", "arbitrary")),
        name="attention_" + mode,
    )(*args)


def _win_kernel(q_ref, k_ref, v_ref, kc_ref, vc_ref, sink_ref, o_ref, *, g_q, dv, tq, seq, band):
    grp = pl.program_id(1)
    qi = pl.program_id(2)
    start = pl.multiple_of(jnp.clip(qi * tq - WINDOW, 0, seq - band), WINDOW)
    k_ctx = kc_ref[...]
    v_ctx = vc_ref[...]
    k_band = k_ref[pl.ds(start, band), :]
    v_band = v_ref[pl.ds(start, band), :]
    qpos = qi * tq + lax.broadcasted_iota(jnp.int32, (tq, band), 0)
    kpos = start + lax.broadcasted_iota(jnp.int32, (tq, band), 1)
    valid = jnp.abs(kpos - qpos) <= WINDOW
    for g in range(g_q):
        q = q_ref[g]
        s_ctx = _nt_dot(q, k_ctx)
        s_band = jnp.where(valid, _nt_dot(q, k_band), NEG_INF)
        sk = sink_ref[grp * g_q + g] * LOG2E
        m = jnp.maximum(jnp.maximum(jnp.max(s_ctx, axis=-1, keepdims=True),
                                    jnp.max(s_band, axis=-1, keepdims=True)), sk)
        p_ctx = jnp.exp2(s_ctx - m)
        p_band = jnp.exp2(s_band - m)
        den = (jnp.sum(p_ctx, axis=-1, keepdims=True) + jnp.sum(p_band, axis=-1, keepdims=True)
               + jnp.exp2(sk - m))
        num = _dot(p_ctx.astype(BF16), v_ctx) + _dot(p_band.astype(BF16), v_band)
        o_ref[:, g * dv:(g + 1) * dv] = num * (1.0 / den)


def _window_attention(q, k, v, kc, vc, sink, *, q_off, k_off, tq=256):
    hq, _, d = q.shape
    hk = k.shape[0]
    g_q = hq // hk
    band = tq + 2 * WINDOW
    nq = DEC_SEQ // tq
    return pl.pallas_call(
        functools.partial(_win_kernel, g_q=g_q, dv=d, tq=tq, seq=DEC_SEQ, band=band),
        grid=(DEC_BATCH, hk, nq),
        in_specs=[
            pl.BlockSpec((g_q, tq, d), lambda b, g, qi: (g, (q_off + b * DEC_SEQ) // tq + qi, 0)),
            pl.BlockSpec((None, DEC_SEQ, d), lambda b, g, qi: (g, k_off // DEC_SEQ + b, 0)),
            pl.BlockSpec((None, DEC_SEQ, d), lambda b, g, qi: (g, k_off // DEC_SEQ + b, 0)),
            pl.BlockSpec((None, PAST_LEN, d), lambda b, g, qi: (g, b, 0)),
            pl.BlockSpec((None, PAST_LEN, d), lambda b, g, qi: (g, b, 0)),
            pl.BlockSpec(memory_space=pltpu.SMEM),
        ],
        out_specs=pl.BlockSpec((tq, g_q * d), lambda b, g, qi: (b * nq + qi, g)),
        out_shape=jax.ShapeDtypeStruct((DEC_BATCH * DEC_SEQ, hq * d), F32),
        compiler_params=_params(("arbitrary", "arbitrary", "arbitrary")),
        name="attention_window",
    )(q, k, v, kc, vc, sink)


def _merge_kernel(x_ref, h_ref, oac_ref, oal_ref, obc_ref, obl_ref, occ_ref, ocl_ref, odc_ref, odl_ref,
                  mod_ref, wg_ref, wb_ref, wo_ref, n2g_ref, x1_ref, h2_ref, *, ctx_tiles):
    is_ctx = pl.program_id(0) < ctx_tiles
    hb = h_ref[...]
    outs = ((oac_ref, oal_ref), (obc_ref, obl_ref), (occ_ref, ocl_ref), (odc_ref, odl_ref))
    acc = None
    for i in range(N_BRANCH):
        gt = jax.nn.sigmoid(_dot(hb, wg_ref[:, i * D_MODEL:(i + 1) * D_MODEL]))
        o = jnp.where(is_ctx, outs[i][0][...], outs[i][1][...])
        br = _dot(o.astype(BF16), wb_ref[i])
        acc = gt * br if acc is None else acc + gt * br
    y = _dot(acc.astype(BF16), wo_ref[...])
    x1 = x_ref[...] + mod_ref[2:3, :] * y
    x1_ref[...] = x1
    h2 = _row_rms(x1, n2g_ref[...]) * (1.0 + mod_ref[4:5, :]) + mod_ref[3:4, :]
    h2_ref[...] = h2.astype(BF16)


def _merge_stage(x, h, branches, mod3, wg, wb, wo, n2g, *, tm=256):
    n = x.shape[0]
    ctx_tiles = N_CTX // tm

    def tok(i):
        return (i, 0)

    ctx_spec = pl.BlockSpec((tm, BRANCH_W), lambda i: (jnp.minimum(i, ctx_tiles - 1), 0))
    lat_spec = pl.BlockSpec((tm, BRANCH_W), lambda i: (jnp.maximum(i - ctx_tiles, 0), 0))
    return pl.pallas_call(
        functools.partial(_merge_kernel, ctx_tiles=ctx_tiles),
        grid=(n // tm,),
        in_specs=[
            pl.BlockSpec((tm, D_MODEL), tok),
            pl.BlockSpec((tm, D_MODEL), tok),
            ctx_spec, lat_spec, ctx_spec, lat_spec, ctx_spec, lat_spec, ctx_spec, lat_spec,
            pl.BlockSpec((None, 6, D_MODEL), lambda i: ((i * tm) // DEC_SEQ, 0, 0)),
            pl.BlockSpec((D_MODEL, N_BRANCH * D_MODEL), lambda i: (0, 0)),
            pl.BlockSpec((N_BRANCH, BRANCH_W, D_MODEL), lambda i: (0, 0, 0)),
            pl.BlockSpec((D_MODEL, D_MODEL), lambda i: (0, 0)),
            pl.BlockSpec((1, D_MODEL), lambda i: (0, 0)),
        ],
        out_specs=[pl.BlockSpec((tm, D_MODEL), tok), pl.BlockSpec((tm, D_MODEL), tok)],
        out_shape=[jax.ShapeDtypeStruct((n, D_MODEL), F32), jax.ShapeDtypeStruct((n, D_MODEL), BF16)],
        compiler_params=_params(("arbitrary",)),
        name="merge_stage",
    )(x, h, *[part for pair in branches for part in pair], mod3, wg, wb, wo, n2g)


def _oddeven_mergesort_pairs(n):
    pairs = []

    def merge(lo, m, r):
        step = 2 * r
        if step < m:
            merge(lo, m, step)
            merge(lo + r, m, step)
            for i in range(lo + r, lo + m - r, step):
                pairs.append((i, i + r))
        else:
            pairs.append((lo, lo + r))

    def sort(lo, m):
        if m > 1:
            sort(lo, m // 2)
            sort(lo + m // 2, m // 2)
            merge(lo, m, 1)

    sort(0, n)
    return pairs


_SORT_PAIRS = _oddeven_mergesort_pairs(PEER_TOPK)


def _exchange(xs, i, j):
    xs[i], xs[j] = jnp.maximum(xs[i], xs[j]), jnp.minimum(xs[i], xs[j])


def _top16_of_groups(groups):
    n = len(groups)
    xs = list(groups)
    for i, j in _SORT_PAIRS:
        _exchange(xs, i, j)
    for shift in (4, 2, 1):
        xs = [jnp.maximum(xs[k], pltpu.roll(xs[n - 1 - k], shift, 0)) for k in range(n)]
        s = n // 2
        while s >= 1:
            for i in range(n):
                if not i & s:
                    _exchange(xs, i, i + s)
            s //= 2
    return xs


def _sublane_sum(x):
    for shift in (4, 2, 1):
        x = x + pltpu.roll(x, shift, 0)
    return x


def _sublane_spread(vals):
    row = lax.broadcasted_iota(jnp.int32, vals[0].shape, 0)
    out = vals[-1]
    for j in range(len(vals) - 2, -1, -1):
        out = jnp.where(row == j, vals[j], out)
    return out


def _peer_select(g1, g2):
    k = PEER_TOPK
    half = k // 2
    v1 = _top16_of_groups(g1)
    v2 = _top16_of_groups(g2)
    v2lo = _sublane_spread(v2[:half])
    v2hi = _sublane_spread(v2[half:])
    v1hi = _sublane_spread(v1[half:])
    cands = [v1[0] + v2lo, v1[0] + v2hi] + [v1[i] + v2lo for i in range(1, half)] + [v1hi + v2[0]]
    pad = jnp.full(cands[0].shape, EXHAUSTED, F32)
    top = _top16_of_groups(cands + [pad] * (k - len(cands)))
    tau = top[k - 1]
    z = None
    for c in cands:
        e = jnp.where(c >= tau, jnp.exp(c - top[0]), 0.0)
        z = e if z is None else z + e
    z = _sublane_sum(z)
    thr = [jnp.full(g1[0].shape, -1.0, F32) for _ in g1]
    q2 = [jnp.full(g2[0].shape, float(k), F32) for _ in g2]
    for a in range(k - 1, -1, -1):
        cnt = _sublane_sum(jnp.where(v1[a] + v2lo >= tau, 1.0, 0.0) + jnp.where(v1[a] + v2hi >= tau, 1.0, 0.0))
        thr = [jnp.where(x == v1[a], cnt - 1.0, t) for x, t in zip(g1, thr)]
        q2 = [jnp.where(x == v2[a], float(a), q) for x, q in zip(g2, q2)]
    p1 = [jnp.exp(x - v1[0]) for x in g1]
    scale = 0.5 / z
    p2 = [jnp.exp(x - v2[0]) * scale for x in g2]
    return thr, p1, q2, p2


def _peer_query_kernel(h2_ref, wqt_ref, sk_ref, thr_ref, p1_ref, q2_ref, p2_ref):
    qt = _nt_dot(wqt_ref[...], h2_ref[...]).astype(BF16)
    n_lt = qt.shape[1] // 128
    n_grp = PEER_KEYS // 8
    for hd in range(PEER_HEADS):
        s1 = _dot(sk_ref[2 * hd], qt[(2 * hd) * PEER_HALF:(2 * hd + 1) * PEER_HALF, :])
        s2 = _dot(sk_ref[2 * hd + 1], qt[(2 * hd + 1) * PEER_HALF:(2 * hd + 2) * PEER_HALF, :])
        for lt in range(n_lt):
            lanes = slice(lt * 128, (lt + 1) * 128)
            dst = (lt + 1) % n_lt
            rot = slice(dst * 128, (dst + 1) * 128)
            g1 = [s1[8 * i:8 * i + 8, lanes] for i in range(n_grp)]
            g2 = [s2[8 * i:8 * i + 8, lanes] for i in range(n_grp)]
            thr, p1, q2, p2 = _peer_select(g1, g2)
            thr_ref[hd, :, lanes] = jnp.concatenate(thr, axis=0)
            p1_ref[hd, :, lanes] = jnp.concatenate(p1, axis=0)
            q2_ref[hd, :, lanes] = jnp.concatenate(q2, axis=0).astype(BF16)
            p2_ref[hd, :, rot] = jnp.concatenate(p2, axis=0).astype(BF16)


def _peer_query(h2, wqt, sk, *, tm=512):
    n = h2.shape[0]
    big = jax.ShapeDtypeStruct((PEER_HEADS, PEER_KEYS, n), F32)
    half = jax.ShapeDtypeStruct((PEER_HEADS, PEER_KEYS, n), BF16)
    bspec = pl.BlockSpec((PEER_HEADS, PEER_KEYS, tm), lambda i: (0, 0, i))
    return pl.pallas_call(
        _peer_query_kernel,
        grid=(n // tm,),
        in_specs=[
            pl.BlockSpec((tm, D_MODEL), lambda i: (i, 0)),
            pl.BlockSpec((PEER_HEADS * PEER_QDIM, D_MODEL), lambda i: (0, 0)),
            pl.BlockSpec((2 * PEER_HEADS, PEER_KEYS, PEER_HALF), lambda i: (0, 0, 0)),
        ],
        out_specs=[bspec, bspec, bspec, bspec],
        out_shape=[big, big, half, half],
        compiler_params=_params(("arbitrary",)),
        name="peer_query",
    )(h2, wqt, sk)


def _peer_dense_kernel(h2_ref, x1_ref, mod_ref, u_ref, vt_ref, thr_ref, p1_ref, q2_ref, p2_ref,
                       *rest, e1_per_step, split_tiles):
    out_refs = rest[:-5]
    acc_ref, g_ref, p_ref, q2s_ref, p2s_ref = rest[-5:]
    step = pl.program_id(1)
    tm = g_ref.shape[1]
    n_lt = tm // 128
    rows = 16
    n_r = PEER_KEYS // rows

    def gate_pass():
        for c in range(e1_per_step):
            for lt in range(n_lt):
                lanes = slice(lt * 128, (lt + 1) * 128)
                rot = slice(((lt + 1) % n_lt) * 128, ((lt + 1) % n_lt + 1) * 128)
                g = [None] * n_r
                for hd in range(PEER_HEADS):
                    thr_b = jnp.broadcast_to(thr_ref[hd, c:c + 1, lanes], (rows, 128)).astype(BF16)
                    p1_b = jnp.broadcast_to(p1_ref[hd, c:c + 1, lanes], (rows, 128)).astype(BF16)
                    for r in range(n_r):
                        rs = slice(r * rows, (r + 1) * rows)
                        w = jnp.where(q2s_ref[hd, rs, lanes] <= thr_b, p2s_ref[hd, rs, rot] * p1_b,
                                      jnp.zeros((), BF16))
                        g[r] = w if g[r] is None else g[r] + w
                for r in range(n_r):
                    g_ref[c * PEER_KEYS + r * rows:c * PEER_KEYS + (r + 1) * rows, lanes] = g[r]

    @pl.when(step == 0)
    def _init():
        acc_ref[...] = jnp.zeros_like(acc_ref)
        q2s_ref[...] = q2_ref[...]
        p2s_ref[...] = p2_ref[...]

    pl.when(step >= 0)(gate_pass)

    a = _nt_dot(u_ref[...], h2_ref[...])
    act = (a + a * lax.erf(a * (2.0 ** -0.5))).astype(BF16)
    p_ref[...] = g_ref[...] * act
    acc_ref[...] += _dot(vt_ref[...], p_ref[...])

    @pl.when(step == pl.num_programs(1) - 1)
    def _fin():
        val = x1_ref[...] + mod_ref[5:6, :] * acc_ref[...].T
        if split_tiles is None:
            out_refs[0][...] = val
        else:
            is_ctx = pl.program_id(0) < split_tiles

            @pl.when(is_ctx)
            def _ctx_rows():
                out_refs[0][...] = val

            @pl.when(jnp.logical_not(is_ctx))
            def _lat_rows():
                out_refs[1][...] = val


def _peer_dense(h2, x1, mod3, u, vt, thr, p1, q2, p2, *, layer, split=False, tm=512, e1_per_step=16):
    n = h2.shape[0]
    te = e1_per_step * PEER_KEYS
    tok = lambda i, j: (i, 0)
    big = pl.BlockSpec((PEER_HEADS, PEER_KEYS, tm), lambda i, j: (0, 0, i))
    rows = pl.BlockSpec((PEER_HEADS, e1_per_step, tm), lambda i, j: (0, j, i))
    ctx_tiles = N_CTX // tm
    if split:
        out_specs = [pl.BlockSpec((tm, D_MODEL), lambda i, j: (jnp.minimum(i, ctx_tiles - 1), 0)),
                     pl.BlockSpec((tm, D_MODEL), lambda i, j: (jnp.maximum(i - ctx_tiles, 0), 0))]
        out_shape = [jax.ShapeDtypeStruct((N_CTX, D_MODEL), F32), jax.ShapeDtypeStruct((n - N_CTX, D_MODEL), F32)]
    else:
        out_specs = pl.BlockSpec((tm, D_MODEL), tok)
        out_shape = jax.ShapeDtypeStruct((n, D_MODEL), F32)
    return pl.pallas_call(
        functools.partial(_peer_dense_kernel, e1_per_step=e1_per_step, split_tiles=ctx_tiles if split else None),
        grid=(n // tm, PEER_EXPERTS // te),
        in_specs=[
            pl.BlockSpec((tm, D_MODEL), tok),
            pl.BlockSpec((tm, D_MODEL), tok),
            pl.BlockSpec((None, 6, D_MODEL), lambda i, j: ((i * tm) // DEC_SEQ, 0, 0)),
            pl.BlockSpec((None, te, D_MODEL), lambda i, j: (layer, j, 0)),
            pl.BlockSpec((None, D_MODEL, te), lambda i, j: (layer, 0, j)),
            rows, rows, big, big,
        ],
        out_specs=out_specs,
        out_shape=out_shape,
        scratch_shapes=[pltpu.VMEM((D_MODEL, tm), F32), pltpu.VMEM((te, tm), BF16), pltpu.VMEM((te, tm), BF16),
                        pltpu.VMEM((PEER_HEADS, PEER_KEYS, tm), BF16), pltpu.VMEM((PEER_HEADS, PEER_KEYS, tm), BF16)],
        compiler_params=_params(("arbitrary", "arbitrary")),
        name="peer_dense",
    )(h2, x1, mod3, u, vt, thr, p1, q2, p2)


def _block_ones(width, d):
    idx = jnp.arange(width) // d
    return (idx[:, None] == idx[None, :]).astype(BF16)


def _rope_angles(d):
    q4 = d // 4
    inv = ROPE_BASE ** (-jnp.arange(q4, dtype=F32) / q4)
    t = jnp.arange(DEC_SEQ)
    ar = (t // GRID_W).astype(F32)[:, None] * inv
    ac = (t % GRID_W).astype(F32)[:, None] * inv
    return jnp.concatenate([ar, ar, ac, ac], axis=-1)


def _rope_table(d, group, n_groups):
    ang = _rope_angles(d)
    q4 = d // 4
    first = (jnp.arange(d) % (2 * q4)) < q4
    cos = jnp.cos(ang)
    sin = jnp.sin(ang)
    sa = jnp.where(first[None, :], -sin, 0.0)
    sb = jnp.where(first[None, :], 0.0, sin)
    pad = group - d
    cos = jnp.pad(cos, ((0, 0), (pad, 0)), constant_values=1.0)
    sa = jnp.pad(sa, ((0, 0), (pad, 0)))
    sb = jnp.pad(sb, ((0, 0), (pad, 0)))
    return jnp.stack([jnp.tile(cos, (1, n_groups)), jnp.tile(sa, (1, n_groups)), jnp.tile(sb, (1, n_groups))])


def _rot_matrix(d):
    q4 = d // 4
    i = jnp.arange(d)
    first = (i % (2 * q4)) < q4
    src = jnp.where(first, i + q4, i - q4)
    sign = jnp.where(first, -1.0, 1.0)
    return (jnp.zeros((d, d), F32).at[src, i].set(sign)).astype(BF16)


def _pad_row(v, width):
    return jnp.pad(v, (0, width - v.shape[0]))


def kernel(x_prompt, x_sample, c, cache_a_k, cache_a_v, cache_b_k, cache_b_v, cache_c_k, cache_c_v, cache_d_ckv, cache_d_krope, c_ctx, ada_w, ada_b, norm1_g, norm2_g, w_in, a_lam, a_qk_g, a_subln_g, b_sink, b_qk_g, c_qk_g, d_qnorm_g, d_kvnorm_g, w_d_qb, w_d_kvb, d_qk_g, w_branch, w_out, peer_wq, peer_subkeys, peer_u, peer_v):
    x = jnp.concatenate([x_prompt.reshape(N_CTX, D_MODEL), x_sample.reshape(N_LAT, D_MODEL)], axis=0)
    cvec = jnp.concatenate([c_ctx[None, :], c, jnp.zeros((N_MODROWS - 1 - DEC_BATCH, D_MODEL), F32)], axis=0)
    mod_all = _modulation(cvec, ada_w, ada_b)

    e32 = _block_ones(256, A_QK)
    e64 = _block_ones(256, HEAD_DIM)
    e96 = _block_ones(GAIN_W, D_QK)
    tab_a = _rope_table(A_QK, A_QK, 2 * A_HEADS)
    tab_b = _rope_table(HEAD_DIM, HEAD_DIM, B_HEADS)
    tab_d = _rope_table(D_ROPE, D_QK, D_HEADS)
    ang_r = _rope_angles(D_ROPE)
    tab_r = jnp.stack([jnp.cos(ang_r), jnp.sin(ang_r)])
    rot_r = _rot_matrix(D_ROPE)

    u_all = peer_u.astype(BF16)
    vt_all = jnp.swapaxes(peer_v, 1, 2).astype(BF16)
    states = [[] for _ in range(8)]
    for l in range(DEPTH):
        lam_init = 0.8 - 0.6 * math.exp(-0.3 * l)
        mod3 = mod_all[l, :3].reshape(3, 6, D_MODEL)
        ws = jnp.pad(w_in[l, :, :SMALL_COLS], ((0, 0), (0, SMALL_PAD - SMALL_COLS))).astype(BF16)
        wg = w_in[l, :, SMALL_COLS:].astype(BF16)
        gains = jnp.stack([
            _pad_row(jnp.tile(a_qk_g[l, 0], 2 * A_HEADS), GAIN_W),
            _pad_row(jnp.tile(a_qk_g[l, 1], 2 * A_HEADS), GAIN_W),
            _pad_row(jnp.tile(b_qk_g[l, 0], B_HEADS), GAIN_W),
            _pad_row(jnp.tile(b_qk_g[l, 1], B_KV), GAIN_W),
            _pad_row(jnp.tile(c_qk_g[l, 0], C_HEADS), GAIN_W),
            _pad_row(jnp.tile(c_qk_g[l, 1], C_KV), GAIN_W),
            _pad_row(d_qnorm_g[l], GAIN_W),
            _pad_row(d_kvnorm_g[l], GAIN_W),
            jnp.tile(d_qk_g[l, 0], D_HEADS),
        ] + [jnp.zeros((GAIN_W,), F32)] * (GAIN_ROWS - 9))

        (h, aq, ak, av, bq, bk, bv, cq, ck, cv, dq,
         akt, avt, bkt, bvt, ckt, cvt, dckv, dkr) = _input_stage(
            x, mod3, norm1_g[l][None, :], ws, gains, e32, e64, e96, w_d_qb[l].astype(BF16), tab_a, tab_b, tab_d,
            tm=512)

        for lst, s_ in zip(states, (
                akt[:N_CTX].reshape(BATCH, SEQ, A_HEADS, 2, A_QK), avt[:N_CTX].reshape(BATCH, SEQ, A_HEADS, A_V),
                bkt[:N_CTX].reshape(BATCH, SEQ, B_KV, HEAD_DIM), bvt[:N_CTX].reshape(BATCH, SEQ, B_KV, HEAD_DIM),
                ckt[:N_CTX].reshape(BATCH, SEQ, C_KV, HEAD_DIM), cvt[:N_CTX].reshape(BATCH, SEQ, C_KV, HEAD_DIM),
                dckv[:N_CTX].reshape(BATCH, SEQ, D_KV_LORA), dkr[:N_CTX].reshape(BATCH, SEQ, D_ROPE))):
            lst.append(s_)

        ckv_rows = jnp.concatenate([t for b in range(DEC_BATCH) for t in (
            cache_d_ckv[b, l], dckv[N_CTX + b * DEC_SEQ:N_CTX + (b + 1) * DEC_SEQ])] + [dckv[:N_CTX]], axis=0)
        kr_rows = jnp.concatenate([t for b in range(DEC_BATCH) for t in (
            cache_d_krope[b, l], dkr[N_CTX + b * DEC_SEQ:N_CTX + (b + 1) * DEC_SEQ])] + [dkr[:N_CTX]], axis=0)
        kd, vd = _mla_expand(ckv_rows, kr_rows, w_d_kvb[l].astype(BF16), d_qk_g[l, 1][None, :], rot_r, tab_r)

        def cached(cache, nh, d):
            cache = jnp.moveaxis(cache.reshape(DEC_BATCH, PAST_LEN, nh, d), 2, 0)
            return cache.reshape(nh, DEC_BATCH * PAST_LEN, d).astype(BF16)

        a_cache = (cached(cache_a_k[:, l], 2 * A_HEADS, A_QK), cached(cache_a_v[:, l], A_HEADS, A_V))
        b_cache = (cached(cache_b_k[:, l], B_KV, HEAD_DIM), cached(cache_b_v[:, l], B_KV, HEAD_DIM))
        c_cache = (cached(cache_c_k[:, l], C_KV, HEAD_DIM), cached(cache_c_v[:, l], C_KV, HEAD_DIM))

        ctx = dict(batch=BATCH, tq_rows=SEQ, tk_rows=SEQ, q_off=0, lam_init=lam_init, tq=SEQ)
        lat = dict(batch=DEC_BATCH, tq_rows=DEC_SEQ, tk_rows=DEC_SEQ, q_off=N_CTX, k_off=N_CTX,
                   lam_init=lam_init, tq=256)
        a_extra = (a_lam[l], a_subln_g[l][None, :])
        oa = (_attention(aq, ak, av, None, a_extra, k_off=0, g_q=8, g_k=8, g_v=4, dv=A_V, mode="diff", **ctx),
              _attention(aq, ak, av, a_cache, a_extra, g_q=4, g_k=4, g_v=2, dv=A_V, mode="diff", **lat))
        ob = (_attention(bq, bk, bv, None, (b_sink[l],), k_off=0, g_q=4, g_k=2, g_v=2, dv=HEAD_DIM, mode="sink",
                         **ctx),
              _window_attention(bq, bk, bv, *b_cache, b_sink[l], q_off=N_CTX, k_off=N_CTX))
        oc = (_attention(cq, ck, cv, None, (), k_off=0, g_q=4, g_k=2, g_v=2, dv=HEAD_DIM, mode="plain", **ctx),
              _attention(cq, ck, cv, c_cache, (), g_q=2, g_k=1, g_v=1, dv=HEAD_DIM, mode="plain", **lat))
        lat_d = dict(lat, tk_rows=KV_LAT, k_off=0)
        od = (_attention(dq, kd, vd, None, (), k_off=DEC_BATCH * KV_LAT, g_q=4, g_k=4, g_v=4, dv=D_V, mode="plain",
                         **ctx),
              _attention(dq, kd, vd, None, (), g_q=2, g_k=2, g_v=2, dv=D_V, mode="plain", **lat_d))

        x1, h2 = _merge_stage(x, h, (oa, ob, oc, od), mod3, wg, w_branch[l].astype(BF16), w_out[l].astype(BF16),
                              norm2_g[l][None, :])

        thr, p1, q2, p2 = _peer_query(
            h2, peer_wq[l].T.astype(BF16),
            peer_subkeys[l].reshape(2 * PEER_HEADS, PEER_KEYS, PEER_HALF).astype(BF16))
        x = _peer_dense(h2, x1, mod3, u_all, vt_all, thr, p1, q2, p2, layer=l, split=(l == DEPTH - 1))

    y_prompt = x[0].reshape(BATCH, SEQ, D_MODEL)
    y_sample = x[1].reshape(DEC_BATCH, DEC_SEQ, D_MODEL)
    return (y_prompt, y_sample) + tuple(jnp.stack(s_, axis=1) for s_ in states)
```

```python
import functools
import math

import jax
import jax.numpy as jnp
from jax import lax
from jax.experimental import pallas as pl
from jax.experimental.pallas import tpu as pltpu

F32 = jnp.float32
BF16 = jnp.bfloat16

D_MODEL = 1024
BATCH = 16
SEQ = 256
DEPTH = 2
DEC_BATCH = 2
DEC_SEQ = 4096
PAST_LEN = 512
GRID_W = 64
Q_BLOCK = 128
HEAD_DIM = 64
ROPE_BASE = 10000.0
EPS = 1e-6
NEG_INF = -1e30
A_HEADS = 4
A_QK = 32
A_V = 64
B_HEADS = 4
B_KV = 2
WINDOW = 128
C_HEADS = 4
C_KV = 2
D_HEADS = 4
D_Q_LORA = 256
D_KV_LORA = 128
D_NOPE = 64
D_ROPE = 32
D_V = 64
D_QK = D_NOPE + D_ROPE
N_BRANCH = 4
BRANCH_W = 256
PEER_HEADS = 8
PEER_KEYS = 128
PEER_EXPERTS = PEER_KEYS * PEER_KEYS
PEER_QDIM = 256
PEER_HALF = PEER_QDIM // 2
PEER_TOPK = 16

N_CTX = BATCH * SEQ
N_LAT = DEC_BATCH * DEC_SEQ
N_TOK = N_CTX + N_LAT
KV_LAT = PAST_LEN + DEC_SEQ
N_MODROWS = 8
SMALL_COLS = 2208
SMALL_PAD = 2304

VMEM_LIMIT = 56 * 1024 * 1024
EXHAUSTED = -3.0e38
LOG2E = math.log2(math.e)


def _params(sem, flags=None):
    return pltpu.CompilerParams(dimension_semantics=sem, vmem_limit_bytes=VMEM_LIMIT, flags=flags)


def _nt_dot(a, b):
    return lax.dot_general(a, b, (((1,), (1,)), ((), ())), preferred_element_type=F32)


def _dot(a, b):
    return jnp.dot(a, b, preferred_element_type=F32)


def _mod_kernel(c_ref, w_ref, b_ref, o_ref):
    c = c_ref[...]
    s = c * jax.nn.sigmoid(c)
    o_ref[...] = jnp.dot(s, w_ref[...], preferred_element_type=F32, precision=lax.Precision.HIGHEST) + b_ref[...]


def _modulation(cvec, ada_w, ada_b):
    depth = ada_w.shape[0]
    ncol = ada_w.shape[2] // D_MODEL
    return pl.pallas_call(
        _mod_kernel,
        grid=(depth, ncol),
        in_specs=[
            pl.BlockSpec((N_MODROWS, D_MODEL), lambda l, j: (0, 0)),
            pl.BlockSpec((None, D_MODEL, D_MODEL), lambda l, j: (l, 0, j)),
            pl.BlockSpec((None, 1, D_MODEL), lambda l, j: (l, 0, j)),
        ],
        out_specs=pl.BlockSpec((None, N_MODROWS, D_MODEL), lambda l, j: (l, 0, j)),
        out_shape=jax.ShapeDtypeStruct((depth, N_MODROWS, ada_w.shape[2]), F32),
        compiler_params=_params(("arbitrary", "arbitrary")),
        name="modulation",
    )(cvec, ada_w, ada_b.reshape(depth, 1, -1))


def _group_rms(x, ones_bd, gain, d):
    sq = x * x
    hi = sq.astype(BF16)
    lo = (sq - hi.astype(F32)).astype(BF16)
    ss = _dot(hi, ones_bd) + _dot(lo, ones_bd)
    return x * lax.rsqrt(ss * (1.0 / d) + EPS) * gain


def _row_rms(x, gain):
    ms = jnp.mean(x * x, axis=-1, keepdims=True)
    return x * lax.rsqrt(ms + EPS) * gain


def _rope(x, tab_ref, q4):
    w = x.shape[-1]
    return x * tab_ref[0] + pltpu.roll(x, w - q4, 1) * tab_ref[1] + pltpu.roll(x, q4, 1) * tab_ref[2]


GAIN_ROWS = 16
GAIN_W = D_HEADS * D_QK


def _in_kernel(x_ref, mod_ref, n1g_ref, ws_ref, gains_ref, e32_ref, e64_ref, e96_ref, wdq_ref,
               ta_ref, tb_ref, td_ref,
               h_ref, aq_ref, ak_ref, av_ref, bq_ref, bk_ref, bv_ref, cq_ref, ck_ref, cv_ref, dq_ref,
               akt_ref, avt_ref, bkt_ref, bvt_ref, ckt_ref, cvt_ref, dckv_ref, dkr_ref, *, tm, n_ctx):
    is_lat = pl.program_id(0) * tm >= n_ctx
    x = x_ref[...]
    h = _row_rms(x, n1g_ref[...]) * (1.0 + mod_ref[1:2, :]) + mod_ref[0:1, :]
    hb = h.astype(BF16)
    h_ref[...] = hb
    p = _dot(hb, ws_ref[...])

    def gain(r, w):
        return gains_ref[r:r + 1, 0:w]

    def heads_out(ref, val, nh, d):
        for i in range(nh):
            ref[i] = val[:, i * d:(i + 1) * d].astype(BF16)

    def maybe_rope(val, tab_ref, q4):
        return jnp.where(is_lat, _rope(val, tab_ref, q4), val)

    aq = maybe_rope(_group_rms(p[:, 0:256], e32_ref[...], gain(0, 256), A_QK), ta_ref, A_QK // 4)
    ak = maybe_rope(_group_rms(p[:, 256:512], e32_ref[...], gain(1, 256), A_QK), ta_ref, A_QK // 4)
    av = p[:, 512:768]
    heads_out(aq_ref, aq * (A_QK ** -0.5 * LOG2E), 2 * A_HEADS, A_QK)
    heads_out(ak_ref, ak, 2 * A_HEADS, A_QK)
    heads_out(av_ref, av, A_HEADS, A_V)
    akt_ref[...] = ak
    avt_ref[...] = av
    bq = maybe_rope(_group_rms(p[:, 768:1024], e64_ref[...], gain(2, 256), HEAD_DIM), tb_ref, HEAD_DIM // 4)
    bk = _group_rms(p[:, 1024:1152], e64_ref[0:128, 0:128], gain(3, 128), HEAD_DIM)
    bk = jnp.where(is_lat, bk * tb_ref[0, :, 0:128] + pltpu.roll(bk, 128 - 16, 1) * tb_ref[1, :, 0:128]
                   + pltpu.roll(bk, 16, 1) * tb_ref[2, :, 0:128], bk)
    bv = p[:, 1152:1280]
    heads_out(bq_ref, bq * (HEAD_DIM ** -0.5 * LOG2E), B_HEADS, HEAD_DIM)
    heads_out(bk_ref, bk, B_KV, HEAD_DIM)
    heads_out(bv_ref, bv, B_KV, HEAD_DIM)
    bkt_ref[...] = bk
    bvt_ref[...] = bv
    cq = maybe_rope(_group_rms(p[:, 1280:1536], e64_ref[...], gain(4, 256), HEAD_DIM), tb_ref, HEAD_DIM // 4)
    ck = _group_rms(p[:, 1536:1664], e64_ref[0:128, 0:128], gain(5, 128), HEAD_DIM)
    ck = jnp.where(is_lat, ck * tb_ref[0, :, 0:128] + pltpu.roll(ck, 128 - 16, 1) * tb_ref[1, :, 0:128]
                   + pltpu.roll(ck, 16, 1) * tb_ref[2, :, 0:128], ck)
    cv = p[:, 1664:1792]
    heads_out(cq_ref, cq * (HEAD_DIM ** -0.5 * LOG2E), C_HEADS, HEAD_DIM)
    heads_out(ck_ref, ck, C_KV, HEAD_DIM)
    heads_out(cv_ref, cv, C_KV, HEAD_DIM)
    ckt_ref[...] = ck
    cvt_ref[...] = cv
    dqa = _row_rms(p[:, 1792:2048], gain(6, 256))
    dq = _dot(dqa.astype(BF16), wdq_ref[...])
    dq = maybe_rope(_group_rms(dq, e96_ref[...], gain(8, GAIN_W), D_QK), td_ref, D_ROPE // 4)
    heads_out(dq_ref, dq * (D_QK ** -0.5 * LOG2E), D_HEADS, D_QK)
    dckv_ref[...] = _row_rms(p[:, 2048:2176], gain(7, 128))
    dkr_ref[...] = p[:, 2176:2208]


def _input_stage(x, mod3, n1g, ws, gains, e32, e64, e96, wdq, tab_a, tab_b, tab_d, *, tm=256):
    n = x.shape[0]
    nt = n // tm
    lat0 = N_CTX // tm
    pos_blocks = DEC_SEQ // tm

    def tok(i):
        return (i, 0)

    def pos(i):
        return (0, jnp.maximum(i - lat0, 0) % pos_blocks, 0)

    def const2(i):
        return (0, 0)

    def hm(i):
        return (0, i, 0)

    def hspec(nh, d):
        return pl.BlockSpec((nh, tm, d), hm)

    def hshape(nh, d):
        return jax.ShapeDtypeStruct((nh, n, d), BF16)

    def tspec(w):
        return pl.BlockSpec((tm, w), tok)

    def tshape(w):
        return jax.ShapeDtypeStruct((n, w), F32)

    in_specs = [
        pl.BlockSpec((tm, D_MODEL), tok),
        pl.BlockSpec((None, 6, D_MODEL), lambda i: ((i * tm) // DEC_SEQ, 0, 0)),
        pl.BlockSpec((1, D_MODEL), const2),
        pl.BlockSpec((D_MODEL, SMALL_PAD), const2),
        pl.BlockSpec((GAIN_ROWS, GAIN_W), const2),
        pl.BlockSpec((256, 256), const2),
        pl.BlockSpec((256, 256), const2),
        pl.BlockSpec((GAIN_W, GAIN_W), const2),
        pl.BlockSpec((D_Q_LORA, GAIN_W), const2),
        pl.BlockSpec((3, tm, 256), pos),
        pl.BlockSpec((3, tm, 256), pos),
        pl.BlockSpec((3, tm, GAIN_W), pos),
    ]
    out_specs = [
        pl.BlockSpec((tm, D_MODEL), tok),
        hspec(8, A_QK), hspec(8, A_QK), hspec(4, A_V),
        hspec(4, HEAD_DIM), hspec(2, HEAD_DIM), hspec(2, HEAD_DIM),
        hspec(4, HEAD_DIM), hspec(2, HEAD_DIM), hspec(2, HEAD_DIM),
        hspec(4, D_QK),
        tspec(256), tspec(256), tspec(128), tspec(128), tspec(128), tspec(128), tspec(128), tspec(D_ROPE),
    ]
    out_shape = [
        jax.ShapeDtypeStruct((n, D_MODEL), BF16),
        hshape(8, A_QK), hshape(8, A_QK), hshape(4, A_V),
        hshape(4, HEAD_DIM), hshape(2, HEAD_DIM), hshape(2, HEAD_DIM),
        hshape(4, HEAD_DIM), hshape(2, HEAD_DIM), hshape(2, HEAD_DIM),
        hshape(4, D_QK),
        tshape(256), tshape(256), tshape(128), tshape(128), tshape(128), tshape(128), tshape(128), tshape(D_ROPE),
    ]
    return pl.pallas_call(
        functools.partial(_in_kernel, tm=tm, n_ctx=N_CTX),
        grid=(nt,),
        in_specs=in_specs,
        out_specs=out_specs,
        out_shape=out_shape,
        compiler_params=_params(("arbitrary",)),
        name="input_stage",
    )(x, mod3, n1g, ws, gains, e32, e64, e96, wdq, tab_a, tab_b, tab_d)


def _mla_kernel(ckv_ref, kr_ref, wkvb_ref, g_ref, rot_ref, tab_ref, kd_ref, vd_ref, *, tm, lat_tiles, per_batch_tiles):
    i = pl.program_id(0)
    rotate = jnp.logical_and(i < lat_tiles, i % per_batch_tiles >= PAST_LEN // tm)
    kv = _dot(ckv_ref[...].astype(BF16), wkvb_ref[...])
    kr = kr_ref[...]
    ssr = jnp.sum(kr * kr, axis=-1, keepdims=True)
    z = kr * g_ref[:, D_NOPE:D_QK]
    zh = z.astype(BF16)
    zl = (z - zh.astype(F32)).astype(BF16)
    rot = _dot(zh, rot_ref[...]) + _dot(zl, rot_ref[...])
    rz = jnp.where(rotate, z * tab_ref[0] + rot * tab_ref[1], z)
    g_nope = g_ref[:, 0:D_NOPE]
    for hd in range(D_HEADS):
        kn = kv[:, hd * 128:hd * 128 + D_NOPE]
        r = lax.rsqrt((jnp.sum(kn * kn, axis=-1, keepdims=True) + ssr) * (1.0 / D_QK) + EPS)
        kd_ref[hd] = jnp.concatenate([kn * r * g_nope, rz * r], axis=-1).astype(BF16)
        vd_ref[hd] = kv[:, hd * 128 + D_NOPE:(hd + 1) * 128].astype(BF16)


def _mla_expand(ckv, kr, wkvb, g, rot, tab, *, tm=512):
    rows = ckv.shape[0]
    per_batch_tiles = KV_LAT // tm
    lat_tiles = DEC_BATCH * per_batch_tiles
    past_tiles = PAST_LEN // tm
    pos_blocks = DEC_SEQ // tm

    def pos(i):
        return (0, jnp.maximum(i % per_batch_tiles - past_tiles, 0) % pos_blocks, 0)

    return pl.pallas_call(
        functools.partial(_mla_kernel, tm=tm, lat_tiles=lat_tiles, per_batch_tiles=per_batch_tiles),
        grid=(rows // tm,),
        in_specs=[
            pl.BlockSpec((tm, D_KV_LORA), lambda i: (i, 0)),
            pl.BlockSpec((tm, D_ROPE), lambda i: (i, 0)),
            pl.BlockSpec((D_KV_LORA, D_HEADS * (D_NOPE + D_V)), lambda i: (0, 0)),
            pl.BlockSpec((1, D_QK), lambda i: (0, 0)),
            pl.BlockSpec((D_ROPE, D_ROPE), lambda i: (0, 0)),
            pl.BlockSpec((2, tm, D_ROPE), pos),
        ],
        out_specs=[
            pl.BlockSpec((D_HEADS, tm, D_QK), lambda i: (0, i, 0)),
            pl.BlockSpec((D_HEADS, tm, D_V), lambda i: (0, i, 0)),
        ],
        out_shape=[
            jax.ShapeDtypeStruct((D_HEADS, rows, D_QK), BF16),
            jax.ShapeDtypeStruct((D_HEADS, rows, D_V), BF16),
        ],
        compiler_params=_params(("arbitrary",)),
        name="mla_expand",
    )(ckv, kr, wkvb, g, rot, tab)


def _attn_kernel(*refs, g_q, g_k, g_v, dv, mode, lam_init, has_cache):
    refs = list(refs)
    q_ref, k_ref, v_ref = refs[:3]
    pos = 3
    sink_ref = lam_ref = sub_ref = None
    if has_cache:
        kc_ref, vc_ref = refs[pos], refs[pos + 1]
        pos += 2
    if mode == "sink":
        sink_ref = refs[pos]
        pos += 1
    elif mode == "diff":
        lam_ref, sub_ref = refs[pos], refs[pos + 1]
        pos += 2
    o_ref = refs[pos]
    grp = pl.program_id(1)

    if has_cache:
        kall_ref, vall_ref = refs[pos + 1], refs[pos + 2]
        past = kc_ref.shape[1]

        @pl.when(pl.program_id(2) == 0)
        def _join():
            kall_ref[:, 0:past, :] = kc_ref[...]
            kall_ref[:, past:, :] = k_ref[...]
            vall_ref[:, 0:past, :] = vc_ref[...]
            vall_ref[:, past:, :] = v_ref[...]

        k_ref, v_ref = kall_ref, vall_ref

    def probs(g):
        s = _nt_dot(q_ref[g], k_ref[g * g_k // g_q])
        m = jnp.max(s, axis=-1, keepdims=True)
        if mode == "sink":
            sk = sink_ref[grp * g_q + g] * LOG2E
            m = jnp.maximum(m, sk)
        p = jnp.exp2(s - m)
        den = jnp.sum(p, axis=-1, keepdims=True)
        if mode == "sink":
            den = den + jnp.exp2(sk - m)
        return p, den

    if mode == "diff":
        lp = lam_ref[...]
        lam = (jnp.exp(jnp.sum(lp[0:1] * lp[1:2], axis=-1, keepdims=True))
               - jnp.exp(jnp.sum(lp[2:3] * lp[3:4], axis=-1, keepdims=True)) + lam_init)
        for i in range(g_q // 2):
            p1, den1 = probs(2 * i)
            p2, den2 = probs(2 * i + 1)
            w1 = 1.0 / den1
            comb = p1 - p2 * (lam * den1 / den2)
            d = _dot(comb.astype(BF16), v_ref[i * g_v // (g_q // 2)]) * w1
            o_ref[:, i * dv:(i + 1) * dv] = _row_rms(d, sub_ref[...]) * (1.0 - lam_init)
    else:
        for g in range(g_q):
            p, den = probs(g)
            o_ref[:, g * dv:(g + 1) * dv] = _dot(p.astype(BF16), v_ref[g * g_v // g_q]) * (1.0 / den)


def _attention(q, k, v, cache, extras, *, batch, tq_rows, tk_rows, q_off, k_off,
               g_q, g_k, g_v, dv, mode, lam_init, tq):
    hq, _, dq = q.shape
    n_groups = hq // g_q
    g_out = g_q // 2 if mode == "diff" else g_q
    nq = tq_rows // tq
    tk = tk_rows
    in_specs = [
        pl.BlockSpec((g_q, tq, dq), lambda b, g, qi: (g, (q_off + b * tq_rows) // tq + qi, 0)),
        pl.BlockSpec((g_k, tk, dq), lambda b, g, qi: (g, (k_off + b * tk_rows) // tk, 0)),
        pl.BlockSpec((g_v, tk, dv), lambda b, g, qi: (g, (k_off + b * tk_rows) // tk, 0)),
    ]
    args = [q, k, v]
    if cache is not None:
        in_specs.append(pl.BlockSpec((g_k, PAST_LEN, dq), lambda b, g, qi: (g, b, 0)))
        in_specs.append(pl.BlockSpec((g_v, PAST_LEN, dv), lambda b, g, qi: (g, b, 0)))
        args += list(cache)
    if mode == "sink":
        in_specs.append(pl.BlockSpec(memory_space=pltpu.SMEM))
        args.append(extras[0])
    elif mode == "diff":
        in_specs.append(pl.BlockSpec((4, A_QK), lambda b, g, qi: (0, 0)))
        in_specs.append(pl.BlockSpec((1, dv), lambda b, g, qi: (0, 0)))
        args += list(extras)
    return pl.pallas_call(
        functools.partial(_attn_kernel, g_q=g_q, g_k=g_k, g_v=g_v, dv=dv, mode=mode, lam_init=lam_init,
                          has_cache=cache is not None),
        grid=(batch, n_groups, nq),
        in_specs=in_specs,
        out_specs=pl.BlockSpec((tq, g_out * dv), lambda b, g, qi: (b * nq + qi, g)),
        out_shape=jax.ShapeDtypeStruct((batch * tq_rows, n_groups * g_out * dv), F32),
        scratch_shapes=([pltpu.VMEM((g_k, PAST_LEN + tk, dq), BF16), pltpu.VMEM((g_v, PAST_LEN + tk, dv), BF16)]
                        if cache is not None else []),
        compiler_params=_params(("arbitrary", "arbitrary", "arbitrary")),
        name="attention_" + mode,
    )(*args)


def _win_kernel(q_ref, k_ref, v_ref, kc_ref, vc_ref, sink_ref, o_ref, *, g_q, dv, tq, seq, band):
    grp = pl.program_id(1)
    qi = pl.program_id(2)
    start = pl.multiple_of(jnp.clip(qi * tq - WINDOW, 0, seq - band), WINDOW)
    k_ctx = kc_ref[...]
    v_ctx = vc_ref[...]
    k_band = k_ref[pl.ds(start, band), :]
    v_band = v_ref[pl.ds(start, band), :]
    qpos = qi * tq + lax.broadcasted_iota(jnp.int32, (tq, band), 0)
    kpos = start + lax.broadcasted_iota(jnp.int32, (tq, band), 1)
    valid = jnp.abs(kpos - qpos) <= WINDOW
    for g in range(g_q):
        q = q_ref[g]
        s_ctx = _nt_dot(q, k_ctx)
        s_band = jnp.where(valid, _nt_dot(q, k_band), NEG_INF)
        sk = sink_ref[grp * g_q + g] * LOG2E
        m = jnp.maximum(jnp.maximum(jnp.max(s_ctx, axis=-1, keepdims=True),
                                    jnp.max(s_band, axis=-1, keepdims=True)), sk)
        p_ctx = jnp.exp2(s_ctx - m)
        p_band = jnp.exp2(s_band - m)
        den = (jnp.sum(p_ctx, axis=-1, keepdims=True) + jnp.sum(p_band, axis=-1, keepdims=True)
               + jnp.exp2(sk - m))
        num = _dot(p_ctx.astype(BF16), v_ctx) + _dot(p_band.astype(BF16), v_band)
        o_ref[:, g * dv:(g + 1) * dv] = num * (1.0 / den)


def _window_attention(q, k, v, kc, vc, sink, *, q_off, k_off, tq=256):
    hq, _, d = q.shape
    hk = k.shape[0]
    g_q = hq // hk
    band = tq + 2 * WINDOW
    nq = DEC_SEQ // tq
    return pl.pallas_call(
        functools.partial(_win_kernel, g_q=g_q, dv=d, tq=tq, seq=DEC_SEQ, band=band),
        grid=(DEC_BATCH, hk, nq),
        in_specs=[
            pl.BlockSpec((g_q, tq, d), lambda b, g, qi: (g, (q_off + b * DEC_SEQ) // tq + qi, 0)),
            pl.BlockSpec((None, DEC_SEQ, d), lambda b, g, qi: (g, k_off // DEC_SEQ + b, 0)),
            pl.BlockSpec((None, DEC_SEQ, d), lambda b, g, qi: (g, k_off // DEC_SEQ + b, 0)),
            pl.BlockSpec((None, PAST_LEN, d), lambda b, g, qi: (g, b, 0)),
            pl.BlockSpec((None, PAST_LEN, d), lambda b, g, qi: (g, b, 0)),
            pl.BlockSpec(memory_space=pltpu.SMEM),
        ],
        out_specs=pl.BlockSpec((tq, g_q * d), lambda b, g, qi: (b * nq + qi, g)),
        out_shape=jax.ShapeDtypeStruct((DEC_BATCH * DEC_SEQ, hq * d), F32),
        compiler_params=_params(("arbitrary", "arbitrary", "arbitrary")),
        name="attention_window",
    )(q, k, v, kc, vc, sink)


def _merge_kernel(x_ref, h_ref, oac_ref, oal_ref, obc_ref, obl_ref, occ_ref, ocl_ref, odc_ref, odl_ref,
                  mod_ref, wg_ref, wb_ref, wo_ref, n2g_ref, x1_ref, h2_ref, *, ctx_tiles):
    is_ctx = pl.program_id(0) < ctx_tiles
    hb = h_ref[...]
    outs = ((oac_ref, oal_ref), (obc_ref, obl_ref), (occ_ref, ocl_ref), (odc_ref, odl_ref))
    acc = None
    for i in range(N_BRANCH):
        gt = jax.nn.sigmoid(_dot(hb, wg_ref[:, i * D_MODEL:(i + 1) * D_MODEL]))
        o = jnp.where(is_ctx, outs[i][0][...], outs[i][1][...])
        br = _dot(o.astype(BF16), wb_ref[i])
        acc = gt * br if acc is None else acc + gt * br
    y = _dot(acc.astype(BF16), wo_ref[...])
    x1 = x_ref[...] + mod_ref[2:3, :] * y
    x1_ref[...] = x1
    h2 = _row_rms(x1, n2g_ref[...]) * (1.0 + mod_ref[4:5, :]) + mod_ref[3:4, :]
    h2_ref[...] = h2.astype(BF16)


def _merge_stage(x, h, branches, mod3, wg, wb, wo, n2g, *, tm=256):
    n = x.shape[0]
    ctx_tiles = N_CTX // tm

    def tok(i):
        return (i, 0)

    ctx_spec = pl.BlockSpec((tm, BRANCH_W), lambda i: (jnp.minimum(i, ctx_tiles - 1), 0))
    lat_spec = pl.BlockSpec((tm, BRANCH_W), lambda i: (jnp.maximum(i - ctx_tiles, 0), 0))
    return pl.pallas_call(
        functools.partial(_merge_kernel, ctx_tiles=ctx_tiles),
        grid=(n // tm,),
        in_specs=[
            pl.BlockSpec((tm, D_MODEL), tok),
            pl.BlockSpec((tm, D_MODEL), tok),
            ctx_spec, lat_spec, ctx_spec, lat_spec, ctx_spec, lat_spec, ctx_spec, lat_spec,
            pl.BlockSpec((None, 6, D_MODEL), lambda i: ((i * tm) // DEC_SEQ, 0, 0)),
            pl.BlockSpec((D_MODEL, N_BRANCH * D_MODEL), lambda i: (0, 0)),
            pl.BlockSpec((N_BRANCH, BRANCH_W, D_MODEL), lambda i: (0, 0, 0)),
            pl.BlockSpec((D_MODEL, D_MODEL), lambda i: (0, 0)),
            pl.BlockSpec((1, D_MODEL), lambda i: (0, 0)),
        ],
        out_specs=[pl.BlockSpec((tm, D_MODEL), tok), pl.BlockSpec((tm, D_MODEL), tok)],
        out_shape=[jax.ShapeDtypeStruct((n, D_MODEL), F32), jax.ShapeDtypeStruct((n, D_MODEL), BF16)],
        compiler_params=_params(("arbitrary",)),
        name="merge_stage",
    )(x, h, *[part for pair in branches for part in pair], mod3, wg, wb, wo, n2g)


def _oddeven_mergesort_pairs(n):
    pairs = []

    def merge(lo, m, r):
        step = 2 * r
        if step < m:
            merge(lo, m, step)
            merge(lo + r, m, step)
            for i in range(lo + r, lo + m - r, step):
                pairs.append((i, i + r))
        else:
            pairs.append((lo, lo + r))

    def sort(lo, m):
        if m > 1:
            sort(lo, m // 2)
            sort(lo + m // 2, m // 2)
            merge(lo, m, 1)

    sort(0, n)
    return pairs


_SORT_PAIRS = _oddeven_mergesort_pairs(PEER_TOPK)


def _exchange(xs, i, j):
    xs[i], xs[j] = jnp.maximum(xs[i], xs[j]), jnp.minimum(xs[i], xs[j])


def _top16_of_groups(groups):
    n = len(groups)
    xs = list(groups)
    for i, j in _SORT_PAIRS:
        _exchange(xs, i, j)
    for shift in (4, 2, 1):
        xs = [jnp.maximum(xs[k], pltpu.roll(xs[n - 1 - k], shift, 0)) for k in range(n)]
        s = n // 2
        while s >= 1:
            for i in range(n):
                if not i & s:
                    _exchange(xs, i, i + s)
            s //= 2
    return xs


def _sublane_sum(x):
    for shift in (4, 2, 1):
        x = x + pltpu.roll(x, shift, 0)
    return x


def _sublane_spread(vals):
    row = lax.broadcasted_iota(jnp.int32, vals[0].shape, 0)
    out = vals[-1]
    for j in range(len(vals) - 2, -1, -1):
        out = jnp.where(row == j, vals[j], out)
    return out


def _peer_select(g1, g2):
    k = PEER_TOPK
    half = k // 2
    v1 = _top16_of_groups(g1)
    v2 = _top16_of_groups(g2)
    v2lo = _sublane_spread(v2[:half])
    v2hi = _sublane_spread(v2[half:])
    v1hi = _sublane_spread(v1[half:])
    cands = [v1[0] + v2lo, v1[0] + v2hi] + [v1[i] + v2lo for i in range(1, half)] + [v1hi + v2[0]]
    pad = jnp.full(cands[0].shape, EXHAUSTED, F32)
    top = _top16_of_groups(cands + [pad] * (k - len(cands)))
    tau = top[k - 1]
    z = None
    for c in cands:
        e = jnp.where(c >= tau, jnp.exp(c - top[0]), 0.0)
        z = e if z is None else z + e
    z = _sublane_sum(z)
    thr = [jnp.full(g1[0].shape, -1.0, F32) for _ in g1]
    q2 = [jnp.full(g2[0].shape, float(k), F32) for _ in g2]
    for a in range(k - 1, -1, -1):
        cnt = _sublane_sum(jnp.where(v1[a] + v2lo >= tau, 1.0, 0.0) + jnp.where(v1[a] + v2hi >= tau, 1.0, 0.0))
        thr = [jnp.where(x == v1[a], cnt - 1.0, t) for x, t in zip(g1, thr)]
        q2 = [jnp.where(x == v2[a], float(a), q) for x, q in zip(g2, q2)]
    p1 = [jnp.exp(x - v1[0]) for x in g1]
    scale = 0.5 / z
    p2 = [jnp.exp(x - v2[0]) * scale for x in g2]
    return thr, p1, q2, p2


def _peer_query_kernel(h2_ref, wqt_ref, sk_ref, thr_ref, p1_ref, q2_ref, p2_ref):
    qt = _nt_dot(wqt_ref[...], h2_ref[...]).astype(BF16)
    n_lt = qt.shape[1] // 128
    n_grp = PEER_KEYS // 8
    for hd in range(PEER_HEADS):
        s1 = _dot(sk_ref[2 * hd], qt[(2 * hd) * PEER_HALF:(2 * hd + 1) * PEER_HALF, :])
        s2 = _dot(sk_ref[2 * hd + 1], qt[(2 * hd + 1) * PEER_HALF:(2 * hd + 2) * PEER_HALF, :])
        for lt in range(n_lt):
            lanes = slice(lt * 128, (lt + 1) * 128)
            dst = (lt + 1) % n_lt
            rot = slice(dst * 128, (dst + 1) * 128)
            g1 = [s1[8 * i:8 * i + 8, lanes] for i in range(n_grp)]
            g2 = [s2[8 * i:8 * i + 8, lanes] for i in range(n_grp)]
            thr, p1, q2, p2 = _peer_select(g1, g2)
            thr_ref[hd, :, lanes] = jnp.concatenate(thr, axis=0)
            p1_ref[hd, :, lanes] = jnp.concatenate(p1, axis=0)
            q2_ref[hd, :, lanes] = jnp.concatenate(q2, axis=0).astype(BF16)
            p2_ref[hd, :, rot] = jnp.concatenate(p2, axis=0).astype(BF16)


def _peer_query(h2, wqt, sk, *, tm=512):
    n = h2.shape[0]
    big = jax.ShapeDtypeStruct((PEER_HEADS, PEER_KEYS, n), F32)
    half = jax.ShapeDtypeStruct((PEER_HEADS, PEER_KEYS, n), BF16)
    bspec = pl.BlockSpec((PEER_HEADS, PEER_KEYS, tm), lambda i: (0, 0, i))
    return pl.pallas_call(
        _peer_query_kernel,
        grid=(n // tm,),
        in_specs=[
            pl.BlockSpec((tm, D_MODEL), lambda i: (i, 0)),
            pl.BlockSpec((PEER_HEADS * PEER_QDIM, D_MODEL), lambda i: (0, 0)),
            pl.BlockSpec((2 * PEER_HEADS, PEER_KEYS, PEER_HALF), lambda i: (0, 0, 0)),
        ],
        out_specs=[bspec, bspec, bspec, bspec],
        out_shape=[big, big, half, half],
        compiler_params=_params(("arbitrary",)),
        name="peer_query",
    )(h2, wqt, sk)


def _peer_dense_kernel(h2_ref, x1_ref, mod_ref, u_ref, vt_ref, thr_ref, p1_ref, q2_ref, p2_ref,
                       *rest, e1_per_step, split_tiles):
    out_refs = rest[:-5]
    acc_ref, g_ref, p_ref, q2s_ref, p2s_ref = rest[-5:]
    step = pl.program_id(1)
    tm = g_ref.shape[1]
    n_lt = tm // 128
    rows = 16
    n_r = PEER_KEYS // rows

    def gate_pass():
        for c in range(e1_per_step):
            for lt in range(n_lt):
                lanes = slice(lt * 128, (lt + 1) * 128)
                rot = slice(((lt + 1) % n_lt) * 128, ((lt + 1) % n_lt + 1) * 128)
                g = [None] * n_r
                for hd in range(PEER_HEADS):
                    thr_b = jnp.broadcast_to(thr_ref[hd, c:c + 1, lanes], (rows, 128)).astype(BF16)
                    p1_b = jnp.broadcast_to(p1_ref[hd, c:c + 1, lanes], (rows, 128)).astype(BF16)
                    for r in range(n_r):
                        rs = slice(r * rows, (r + 1) * rows)
                        w = jnp.where(q2s_ref[hd, rs, lanes] <= thr_b, p2s_ref[hd, rs, rot] * p1_b,
                                      jnp.zeros((), BF16))
                        g[r] = w if g[r] is None else g[r] + w
                for r in range(n_r):
                    g_ref[c * PEER_KEYS + r * rows:c * PEER_KEYS + (r + 1) * rows, lanes] = g[r]

    @pl.when(step == 0)
    def _init():
        acc_ref[...] = jnp.zeros_like(acc_ref)
        q2s_ref[...] = q2_ref[...]
        p2s_ref[...] = p2_ref[...]

    pl.when(step >= 0)(gate_pass)

    a = _nt_dot(u_ref[...], h2_ref[...])
    act = (a + a * lax.erf(a * (2.0 ** -0.5))).astype(BF16)
    p_ref[...] = g_ref[...] * act
    acc_ref[...] += _dot(vt_ref[...], p_ref[...])

    @pl.when(step == pl.num_programs(1) - 1)
    def _fin():
        val = x1_ref[...] + mod_ref[5:6, :] * acc_ref[...].T
        if split_tiles is None:
            out_refs[0][...] = val
        else:
            is_ctx = pl.program_id(0) < split_tiles

            @pl.when(is_ctx)
            def _ctx_rows():
                out_refs[0][...] = val

            @pl.when(jnp.logical_not(is_ctx))
            def _lat_rows():
                out_refs[1][...] = val


def _peer_dense(h2, x1, mod3, u, vt, thr, p1, q2, p2, *, layer, split=False, tm=512, e1_per_step=16):
    n = h2.shape[0]
    te = e1_per_step * PEER_KEYS
    tok = lambda i, j: (i, 0)
    big = pl.BlockSpec((PEER_HEADS, PEER_KEYS, tm), lambda i, j: (0, 0, i))
    rows = pl.BlockSpec((PEER_HEADS, e1_per_step, tm), lambda i, j: (0, j, i))
    ctx_tiles = N_CTX // tm
    if split:
        out_specs = [pl.BlockSpec((tm, D_MODEL), lambda i, j: (jnp.minimum(i, ctx_tiles - 1), 0)),
                     pl.BlockSpec((tm, D_MODEL), lambda i, j: (jnp.maximum(i - ctx_tiles, 0), 0))]
        out_shape = [jax.ShapeDtypeStruct((N_CTX, D_MODEL), F32), jax.ShapeDtypeStruct((n - N_CTX, D_MODEL), F32)]
    else:
        out_specs = pl.BlockSpec((tm, D_MODEL), tok)
        out_shape = jax.ShapeDtypeStruct((n, D_MODEL), F32)
    return pl.pallas_call(
        functools.partial(_peer_dense_kernel, e1_per_step=e1_per_step, split_tiles=ctx_tiles if split else None),
        grid=(n // tm, PEER_EXPERTS // te),
        in_specs=[
            pl.BlockSpec((tm, D_MODEL), tok),
            pl.BlockSpec((tm, D_MODEL), tok),
            pl.BlockSpec((None, 6, D_MODEL), lambda i, j: ((i * tm) // DEC_SEQ, 0, 0)),
            pl.BlockSpec((None, te, D_MODEL), lambda i, j: (layer, j, 0)),
            pl.BlockSpec((None, D_MODEL, te), lambda i, j: (layer, 0, j)),
            rows, rows, big, big,
        ],
        out_specs=out_specs,
        out_shape=out_shape,
        scratch_shapes=[pltpu.VMEM((D_MODEL, tm), F32), pltpu.VMEM((te, tm), BF16), pltpu.VMEM((te, tm), BF16),
                        pltpu.VMEM((PEER_HEADS, PEER_KEYS, tm), BF16), pltpu.VMEM((PEER_HEADS, PEER_KEYS, tm), BF16)],
        compiler_params=_params(("arbitrary", "arbitrary")),
        name="peer_dense",
    )(h2, x1, mod3, u, vt, thr, p1, q2, p2)


def _block_ones(width, d):
    idx = jnp.arange(width) // d
    return (idx[:, None] == idx[None, :]).astype(BF16)


def _rope_angles(d):
    q4 = d // 4
    inv = ROPE_BASE ** (-jnp.arange(q4, dtype=F32) / q4)
    t = jnp.arange(DEC_SEQ)
    ar = (t // GRID_W).astype(F32)[:, None] * inv
    ac = (t % GRID_W).astype(F32)[:, None] * inv
    return jnp.concatenate([ar, ar, ac, ac], axis=-1)


def _rope_table(d, group, n_groups):
    ang = _rope_angles(d)
    q4 = d // 4
    first = (jnp.arange(d) % (2 * q4)) < q4
    cos = jnp.cos(ang)
    sin = jnp.sin(ang)
    sa = jnp.where(first[None, :], -sin, 0.0)
    sb = jnp.where(first[None, :], 0.0, sin)
    pad = group - d
    cos = jnp.pad(cos, ((0, 0), (pad, 0)), constant_values=1.0)
    sa = jnp.pad(sa, ((0, 0), (pad, 0)))
    sb = jnp.pad(sb, ((0, 0), (pad, 0)))
    return jnp.stack([jnp.tile(cos, (1, n_groups)), jnp.tile(sa, (1, n_groups)), jnp.tile(sb, (1, n_groups))])


def _rot_matrix(d):
    q4 = d // 4
    i = jnp.arange(d)
    first = (i % (2 * q4)) < q4
    src = jnp.where(first, i + q4, i - q4)
    sign = jnp.where(first, -1.0, 1.0)
    return (jnp.zeros((d, d), F32).at[src, i].set(sign)).astype(BF16)


def _pad_row(v, width):
    return jnp.pad(v, (0, width - v.shape[0]))


def kernel(x_prompt, x_sample, c, cache_a_k, cache_a_v, cache_b_k, cache_b_v, cache_c_k, cache_c_v, cache_d_ckv, cache_d_krope, c_ctx, ada_w, ada_b, norm1_g, norm2_g, w_in, a_lam, a_qk_g, a_subln_g, b_sink, b_qk_g, c_qk_g, d_qnorm_g, d_kvnorm_g, w_d_qb, w_d_kvb, d_qk_g, w_branch, w_out, peer_wq, peer_subkeys, peer_u, peer_v):
    x = jnp.concatenate([x_prompt.reshape(N_CTX, D_MODEL), x_sample.reshape(N_LAT, D_MODEL)], axis=0)
    cvec = jnp.concatenate([c_ctx[None, :], c, jnp.zeros((N_MODROWS - 1 - DEC_BATCH, D_MODEL), F32)], axis=0)
    mod_all = _modulation(cvec, ada_w, ada_b)

    e32 = _block_ones(256, A_QK)
    e64 = _block_ones(256, HEAD_DIM)
    e96 = _block_ones(GAIN_W, D_QK)
    tab_a = _rope_table(A_QK, A_QK, 2 * A_HEADS)
    tab_b = _rope_table(HEAD_DIM, HEAD_DIM, B_HEADS)
    tab_d = _rope_table(D_ROPE, D_QK, D_HEADS)
    ang_r = _rope_angles(D_ROPE)
    tab_r = jnp.stack([jnp.cos(ang_r), jnp.sin(ang_r)])
    rot_r = _rot_matrix(D_ROPE)

    u_all = peer_u.astype(BF16)
    vt_all = jnp.swapaxes(peer_v, 1, 2).astype(BF16)
    states = [[] for _ in range(8)]
    for l in range(DEPTH):
        lam_init = 0.8 - 0.6 * math.exp(-0.3 * l)
        mod3 = mod_all[l, :3].reshape(3, 6, D_MODEL)
        ws = jnp.pad(w_in[l, :, :SMALL_COLS], ((0, 0), (0, SMALL_PAD - SMALL_COLS))).astype(BF16)
        wg = w_in[l, :, SMALL_COLS:].astype(BF16)
        gains = jnp.stack([
            _pad_row(jnp.tile(a_qk_g[l, 0], 2 * A_HEADS), GAIN_W),
            _pad_row(jnp.tile(a_qk_g[l, 1], 2 * A_HEADS), GAIN_W),
            _pad_row(jnp.tile(b_qk_g[l, 0], B_HEADS), GAIN_W),
            _pad_row(jnp.tile(b_qk_g[l, 1], B_KV), GAIN_W),
            _pad_row(jnp.tile(c_qk_g[l, 0], C_HEADS), GAIN_W),
            _pad_row(jnp.tile(c_qk_g[l, 1], C_KV), GAIN_W),
            _pad_row(d_qnorm_g[l], GAIN_W),
            _pad_row(d_kvnorm_g[l], GAIN_W),
            jnp.tile(d_qk_g[l, 0], D_HEADS),
        ] + [jnp.zeros((GAIN_W,), F32)] * (GAIN_ROWS - 9))

        (h, aq, ak, av, bq, bk, bv, cq, ck, cv, dq,
         akt, avt, bkt, bvt, ckt, cvt, dckv, dkr) = _input_stage(
            x, mod3, norm1_g[l][None, :], ws, gains, e32, e64, e96, w_d_qb[l].astype(BF16), tab_a, tab_b, tab_d,
            tm=512)

        for lst, s_ in zip(states, (
                akt[:N_CTX].reshape(BATCH, SEQ, A_HEADS, 2, A_QK), avt[:N_CTX].reshape(BATCH, SEQ, A_HEADS, A_V),
                bkt[:N_CTX].reshape(BATCH, SEQ, B_KV, HEAD_DIM), bvt[:N_CTX].reshape(BATCH, SEQ, B_KV, HEAD_DIM),
                ckt[:N_CTX].reshape(BATCH, SEQ, C_KV, HEAD_DIM), cvt[:N_CTX].reshape(BATCH, SEQ, C_KV, HEAD_DIM),
                dckv[:N_CTX].reshape(BATCH, SEQ, D_KV_LORA), dkr[:N_CTX].reshape(BATCH, SEQ, D_ROPE))):
            lst.append(s_)

        ckv_rows = jnp.concatenate([t for b in range(DEC_BATCH) for t in (
            cache_d_ckv[b, l], dckv[N_CTX + b * DEC_SEQ:N_CTX + (b + 1) * DEC_SEQ])] + [dckv[:N_CTX]], axis=0)
        kr_rows = jnp.concatenate([t for b in range(DEC_BATCH) for t in (
            cache_d_krope[b, l], dkr[N_CTX + b * DEC_SEQ:N_CTX + (b + 1) * DEC_SEQ])] + [dkr[:N_CTX]], axis=0)
        kd, vd = _mla_expand(ckv_rows, kr_rows, w_d_kvb[l].astype(BF16), d_qk_g[l, 1][None, :], rot_r, tab_r)

        def cached(cache, nh, d):
            cache = jnp.moveaxis(cache.reshape(DEC_BATCH, PAST_LEN, nh, d), 2, 0)
            return cache.reshape(nh, DEC_BATCH * PAST_LEN, d).astype(BF16)

        a_cache = (cached(cache_a_k[:, l], 2 * A_HEADS, A_QK), cached(cache_a_v[:, l], A_HEADS, A_V))
        b_cache = (cached(cache_b_k[:, l], B_KV, HEAD_DIM), cached(cache_b_v[:, l], B_KV, HEAD_DIM))
        c_cache = (cached(cache_c_k[:, l], C_KV, HEAD_DIM), cached(cache_c_v[:, l], C_KV, HEAD_DIM))

        ctx = dict(batch=BATCH, tq_rows=SEQ, tk_rows=SEQ, q_off=0, lam_init=lam_init, tq=SEQ)
        lat = dict(batch=DEC_BATCH, tq_rows=DEC_SEQ, tk_rows=DEC_SEQ, q_off=N_CTX, k_off=N_CTX,
                   lam_init=lam_init, tq=256)
        a_extra = (a_lam[l], a_subln_g[l][None, :])
        oa = (_attention(aq, ak, av, None, a_extra, k_off=0, g_q=8, g_k=8, g_v=4, dv=A_V, mode="diff", **ctx),
              _attention(aq, ak, av, a_cache, a_extra, g_q=4, g_k=4, g_v=2, dv=A_V, mode="diff", **lat))
        ob = (_attention(bq, bk, bv, None, (b_sink[l],), k_off=0, g_q=4, g_k=2, g_v=2, dv=HEAD_DIM, mode="sink",
                         **ctx),
              _window_attention(bq, bk, bv, *b_cache, b_sink[l], q_off=N_CTX, k_off=N_CTX))
        oc = (_attention(cq, ck, cv, None, (), k_off=0, g_q=4, g_k=2, g_v=2, dv=HEAD_DIM, mode="plain", **ctx),
              _attention(cq, ck, cv, c_cache, (), g_q=4, g_k=2, g_v=2, dv=HEAD_DIM, mode="plain", **lat))
        lat_d = dict(lat, tk_rows=KV_LAT, k_off=0)
        od = (_attention(dq, kd, vd, None, (), k_off=DEC_BATCH * KV_LAT, g_q=4, g_k=4, g_v=4, dv=D_V, mode="plain",
                         **ctx),
              _attention(dq, kd, vd, None, (), g_q=4, g_k=4, g_v=4, dv=D_V, mode="plain", **lat_d))

        x1, h2 = _merge_stage(x, h, (oa, ob, oc, od), mod3, wg, w_branch[l].astype(BF16), w_out[l].astype(BF16),
                              norm2_g[l][None, :])

        thr, p1, q2, p2 = _peer_query(
            h2, peer_wq[l].T.astype(BF16),
            peer_subkeys[l].reshape(2 * PEER_HEADS, PEER_KEYS, PEER_HALF).astype(BF16))
        x = _peer_dense(h2, x1, mod3, u_all, vt_all, thr, p1, q2, p2, layer=l, split=(l == DEPTH - 1))

    y_prompt = x[0].reshape(BATCH, SEQ, D_MODEL)
    y_sample = x[1].reshape(DEC_BATCH, DEC_SEQ, D_MODEL)
    return (y_prompt, y_sample) + tuple(jnp.stack(s_, axis=1) for s_ in states)
```

```python
import functools
import math

import jax
import jax.numpy as jnp
from jax import lax
from jax.experimental import pallas as pl
from jax.experimental.pallas import tpu as pltpu

F32 = jnp.float32
BF16 = jnp.bfloat16

D_MODEL = 1024
BATCH = 16
SEQ = 256
DEPTH = 2
DEC_BATCH = 2
DEC_SEQ = 4096
PAST_LEN = 512
GRID_W = 64
Q_BLOCK = 128
HEAD_DIM = 64
ROPE_BASE = 10000.0
EPS = 1e-6
NEG_INF = -1e30
A_HEADS = 4
A_QK = 32
A_V = 64
B_HEADS = 4
B_KV = 2
WINDOW = 128
C_HEADS = 4
C_KV = 2
D_HEADS = 4
D_Q_LORA = 256
D_KV_LORA = 128
D_NOPE = 64
D_ROPE = 32
D_V = 64
D_QK = D_NOPE + D_ROPE
N_BRANCH = 4
BRANCH_W = 256
PEER_HEADS = 8
PEER_KEYS = 128
PEER_EXPERTS = PEER_KEYS * PEER_KEYS
PEER_QDIM = 256
PEER_HALF = PEER_QDIM // 2
PEER_TOPK = 16

N_CTX = BATCH * SEQ
N_LAT = DEC_BATCH * DEC_SEQ
N_TOK = N_CTX + N_LAT
KV_LAT = PAST_LEN + DEC_SEQ
N_MODROWS = 8
SMALL_COLS = 2208
SMALL_PAD = 2304

VMEM_LIMIT = 56 * 1024 * 1024
EXHAUSTED = -3.0e38
LOG2E = math.log2(math.e)


def _params(sem, flags=None):
    return pltpu.CompilerParams(dimension_semantics=sem, vmem_limit_bytes=VMEM_LIMIT, flags=flags)


def _nt_dot(a, b):
    return lax.dot_general(a, b, (((1,), (1,)), ((), ())), preferred_element_type=F32)


def _dot(a, b):
    return jnp.dot(a, b, preferred_element_type=F32)


def _mod_kernel(c_ref, w_ref, b_ref, o_ref):
    c = c_ref[...]
    s = c * jax.nn.sigmoid(c)
    o_ref[...] = jnp.dot(s, w_ref[...], preferred_element_type=F32, precision=lax.Precision.HIGHEST) + b_ref[...]


def _modulation(cvec, ada_w, ada_b):
    depth = ada_w.shape[0]
    ncol = ada_w.shape[2] // D_MODEL
    return pl.pallas_call(
        _mod_kernel,
        grid=(depth, ncol),
        in_specs=[
            pl.BlockSpec((N_MODROWS, D_MODEL), lambda l, j: (0, 0)),
            pl.BlockSpec((None, D_MODEL, D_MODEL), lambda l, j: (l, 0, j)),
            pl.BlockSpec((None, 1, D_MODEL), lambda l, j: (l, 0, j)),
        ],
        out_specs=pl.BlockSpec((None, N_MODROWS, D_MODEL), lambda l, j: (l, 0, j)),
        out_shape=jax.ShapeDtypeStruct((depth, N_MODROWS, ada_w.shape[2]), F32),
        compiler_params=_params(("arbitrary", "arbitrary")),
        name="modulation",
    )(cvec, ada_w, ada_b.reshape(depth, 1, -1))


def _group_rms(x, ones_bd, gain, d):
    sq = x * x
    hi = sq.astype(BF16)
    lo = (sq - hi.astype(F32)).astype(BF16)
    ss = _dot(hi, ones_bd) + _dot(lo, ones_bd)
    return x * lax.rsqrt(ss * (1.0 / d) + EPS) * gain


def _row_rms(x, gain):
    ms = jnp.mean(x * x, axis=-1, keepdims=True)
    return x * lax.rsqrt(ms + EPS) * gain


def _rope(x, tab_ref, q4):
    w = x.shape[-1]
    return x * tab_ref[0] + pltpu.roll(x, w - q4, 1) * tab_ref[1] + pltpu.roll(x, q4, 1) * tab_ref[2]


GAIN_ROWS = 16
GAIN_W = D_HEADS * D_QK


def _in_kernel(x_ref, mod_ref, n1g_ref, ws_ref, gains_ref, e32_ref, e64_ref, e96_ref, wdq_ref,
               ta_ref, tb_ref, td_ref,
               h_ref, aq_ref, ak_ref, av_ref, bq_ref, bk_ref, bv_ref, cq_ref, ck_ref, cv_ref, dq_ref,
               akt_ref, avt_ref, bkt_ref, bvt_ref, ckt_ref, cvt_ref, dckv_ref, dkr_ref, *, tm, n_ctx):
    is_lat = pl.program_id(0) * tm >= n_ctx
    x = x_ref[...]
    h = _row_rms(x, n1g_ref[...]) * (1.0 + mod_ref[1:2, :]) + mod_ref[0:1, :]
    hb = h.astype(BF16)
    h_ref[...] = hb
    p = _dot(hb, ws_ref[...])

    def gain(r, w):
        return gains_ref[r:r + 1, 0:w]

    def heads_out(ref, val, nh, d):
        for i in range(nh):
            ref[i] = val[:, i * d:(i + 1) * d].astype(BF16)

    def maybe_rope(val, tab_ref, q4):
        return jnp.where(is_lat, _rope(val, tab_ref, q4), val)

    aq = maybe_rope(_group_rms(p[:, 0:256], e32_ref[...], gain(0, 256), A_QK), ta_ref, A_QK // 4)
    ak = maybe_rope(_group_rms(p[:, 256:512], e32_ref[...], gain(1, 256), A_QK), ta_ref, A_QK // 4)
    av = p[:, 512:768]
    heads_out(aq_ref, aq * (A_QK ** -0.5 * LOG2E), 2 * A_HEADS, A_QK)
    heads_out(ak_ref, ak, 2 * A_HEADS, A_QK)
    heads_out(av_ref, av, A_HEADS, A_V)
    akt_ref[...] = ak
    avt_ref[...] = av
    bq = maybe_rope(_group_rms(p[:, 768:1024], e64_ref[...], gain(2, 256), HEAD_DIM), tb_ref, HEAD_DIM // 4)
    bk = _group_rms(p[:, 1024:1152], e64_ref[0:128, 0:128], gain(3, 128), HEAD_DIM)
    bk = jnp.where(is_lat, bk * tb_ref[0, :, 0:128] + pltpu.roll(bk, 128 - 16, 1) * tb_ref[1, :, 0:128]
                   + pltpu.roll(bk, 16, 1) * tb_ref[2, :, 0:128], bk)
    bv = p[:, 1152:1280]
    heads_out(bq_ref, bq * (HEAD_DIM ** -0.5 * LOG2E), B_HEADS, HEAD_DIM)
    heads_out(bk_ref, bk, B_KV, HEAD_DIM)
    heads_out(bv_ref, bv, B_KV, HEAD_DIM)
    bkt_ref[...] = bk
    bvt_ref[...] = bv
    cq = maybe_rope(_group_rms(p[:, 1280:1536], e64_ref[...], gain(4, 256), HEAD_DIM), tb_ref, HEAD_DIM // 4)
    ck = _group_rms(p[:, 1536:1664], e64_ref[0:128, 0:128], gain(5, 128), HEAD_DIM)
    ck = jnp.where(is_lat, ck * tb_ref[0, :, 0:128] + pltpu.roll(ck, 128 - 16, 1) * tb_ref[1, :, 0:128]
                   + pltpu.roll(ck, 16, 1) * tb_ref[2, :, 0:128], ck)
    cv = p[:, 1664:1792]
    heads_out(cq_ref, cq * (HEAD_DIM ** -0.5 * LOG2E), C_HEADS, HEAD_DIM)
    heads_out(ck_ref, ck, C_KV, HEAD_DIM)
    heads_out(cv_ref, cv, C_KV, HEAD_DIM)
    ckt_ref[...] = ck
    cvt_ref[...] = cv
    dqa = _row_rms(p[:, 1792:2048], gain(6, 256))
    dq = _dot(dqa.astype(BF16), wdq_ref[...])
    dq = maybe_rope(_group_rms(dq, e96_ref[...], gain(8, GAIN_W), D_QK), td_ref, D_ROPE // 4)
    heads_out(dq_ref, dq * (D_QK ** -0.5 * LOG2E), D_HEADS, D_QK)
    dckv_ref[...] = _row_rms(p[:, 2048:2176], gain(7, 128))
    dkr_ref[...] = p[:, 2176:2208]


def _input_stage(x, mod3, n1g, ws, gains, e32, e64, e96, wdq, tab_a, tab_b, tab_d, *, tm=256):
    n = x.shape[0]
    nt = n // tm
    lat0 = N_CTX // tm
    pos_blocks = DEC_SEQ // tm

    def tok(i):
        return (i, 0)

    def pos(i):
        return (0, jnp.maximum(i - lat0, 0) % pos_blocks, 0)

    def const2(i):
        return (0, 0)

    def hm(i):
        return (0, i, 0)

    def hspec(nh, d):
        return pl.BlockSpec((nh, tm, d), hm)

    def hshape(nh, d):
        return jax.ShapeDtypeStruct((nh, n, d), BF16)

    def tspec(w):
        return pl.BlockSpec((tm, w), tok)

    def tshape(w):
        return jax.ShapeDtypeStruct((n, w), F32)

    in_specs = [
        pl.BlockSpec((tm, D_MODEL), tok),
        pl.BlockSpec((None, 6, D_MODEL), lambda i: ((i * tm) // DEC_SEQ, 0, 0)),
        pl.BlockSpec((1, D_MODEL), const2),
        pl.BlockSpec((D_MODEL, SMALL_PAD), const2),
        pl.BlockSpec((GAIN_ROWS, GAIN_W), const2),
        pl.BlockSpec((256, 256), const2),
        pl.BlockSpec((256, 256), const2),
        pl.BlockSpec((GAIN_W, GAIN_W), const2),
        pl.BlockSpec((D_Q_LORA, GAIN_W), const2),
        pl.BlockSpec((3, tm, 256), pos),
        pl.BlockSpec((3, tm, 256), pos),
        pl.BlockSpec((3, tm, GAIN_W), pos),
    ]
    out_specs = [
        pl.BlockSpec((tm, D_MODEL), tok),
        hspec(8, A_QK), hspec(8, A_QK), hspec(4, A_V),
        hspec(4, HEAD_DIM), hspec(2, HEAD_DIM), hspec(2, HEAD_DIM),
        hspec(4, HEAD_DIM), hspec(2, HEAD_DIM), hspec(2, HEAD_DIM),
        hspec(4, D_QK),
        tspec(256), tspec(256), tspec(128), tspec(128), tspec(128), tspec(128), tspec(128), tspec(D_ROPE),
    ]
    out_shape = [
        jax.ShapeDtypeStruct((n, D_MODEL), BF16),
        hshape(8, A_QK), hshape(8, A_QK), hshape(4, A_V),
        hshape(4, HEAD_DIM), hshape(2, HEAD_DIM), hshape(2, HEAD_DIM),
        hshape(4, HEAD_DIM), hshape(2, HEAD_DIM), hshape(2, HEAD_DIM),
        hshape(4, D_QK),
        tshape(256), tshape(256), tshape(128), tshape(128), tshape(128), tshape(128), tshape(128), tshape(D_ROPE),
    ]
    return pl.pallas_call(
        functools.partial(_in_kernel, tm=tm, n_ctx=N_CTX),
        grid=(nt,),
        in_specs=in_specs,
        out_specs=out_specs,
        out_shape=out_shape,
        compiler_params=_params(("arbitrary",)),
        name="input_stage",
    )(x, mod3, n1g, ws, gains, e32, e64, e96, wdq, tab_a, tab_b, tab_d)


def _mla_kernel(ckv_ref, kr_ref, wkvb_ref, g_ref, rot_ref, tab_ref, kd_ref, vd_ref, *, tm, lat_tiles, per_batch_tiles):
    i = pl.program_id(0)
    rotate = jnp.logical_and(i < lat_tiles, i % per_batch_tiles >= PAST_LEN // tm)
    kv = _dot(ckv_ref[...].astype(BF16), wkvb_ref[...])
    kr = kr_ref[...]
    ssr = jnp.sum(kr * kr, axis=-1, keepdims=True)
    z = kr * g_ref[:, D_NOPE:D_QK]
    zh = z.astype(BF16)
    zl = (z - zh.astype(F32)).astype(BF16)
    rot = _dot(zh, rot_ref[...]) + _dot(zl, rot_ref[...])
    rz = jnp.where(rotate, z * tab_ref[0] + rot * tab_ref[1], z)
    g_nope = g_ref[:, 0:D_NOPE]
    for hd in range(D_HEADS):
        kn = kv[:, hd * 128:hd * 128 + D_NOPE]
        r = lax.rsqrt((jnp.sum(kn * kn, axis=-1, keepdims=True) + ssr) * (1.0 / D_QK) + EPS)
        kd_ref[hd] = jnp.concatenate([kn * r * g_nope, rz * r], axis=-1).astype(BF16)
        vd_ref[hd] = kv[:, hd * 128 + D_NOPE:(hd + 1) * 128].astype(BF16)


def _mla_expand(ckv, kr, wkvb, g, rot, tab, *, tm=512):
    rows = ckv.shape[0]
    per_batch_tiles = KV_LAT // tm
    lat_tiles = DEC_BATCH * per_batch_tiles
    past_tiles = PAST_LEN // tm
    pos_blocks = DEC_SEQ // tm

    def pos(i):
        return (0, jnp.maximum(i % per_batch_tiles - past_tiles, 0) % pos_blocks, 0)

    return pl.pallas_call(
        functools.partial(_mla_kernel, tm=tm, lat_tiles=lat_tiles, per_batch_tiles=per_batch_tiles),
        grid=(rows // tm,),
        in_specs=[
            pl.BlockSpec((tm, D_KV_LORA), lambda i: (i, 0)),
            pl.BlockSpec((tm, D_ROPE), lambda i: (i, 0)),
            pl.BlockSpec((D_KV_LORA, D_HEADS * (D_NOPE + D_V)), lambda i: (0, 0)),
            pl.BlockSpec((1, D_QK), lambda i: (0, 0)),
            pl.BlockSpec((D_ROPE, D_ROPE), lambda i: (0, 0)),
            pl.BlockSpec((2, tm, D_ROPE), pos),
        ],
        out_specs=[
            pl.BlockSpec((D_HEADS, tm, D_QK), lambda i: (0, i, 0)),
            pl.BlockSpec((D_HEADS, tm, D_V), lambda i: (0, i, 0)),
        ],
        out_shape=[
            jax.ShapeDtypeStruct((D_HEADS, rows, D_QK), BF16),
            jax.ShapeDtypeStruct((D_HEADS, rows, D_V), BF16),
        ],
        compiler_params=_params(("arbitrary",)),
        name="mla_expand",
    )(ckv, kr, wkvb, g, rot, tab)


def _attn_kernel(*refs, g_q, g_k, g_v, dv, mode, lam_init, has_cache):
    refs = list(refs)
    q_ref, k_ref, v_ref = refs[:3]
    pos = 3
    sink_ref = lam_ref = sub_ref = None
    if has_cache:
        kc_ref, vc_ref = refs[pos], refs[pos + 1]
        pos += 2
    if mode == "sink":
        sink_ref = refs[pos]
        pos += 1
    elif mode == "diff":
        lam_ref, sub_ref = refs[pos], refs[pos + 1]
        pos += 2
    o_ref = refs[pos]
    grp = pl.program_id(1)

    if has_cache:
        kall_ref, vall_ref = refs[pos + 1], refs[pos + 2]
        past = kc_ref.shape[1]

        @pl.when(pl.program_id(2) == 0)
        def _join():
            kall_ref[:, 0:past, :] = kc_ref[...]
            kall_ref[:, past:, :] = k_ref[...]
            vall_ref[:, 0:past, :] = vc_ref[...]
            vall_ref[:, past:, :] = v_ref[...]

        k_ref, v_ref = kall_ref, vall_ref

    def probs(g):
        s = _nt_dot(q_ref[g], k_ref[g * g_k // g_q])
        m = jnp.max(s, axis=-1, keepdims=True)
        if mode == "sink":
            sk = sink_ref[grp * g_q + g] * LOG2E
            m = jnp.maximum(m, sk)
        p = jnp.exp2(s - m)
        den = jnp.sum(p, axis=-1, keepdims=True)
        if mode == "sink":
            den = den + jnp.exp2(sk - m)
        return p, den

    if mode == "diff":
        lp = lam_ref[...]
        lam = (jnp.exp(jnp.sum(lp[0:1] * lp[1:2], axis=-1, keepdims=True))
               - jnp.exp(jnp.sum(lp[2:3] * lp[3:4], axis=-1, keepdims=True)) + lam_init)
        for i in range(g_q // 2):
            p1, den1 = probs(2 * i)
            p2, den2 = probs(2 * i + 1)
            w1 = 1.0 / den1
            comb = p1 - p2 * (lam * den1 / den2)
            d = _dot(comb.astype(BF16), v_ref[i * g_v // (g_q // 2)]) * w1
            o_ref[:, i * dv:(i + 1) * dv] = _row_rms(d, sub_ref[...]) * (1.0 - lam_init)
    else:
        for g in range(g_q):
            p, den = probs(g)
            o_ref[:, g * dv:(g + 1) * dv] = _dot(p.astype(BF16), v_ref[g * g_v // g_q]) * (1.0 / den)


def _attention(q, k, v, cache, extras, *, batch, tq_rows, tk_rows, q_off, k_off,
               g_q, g_k, g_v, dv, mode, lam_init, tq):
    hq, _, dq = q.shape
    n_groups = hq // g_q
    g_out = g_q // 2 if mode == "diff" else g_q
    nq = tq_rows // tq
    tk = tk_rows
    in_specs = [
        pl.BlockSpec((g_q, tq, dq), lambda b, g, qi: (g, (q_off + b * tq_rows) // tq + qi, 0)),
        pl.BlockSpec((g_k, tk, dq), lambda b, g, qi: (g, (k_off + b * tk_rows) // tk, 0)),
        pl.BlockSpec((g_v, tk, dv), lambda b, g, qi: (g, (k_off + b * tk_rows) // tk, 0)),
    ]
    args = [q, k, v]
    if cache is not None:
        in_specs.append(pl.BlockSpec((g_k, PAST_LEN, dq), lambda b, g, qi: (g, b, 0)))
        in_specs.append(pl.BlockSpec((g_v, PAST_LEN, dv), lambda b, g, qi: (g, b, 0)))
        args += list(cache)
    if mode == "sink":
        in_specs.append(pl.BlockSpec(memory_space=pltpu.SMEM))
        args.append(extras[0])
    elif mode == "diff":
        in_specs.append(pl.BlockSpec((4, A_QK), lambda b, g, qi: (0, 0)))
        in_specs.append(pl.BlockSpec((1, dv), lambda b, g, qi: (0, 0)))
        args += list(extras)
    return pl.pallas_call(
        functools.partial(_attn_kernel, g_q=g_q, g_k=g_k, g_v=g_v, dv=dv, mode=mode, lam_init=lam_init,
                          has_cache=cache is not None),
        grid=(batch, n_groups, nq),
        in_specs=in_specs,
        out_specs=pl.BlockSpec((tq, g_out * dv), lambda b, g, qi: (b * nq + qi, g)),
        out_shape=jax.ShapeDtypeStruct((batch * tq_rows, n_groups * g_out * dv), F32),
        scratch_shapes=([pltpu.VMEM((g_k, PAST_LEN + tk, dq), BF16), pltpu.VMEM((g_v, PAST_LEN + tk, dv), BF16)]
                        if cache is not None else []),
        compiler_params=_params(("arbitrary", "arbitrary", "arbitrary")),
        name="attention_" + mode,
    )(*args)


def _win_kernel(q_ref, k_ref, v_ref, kc_ref, vc_ref, sink_ref, o_ref, *, g_q, dv, tq, seq, band):
    qi = pl.program_id(1)
    start = pl.multiple_of(jnp.clip(qi * tq - WINDOW, 0, seq - band), WINDOW)
    qpos = qi * tq + lax.broadcasted_iota(jnp.int32, (tq, band), 0)
    kpos = start + lax.broadcasted_iota(jnp.int32, (tq, band), 1)
    valid = jnp.abs(kpos - qpos) <= WINDOW
    for kvh in range(k_ref.shape[0]):
        k_ctx = kc_ref[kvh]
        v_ctx = vc_ref[kvh]
        k_band = k_ref[kvh, pl.ds(start, band), :]
        v_band = v_ref[kvh, pl.ds(start, band), :]
        for g in range(g_q):
            head = kvh * g_q + g
            q = q_ref[head]
            s_ctx = _nt_dot(q, k_ctx)
            s_band = jnp.where(valid, _nt_dot(q, k_band), NEG_INF)
            sk = sink_ref[head] * LOG2E
            m = jnp.maximum(jnp.maximum(jnp.max(s_ctx, axis=-1, keepdims=True),
                                        jnp.max(s_band, axis=-1, keepdims=True)), sk)
            p_ctx = jnp.exp2(s_ctx - m)
            p_band = jnp.exp2(s_band - m)
            den = (jnp.sum(p_ctx, axis=-1, keepdims=True) + jnp.sum(p_band, axis=-1, keepdims=True)
                   + jnp.exp2(sk - m))
            num = _dot(p_ctx.astype(BF16), v_ctx) + _dot(p_band.astype(BF16), v_band)
            o_ref[:, head * dv:(head + 1) * dv] = num * (1.0 / den)


def _window_attention(q, k, v, kc, vc, sink, *, q_off, k_off, tq=256):
    hq, _, d = q.shape
    hk = k.shape[0]
    g_q = hq // hk
    band = tq + 2 * WINDOW
    nq = DEC_SEQ // tq
    return pl.pallas_call(
        functools.partial(_win_kernel, g_q=g_q, dv=d, tq=tq, seq=DEC_SEQ, band=band),
        grid=(DEC_BATCH, nq),
        in_specs=[
            pl.BlockSpec((hq, tq, d), lambda b, qi: (0, (q_off + b * DEC_SEQ) // tq + qi, 0)),
            pl.BlockSpec((hk, DEC_SEQ, d), lambda b, qi: (0, k_off // DEC_SEQ + b, 0)),
            pl.BlockSpec((hk, DEC_SEQ, d), lambda b, qi: (0, k_off // DEC_SEQ + b, 0)),
            pl.BlockSpec((hk, PAST_LEN, d), lambda b, qi: (0, b, 0)),
            pl.BlockSpec((hk, PAST_LEN, d), lambda b, qi: (0, b, 0)),
            pl.BlockSpec(memory_space=pltpu.SMEM),
        ],
        out_specs=pl.BlockSpec((tq, hq * d), lambda b, qi: (b * nq + qi, 0)),
        out_shape=jax.ShapeDtypeStruct((DEC_BATCH * DEC_SEQ, hq * d), F32),
        compiler_params=_params(("arbitrary", "arbitrary")),
        name="attention_window",
    )(q, k, v, kc, vc, sink)


def _merge_kernel(x_ref, h_ref, oac_ref, oal_ref, obc_ref, obl_ref, occ_ref, ocl_ref, odc_ref, odl_ref,
                  mod_ref, wg_ref, wb_ref, wo_ref, n2g_ref, x1_ref, h2_ref, *, ctx_tiles):
    is_ctx = pl.program_id(0) < ctx_tiles
    hb = h_ref[...]
    outs = ((oac_ref, oal_ref), (obc_ref, obl_ref), (occ_ref, ocl_ref), (odc_ref, odl_ref))
    acc = None
    for i in range(N_BRANCH):
        gt = jax.nn.sigmoid(_dot(hb, wg_ref[:, i * D_MODEL:(i + 1) * D_MODEL]))
        o = jnp.where(is_ctx, outs[i][0][...], outs[i][1][...])
        br = _dot(o.astype(BF16), wb_ref[i])
        acc = gt * br if acc is None else acc + gt * br
    y = _dot(acc.astype(BF16), wo_ref[...])
    x1 = x_ref[...] + mod_ref[2:3, :] * y
    x1_ref[...] = x1
    h2 = _row_rms(x1, n2g_ref[...]) * (1.0 + mod_ref[4:5, :]) + mod_ref[3:4, :]
    h2_ref[...] = h2.astype(BF16)


def _merge_stage(x, h, branches, mod3, wg, wb, wo, n2g, *, tm=256):
    n = x.shape[0]
    ctx_tiles = N_CTX // tm

    def tok(i):
        return (i, 0)

    ctx_spec = pl.BlockSpec((tm, BRANCH_W), lambda i: (jnp.minimum(i, ctx_tiles - 1), 0))
    lat_spec = pl.BlockSpec((tm, BRANCH_W), lambda i: (jnp.maximum(i - ctx_tiles, 0), 0))
    return pl.pallas_call(
        functools.partial(_merge_kernel, ctx_tiles=ctx_tiles),
        grid=(n // tm,),
        in_specs=[
            pl.BlockSpec((tm, D_MODEL), tok),
            pl.BlockSpec((tm, D_MODEL), tok),
            ctx_spec, lat_spec, ctx_spec, lat_spec, ctx_spec, lat_spec, ctx_spec, lat_spec,
            pl.BlockSpec((None, 6, D_MODEL), lambda i: ((i * tm) // DEC_SEQ, 0, 0)),
            pl.BlockSpec((D_MODEL, N_BRANCH * D_MODEL), lambda i: (0, 0)),
            pl.BlockSpec((N_BRANCH, BRANCH_W, D_MODEL), lambda i: (0, 0, 0)),
            pl.BlockSpec((D_MODEL, D_MODEL), lambda i: (0, 0)),
            pl.BlockSpec((1, D_MODEL), lambda i: (0, 0)),
        ],
        out_specs=[pl.BlockSpec((tm, D_MODEL), tok), pl.BlockSpec((tm, D_MODEL), tok)],
        out_shape=[jax.ShapeDtypeStruct((n, D_MODEL), F32), jax.ShapeDtypeStruct((n, D_MODEL), BF16)],
        compiler_params=_params(("arbitrary",)),
        name="merge_stage",
    )(x, h, *[part for pair in branches for part in pair], mod3, wg, wb, wo, n2g)


def _oddeven_mergesort_pairs(n):
    pairs = []

    def merge(lo, m, r):
        step = 2 * r
        if step < m:
            merge(lo, m, step)
            merge(lo + r, m, step)
            for i in range(lo + r, lo + m - r, step):
                pairs.append((i, i + r))
        else:
            pairs.append((lo, lo + r))

    def sort(lo, m):
        if m > 1:
            sort(lo, m // 2)
            sort(lo + m // 2, m // 2)
            merge(lo, m, 1)

    sort(0, n)
    return pairs


_SORT_PAIRS = _oddeven_mergesort_pairs(PEER_TOPK)


def _exchange(xs, i, j):
    xs[i], xs[j] = jnp.maximum(xs[i], xs[j]), jnp.minimum(xs[i], xs[j])


def _top16_of_groups(groups):
    n = len(groups)
    xs = list(groups)
    for i, j in _SORT_PAIRS:
        _exchange(xs, i, j)
    for shift in (4, 2, 1):
        xs = [jnp.maximum(xs[k], pltpu.roll(xs[n - 1 - k], shift, 0)) for k in range(n)]
        s = n // 2
        while s >= 1:
            for i in range(n):
                if not i & s:
                    _exchange(xs, i, i + s)
            s //= 2
    return xs


def _sublane_sum(x):
    for shift in (4, 2, 1):
        x = x + pltpu.roll(x, shift, 0)
    return x


def _sublane_spread(vals):
    row = lax.broadcasted_iota(jnp.int32, vals[0].shape, 0)
    out = vals[-1]
    for j in range(len(vals) - 2, -1, -1):
        out = jnp.where(row == j, vals[j], out)
    return out


def _peer_select(g1, g2):
    k = PEER_TOPK
    half = k // 2
    v1 = _top16_of_groups(g1)
    v2 = _top16_of_groups(g2)
    v2lo = _sublane_spread(v2[:half])
    v2hi = _sublane_spread(v2[half:])
    v1hi = _sublane_spread(v1[half:])
    cands = [v1[0] + v2lo, v1[0] + v2hi] + [v1[i] + v2lo for i in range(1, half)] + [v1hi + v2[0]]
    pad = jnp.full(cands[0].shape, EXHAUSTED, F32)
    top = _top16_of_groups(cands + [pad] * (k - len(cands)))
    tau = top[k - 1]
    z = None
    for c in cands:
        e = jnp.where(c >= tau, jnp.exp(c - top[0]), 0.0)
        z = e if z is None else z + e
    z = _sublane_sum(z)
    thr = [jnp.full(g1[0].shape, -1.0, F32) for _ in g1]
    q2 = [jnp.full(g2[0].shape, float(k), F32) for _ in g2]
    for a in range(k - 1, -1, -1):
        cnt = _sublane_sum(jnp.where(v1[a] + v2lo >= tau, 1.0, 0.0) + jnp.where(v1[a] + v2hi >= tau, 1.0, 0.0))
        thr = [jnp.where(x == v1[a], cnt - 1.0, t) for x, t in zip(g1, thr)]
        q2 = [jnp.where(x == v2[a], float(a), q) for x, q in zip(g2, q2)]
    p1 = [jnp.exp(x - v1[0]) for x in g1]
    scale = 0.5 / z
    p2 = [jnp.exp(x - v2[0]) * scale for x in g2]
    return thr, p1, q2, p2


def _peer_query_kernel(h2_ref, wqt_ref, sk_ref, thr_ref, p1_ref, q2_ref, p2_ref):
    qt = _nt_dot(wqt_ref[...], h2_ref[...]).astype(BF16)
    n_lt = qt.shape[1] // 128
    n_grp = PEER_KEYS // 8
    for hd in range(PEER_HEADS):
        s1 = _dot(sk_ref[2 * hd], qt[(2 * hd) * PEER_HALF:(2 * hd + 1) * PEER_HALF, :])
        s2 = _dot(sk_ref[2 * hd + 1], qt[(2 * hd + 1) * PEER_HALF:(2 * hd + 2) * PEER_HALF, :])
        for lt in range(n_lt):
            lanes = slice(lt * 128, (lt + 1) * 128)
            dst = (lt + 1) % n_lt
            rot = slice(dst * 128, (dst + 1) * 128)
            g1 = [s1[8 * i:8 * i + 8, lanes] for i in range(n_grp)]
            g2 = [s2[8 * i:8 * i + 8, lanes] for i in range(n_grp)]
            thr, p1, q2, p2 = _peer_select(g1, g2)
            thr_ref[hd, :, lanes] = jnp.concatenate(thr, axis=0)
            p1_ref[hd, :, lanes] = jnp.concatenate(p1, axis=0)
            q2_ref[hd, :, lanes] = jnp.concatenate(q2, axis=0).astype(BF16)
            p2_ref[hd, :, rot] = jnp.concatenate(p2, axis=0).astype(BF16)


def _peer_query(h2, wqt, sk, *, tm=512):
    n = h2.shape[0]
    big = jax.ShapeDtypeStruct((PEER_HEADS, PEER_KEYS, n), F32)
    half = jax.ShapeDtypeStruct((PEER_HEADS, PEER_KEYS, n), BF16)
    bspec = pl.BlockSpec((PEER_HEADS, PEER_KEYS, tm), lambda i: (0, 0, i))
    return pl.pallas_call(
        _peer_query_kernel,
        grid=(n // tm,),
        in_specs=[
            pl.BlockSpec((tm, D_MODEL), lambda i: (i, 0)),
            pl.BlockSpec((PEER_HEADS * PEER_QDIM, D_MODEL), lambda i: (0, 0)),
            pl.BlockSpec((2 * PEER_HEADS, PEER_KEYS, PEER_HALF), lambda i: (0, 0, 0)),
        ],
        out_specs=[bspec, bspec, bspec, bspec],
        out_shape=[big, big, half, half],
        compiler_params=_params(("arbitrary",)),
        name="peer_query",
    )(h2, wqt, sk)


def _peer_dense_kernel(h2_ref, x1_ref, mod_ref, u_ref, vt_ref, thr_ref, p1_ref, q2_ref, p2_ref,
                       *rest, e1_per_step, split_tiles):
    out_refs = rest[:-5]
    acc_ref, g_ref, p_ref, q2s_ref, p2s_ref = rest[-5:]
    step = pl.program_id(1)
    tm = g_ref.shape[1]
    n_lt = tm // 128
    rows = 16
    n_r = PEER_KEYS // rows

    def gate_pass():
        for c in range(e1_per_step):
            for lt in range(n_lt):
                lanes = slice(lt * 128, (lt + 1) * 128)
                rot = slice(((lt + 1) % n_lt) * 128, ((lt + 1) % n_lt + 1) * 128)
                g = [None] * n_r
                for hd in range(PEER_HEADS):
                    thr_b = jnp.broadcast_to(thr_ref[hd, c:c + 1, lanes], (rows, 128)).astype(BF16)
                    p1_b = jnp.broadcast_to(p1_ref[hd, c:c + 1, lanes], (rows, 128)).astype(BF16)
                    for r in range(n_r):
                        rs = slice(r * rows, (r + 1) * rows)
                        w = jnp.where(q2s_ref[hd, rs, lanes] <= thr_b, p2s_ref[hd, rs, rot] * p1_b,
                                      jnp.zeros((), BF16))
                        g[r] = w if g[r] is None else g[r] + w
                for r in range(n_r):
                    g_ref[c * PEER_KEYS + r * rows:c * PEER_KEYS + (r + 1) * rows, lanes] = g[r]

    @pl.when(step == 0)
    def _init():
        acc_ref[...] = jnp.zeros_like(acc_ref)
        q2s_ref[...] = q2_ref[...]
        p2s_ref[...] = p2_ref[...]

    pl.when(step >= 0)(gate_pass)

    a = _nt_dot(u_ref[...], h2_ref[...])
    act = (a + a * lax.erf(a * (2.0 ** -0.5))).astype(BF16)
    p_ref[...] = g_ref[...] * act
    acc_ref[...] += _dot(vt_ref[...], p_ref[...])

    @pl.when(step == pl.num_programs(1) - 1)
    def _fin():
        val = x1_ref[...] + mod_ref[5:6, :] * acc_ref[...].T
        if split_tiles is None:
            out_refs[0][...] = val
        else:
            is_ctx = pl.program_id(0) < split_tiles

            @pl.when(is_ctx)
            def _ctx_rows():
                out_refs[0][...] = val

            @pl.when(jnp.logical_not(is_ctx))
            def _lat_rows():
                out_refs[1][...] = val


def _peer_dense(h2, x1, mod3, u, vt, thr, p1, q2, p2, *, layer, split=False, tm=512, e1_per_step=16):
    n = h2.shape[0]
    te = e1_per_step * PEER_KEYS
    tok = lambda i, j: (i, 0)
    big = pl.BlockSpec((PEER_HEADS, PEER_KEYS, tm), lambda i, j: (0, 0, i))
    rows = pl.BlockSpec((PEER_HEADS, e1_per_step, tm), lambda i, j: (0, j, i))
    ctx_tiles = N_CTX // tm
    if split:
        out_specs = [pl.BlockSpec((tm, D_MODEL), lambda i, j: (jnp.minimum(i, ctx_tiles - 1), 0)),
                     pl.BlockSpec((tm, D_MODEL), lambda i, j: (jnp.maximum(i - ctx_tiles, 0), 0))]
        out_shape = [jax.ShapeDtypeStruct((N_CTX, D_MODEL), F32), jax.ShapeDtypeStruct((n - N_CTX, D_MODEL), F32)]
    else:
        out_specs = pl.BlockSpec((tm, D_MODEL), tok)
        out_shape = jax.ShapeDtypeStruct((n, D_MODEL), F32)
    return pl.pallas_call(
        functools.partial(_peer_dense_kernel, e1_per_step=e1_per_step, split_tiles=ctx_tiles if split else None),
        grid=(n // tm, PEER_EXPERTS // te),
        in_specs=[
            pl.BlockSpec((tm, D_MODEL), tok),
            pl.BlockSpec((tm, D_MODEL), tok),
            pl.BlockSpec((None, 6, D_MODEL), lambda i, j: ((i * tm) // DEC_SEQ, 0, 0)),
            pl.BlockSpec((None, te, D_MODEL), lambda i, j: (layer, j, 0)),
            pl.BlockSpec((None, D_MODEL, te), lambda i, j: (layer, 0, j)),
            rows, rows, big, big,
        ],
        out_specs=out_specs,
        out_shape=out_shape,
        scratch_shapes=[pltpu.VMEM((D_MODEL, tm), F32), pltpu.VMEM((te, tm), BF16), pltpu.VMEM((te, tm), BF16),
                        pltpu.VMEM((PEER_HEADS, PEER_KEYS, tm), BF16), pltpu.VMEM((PEER_HEADS, PEER_KEYS, tm), BF16)],
        compiler_params=_params(("arbitrary", "arbitrary")),
        name="peer_dense",
    )(h2, x1, mod3, u, vt, thr, p1, q2, p2)


def _block_ones(width, d):
    idx = jnp.arange(width) // d
    return (idx[:, None] == idx[None, :]).astype(BF16)


def _rope_angles(d):
    q4 = d // 4
    inv = ROPE_BASE ** (-jnp.arange(q4, dtype=F32) / q4)
    t = jnp.arange(DEC_SEQ)
    ar = (t // GRID_W).astype(F32)[:, None] * inv
    ac = (t % GRID_W).astype(F32)[:, None] * inv
    return jnp.concatenate([ar, ar, ac, ac], axis=-1)


def _rope_table(d, group, n_groups):
    ang = _rope_angles(d)
    q4 = d // 4
    first = (jnp.arange(d) % (2 * q4)) < q4
    cos = jnp.cos(ang)
    sin = jnp.sin(ang)
    sa = jnp.where(first[None, :], -sin, 0.0)
    sb = jnp.where(first[None, :], 0.0, sin)
    pad = group - d
    cos = jnp.pad(cos, ((0, 0), (pad, 0)), constant_values=1.0)
    sa = jnp.pad(sa, ((0, 0), (pad, 0)))
    sb = jnp.pad(sb, ((0, 0), (pad, 0)))
    return jnp.stack([jnp.tile(cos, (1, n_groups)), jnp.tile(sa, (1, n_groups)), jnp.tile(sb, (1, n_groups))])


def _rot_matrix(d):
    q4 = d // 4
    i = jnp.arange(d)
    first = (i % (2 * q4)) < q4
    src = jnp.where(first, i + q4, i - q4)
    sign = jnp.where(first, -1.0, 1.0)
    return (jnp.zeros((d, d), F32).at[src, i].set(sign)).astype(BF16)


def _pad_row(v, width):
    return jnp.pad(v, (0, width - v.shape[0]))


def kernel(x_prompt, x_sample, c, cache_a_k, cache_a_v, cache_b_k, cache_b_v, cache_c_k, cache_c_v, cache_d_ckv, cache_d_krope, c_ctx, ada_w, ada_b, norm1_g, norm2_g, w_in, a_lam, a_qk_g, a_subln_g, b_sink, b_qk_g, c_qk_g, d_qnorm_g, d_kvnorm_g, w_d_qb, w_d_kvb, d_qk_g, w_branch, w_out, peer_wq, peer_subkeys, peer_u, peer_v):
    x = jnp.concatenate([x_prompt.reshape(N_CTX, D_MODEL), x_sample.reshape(N_LAT, D_MODEL)], axis=0)
    cvec = jnp.concatenate([c_ctx[None, :], c, jnp.zeros((N_MODROWS - 1 - DEC_BATCH, D_MODEL), F32)], axis=0)
    mod_all = _modulation(cvec, ada_w, ada_b)

    e32 = _block_ones(256, A_QK)
    e64 = _block_ones(256, HEAD_DIM)
    e96 = _block_ones(GAIN_W, D_QK)
    tab_a = _rope_table(A_QK, A_QK, 2 * A_HEADS)
    tab_b = _rope_table(HEAD_DIM, HEAD_DIM, B_HEADS)
    tab_d = _rope_table(D_ROPE, D_QK, D_HEADS)
    ang_r = _rope_angles(D_ROPE)
    tab_r = jnp.stack([jnp.cos(ang_r), jnp.sin(ang_r)])
    rot_r = _rot_matrix(D_ROPE)

    u_all = peer_u.astype(BF16)
    vt_all = jnp.swapaxes(peer_v, 1, 2).astype(BF16)
    states = [[] for _ in range(8)]
    for l in range(DEPTH):
        lam_init = 0.8 - 0.6 * math.exp(-0.3 * l)
        mod3 = mod_all[l, :3].reshape(3, 6, D_MODEL)
        ws = jnp.pad(w_in[l, :, :SMALL_COLS], ((0, 0), (0, SMALL_PAD - SMALL_COLS))).astype(BF16)
        wg = w_in[l, :, SMALL_COLS:].astype(BF16)
        gains = jnp.stack([
            _pad_row(jnp.tile(a_qk_g[l, 0], 2 * A_HEADS), GAIN_W),
            _pad_row(jnp.tile(a_qk_g[l, 1], 2 * A_HEADS), GAIN_W),
            _pad_row(jnp.tile(b_qk_g[l, 0], B_HEADS), GAIN_W),
            _pad_row(jnp.tile(b_qk_g[l, 1], B_KV), GAIN_W),
            _pad_row(jnp.tile(c_qk_g[l, 0], C_HEADS), GAIN_W),
            _pad_row(jnp.tile(c_qk_g[l, 1], C_KV), GAIN_W),
            _pad_row(d_qnorm_g[l], GAIN_W),
            _pad_row(d_kvnorm_g[l], GAIN_W),
            jnp.tile(d_qk_g[l, 0], D_HEADS),
        ] + [jnp.zeros((GAIN_W,), F32)] * (GAIN_ROWS - 9))

        (h, aq, ak, av, bq, bk, bv, cq, ck, cv, dq,
         akt, avt, bkt, bvt, ckt, cvt, dckv, dkr) = _input_stage(
            x, mod3, norm1_g[l][None, :], ws, gains, e32, e64, e96, w_d_qb[l].astype(BF16), tab_a, tab_b, tab_d,
            tm=512)

        for lst, s_ in zip(states, (
                akt[:N_CTX].reshape(BATCH, SEQ, A_HEADS, 2, A_QK), avt[:N_CTX].reshape(BATCH, SEQ, A_HEADS, A_V),
                bkt[:N_CTX].reshape(BATCH, SEQ, B_KV, HEAD_DIM), bvt[:N_CTX].reshape(BATCH, SEQ, B_KV, HEAD_DIM),
                ckt[:N_CTX].reshape(BATCH, SEQ, C_KV, HEAD_DIM), cvt[:N_CTX].reshape(BATCH, SEQ, C_KV, HEAD_DIM),
                dckv[:N_CTX].reshape(BATCH, SEQ, D_KV_LORA), dkr[:N_CTX].reshape(BATCH, SEQ, D_ROPE))):
            lst.append(s_)

        ckv_rows = jnp.concatenate([t for b in range(DEC_BATCH) for t in (
            cache_d_ckv[b, l], dckv[N_CTX + b * DEC_SEQ:N_CTX + (b + 1) * DEC_SEQ])] + [dckv[:N_CTX]], axis=0)
        kr_rows = jnp.concatenate([t for b in range(DEC_BATCH) for t in (
            cache_d_krope[b, l], dkr[N_CTX + b * DEC_SEQ:N_CTX + (b + 1) * DEC_SEQ])] + [dkr[:N_CTX]], axis=0)
        kd, vd = _mla_expand(ckv_rows, kr_rows, w_d_kvb[l].astype(BF16), d_qk_g[l, 1][None, :], rot_r, tab_r)

        def cached(cache, nh, d):
            cache = jnp.moveaxis(cache.reshape(DEC_BATCH, PAST_LEN, nh, d), 2, 0)
            return cache.reshape(nh, DEC_BATCH * PAST_LEN, d).astype(BF16)

        a_cache = (cached(cache_a_k[:, l], 2 * A_HEADS, A_QK), cached(cache_a_v[:, l], A_HEADS, A_V))
        b_cache = (cached(cache_b_k[:, l], B_KV, HEAD_DIM), cached(cache_b_v[:, l], B_KV, HEAD_DIM))
        c_cache = (cached(cache_c_k[:, l], C_KV, HEAD_DIM), cached(cache_c_v[:, l], C_KV, HEAD_DIM))

        ctx = dict(batch=BATCH, tq_rows=SEQ, tk_rows=SEQ, q_off=0, lam_init=lam_init, tq=SEQ)
        lat = dict(batch=DEC_BATCH, tq_rows=DEC_SEQ, tk_rows=DEC_SEQ, q_off=N_CTX, k_off=N_CTX,
                   lam_init=lam_init, tq=256)
        a_extra = (a_lam[l], a_subln_g[l][None, :])
        oa = (_attention(aq, ak, av, None, a_extra, k_off=0, g_q=8, g_k=8, g_v=4, dv=A_V, mode="diff", **ctx),
              _attention(aq, ak, av, a_cache, a_extra, g_q=4, g_k=4, g_v=2, dv=A_V, mode="diff", **lat))
        ob = (_attention(bq, bk, bv, None, (b_sink[l],), k_off=0, g_q=4, g_k=2, g_v=2, dv=HEAD_DIM, mode="sink",
                         **ctx),
              _window_attention(bq, bk, bv, *b_cache, b_sink[l], q_off=N_CTX, k_off=N_CTX))
        oc = (_attention(cq, ck, cv, None, (), k_off=0, g_q=4, g_k=2, g_v=2, dv=HEAD_DIM, mode="plain", **ctx),
              _attention(cq, ck, cv, c_cache, (), g_q=4, g_k=2, g_v=2, dv=HEAD_DIM, mode="plain", **lat))
        lat_d = dict(lat, tk_rows=KV_LAT, k_off=0)
        od = (_attention(dq, kd, vd, None, (), k_off=DEC_BATCH * KV_LAT, g_q=4, g_k=4, g_v=4, dv=D_V, mode="plain",
                         **ctx),
              _attention(dq, kd, vd, None, (), g_q=4, g_k=4, g_v=4, dv=D_V, mode="plain", **lat_d))

        x1, h2 = _merge_stage(x, h, (oa, ob, oc, od), mod3, wg, w_branch[l].astype(BF16), w_out[l].astype(BF16),
                              norm2_g[l][None, :])

        thr, p1, q2, p2 = _peer_query(
            h2, peer_wq[l].T.astype(BF16),
            peer_subkeys[l].reshape(2 * PEER_HEADS, PEER_KEYS, PEER_HALF).astype(BF16))
        x = _peer_dense(h2, x1, mod3, u_all, vt_all, thr, p1, q2, p2, layer=l, split=(l == DEPTH - 1))

    y_prompt = x[0].reshape(BATCH, SEQ, D_MODEL)
    y_sample = x[1].reshape(DEC_BATCH, DEC_SEQ, D_MODEL)
    return (y_prompt, y_sample) + tuple(jnp.stack(s_, axis=1) for s_ in states)
```
